```python
import jax
import jax.numpy as jnp
from jax import lax
import numpy as np

D_MODEL = 1024
BATCH = 2
SEQ = 8192
DEPTH = 2

HEAD_DIM = 64
N_MIXERS = 4
HEADS_PER_MIXER = D_MODEL // (N_MIXERS * HEAD_DIM)
GROUP_WIDTH = HEADS_PER_MIXER * HEAD_DIM
MIX_WIDTH = N_MIXERS * GROUP_WIDTH
BLOCK = 128
NORM_EPS = 1e-6
MLA_Q_RANK = 192
MLA_KV_RANK = 128
MLA_NOPE = 32
MLA_ROPE = 32
MLA_V = HEAD_DIM
ROPE_BASE = 10000.0
DILATED_PAIRS = ((128, 1), (512, 4), (2048, 16))
DIL_ALIGN = BLOCK * 16
SWA_WINDOW = 128
SWA_KV_HEADS = 2
SWA_GROUP = HEADS_PER_MIXER // SWA_KV_HEADS
N_GROUPS = 4
EXPERTS_PER_GROUP = 8
N_EXPERTS = N_GROUPS * EXPERTS_PER_GROUP
TOP_K = 2
D_EXPERT = 256
EXPERT_BLOCK = 128
IN_SPLITS = (MLA_Q_RANK, MLA_KV_RANK, MLA_ROPE, 3 * GROUP_WIDTH, 3 * GROUP_WIDTH, HEADS_PER_MIXER, GROUP_WIDTH, 2 * SWA_KV_HEADS * HEAD_DIM)
IN_WIDTH = sum(IN_SPLITS)

kernel_name = 'hybrid_headgroup_mla_dilated_fox_swa_hmoe'


def rmsnorm(x, g):
    xf = x.astype(jnp.float32)
    y = xf * lax.rsqrt(jnp.mean(xf * xf, axis=-1, keepdims=True) + NORM_EPS)
    return (y * g.astype(jnp.float32)).astype(x.dtype)


def rope(x):
    S, dr = x.shape[1], x.shape[-1]
    inv = ROPE_BASE ** (-jnp.arange(0, dr, 2, dtype=jnp.float32) / dr)
    ang = jnp.arange(S, dtype=jnp.float32)[:, None] * inv[None, :]
    cos = jnp.cos(ang)[None, :, None, :]
    sin = jnp.sin(ang)[None, :, None, :]
    xf = x.astype(jnp.float32)
    x1, x2 = xf[..., : dr // 2], xf[..., dr // 2:]
    return jnp.concatenate([x1 * cos - x2 * sin, x2 * cos + x1 * sin], axis=-1).astype(x.dtype)


def alibi_slopes():
    n = 2 * HEADS_PER_MIXER
    s = 2.0 ** (-8.0 * jnp.arange(1, n + 1, dtype=jnp.float32) / n)
    return s[HEADS_PER_MIXER:], s[:HEADS_PER_MIXER]


def banded_attention(q, k, v, max_dist, unit, slopes, sink=None):
    b, L, hk, g, dh = q.shape
    nb = L // BLOCK
    qb = q.reshape(b, nb, BLOCK, hk, g, dh)

    def band(t):
        t = t.reshape(b, nb, BLOCK, hk, dh)
        prev = jnp.concatenate([jnp.zeros_like(t[:, :1]), t[:, :-1]], axis=1)
        return jnp.concatenate([prev, t], axis=2)

    kc, vc = band(k), band(v)
    s = jnp.einsum('bnqhgd,bnkhd->bnhgqk', qb, kc).astype(jnp.float32) * (dh ** -0.5)
    dist = jnp.arange(BLOCK)[:, None] + BLOCK - jnp.arange(2 * BLOCK)[None, :]
    kpos = jnp.arange(nb)[:, None] * BLOCK + jnp.arange(2 * BLOCK)[None, :] - BLOCK
    valid = ((dist >= 0) & (dist <= max_dist))[None] & (kpos >= 0)[:, None, :]
    s = s - slopes[:, :, None, None] * (dist * unit).astype(jnp.float32)
    s = jnp.where(valid[None, :, None, None], s, -jnp.inf)
    lse = jax.nn.logsumexp(s, axis=-1)
    if sink is not None:
        lse = jnp.logaddexp(lse, sink.astype(jnp.float32)[:, :, None])
    p = jnp.exp(s - lse[..., None]).astype(v.dtype)
    out = jnp.einsum('bnhgqk,bnkhd->bnqhgd', p, vc).reshape(b, L, hk, g, dh)
    lse = jnp.moveaxis(lse, -1, 2).reshape(b, L, hk, g)
    return out, lse


def causal_block_attention(q, k, v, cum=None):
    b, S, h, dk = q.shape
    nb = S // BLOCK
    kpos = jnp.arange(S)
    cum_t = None if cum is None else jnp.swapaxes(cum, 1, 2)

    def one_block(n):
        start = n * BLOCK
        qblk = lax.dynamic_slice_in_dim(q, start, BLOCK, axis=1)
        s = jnp.einsum('bqhd,bkhd->bhqk', qblk, k).astype(jnp.float32) * (dk ** -0.5)
        if cum_t is not None:
            cq = lax.dynamic_slice_in_dim(cum_t, start, BLOCK, axis=2)
            s = s + cq[..., None] - cum_t[:, :, None, :]
        qpos = start + jnp.arange(BLOCK)
        s = jnp.where(kpos[None, :] <= qpos[:, None], s, -jnp.inf)
        p = jax.nn.softmax(s, axis=-1).astype(v.dtype)
        return jnp.einsum('bhqk,bkhd->bqhd', p, v)

    out = lax.map(one_block, jnp.arange(nb))
    return jnp.moveaxis(out, 0, 1).reshape(b, S, h, v.shape[-1])


def dilated_attention(q, k, v, slopes):
    b, S, h, dh = q.shape
    Sp = -(-S // DIL_ALIGN) * DIL_ALIGN
    pad = ((0, 0), (0, Sp - S), (0, 0), (0, 0))
    q, k, v = (jnp.pad(t, pad) for t in (q, k, v))
    outs, lses = [], []
    for window, dil in DILATED_PAIRS:
        L = Sp // dil

        def fold(t):
            return t.reshape(b, L, dil, h, dh).transpose(0, 2, 1, 3, 4).reshape(b * dil, L, h, dh)

        o, lse = banded_attention(fold(q)[:, :, :, None], fold(k), fold(v), window // dil, dil, slopes[:, None])
        outs.append(o[:, :, :, 0].reshape(b, dil, L, h, dh).transpose(0, 2, 1, 3, 4).reshape(b, Sp, h, dh))
        lses.append(lse[..., 0].reshape(b, dil, L, h).transpose(0, 2, 1, 3).reshape(b, Sp, h))
    w = jax.nn.softmax(jnp.stack(lses), axis=0).astype(q.dtype)
    out = jnp.einsum('nbsh,nbshd->bshd', w, jnp.stack(outs))
    return out[:, :S]


def hierarchical_moe(xn, w_rg, b_rg, w_re, b_re, w1, w3, w2):
    b, S, d = xn.shape
    xt = xn.reshape(-1, d)
    T = xt.shape[0]
    g_prob = jax.nn.softmax((xt @ w_rg).astype(jnp.float32) + b_rg.astype(jnp.float32), axis=-1)
    g_sel = jnp.argmax(g_prob, axis=-1)
    p_g = jnp.max(g_prob, axis=-1)
    e_logit = ((xt @ w_re).astype(jnp.float32) + b_re.astype(jnp.float32)).reshape(T, N_GROUPS, EXPERTS_PER_GROUP)
    e_logit = jnp.take_along_axis(e_logit, g_sel[:, None, None], axis=1)[:, 0]
    top_p, top_i = lax.top_k(jax.nn.softmax(e_logit, axis=-1), TOP_K)
    gate = p_g[:, None] * top_p / jnp.sum(top_p, axis=-1, keepdims=True)
    expert = g_sel[:, None] * EXPERTS_PER_GROUP + top_i

    A = T * TOP_K
    flat_e = expert.reshape(-1)
    flat_tok = jnp.repeat(jnp.arange(T), TOP_K)
    flat_gate = gate.reshape(-1)
    order = jnp.argsort(flat_e)
    se = flat_e[order]
    counts = jnp.bincount(flat_e, length=N_EXPERTS)
    start = jnp.cumsum(counts) - counts
    padded = (counts + EXPERT_BLOCK - 1) // EXPERT_BLOCK * EXPERT_BLOCK
    pend = jnp.cumsum(padded)
    pstart = pend - padded
    dest = pstart[se] + jnp.arange(A) - start[se]
    nblk = -(-A // EXPERT_BLOCK) + N_EXPERTS
    P = nblk * EXPERT_BLOCK
    buf_tok = jnp.zeros((P,), jnp.int32).at[dest].set(flat_tok[order])
    buf_gate = jnp.zeros((P,), jnp.float32).at[dest].set(flat_gate[order])
    blk_e = jnp.minimum(jnp.searchsorted(pend, jnp.arange(nblk) * EXPERT_BLOCK, side='right'), N_EXPERTS - 1)
    xbuf = xt[buf_tok].reshape(nblk, EXPERT_BLOCK, d)

    def expert_block(args):
        xb, e = args
        hid = jax.nn.silu(xb @ w1[e]) * (xb @ w3[e])
        return hid @ w2[e]

    ybuf = lax.map(expert_block, (xbuf, blk_e)).reshape(P, d)
    y = jnp.zeros((T, d), xn.dtype).at[buf_tok].add(ybuf * buf_gate[:, None].astype(ybuf.dtype))
    return y.reshape(b, S, d)


def hybrid_layer(x, norm1_g, w_in, mla_gcq, mla_gckv, mla_wuq, mla_wukv, qk_gq, qk_gk, fox_bf, sink, w_o,
                 norm2_g, w_rg, b_rg, w_re, b_re, w1, w3, w2):
    b, S, _ = x.shape
    H, hd = HEADS_PER_MIXER, HEAD_DIM
    slopes_dil, slopes_swa = alibi_slopes()
    u = rmsnorm(x, norm1_g)
    proj = u @ w_in
    offs = [int(o) for o in np.cumsum(IN_SPLITS)[:-1]]
    cq, ckv, kr, pb, pc, fg, pdq, pdkv = jnp.split(proj, offs, axis=-1)

    qa = (rmsnorm(cq, mla_gcq) @ mla_wuq).reshape(b, S, H, MLA_NOPE + MLA_ROPE)
    kva = (rmsnorm(ckv, mla_gckv) @ mla_wukv).reshape(b, S, H, MLA_NOPE + MLA_V)
    ka = jnp.concatenate([kva[..., :MLA_NOPE], jnp.broadcast_to(kr[:, :, None, :], (b, S, H, MLA_ROPE))], axis=-1)
    va = kva[..., MLA_NOPE:]
    qa = rmsnorm(qa, qk_gq[0])
    ka = rmsnorm(ka, qk_gk[0])
    qa = jnp.concatenate([qa[..., :MLA_NOPE], rope(qa[..., MLA_NOPE:])], axis=-1)
    ka = jnp.concatenate([ka[..., :MLA_NOPE], rope(ka[..., MLA_NOPE:])], axis=-1)
    oa = causal_block_attention(qa, ka, va)

    pb = pb.reshape(b, S, 3, H, hd)
    ob = dilated_attention(rmsnorm(pb[:, :, 0], qk_gq[1]), rmsnorm(pb[:, :, 1], qk_gk[1]), pb[:, :, 2], slopes_dil)

    pc = pc.reshape(b, S, 3, H, hd)
    cum = jnp.cumsum(jax.nn.log_sigmoid(fg.astype(jnp.float32) + fox_bf.astype(jnp.float32)), axis=1)
    oc = causal_block_attention(rmsnorm(pc[:, :, 0], qk_gq[2]), rmsnorm(pc[:, :, 1], qk_gk[2]), pc[:, :, 2], cum)

    qd = rmsnorm(pdq.reshape(b, S, SWA_KV_HEADS, SWA_GROUP, hd), qk_gq[3])
    pdkv = pdkv.reshape(b, S, 2, SWA_KV_HEADS, hd)
    od, _ = banded_attention(qd, rmsnorm(pdkv[:, :, 0], qk_gk[3]), pdkv[:, :, 1], SWA_WINDOW - 1, 1,
                             slopes_swa.reshape(SWA_KV_HEADS, SWA_GROUP), sink.reshape(SWA_KV_HEADS, SWA_GROUP))

    mix = jnp.concatenate([oa.reshape(b, S, GROUP_WIDTH), ob.reshape(b, S, GROUP_WIDTH),
                           oc.reshape(b, S, GROUP_WIDTH), od.reshape(b, S, GROUP_WIDTH)], axis=-1)
    h = x + mix @ w_o
    return h + hierarchical_moe(rmsnorm(h, norm2_g), w_rg, b_rg, w_re, b_re, w1, w3, w2)


def setup_inputs(seed: int = 0) -> dict:
    key = jax.random.key(seed)
    ks = jax.random.split(key, 20)

    def nrm(k, shape, scale):
        return jax.random.normal(k, shape, jnp.float32) * scale

    def gain(k, shape):
        return 1.0 + 0.02 * jax.random.normal(k, shape, jnp.float32)

    H = HEADS_PER_MIXER
    return {
        'x': nrm(ks[0], (BATCH, SEQ, D_MODEL), 1.0),
        'norm1_g': gain(ks[1], (DEPTH, D_MODEL)),
        'w_in': nrm(ks[2], (DEPTH, D_MODEL, IN_WIDTH), D_MODEL ** -0.5),
        'mla_gcq': gain(ks[3], (DEPTH, MLA_Q_RANK)),
        'mla_gckv': gain(ks[4], (DEPTH, MLA_KV_RANK)),
        'mla_wuq': nrm(ks[5], (DEPTH, MLA_Q_RANK, H * (MLA_NOPE + MLA_ROPE)), MLA_Q_RANK ** -0.5),
        'mla_wukv': nrm(ks[6], (DEPTH, MLA_KV_RANK, H * (MLA_NOPE + MLA_V)), MLA_KV_RANK ** -0.5),
        'qk_gq': gain(ks[7], (DEPTH, N_MIXERS, HEAD_DIM)),
        'qk_gk': gain(ks[8], (DEPTH, N_MIXERS, HEAD_DIM)),
        'fox_bf': 2.0 + 0.5 * jax.random.normal(ks[9], (DEPTH, H), jnp.float32),
        'sink': nrm(ks[10], (DEPTH, H), 0.5),
        'w_o': nrm(ks[11], (DEPTH, MIX_WIDTH, D_MODEL), 0.5 * MIX_WIDTH ** -0.5),
        'norm2_g': gain(ks[12], (DEPTH, D_MODEL)),
        'w_rg': nrm(ks[13], (DEPTH, D_MODEL, N_GROUPS), D_MODEL ** -0.5),
        'b_rg': nrm(ks[14], (DEPTH, N_GROUPS), 0.01),
        'w_re': nrm(ks[15], (DEPTH, D_MODEL, N_EXPERTS), D_MODEL ** -0.5),
        'b_re': nrm(ks[16], (DEPTH, N_EXPERTS), 0.01),
        'w1': nrm(ks[17], (DEPTH, N_EXPERTS, D_MODEL, D_EXPERT), D_MODEL ** -0.5),
        'w3': nrm(ks[18], (DEPTH, N_EXPERTS, D_MODEL, D_EXPERT), D_MODEL ** -0.5),
        'w2': nrm(ks[19], (DEPTH, N_EXPERTS, D_EXPERT, D_MODEL), D_EXPERT ** -0.5),
    }


def reference(x, norm1_g, w_in, mla_gcq, mla_gckv, mla_wuq, mla_wukv, qk_gq, qk_gk, fox_bf, sink, w_o,
              norm2_g, w_rg, b_rg, w_re, b_re, w1, w3, w2):
    for l in range(DEPTH):
        x = hybrid_layer(x, norm1_g[l], w_in[l], mla_gcq[l], mla_gckv[l], mla_wuq[l], mla_wukv[l], qk_gq[l],
                         qk_gk[l], fox_bf[l], sink[l], w_o[l], norm2_g[l], w_rg[l], b_rg[l], w_re[l], b_re[l],
                         w1[l], w3[l], w2[l])
    return x
```

```python
import functools

import numpy as np
import jax
import jax.numpy as jnp
from jax import lax
from jax.experimental import pallas as pl
from jax.experimental.pallas import tpu as pltpu

F32 = jnp.float32
BF16 = jnp.bfloat16

D_MODEL = 1024
HEAD_DIM = 64
HEADS = 4
GROUP_W = HEADS * HEAD_DIM
NORM_EPS = 1e-6
MLA_Q_RANK, MLA_KV_RANK, MLA_NOPE, MLA_ROPE = 192, 128, 32, 32
ROPE_BASE = 10000.0
DILATED_PAIRS = ((128, 1), (512, 4), (2048, 16))
SWA_WINDOW = 128
N_GROUPS, EXPERTS_PER_GROUP, N_EXPERTS, TOP_K, D_EXPERT = 4, 8, 32, 2, 256
IN_SPLITS = (MLA_Q_RANK, MLA_KV_RANK, MLA_ROPE, 3 * GROUP_W, 3 * GROUP_W, HEADS, GROUP_W, 2 * 2 * HEAD_DIM)

LANES = 128
W_COLS = 2688
QKV_COLS = 2816
QA, KA, VA, QB, KB, VB, QC, KC, VC, QD, KVD = range(11)
FG_LANE = 64
ROW_BLOCK = 256
VMEM_LIMIT = 56 * 1024 * 1024

NEG_INF = float("-inf")


def _cparams(sem):
    return pltpu.CompilerParams(dimension_semantics=sem, vmem_limit_bytes=VMEM_LIMIT)


def _full(shape):
    zeros = (0,) * len(shape)
    return pl.BlockSpec(shape, lambda *_: zeros)


def _head_norm(y, g, e):
    ss = jnp.dot((y * y).astype(BF16), e, preferred_element_type=F32)
    return y * lax.rsqrt(ss * (1.0 / HEAD_DIM) + NORM_EPS) * g


def _rope(y, rc, rsa, rsb):
    outs = []
    for c in range(y.shape[1] // LANES):
        yc = y[:, c * LANES:(c + 1) * LANES]
        outs.append(yc * rc + pltpu.roll(yc, LANES - 16, 1) * rsa + pltpu.roll(yc, 16, 1) * rsb)
    return jnp.concatenate(outs, axis=1)


def _inproj_kernel(x_ref, g1_ref, w_ref, gcq_ref, gckv_ref, wuq_ref, wuk_ref, wuv_ref, gq_ref, gk_ref, e_ref,
                   rc_ref, rsa_ref, rsb_ref, fb_ref, tril_ref, qkv_ref, cum_ref, acc_ref, carry_ref,
                   *, tiles_per_seq):
    i = pl.program_id(0)
    tm = x_ref.shape[0]

    @pl.when(i % tiles_per_seq == 0)
    def _():
        carry_ref[...] = jnp.zeros_like(carry_ref)

    x = x_ref[...]
    ms = jnp.mean(x * x, axis=-1, keepdims=True)
    xn = (x * lax.rsqrt(ms + NORM_EPS) * g1_ref[...]).astype(BF16)
    acc_ref[...] = jnp.dot(xn, w_ref[...], preferred_element_type=F32)

    e = e_ref[...]
    rc, rsa, rsb = rc_ref[...], rsa_ref[...], rsb_ref[...]

    ckv = acc_ref[:, 0:128]
    ckvn = (ckv * lax.rsqrt(jnp.mean(ckv * ckv, axis=-1, keepdims=True) + NORM_EPS) * gckv_ref[...]).astype(BF16)
    u = acc_ref[:, 128:384]
    lane256 = lax.broadcasted_iota(jnp.int32, (tm, 256), 1)
    ssq = jnp.sum(jnp.where(lane256 < MLA_Q_RANK, u * u, 0.0), axis=-1, keepdims=True) * (1.0 / MLA_Q_RANK)
    un = (u * lax.rsqrt(ssq + NORM_EPS) * gcq_ref[...]).astype(BF16)
    qa = jnp.dot(un, wuq_ref[...], preferred_element_type=F32)
    ka = jnp.dot(ckvn, wuk_ref[...], preferred_element_type=F32) + acc_ref[:, 384:640]
    va = jnp.dot(ckvn, wuv_ref[...], preferred_element_type=F32)
    qa = _rope(_head_norm(qa, gq_ref[0:1, :], e), rc, rsa, rsb)
    ka = _rope(_head_norm(ka, gk_ref[0:1, :], e), rc, rsa, rsb)
    qkv_ref[:, QA * 256:(QA + 1) * 256] = qa.astype(BF16)
    qkv_ref[:, KA * 256:(KA + 1) * 256] = ka.astype(BF16)
    qkv_ref[:, VA * 256:(VA + 1) * 256] = va.astype(BF16)

    for m, (jq, jk, jv, base) in enumerate(((QB, KB, VB, 640), (QC, KC, VC, 1408))):
        q = _head_norm(acc_ref[:, base:base + 256], gq_ref[m + 1:m + 2, :], e)
        k = _head_norm(acc_ref[:, base + 256:base + 512], gk_ref[m + 1:m + 2, :], e)
        qkv_ref[:, jq * 256:(jq + 1) * 256] = q.astype(BF16)
        qkv_ref[:, jk * 256:(jk + 1) * 256] = k.astype(BF16)
        qkv_ref[:, jv * 256:(jv + 1) * 256] = acc_ref[:, base + 512:base + 768].astype(BF16)

    qd = _head_norm(acc_ref[:, 2176:2432], gq_ref[3:4, :], e)
    kd = _head_norm(acc_ref[:, 2432:2560], gk_ref[3:4, 0:128], e[0:128, 0:128])
    qkv_ref[:, QD * 256:(QD + 1) * 256] = qd.astype(BF16)
    qkv_ref[:, KVD * 256:KVD * 256 + 128] = kd.astype(BF16)
    qkv_ref[:, KVD * 256 + 128:(KVD + 1) * 256] = acc_ref[:, 2560:2688].astype(BF16)

    z = u[:, 128:256] + fb_ref[...]
    ls = jnp.minimum(z, 0.0) - jnp.log(1.0 + jnp.exp(-jnp.abs(z)))
    lane128 = lax.broadcasted_iota(jnp.int32, (tm, LANES), 1)
    ls = jnp.where((lane128 >= FG_LANE) & (lane128 < FG_LANE + HEADS), ls, 0.0)
    hi = ls.astype(BF16)
    r1 = ls - hi.astype(F32)
    mid = r1.astype(BF16)
    lo = (r1 - mid.astype(F32)).astype(BF16)
    tril = tril_ref[...]
    local = (jnp.dot(tril, hi, preferred_element_type=F32) + jnp.dot(tril, mid, preferred_element_type=F32)
             + jnp.dot(tril, lo, preferred_element_type=F32))
    cum = local + carry_ref[0:1, :]
    cum_ref[...] = cum
    carry_ref[...] = jnp.broadcast_to(cum[tm - 1:tm, :], carry_ref.shape)


def _inproj(xt, seq_len, lw, tabs, tm=512):
    T = xt.shape[0]
    tps = seq_len // tm
    kern = functools.partial(_inproj_kernel, tiles_per_seq=tps)
    tab_spec = pl.BlockSpec((tm, LANES), lambda i: (i % tps, 0))
    return pl.pallas_call(
        kern,
        grid=(T // tm,),
        in_specs=[pl.BlockSpec((tm, D_MODEL), lambda i: (i, 0)),
                  _full((1, D_MODEL)), _full((D_MODEL, W_COLS)), _full((1, 256)), _full((1, 128)),
                  _full((256, 256)), _full((128, 256)), _full((128, 256)), _full((4, 256)), _full((4, 256)),
                  _full((256, 256)), tab_spec, tab_spec, tab_spec, _full((1, LANES)), _full((tm, tm))],
        out_specs=[pl.BlockSpec((tm, QKV_COLS), lambda i: (i, 0)), pl.BlockSpec((tm, LANES), lambda i: (i, 0))],
        out_shape=[jax.ShapeDtypeStruct((T, QKV_COLS), BF16), jax.ShapeDtypeStruct((T, LANES), F32)],
        scratch_shapes=[pltpu.VMEM((tm, W_COLS), F32), pltpu.VMEM((8, LANES), F32)],
        compiler_params=_cparams(("arbitrary",)),
        name="inproj",
    )(xt, lw["g1"], lw["wp"], lw["gcq"], lw["gckv"], lw["wuq"], lw["wuk"], lw["wuv"], lw["gq"], lw["gk"],
      tabs["e"], tabs["rc"], tabs["rsa"], tabs["rsb"], lw["fb"], tabs["tril"])


def _dense_attn_kernel(*refs, tq, fox):
    if fox:
        q_ref, k_ref, v_ref, cum_ref, o_ref, m_scr, l_scr, acc_scr = refs
    else:
        q_ref, k_ref, v_ref, o_ref, m_scr, l_scr, acc_scr = refs
        cum_ref = None
    tk = tq
    qi = pl.program_id(2)
    q = q_ref[0]
    lane = lax.broadcasted_iota(jnp.int32, (tq, LANES), 1)
    half0 = lane < HEAD_DIM
    zero = jnp.zeros_like(q)
    qh = (jnp.where(half0, q, zero), jnp.where(half0, zero, q))
    m_scr[...] = jnp.full(m_scr.shape, NEG_INF, F32)
    l_scr[...] = jnp.zeros_like(l_scr)
    acc_scr[...] = jnp.zeros_like(acc_scr)
    if fox:
        c0 = [cum_ref[0, 0, h:h + 1, pl.ds(pl.multiple_of(qi * tq, tq), LANES)][:, 0:1] for h in range(2)]

    def step(j, masked):
        start = pl.multiple_of(j * tk, tk)
        k = k_ref[0, pl.ds(start, tk), :]
        v = v_ref[0, pl.ds(start, tk), :]
        for h in range(2):
            s = lax.dot_general(qh[h], k, (((1,), (1,)), ((), ())), preferred_element_type=F32)
            if fox:
                s = s + (c0[h] - cum_ref[0, 0, h:h + 1, pl.ds(start, tk)])
            if masked:
                row = lax.broadcasted_iota(jnp.int32, (tq, tk), 0)
                col = lax.broadcasted_iota(jnp.int32, (tq, tk), 1)
                s = jnp.where(col <= row, s, NEG_INF)
            m_prev = m_scr[h]
            m_next = jnp.maximum(m_prev, jnp.max(s, axis=1, keepdims=True))
            p = jnp.exp(s - jnp.concatenate([m_next] * (tk // LANES), axis=1))
            alpha = jnp.exp(m_prev - m_next)
            l_scr[h] = alpha * l_scr[h] + jnp.sum(p, axis=1, keepdims=True)
            m_scr[h] = m_next
            pv = jnp.dot(p.astype(BF16), v, preferred_element_type=F32)
            acc_scr[h] = acc_scr[h] * alpha + pv

    def body(j, carry):
        step(j, False)
        return carry

    lax.fori_loop(0, qi, body, 0)
    step(qi, True)
    o = jnp.where(half0, acc_scr[0] / l_scr[0], acc_scr[1] / l_scr[1])
    o_ref[0] = o.astype(o_ref.dtype)


def _dense_attn(qkv3, cum8, jq, jk, jv, tq=512):
    B, S, _ = qkv3.shape
    fox = cum8 is not None
    in_specs = [pl.BlockSpec((1, tq, LANES), lambda b, p, i: (b, i, 2 * jq + p)),
                pl.BlockSpec((1, S, LANES), lambda b, p, i: (b, 0, 2 * jk + p)),
                pl.BlockSpec((1, S, LANES), lambda b, p, i: (b, 0, 2 * jv + p))]
    args = [qkv3, qkv3, qkv3]
    if fox:
        in_specs.append(pl.BlockSpec((1, 1, 8, S), lambda b, p, i: (b, p, 0, 0)))
        args.append(cum8)
    return pl.pallas_call(
        functools.partial(_dense_attn_kernel, tq=tq, fox=fox),
        grid=(B, 2, S // tq),
        in_specs=in_specs,
        out_specs=pl.BlockSpec((1, tq, LANES), lambda b, p, i: (b, i, p)),
        out_shape=jax.ShapeDtypeStruct((B, S, GROUP_W), BF16),
        scratch_shapes=[pltpu.VMEM((2, tq, LANES), F32)] * 3,
        compiler_params=_cparams(("parallel", "parallel", "arbitrary")),
        name="fox_attn" if fox else "mla_attn",
    )(*args)


def _banded_kernel(*refs, tq, maxdist, unit, slopes, kv_chunks, has_sink, want_lse):
    refs = list(refs)
    sink_ref = refs.pop(0) if has_sink else None
    q_ref, k_ref, v_ref, o_ref = refs[:4]
    lse_ref = refs[4] if want_lse else None
    qi = pl.program_id(2)
    q0 = qi * tq
    start = pl.multiple_of(jnp.maximum(q0 - LANES, 0), LANES)
    w = tq + LANES
    row = q0 + lax.broadcasted_iota(jnp.int32, (tq, w), 0)
    col = start + lax.broadcasted_iota(jnp.int32, (tq, w), 1)
    dist = row - col
    valid = (dist >= 0) & (dist <= maxdist)
    distf = dist.astype(F32) * float(unit)
    lane = lax.broadcasted_iota(jnp.int32, (tq, LANES), 1)
    half0 = lane < HEAD_DIM
    for c in range(2):
        qc = q_ref[0, :, c * LANES:(c + 1) * LANES]
        kc_idx = c if kv_chunks == 2 else 0
        kc = k_ref[0, pl.ds(start, w), kc_idx * LANES:(kc_idx + 1) * LANES]
        vc = v_ref[0, pl.ds(start, w), kc_idx * LANES:(kc_idx + 1) * LANES]
        zero = jnp.zeros_like(qc)
        o_h, lse_h = [], []
        for e in range(2):
            qh = jnp.where(half0, qc, zero) if e == 0 else jnp.where(half0, zero, qc)
            s = lax.dot_general(qh, kc, (((1,), (1,)), ((), ())), preferred_element_type=F32)
            s = jnp.where(valid, s - slopes[2 * c + e] * distf, NEG_INF)
            m = jnp.max(s, axis=1, keepdims=True)
            p = jnp.exp(s - m)
            lse = m + jnp.log(jnp.sum(p, axis=1, keepdims=True))
            if has_sink:
                sk = sink_ref[2 * c + e]
                mx = jnp.maximum(lse, sk)
                lse = mx + jnp.log(jnp.exp(lse - mx) + jnp.exp(sk - mx))
            pv = jnp.dot(p.astype(BF16), vc, preferred_element_type=F32)
            o_h.append(pv * jnp.exp(m - lse))
            lse_h.append(lse)
        o_ref[0, :, c * LANES:(c + 1) * LANES] = jnp.where(half0, o_h[0], o_h[1]).astype(o_ref.dtype)
        if want_lse:
            lse_ref[0, :, c * LANES:(c + 1) * LANES] = jnp.where(half0, lse_h[0], lse_h[1])


def _banded(qkv3, jq, jk, jv, kw, dil, maxdist, slopes, sink, want_lse, tq=256):
    B, S, nq = qkv3.shape
    L = S // dil
    tq = min(tq, L - LANES)
    assert tq >= LANES and L % tq == 0
    view = qkv3.reshape(B, L, dil * nq)
    has_sink = sink is not None
    kern = functools.partial(_banded_kernel, tq=tq, maxdist=maxdist, unit=dil, slopes=tuple(slopes),
                             kv_chunks=kw // LANES, has_sink=has_sink, want_lse=want_lse)
    in_specs = [pl.BlockSpec((1, tq, 256), lambda b, r, i: (b, i, r * (nq // 256) + jq)),
                pl.BlockSpec((1, L, kw), lambda b, r, i: (b, 0, r * (nq // kw) + jk)),
                pl.BlockSpec((1, L, kw), lambda b, r, i: (b, 0, r * (nq // kw) + jv))]
    args = [view, view, view]
    if has_sink:
        in_specs.insert(0, pl.BlockSpec(memory_space=pltpu.SMEM))
        args.insert(0, sink)
    out_spec = pl.BlockSpec((1, tq, 256), lambda b, r, i: (b, i, r))
    out_specs = [out_spec]
    out_shape = [jax.ShapeDtypeStruct((B, L, dil * 256), BF16)]
    if want_lse:
        out_specs.append(out_spec)
        out_shape.append(jax.ShapeDtypeStruct((B, L, dil * 256), F32))
    outs = pl.pallas_call(
        kern,
        grid=(B, dil, L // tq),
        in_specs=in_specs,
        out_specs=out_specs,
        out_shape=out_shape,
        compiler_params=_cparams(("parallel", "parallel", "parallel")),
        name=f"banded_d{dil}" if want_lse else "swa_attn",
    )(*args)
    return [o.reshape(B * S, 256) for o in outs]


def _wo_router_kernel(x_ref, oa_ref, ob0_ref, ob1_ref, ob2_ref, l0_ref, l1_ref, l2_ref, oc_ref, od_ref,
                      wo_ref, g2_ref, wr_ref, br_ref, ltri_ref, h_ref, xn_ref, route_ref, cnt_ref, carry_ref):
    i = pl.program_id(0)
    tm = x_ref.shape[0]

    @pl.when(i == 0)
    def _():
        carry_ref[...] = jnp.zeros_like(carry_ref)

    la, lb, lc = l0_ref[...], l1_ref[...], l2_ref[...]
    mx = jnp.maximum(jnp.maximum(la, lb), lc)
    ea, eb, ec = jnp.exp(la - mx), jnp.exp(lb - mx), jnp.exp(lc - mx)
    ob = (ea * ob0_ref[...].astype(F32) + eb * ob1_ref[...].astype(F32) + ec * ob2_ref[...].astype(F32)) / (ea + eb + ec)
    mix = jnp.concatenate([oa_ref[...], ob.astype(BF16), oc_ref[...], od_ref[...]], axis=1)
    h = x_ref[...] + jnp.dot(mix, wo_ref[...], preferred_element_type=F32)
    h_ref[...] = h
    xn = h * lax.rsqrt(jnp.mean(h * h, axis=-1, keepdims=True) + NORM_EPS) * g2_ref[...]
    xn_ref[...] = xn

    z = jnp.dot(xn, wr_ref[...], preferred_element_type=F32, precision=lax.Precision.HIGHEST) + br_ref[...]
    lane = lax.broadcasted_iota(jnp.int32, (tm, LANES), 1)
    lanef = lane.astype(F32)
    big = float(LANES)
    zg = jnp.where((lane >= N_EXPERTS) & (lane < N_EXPERTS + N_GROUPS), z, NEG_INF)
    mg = jnp.max(zg, axis=1, keepdims=True)
    p_g = 1.0 / jnp.sum(jnp.exp(zg - mg), axis=1, keepdims=True)
    gsel = jnp.min(jnp.where(zg == mg, lanef, big), axis=1, keepdims=True) - float(N_EXPERTS)
    lo = gsel * float(EXPERTS_PER_GROUP)
    ze = jnp.where((lanef >= lo) & (lanef < lo + float(EXPERTS_PER_GROUP)), z, NEG_INF)
    m1 = jnp.max(ze, axis=1, keepdims=True)
    i1 = jnp.min(jnp.where(ze == m1, lanef, big), axis=1, keepdims=True)
    ze2 = jnp.where(lanef == i1, NEG_INF, ze)
    m2 = jnp.max(ze2, axis=1, keepdims=True)
    i2 = jnp.min(jnp.where(ze2 == m2, lanef, big), axis=1, keepdims=True)
    e2 = jnp.exp(m2 - m1)
    gate1 = p_g / (1.0 + e2)
    gate2 = p_g * e2 / (1.0 + e2)

    oh1 = jnp.where(lanef == i1, 1.0, 0.0)
    oh2 = jnp.where(lanef == i2, 1.0, 0.0)
    oh = oh1 + oh2
    before = carry_ref[0:1, :] + jnp.dot(ltri_ref[...], oh.astype(BF16), preferred_element_type=F32)
    r1 = jnp.sum(before * oh1, axis=1, keepdims=True)
    r2 = jnp.sum(before * oh2, axis=1, keepdims=True)
    total = carry_ref[0:1, :] + jnp.sum(oh, axis=0, keepdims=True)
    carry_ref[...] = jnp.broadcast_to(total, carry_ref.shape)
    cnt_ref[...] = jnp.broadcast_to(total, cnt_ref.shape)

    route = jnp.where(lane == 0, i1, jnp.where(lane == 1, i2, jnp.where(lane == 2, r1, jnp.where(
        lane == 3, r2, jnp.where(lane == 4, gate1, jnp.where(lane == 5, gate2, 0.0))))))
    route_ref[...] = route


def _wo_router(xt, oa, obs, lses, oc, od, lw, ltri, tm=512):
    T = xt.shape[0]
    row = lambda w: pl.BlockSpec((tm, w), lambda i: (i, 0))
    return pl.pallas_call(
        _wo_router_kernel,
        grid=(T // tm,),
        in_specs=[row(D_MODEL), row(256), row(256), row(256), row(256), row(256), row(256), row(256), row(256),
                  row(256), _full((D_MODEL, D_MODEL)), _full((1, D_MODEL)), _full((D_MODEL, LANES)),
                  _full((1, LANES)), _full((tm, tm))],
        out_specs=[row(D_MODEL), row(D_MODEL), row(LANES), _full((8, LANES))],
        out_shape=[jax.ShapeDtypeStruct((T, D_MODEL), F32), jax.ShapeDtypeStruct((T, D_MODEL), F32),
                   jax.ShapeDtypeStruct((T, LANES), F32), jax.ShapeDtypeStruct((8, LANES), F32)],
        scratch_shapes=[pltpu.VMEM((8, LANES), F32)],
        compiler_params=_cparams(("arbitrary",)),
        name="wo_router",
    )(xt, oa, obs[0], obs[1], obs[2], lses[0], lses[1], lses[2], oc, od, lw["wo"], lw["g2"], lw["wr"], lw["br"], ltri)


def _row_copy(src, src_row, dst, dst_row, sem):
    return pltpu.make_async_copy(src.at[pl.ds(src_row, 1)], dst.at[pl.ds(dst_row, 1)], sem)


def _scatter_kernel(dest_ref, x_hbm, buf_in, buf_out, sem, *, ts):
    del buf_in
    base = pl.program_id(0) * ts

    def issue(t, carry):
        for k in range(TOP_K):
            _row_copy(x_hbm, base + t, buf_out, dest_ref[TOP_K * t + k], sem).start()
        return carry

    lax.fori_loop(0, ts, issue, 0)

    def drain(t, carry):
        for k in range(TOP_K):
            _row_copy(x_hbm, 0, buf_out, 0, sem).wait()
        return carry

    lax.fori_loop(0, ts, drain, 0)


def _scatter_rows(xn, dest_flat, n_rows, ts=512):
    T, D = xn.shape
    return pl.pallas_call(
        functools.partial(_scatter_kernel, ts=ts),
        grid=(T // ts,),
        in_specs=[pl.BlockSpec((TOP_K * ts,), lambda i: (i,), memory_space=pltpu.SMEM),
                  pl.BlockSpec(memory_space=pl.ANY), pl.BlockSpec(memory_space=pl.ANY)],
        out_specs=pl.BlockSpec(memory_space=pl.ANY),
        out_shape=jax.ShapeDtypeStruct((n_rows, D), F32),
        scratch_shapes=[pltpu.SemaphoreType.DMA(())],
        input_output_aliases={2: 0},
        compiler_params=_cparams(("arbitrary",)),
        name="moe_scatter",
    )(dest_flat, xn, jnp.zeros((n_rows, D), F32))


def _expert_kernel(blk_e_ref, nused_ref, x_ref, w1_ref, w3_ref, w2_ref, y_ref):
    del blk_e_ref
    i = pl.program_id(0)

    @pl.when(i < nused_ref[0])
    def _():
        x = x_ref[...].astype(BF16)
        a = jnp.dot(x, w1_ref[0], preferred_element_type=F32)
        b = jnp.dot(x, w3_ref[0], preferred_element_type=F32)
        hid = (a / (1.0 + jnp.exp(-a))) * b
        y_ref[...] = jnp.dot(hid.astype(BF16), w2_ref[0], preferred_element_type=F32)

    @pl.when(i >= nused_ref[0])
    def _():
        y_ref[...] = jnp.zeros_like(y_ref)


def _expert_mlp(xbuf, blk_e, nused, w1, w3, w2):
    P, D = xbuf.shape
    nblk = P // ROW_BLOCK
    x_map = lambda i, be, nu: (jnp.minimum(i, nu[0] - 1), 0)
    grid_spec = pltpu.PrefetchScalarGridSpec(
        num_scalar_prefetch=2,
        grid=(nblk,),
        in_specs=[pl.BlockSpec((ROW_BLOCK, D), x_map),
                  pl.BlockSpec((1, D, D_EXPERT), lambda i, be, nu: (be[i], 0, 0)),
                  pl.BlockSpec((1, D, D_EXPERT), lambda i, be, nu: (be[i], 0, 0)),
                  pl.BlockSpec((1, D_EXPERT, D), lambda i, be, nu: (be[i], 0, 0))],
        out_specs=pl.BlockSpec((ROW_BLOCK, D), lambda i, be, nu: (i, 0)),
    )
    return pl.pallas_call(
        _expert_kernel,
        grid_spec=grid_spec,
        out_shape=jax.ShapeDtypeStruct((P, D), F32),
        compiler_params=_cparams(("arbitrary",)),
        name="moe_experts",
    )(blk_e, nused, xbuf, w1, w3, w2)


def _combine_kernel(dest_ref, h_ref, route_ref, y_hbm, o_ref, rows_scr, sem, *, tc):
    def issue(t, carry):
        for k in range(TOP_K):
            _row_copy(y_hbm, dest_ref[TOP_K * t + k], rows_scr.at[k], t, sem).start()
        return carry

    lax.fori_loop(0, tc, issue, 0)

    def drain(t, carry):
        for k in range(TOP_K):
            _row_copy(y_hbm, 0, rows_scr.at[k], 0, sem).wait()
        return carry

    lax.fori_loop(0, tc, drain, 0)
    route = route_ref[...]
    o_ref[...] = h_ref[...] + route[:, 4:5] * rows_scr[0] + route[:, 5:6] * rows_scr[1]


def _combine(h, route, ybuf, dest_flat, tc=256):
    T, D = h.shape
    return pl.pallas_call(
        functools.partial(_combine_kernel, tc=tc),
        grid=(T // tc,),
        in_specs=[pl.BlockSpec((TOP_K * tc,), lambda i: (i,), memory_space=pltpu.SMEM),
                  pl.BlockSpec((tc, D), lambda i: (i, 0)), pl.BlockSpec((tc, LANES), lambda i: (i, 0)),
                  pl.BlockSpec(memory_space=pl.ANY)],
        out_specs=pl.BlockSpec((tc, D), lambda i: (i, 0)),
        out_shape=jax.ShapeDtypeStruct((T, D), F32),
        scratch_shapes=[pltpu.VMEM((TOP_K, tc, D), F32), pltpu.SemaphoreType.DMA(())],
        compiler_params=_cparams(("arbitrary",)),
        name="moe_combine",
    )(dest_flat, h, route, ybuf)


def _layer_weights(l, norm1_g, w_in, mla_gcq, mla_gckv, mla_wuq, mla_wukv, qk_gq, qk_gk, fox_bf, sink, w_o,
                   norm2_g, w_rg, b_rg, w_re, b_re, w1, w3, w2):
    offs = np.concatenate([[0], np.cumsum(IN_SPLITS)])
    w = w_in[l]
    cq, ckv, kr, pb, pc, fg, pdq, pdkv = [w[:, offs[j]:offs[j + 1]] for j in range(8)]
    z = lambda n: jnp.zeros((D_MODEL, n), F32)
    perm = np.array([0, 2, 1, 3])
    ublock = jnp.concatenate([cq, fg, z(256 - MLA_Q_RANK - HEADS)], axis=1)
    kr_rep = jnp.concatenate([z(MLA_NOPE), kr] * HEADS, axis=1)
    dq = pdq.reshape(D_MODEL, HEADS, HEAD_DIM)[:, perm].reshape(D_MODEL, GROUP_W)
    wp = jnp.concatenate([ckv, ublock, kr_rep, pb, pc, dq, pdkv], axis=1).astype(BF16)

    wuq = jnp.concatenate([mla_wuq[l], jnp.zeros((256 - MLA_Q_RANK, GROUP_W), F32)], axis=0).astype(BF16)
    wukv = mla_wukv[l].reshape(MLA_KV_RANK, HEADS, MLA_NOPE + HEAD_DIM)
    wuk = jnp.concatenate([wukv[:, :, :MLA_NOPE], jnp.zeros((MLA_KV_RANK, HEADS, MLA_ROPE), F32)], axis=2)
    wuk = wuk.reshape(MLA_KV_RANK, GROUP_W).astype(BF16)
    wuv = wukv[:, :, MLA_NOPE:].reshape(MLA_KV_RANK, GROUP_W).astype(BF16)
    gcq = jnp.concatenate([mla_gcq[l], jnp.zeros((256 - MLA_Q_RANK,), F32)])[None, :]
    scale = HEAD_DIM ** -0.5
    gq = jnp.tile(qk_gq[l], (1, HEADS)) * scale
    gk = jnp.tile(qk_gk[l], (1, HEADS))
    fb = jnp.zeros((1, LANES), F32).at[0, FG_LANE:FG_LANE + HEADS].set(fox_bf[l])
    wo = w_o[l]
    wo_d = wo[3 * GROUP_W:].reshape(HEADS, HEAD_DIM, D_MODEL)[perm].reshape(GROUP_W, D_MODEL)
    wo = jnp.concatenate([wo[:3 * GROUP_W], wo_d], axis=0).astype(BF16)
    wr = jnp.concatenate([w_re[l], w_rg[l], jnp.zeros((D_MODEL, LANES - N_EXPERTS - N_GROUPS), F32)], axis=1)
    br = jnp.concatenate([b_re[l], b_rg[l], jnp.zeros((LANES - N_EXPERTS - N_GROUPS,), F32)])[None, :]
    return dict(g1=norm1_g[l][None, :], wp=wp, gcq=gcq, gckv=mla_gckv[l][None, :], wuq=wuq, wuk=wuk, wuv=wuv,
                gq=gq, gk=gk, fb=fb, sink=sink[l][perm], wo=wo, g2=norm2_g[l][None, :], wr=wr, br=br,
                w1=w1[l].astype(BF16), w3=w3[l].astype(BF16), w2=w2[l].astype(BF16))


def _tables(seq_len, tm):
    half = MLA_ROPE // 2
    inv = ROPE_BASE ** (-jnp.arange(0, MLA_ROPE, 2, dtype=F32) / MLA_ROPE)
    ang = jnp.arange(seq_len, dtype=F32)[:, None] * inv[None, :]
    cos, sin = jnp.cos(ang), jnp.sin(ang)
    one = jnp.ones((seq_len, MLA_NOPE), F32)
    zn = jnp.zeros((seq_len, MLA_NOPE), F32)
    zh = jnp.zeros((seq_len, half), F32)
    rc = jnp.concatenate([one, cos, cos] * 2, axis=1)
    rsa = jnp.concatenate([zn, -sin, zh] * 2, axis=1)
    rsb = jnp.concatenate([zn, zh, sin] * 2, axis=1)
    seg = np.arange(256) // HEAD_DIM
    e = jnp.asarray(seg[:, None] == seg[None, :], BF16)
    idx = np.arange(tm)
    tril = jnp.asarray(idx[None, :] <= idx[:, None], BF16)
    ltri = jnp.asarray(idx[None, :] < idx[:, None], BF16)
    return dict(rc=rc, rsa=rsa, rsb=rsb, e=e, tril=tril, ltri=ltri)


def _alibi():
    n = 2 * HEADS
    s = [2.0 ** (-8.0 * i / n) for i in range(1, n + 1)]
    return s[HEADS:], s[:HEADS]


def kernel(x, norm1_g, w_in, mla_gcq, mla_gckv, mla_wuq, mla_wukv, qk_gq, qk_gk, fox_bf, sink, w_o,
           norm2_g, w_rg, b_rg, w_re, b_re, w1, w3, w2):
    B, S, D = x.shape
    T = B * S
    depth = w_in.shape[0]
    tm = 512
    tabs = _tables(S, tm)
    slopes_dil, slopes_swa = _alibi()
    slopes_swa_perm = [slopes_swa[0], slopes_swa[2], slopes_swa[1], slopes_swa[3]]
    n_assign = T * TOP_K
    nblk = n_assign // ROW_BLOCK + N_EXPERTS
    n_rows = nblk * ROW_BLOCK

    xt = x.reshape(T, D)
    for l in range(depth):
        lw = _layer_weights(l, norm1_g, w_in, mla_gcq, mla_gckv, mla_wuq, mla_wukv, qk_gq, qk_gk, fox_bf, sink,
                            w_o, norm2_g, w_rg, b_rg, w_re, b_re, w1, w3, w2)
        qkv, cum = _inproj(xt, S, lw, tabs, tm=tm)
        qkv3 = qkv.reshape(B, S, QKV_COLS)
        cum8 = jnp.pad(cum[:, FG_LANE:FG_LANE + HEADS].reshape(B, S, 2, 2).transpose(0, 2, 3, 1),
                       ((0, 0), (0, 0), (0, 6), (0, 0)))
        oa = _dense_attn(qkv3, None, QA, KA, VA).reshape(T, GROUP_W)
        oc = _dense_attn(qkv3, cum8, QC, KC, VC).reshape(T, GROUP_W)
        obs, lses = [], []
        for window, dil in DILATED_PAIRS:
            o, lse = _banded(qkv3, QB, KB, VB, 256, dil, window // dil, slopes_dil, None, True)
            obs.append(o)
            lses.append(lse)
        (od,) = _banded(qkv3, QD, 2 * KVD, 2 * KVD + 1, 128, 1, SWA_WINDOW - 1, slopes_swa_perm, lw["sink"], False)
        h, xn, route, cnt = _wo_router(xt, oa, obs, lses, oc, od, lw, tabs["ltri"], tm=tm)

        counts = cnt[0, :N_EXPERTS].astype(jnp.int32)
        padded = (counts + ROW_BLOCK - 1) // ROW_BLOCK * ROW_BLOCK
        pend = jnp.cumsum(padded)
        pstart = pend - padded
        expert = route[:, 0:TOP_K].astype(jnp.int32)
        rank = route[:, TOP_K:2 * TOP_K].astype(jnp.int32)
        dest = (pstart[expert] + rank).reshape(-1)
        blk_e = jnp.minimum(jnp.searchsorted(pend, jnp.arange(nblk, dtype=jnp.int32) * ROW_BLOCK, side="right"),
                            N_EXPERTS - 1).astype(jnp.int32)
        nused = (pend[-1:] // ROW_BLOCK).astype(jnp.int32)

        xbuf = _scatter_rows(xn, dest, n_rows)
        ybuf = _expert_mlp(xbuf, blk_e, nused, lw["w1"], lw["w3"], lw["w2"])
        xt = _combine(h, route, ybuf, dest)
    return xt.reshape(B, S, D)
```

```python
import functools

import numpy as np
import jax
import jax.numpy as jnp
from jax import lax
from jax.experimental import pallas as pl
from jax.experimental.pallas import tpu as pltpu

F32 = jnp.float32
BF16 = jnp.bfloat16

D_MODEL = 1024
HEAD_DIM = 64
HEADS = 4
GROUP_W = HEADS * HEAD_DIM
NORM_EPS = 1e-6
MLA_Q_RANK, MLA_KV_RANK, MLA_NOPE, MLA_ROPE = 192, 128, 32, 32
ROPE_BASE = 10000.0
DILATED_PAIRS = ((128, 1), (512, 4), (2048, 16))
SWA_WINDOW = 128
N_GROUPS, EXPERTS_PER_GROUP, N_EXPERTS, TOP_K, D_EXPERT = 4, 8, 32, 2, 256
IN_SPLITS = (MLA_Q_RANK, MLA_KV_RANK, MLA_ROPE, 3 * GROUP_W, 3 * GROUP_W, HEADS, GROUP_W, 2 * 2 * HEAD_DIM)

LANES = 128
W_COLS = 2688
QKV_COLS = 2816
QA, KA, VA, QB, KB, VB, QC, KC, VC, QD, KVD = range(11)
FG_LANE = 64
ROW_BLOCK = 256
VMEM_LIMIT = 56 * 1024 * 1024

NEG_INF = float("-inf")


def _cparams(sem):
    return pltpu.CompilerParams(dimension_semantics=sem, vmem_limit_bytes=VMEM_LIMIT)


def _full(shape):
    zeros = (0,) * len(shape)
    return pl.BlockSpec(shape, lambda *_: zeros)


def _head_norm(y, g, e):
    ss = jnp.dot((y * y).astype(BF16), e, preferred_element_type=F32)
    return y * lax.rsqrt(ss * (1.0 / HEAD_DIM) + NORM_EPS) * g


def _rope(y, rc, rsa, rsb):
    outs = []
    for c in range(y.shape[1] // LANES):
        yc = y[:, c * LANES:(c + 1) * LANES]
        outs.append(yc * rc + pltpu.roll(yc, LANES - 16, 1) * rsa + pltpu.roll(yc, 16, 1) * rsb)
    return jnp.concatenate(outs, axis=1)


def _inproj_kernel(x_ref, g1_ref, w_ref, gcq_ref, gckv_ref, wuq_ref, wuk_ref, wuv_ref, gq_ref, gk_ref, e_ref,
                   rc_ref, rsa_ref, rsb_ref, fb_ref, tril_ref, qkv_ref, cum_ref, acc_ref, carry_ref,
                   *, tiles_per_seq):
    i = pl.program_id(0)
    tm = x_ref.shape[0]

    @pl.when(i % tiles_per_seq == 0)
    def _():
        carry_ref[...] = jnp.zeros_like(carry_ref)

    x = x_ref[...]
    ms = jnp.mean(x * x, axis=-1, keepdims=True)
    xn = (x * lax.rsqrt(ms + NORM_EPS) * g1_ref[...]).astype(BF16)
    acc_ref[...] = jnp.dot(xn, w_ref[...], preferred_element_type=F32)

    e = e_ref[...]
    rc, rsa, rsb = rc_ref[...], rsa_ref[...], rsb_ref[...]

    ckv = acc_ref[:, 0:128]
    ckvn = (ckv * lax.rsqrt(jnp.mean(ckv * ckv, axis=-1, keepdims=True) + NORM_EPS) * gckv_ref[...]).astype(BF16)
    u = acc_ref[:, 128:384]
    lane256 = lax.broadcasted_iota(jnp.int32, (tm, 256), 1)
    ssq = jnp.sum(jnp.where(lane256 < MLA_Q_RANK, u * u, 0.0), axis=-1, keepdims=True) * (1.0 / MLA_Q_RANK)
    un = (u * lax.rsqrt(ssq + NORM_EPS) * gcq_ref[...]).astype(BF16)
    qa = jnp.dot(un, wuq_ref[...], preferred_element_type=F32)
    ka = jnp.dot(ckvn, wuk_ref[...], preferred_element_type=F32) + acc_ref[:, 384:640]
    va = jnp.dot(ckvn, wuv_ref[...], preferred_element_type=F32)
    qa = _rope(_head_norm(qa, gq_ref[0:1, :], e), rc, rsa, rsb)
    ka = _rope(_head_norm(ka, gk_ref[0:1, :], e), rc, rsa, rsb)
    qkv_ref[:, QA * 256:(QA + 1) * 256] = qa.astype(BF16)
    qkv_ref[:, KA * 256:(KA + 1) * 256] = ka.astype(BF16)
    qkv_ref[:, VA * 256:(VA + 1) * 256] = va.astype(BF16)

    for m, (jq, jk, jv, base) in enumerate(((QB, KB, VB, 640), (QC, KC, VC, 1408))):
        q = _head_norm(acc_ref[:, base:base + 256], gq_ref[m + 1:m + 2, :], e)
        k = _head_norm(acc_ref[:, base + 256:base + 512], gk_ref[m + 1:m + 2, :], e)
        qkv_ref[:, jq * 256:(jq + 1) * 256] = q.astype(BF16)
        qkv_ref[:, jk * 256:(jk + 1) * 256] = k.astype(BF16)
        qkv_ref[:, jv * 256:(jv + 1) * 256] = acc_ref[:, base + 512:base + 768].astype(BF16)

    qd = _head_norm(acc_ref[:, 2176:2432], gq_ref[3:4, :], e)
    kd = _head_norm(acc_ref[:, 2432:2560], gk_ref[3:4, 0:128], e[0:128, 0:128])
    qkv_ref[:, QD * 256:(QD + 1) * 256] = qd.astype(BF16)
    qkv_ref[:, KVD * 256:KVD * 256 + 128] = kd.astype(BF16)
    qkv_ref[:, KVD * 256 + 128:(KVD + 1) * 256] = acc_ref[:, 2560:2688].astype(BF16)

    z = u[:, 128:256] + fb_ref[...]
    ls = jnp.minimum(z, 0.0) - jnp.log(1.0 + jnp.exp(-jnp.abs(z)))
    lane128 = lax.broadcasted_iota(jnp.int32, (tm, LANES), 1)
    ls = jnp.where((lane128 >= FG_LANE) & (lane128 < FG_LANE + HEADS), ls, 0.0)
    hi = ls.astype(BF16)
    r1 = ls - hi.astype(F32)
    mid = r1.astype(BF16)
    lo = (r1 - mid.astype(F32)).astype(BF16)
    tril = tril_ref[...]
    local = (jnp.dot(tril, hi, preferred_element_type=F32) + jnp.dot(tril, mid, preferred_element_type=F32)
             + jnp.dot(tril, lo, preferred_element_type=F32))
    cum = local + carry_ref[0:1, :]
    cum_ref[...] = cum
    carry_ref[...] = jnp.broadcast_to(cum[tm - 1:tm, :], carry_ref.shape)


def _inproj(xt, seq_len, lw, tabs, tm=512):
    T = xt.shape[0]
    tps = seq_len // tm
    kern = functools.partial(_inproj_kernel, tiles_per_seq=tps)
    tab_spec = pl.BlockSpec((tm, LANES), lambda i: (i % tps, 0))
    return pl.pallas_call(
        kern,
        grid=(T // tm,),
        in_specs=[pl.BlockSpec((tm, D_MODEL), lambda i: (i, 0)),
                  _full((1, D_MODEL)), _full((D_MODEL, W_COLS)), _full((1, 256)), _full((1, 128)),
                  _full((256, 256)), _full((128, 256)), _full((128, 256)), _full((4, 256)), _full((4, 256)),
                  _full((256, 256)), tab_spec, tab_spec, tab_spec, _full((1, LANES)), _full((tm, tm))],
        out_specs=[pl.BlockSpec((tm, QKV_COLS), lambda i: (i, 0)), pl.BlockSpec((tm, LANES), lambda i: (i, 0))],
        out_shape=[jax.ShapeDtypeStruct((T, QKV_COLS), BF16), jax.ShapeDtypeStruct((T, LANES), F32)],
        scratch_shapes=[pltpu.VMEM((tm, W_COLS), F32), pltpu.VMEM((8, LANES), F32)],
        compiler_params=_cparams(("arbitrary",)),
        name="inproj",
    )(xt, lw["g1"], lw["wp"], lw["gcq"], lw["gckv"], lw["wuq"], lw["wuk"], lw["wuv"], lw["gq"], lw["gk"],
      tabs["e"], tabs["rc"], tabs["rsa"], tabs["rsb"], lw["fb"], tabs["tril"])


def _dense_attn_kernel(*refs, tq, fox):
    if fox:
        q_ref, k_ref, v_ref, cum_ref, o_ref, m_scr, l_scr, acc_scr = refs
    else:
        q_ref, k_ref, v_ref, o_ref, m_scr, l_scr, acc_scr = refs
        cum_ref = None
    tk = tq
    qi = pl.program_id(2)
    q = q_ref[0]
    lane = lax.broadcasted_iota(jnp.int32, (tq, LANES), 1)
    half0 = lane < HEAD_DIM
    zero = jnp.zeros_like(q)
    qh = (jnp.where(half0, q, zero), jnp.where(half0, zero, q))
    m_scr[...] = jnp.full(m_scr.shape, NEG_INF, F32)
    l_scr[...] = jnp.zeros_like(l_scr)
    acc_scr[...] = jnp.zeros_like(acc_scr)
    if fox:
        c0 = [cum_ref[0, 0, h:h + 1, pl.ds(pl.multiple_of(qi * tq, tq), LANES)][:, 0:1] for h in range(2)]

    def step(j, masked):
        start = pl.multiple_of(j * tk, tk)
        k = k_ref[0, pl.ds(start, tk), :]
        v = v_ref[0, pl.ds(start, tk), :]
        for h in range(2):
            s = lax.dot_general(qh[h], k, (((1,), (1,)), ((), ())), preferred_element_type=F32)
            if fox:
                s = s + (c0[h] - cum_ref[0, 0, h:h + 1, pl.ds(start, tk)])
            if masked:
                row = lax.broadcasted_iota(jnp.int32, (tq, tk), 0)
                col = lax.broadcasted_iota(jnp.int32, (tq, tk), 1)
                s = jnp.where(col <= row, s, NEG_INF)
            m_prev = m_scr[h]
            m_next = jnp.maximum(m_prev, jnp.max(s, axis=1, keepdims=True))
            p = jnp.exp(s - jnp.concatenate([m_next] * (tk // LANES), axis=1))
            alpha = jnp.exp(m_prev - m_next)
            l_scr[h] = alpha * l_scr[h] + jnp.sum(p, axis=1, keepdims=True)
            m_scr[h] = m_next
            pv = jnp.dot(p.astype(BF16), v, preferred_element_type=F32)
            acc_scr[h] = acc_scr[h] * alpha + pv

    def body(j, carry):
        step(j, False)
        return carry

    lax.fori_loop(0, qi, body, 0)
    step(qi, True)
    o = jnp.where(half0, acc_scr[0] / l_scr[0], acc_scr[1] / l_scr[1])
    o_ref[0] = o.astype(o_ref.dtype)


def _dense_attn(qkv3, cum8, jq, jk, jv, tq=512):
    B, S, _ = qkv3.shape
    fox = cum8 is not None
    in_specs = [pl.BlockSpec((1, tq, LANES), lambda b, p, i: (b, i, 2 * jq + p)),
                pl.BlockSpec((1, S, LANES), lambda b, p, i: (b, 0, 2 * jk + p)),
                pl.BlockSpec((1, S, LANES), lambda b, p, i: (b, 0, 2 * jv + p))]
    args = [qkv3, qkv3, qkv3]
    if fox:
        in_specs.append(pl.BlockSpec((1, 1, 8, S), lambda b, p, i: (b, p, 0, 0)))
        args.append(cum8)
    return pl.pallas_call(
        functools.partial(_dense_attn_kernel, tq=tq, fox=fox),
        grid=(B, 2, S // tq),
        in_specs=in_specs,
        out_specs=pl.BlockSpec((1, tq, LANES), lambda b, p, i: (b, i, p)),
        out_shape=jax.ShapeDtypeStruct((B, S, GROUP_W), BF16),
        scratch_shapes=[pltpu.VMEM((2, tq, LANES), F32)] * 3,
        compiler_params=_cparams(("parallel", "parallel", "arbitrary")),
        name="fox_attn" if fox else "mla_attn",
    )(*args)


def _banded_kernel(*refs, tq, maxdist, unit, slopes, kv_chunks, has_sink, want_lse):
    refs = list(refs)
    sink_ref = refs.pop(0) if has_sink else None
    q_ref, k_ref, v_ref, o_ref = refs[:4]
    lse_ref = refs[4] if want_lse else None
    qi = pl.program_id(2)
    q0 = qi * tq
    start = pl.multiple_of(jnp.maximum(q0 - LANES, 0), LANES)
    w = tq + LANES
    row = q0 + lax.broadcasted_iota(jnp.int32, (tq, w), 0)
    col = start + lax.broadcasted_iota(jnp.int32, (tq, w), 1)
    dist = row - col
    valid = (dist >= 0) & (dist <= maxdist)
    distf = dist.astype(F32) * float(unit)
    lane = lax.broadcasted_iota(jnp.int32, (tq, LANES), 1)
    half0 = lane < HEAD_DIM
    for c in range(2):
        qc = q_ref[0, :, c * LANES:(c + 1) * LANES]
        kc_idx = c if kv_chunks == 2 else 0
        kc = k_ref[0, pl.ds(start, w), kc_idx * LANES:(kc_idx + 1) * LANES]
        vc = v_ref[0, pl.ds(start, w), kc_idx * LANES:(kc_idx + 1) * LANES]
        zero = jnp.zeros_like(qc)
        o_h, lse_h = [], []
        for e in range(2):
            qh = jnp.where(half0, qc, zero) if e == 0 else jnp.where(half0, zero, qc)
            s = lax.dot_general(qh, kc, (((1,), (1,)), ((), ())), preferred_element_type=F32)
            s = jnp.where(valid, s - slopes[2 * c + e] * distf, NEG_INF)
            m = jnp.max(s, axis=1, keepdims=True)
            p = jnp.exp(s - m)
            lse = m + jnp.log(jnp.sum(p, axis=1, keepdims=True))
            if has_sink:
                sk = sink_ref[2 * c + e]
                mx = jnp.maximum(lse, sk)
                lse = mx + jnp.log(jnp.exp(lse - mx) + jnp.exp(sk - mx))
            pv = jnp.dot(p.astype(BF16), vc, preferred_element_type=F32)
            o_h.append(pv * jnp.exp(m - lse))
            lse_h.append(lse)
        o_ref[0, :, c * LANES:(c + 1) * LANES] = jnp.where(half0, o_h[0], o_h[1]).astype(o_ref.dtype)
        if want_lse:
            lse_ref[0, :, c * LANES:(c + 1) * LANES] = jnp.where(half0, lse_h[0], lse_h[1])


def _banded(qkv3, jq, jk, jv, kw, dil, maxdist, slopes, sink, want_lse, tq=256):
    B, S, nq = qkv3.shape
    L = S // dil
    tq = min(tq, L - LANES)
    assert tq >= LANES and L % tq == 0
    view = qkv3.reshape(B, L, dil * nq)
    has_sink = sink is not None
    kern = functools.partial(_banded_kernel, tq=tq, maxdist=maxdist, unit=dil, slopes=tuple(slopes),
                             kv_chunks=kw // LANES, has_sink=has_sink, want_lse=want_lse)
    in_specs = [pl.BlockSpec((1, tq, 256), lambda b, r, i: (b, i, r * (nq // 256) + jq)),
                pl.BlockSpec((1, L, kw), lambda b, r, i: (b, 0, r * (nq // kw) + jk)),
                pl.BlockSpec((1, L, kw), lambda b, r, i: (b, 0, r * (nq // kw) + jv))]
    args = [view, view, view]
    if has_sink:
        in_specs.insert(0, pl.BlockSpec(memory_space=pltpu.SMEM))
        args.insert(0, sink)
    out_spec = pl.BlockSpec((1, tq, 256), lambda b, r, i: (b, i, r))
    out_specs = [out_spec]
    out_shape = [jax.ShapeDtypeStruct((B, L, dil * 256), BF16)]
    if want_lse:
        out_specs.append(out_spec)
        out_shape.append(jax.ShapeDtypeStruct((B, L, dil * 256), F32))
    outs = pl.pallas_call(
        kern,
        grid=(B, dil, L // tq),
        in_specs=in_specs,
        out_specs=out_specs,
        out_shape=out_shape,
        compiler_params=_cparams(("parallel", "parallel", "parallel")),
        name=f"banded_d{dil}" if want_lse else "swa_attn",
    )(*args)
    return [o.reshape(B * S, 256) for o in outs]


def _wo_router_kernel(x_ref, oa_ref, ob0_ref, ob1_ref, ob2_ref, l0_ref, l1_ref, l2_ref, oc_ref, od_ref,
                      wo_ref, g2_ref, wr_ref, br_ref, ltri_ref, h_ref, xn_ref, route_ref, cnt_ref, carry_ref):
    i = pl.program_id(0)
    tm = x_ref.shape[0]

    @pl.when(i == 0)
    def _():
        carry_ref[...] = jnp.zeros_like(carry_ref)

    la, lb, lc = l0_ref[...], l1_ref[...], l2_ref[...]
    mx = jnp.maximum(jnp.maximum(la, lb), lc)
    ea, eb, ec = jnp.exp(la - mx), jnp.exp(lb - mx), jnp.exp(lc - mx)
    ob = (ea * ob0_ref[...].astype(F32) + eb * ob1_ref[...].astype(F32) + ec * ob2_ref[...].astype(F32)) / (ea + eb + ec)
    mix = jnp.concatenate([oa_ref[...], ob.astype(BF16), oc_ref[...], od_ref[...]], axis=1)
    h = x_ref[...] + jnp.dot(mix, wo_ref[...], preferred_element_type=F32)
    h_ref[...] = h
    xn = h * lax.rsqrt(jnp.mean(h * h, axis=-1, keepdims=True) + NORM_EPS) * g2_ref[...]
    xn_ref[...] = xn

    z = jnp.dot(xn, wr_ref[...], preferred_element_type=F32, precision=lax.Precision.HIGHEST) + br_ref[...]
    lane = lax.broadcasted_iota(jnp.int32, (tm, LANES), 1)
    lanef = lane.astype(F32)
    big = float(LANES)
    zg = jnp.where((lane >= N_EXPERTS) & (lane < N_EXPERTS + N_GROUPS), z, NEG_INF)
    mg = jnp.max(zg, axis=1, keepdims=True)
    p_g = 1.0 / jnp.sum(jnp.exp(zg - mg), axis=1, keepdims=True)
    gsel = jnp.min(jnp.where(zg == mg, lanef, big), axis=1, keepdims=True) - float(N_EXPERTS)
    lo = gsel * float(EXPERTS_PER_GROUP)
    ze = jnp.where((lanef >= lo) & (lanef < lo + float(EXPERTS_PER_GROUP)), z, NEG_INF)
    m1 = jnp.max(ze, axis=1, keepdims=True)
    i1 = jnp.min(jnp.where(ze == m1, lanef, big), axis=1, keepdims=True)
    ze2 = jnp.where(lanef == i1, NEG_INF, ze)
    m2 = jnp.max(ze2, axis=1, keepdims=True)
    i2 = jnp.min(jnp.where(ze2 == m2, lanef, big), axis=1, keepdims=True)
    e2 = jnp.exp(m2 - m1)
    gate1 = p_g / (1.0 + e2)
    gate2 = p_g * e2 / (1.0 + e2)

    oh1 = jnp.where(lanef == i1, 1.0, 0.0)
    oh2 = jnp.where(lanef == i2, 1.0, 0.0)
    oh = oh1 + oh2
    before = carry_ref[0:1, :] + jnp.dot(ltri_ref[...], oh.astype(BF16), preferred_element_type=F32)
    r1 = jnp.sum(before * oh1, axis=1, keepdims=True)
    r2 = jnp.sum(before * oh2, axis=1, keepdims=True)
    total = carry_ref[0:1, :] + jnp.sum(oh, axis=0, keepdims=True)
    carry_ref[...] = jnp.broadcast_to(total, carry_ref.shape)
    cnt_ref[...] = jnp.broadcast_to(total, cnt_ref.shape)

    route = jnp.where(lane == 0, i1, jnp.where(lane == 1, i2, jnp.where(lane == 2, r1, jnp.where(
        lane == 3, r2, jnp.where(lane == 4, gate1, jnp.where(lane == 5, gate2, 0.0))))))
    route_ref[...] = route


def _wo_router(xt, oa, obs, lses, oc, od, lw, ltri, tm=512):
    T = xt.shape[0]
    row = lambda w: pl.BlockSpec((tm, w), lambda i: (i, 0))
    return pl.pallas_call(
        _wo_router_kernel,
        grid=(T // tm,),
        in_specs=[row(D_MODEL), row(256), row(256), row(256), row(256), row(256), row(256), row(256), row(256),
                  row(256), _full((D_MODEL, D_MODEL)), _full((1, D_MODEL)), _full((D_MODEL, LANES)),
                  _full((1, LANES)), _full((tm, tm))],
        out_specs=[row(D_MODEL), row(D_MODEL), row(LANES), _full((8, LANES))],
        out_shape=[jax.ShapeDtypeStruct((T, D_MODEL), F32), jax.ShapeDtypeStruct((T, D_MODEL), F32),
                   jax.ShapeDtypeStruct((T, LANES), F32), jax.ShapeDtypeStruct((8, LANES), F32)],
        scratch_shapes=[pltpu.VMEM((8, LANES), F32)],
        compiler_params=_cparams(("arbitrary",)),
        name="wo_router",
    )(xt, oa, obs[0], obs[1], obs[2], lses[0], lses[1], lses[2], oc, od, lw["wo"], lw["g2"], lw["wr"], lw["br"], ltri)


def _row_copy(src, src_row, dst, dst_row, sem):
    return pltpu.make_async_copy(src.at[pl.ds(src_row, 1)], dst.at[pl.ds(dst_row, 1)], sem)


def _scatter_kernel(dest_ref, x_ref, buf_in, buf_out, sem, *, ts):
    del buf_in

    def issue(t, carry):
        for k in range(TOP_K):
            _row_copy(x_ref, t, buf_out, dest_ref[TOP_K * t + k], sem).start()
        return carry

    lax.fori_loop(0, ts, issue, 0)

    def drain(t, carry):
        for k in range(TOP_K):
            _row_copy(x_ref, 0, buf_out, 0, sem).wait()
        return carry

    lax.fori_loop(0, ts, drain, 0)


def _scatter_rows(xn, dest_flat, n_rows, ts=512):
    T, D = xn.shape
    return pl.pallas_call(
        functools.partial(_scatter_kernel, ts=ts),
        grid=(T // ts,),
        in_specs=[pl.BlockSpec((TOP_K * ts,), lambda i: (i,), memory_space=pltpu.SMEM),
                  pl.BlockSpec((ts, D), lambda i: (i, 0)), pl.BlockSpec(memory_space=pl.ANY)],
        out_specs=pl.BlockSpec(memory_space=pl.ANY),
        out_shape=jax.ShapeDtypeStruct((n_rows, D), F32),
        scratch_shapes=[pltpu.SemaphoreType.DMA(())],
        input_output_aliases={2: 0},
        compiler_params=_cparams(("arbitrary",)),
        name="moe_scatter",
    )(dest_flat, xn, jnp.zeros((n_rows, D), F32))


def _expert_kernel(blk_e_ref, nused_ref, x_ref, w1_ref, w3_ref, w2_ref, y_ref):
    del blk_e_ref
    i = pl.program_id(0)

    @pl.when(i < nused_ref[0])
    def _():
        x = x_ref[...].astype(BF16)
        a = jnp.dot(x, w1_ref[0], preferred_element_type=F32)
        b = jnp.dot(x, w3_ref[0], preferred_element_type=F32)
        hid = (a / (1.0 + jnp.exp(-a))) * b
        y_ref[...] = jnp.dot(hid.astype(BF16), w2_ref[0], preferred_element_type=F32)

    @pl.when(i >= nused_ref[0])
    def _():
        y_ref[...] = jnp.zeros_like(y_ref)


def _expert_mlp(xbuf, blk_e, nused, w1, w3, w2):
    P, D = xbuf.shape
    nblk = P // ROW_BLOCK
    x_map = lambda i, be, nu: (jnp.minimum(i, nu[0] - 1), 0)
    grid_spec = pltpu.PrefetchScalarGridSpec(
        num_scalar_prefetch=2,
        grid=(nblk,),
        in_specs=[pl.BlockSpec((ROW_BLOCK, D), x_map),
                  pl.BlockSpec((1, D, D_EXPERT), lambda i, be, nu: (be[i], 0, 0)),
                  pl.BlockSpec((1, D, D_EXPERT), lambda i, be, nu: (be[i], 0, 0)),
                  pl.BlockSpec((1, D_EXPERT, D), lambda i, be, nu: (be[i], 0, 0))],
        out_specs=pl.BlockSpec((ROW_BLOCK, D), lambda i, be, nu: (i, 0)),
    )
    return pl.pallas_call(
        _expert_kernel,
        grid_spec=grid_spec,
        out_shape=jax.ShapeDtypeStruct((P, D), F32),
        compiler_params=_cparams(("arbitrary",)),
        name="moe_experts",
    )(blk_e, nused, xbuf, w1, w3, w2)


def _combine_kernel(dest_ref, h_ref, route_ref, y_hbm, o_ref, rows_scr, sem, *, tc):
    def issue(t, carry):
        for k in range(TOP_K):
            _row_copy(y_hbm, dest_ref[TOP_K * t + k], rows_scr.at[k], t, sem).start()
        return carry

    lax.fori_loop(0, tc, issue, 0)

    def drain(t, carry):
        for k in range(TOP_K):
            _row_copy(y_hbm, 0, rows_scr.at[k], 0, sem).wait()
        return carry

    lax.fori_loop(0, tc, drain, 0)
    route = route_ref[...]
    o_ref[...] = h_ref[...] + route[:, 4:5] * rows_scr[0] + route[:, 5:6] * rows_scr[1]


def _combine(h, route, ybuf, dest_flat, tc=256):
    T, D = h.shape
    return pl.pallas_call(
        functools.partial(_combine_kernel, tc=tc),
        grid=(T // tc,),
        in_specs=[pl.BlockSpec((TOP_K * tc,), lambda i: (i,), memory_space=pltpu.SMEM),
                  pl.BlockSpec((tc, D), lambda i: (i, 0)), pl.BlockSpec((tc, LANES), lambda i: (i, 0)),
                  pl.BlockSpec(memory_space=pl.ANY)],
        out_specs=pl.BlockSpec((tc, D), lambda i: (i, 0)),
        out_shape=jax.ShapeDtypeStruct((T, D), F32),
        scratch_shapes=[pltpu.VMEM((TOP_K, tc, D), F32), pltpu.SemaphoreType.DMA(())],
        compiler_params=_cparams(("arbitrary",)),
        name="moe_combine",
    )(dest_flat, h, route, ybuf)


def _layer_weights(l, norm1_g, w_in, mla_gcq, mla_gckv, mla_wuq, mla_wukv, qk_gq, qk_gk, fox_bf, sink, w_o,
                   norm2_g, w_rg, b_rg, w_re, b_re, w1, w3, w2):
    offs = np.concatenate([[0], np.cumsum(IN_SPLITS)])
    w = w_in[l]
    cq, ckv, kr, pb, pc, fg, pdq, pdkv = [w[:, offs[j]:offs[j + 1]] for j in range(8)]
    z = lambda n: jnp.zeros((D_MODEL, n), F32)
    perm = np.array([0, 2, 1, 3])
    ublock = jnp.concatenate([cq, fg, z(256 - MLA_Q_RANK - HEADS)], axis=1)
    kr_rep = jnp.concatenate([z(MLA_NOPE), kr] * HEADS, axis=1)
    dq = pdq.reshape(D_MODEL, HEADS, HEAD_DIM)[:, perm].reshape(D_MODEL, GROUP_W)
    wp = jnp.concatenate([ckv, ublock, kr_rep, pb, pc, dq, pdkv], axis=1).astype(BF16)

    wuq = jnp.concatenate([mla_wuq[l], jnp.zeros((256 - MLA_Q_RANK, GROUP_W), F32)], axis=0).astype(BF16)
    wukv = mla_wukv[l].reshape(MLA_KV_RANK, HEADS, MLA_NOPE + HEAD_DIM)
    wuk = jnp.concatenate([wukv[:, :, :MLA_NOPE], jnp.zeros((MLA_KV_RANK, HEADS, MLA_ROPE), F32)], axis=2)
    wuk = wuk.reshape(MLA_KV_RANK, GROUP_W).astype(BF16)
    wuv = wukv[:, :, MLA_NOPE:].reshape(MLA_KV_RANK, GROUP_W).astype(BF16)
    gcq = jnp.concatenate([mla_gcq[l], jnp.zeros((256 - MLA_Q_RANK,), F32)])[None, :]
    scale = HEAD_DIM ** -0.5
    gq = jnp.tile(qk_gq[l], (1, HEADS)) * scale
    gk = jnp.tile(qk_gk[l], (1, HEADS))
    fb = jnp.zeros((1, LANES), F32).at[0, FG_LANE:FG_LANE + HEADS].set(fox_bf[l])
    wo = w_o[l]
    wo_d = wo[3 * GROUP_W:].reshape(HEADS, HEAD_DIM, D_MODEL)[perm].reshape(GROUP_W, D_MODEL)
    wo = jnp.concatenate([wo[:3 * GROUP_W], wo_d], axis=0).astype(BF16)
    wr = jnp.concatenate([w_re[l], w_rg[l], jnp.zeros((D_MODEL, LANES - N_EXPERTS - N_GROUPS), F32)], axis=1)
    br = jnp.concatenate([b_re[l], b_rg[l], jnp.zeros((LANES - N_EXPERTS - N_GROUPS,), F32)])[None, :]
    return dict(g1=norm1_g[l][None, :], wp=wp, gcq=gcq, gckv=mla_gckv[l][None, :], wuq=wuq, wuk=wuk, wuv=wuv,
                gq=gq, gk=gk, fb=fb, sink=sink[l][perm], wo=wo, g2=norm2_g[l][None, :], wr=wr, br=br,
                w1=w1[l].astype(BF16), w3=w3[l].astype(BF16), w2=w2[l].astype(BF16))


def _tables(seq_len, tm):
    half = MLA_ROPE // 2
    inv = ROPE_BASE ** (-jnp.arange(0, MLA_ROPE, 2, dtype=F32) / MLA_ROPE)
    ang = jnp.arange(seq_len, dtype=F32)[:, None] * inv[None, :]
    cos, sin = jnp.cos(ang), jnp.sin(ang)
    one = jnp.ones((seq_len, MLA_NOPE), F32)
    zn = jnp.zeros((seq_len, MLA_NOPE), F32)
    zh = jnp.zeros((seq_len, half), F32)
    rc = jnp.concatenate([one, cos, cos] * 2, axis=1)
    rsa = jnp.concatenate([zn, -sin, zh] * 2, axis=1)
    rsb = jnp.concatenate([zn, zh, sin] * 2, axis=1)
    seg = np.arange(256) // HEAD_DIM
    e = jnp.asarray(seg[:, None] == seg[None, :], BF16)
    idx = np.arange(tm)
    tril = jnp.asarray(idx[None, :] <= idx[:, None], BF16)
    ltri = jnp.asarray(idx[None, :] < idx[:, None], BF16)
    return dict(rc=rc, rsa=rsa, rsb=rsb, e=e, tril=tril, ltri=ltri)


def _alibi():
    n = 2 * HEADS
    s = [2.0 ** (-8.0 * i / n) for i in range(1, n + 1)]
    return s[HEADS:], s[:HEADS]


def kernel(x, norm1_g, w_in, mla_gcq, mla_gckv, mla_wuq, mla_wukv, qk_gq, qk_gk, fox_bf, sink, w_o,
           norm2_g, w_rg, b_rg, w_re, b_re, w1, w3, w2):
    B, S, D = x.shape
    T = B * S
    depth = w_in.shape[0]
    tm = 512
    tabs = _tables(S, tm)
    slopes_dil, slopes_swa = _alibi()
    slopes_swa_perm = [slopes_swa[0], slopes_swa[2], slopes_swa[1], slopes_swa[3]]
    n_assign = T * TOP_K
    nblk = n_assign // ROW_BLOCK + N_EXPERTS
    n_rows = nblk * ROW_BLOCK

    xt = x.reshape(T, D)
    for l in range(depth):
        lw = _layer_weights(l, norm1_g, w_in, mla_gcq, mla_gckv, mla_wuq, mla_wukv, qk_gq, qk_gk, fox_bf, sink,
                            w_o, norm2_g, w_rg, b_rg, w_re, b_re, w1, w3, w2)
        qkv, cum = _inproj(xt, S, lw, tabs, tm=tm)
        qkv3 = qkv.reshape(B, S, QKV_COLS)
        cum8 = jnp.pad(cum[:, FG_LANE:FG_LANE + HEADS].reshape(B, S, 2, 2).transpose(0, 2, 3, 1),
                       ((0, 0), (0, 0), (0, 6), (0, 0)))
        oa = _dense_attn(qkv3, None, QA, KA, VA).reshape(T, GROUP_W)
        oc = _dense_attn(qkv3, cum8, QC, KC, VC).reshape(T, GROUP_W)
        obs, lses = [], []
        for window, dil in DILATED_PAIRS:
            o, lse = _banded(qkv3, QB, KB, VB, 256, dil, window // dil, slopes_dil, None, True)
            obs.append(o)
            lses.append(lse)
        (od,) = _banded(qkv3, QD, 2 * KVD, 2 * KVD + 1, 128, 1, SWA_WINDOW - 1, slopes_swa_perm, lw["sink"], False)
        h, xn, route, cnt = _wo_router(xt, oa, obs, lses, oc, od, lw, tabs["ltri"], tm=tm)

        counts = cnt[0, :N_EXPERTS].astype(jnp.int32)
        padded = (counts + ROW_BLOCK - 1) // ROW_BLOCK * ROW_BLOCK
        pend = jnp.cumsum(padded)
        pstart = pend - padded
        expert = route[:, 0:TOP_K].astype(jnp.int32)
        rank = route[:, TOP_K:2 * TOP_K].astype(jnp.int32)
        dest = (pstart[expert] + rank).reshape(-1)
        blk_start = jnp.arange(nblk, dtype=jnp.int32) * ROW_BLOCK
        blk_e = jnp.minimum(jnp.sum((pend[None, :] <= blk_start[:, None]).astype(jnp.int32), axis=1), N_EXPERTS - 1)
        nused = (pend[-1:] // ROW_BLOCK).astype(jnp.int32)

        xbuf = _scatter_rows(xn, dest, n_rows)
        ybuf = _expert_mlp(xbuf, blk_e, nused, lw["w1"], lw["w3"], lw["w2"])
        xt = _combine(h, route, ybuf, dest)
    return xt.reshape(B, S, D)
```

```python
import functools

import numpy as np
import jax
import jax.numpy as jnp
from jax import lax
from jax.experimental import pallas as pl
from jax.experimental.pallas import tpu as pltpu

F32 = jnp.float32
BF16 = jnp.bfloat16

D_MODEL = 1024
HEAD_DIM = 64
HEADS = 4
GROUP_W = HEADS * HEAD_DIM
NORM_EPS = 1e-6
MLA_Q_RANK, MLA_KV_RANK, MLA_NOPE, MLA_ROPE = 192, 128, 32, 32
ROPE_BASE = 10000.0
DILATED_PAIRS = ((128, 1), (512, 4), (2048, 16))
SWA_WINDOW = 128
N_GROUPS, EXPERTS_PER_GROUP, N_EXPERTS, TOP_K, D_EXPERT = 4, 8, 32, 2, 256
IN_SPLITS = (MLA_Q_RANK, MLA_KV_RANK, MLA_ROPE, 3 * GROUP_W, 3 * GROUP_W, HEADS, GROUP_W, 2 * 2 * HEAD_DIM)

LANES = 128
W_COLS = 2688
QKV_COLS = 2816
QA, KA, VA, QB, KB, VB, QC, KC, VC, QD, KVD = range(11)
FG_LANE = 64
ROW_BLOCK = 256
VMEM_LIMIT = 56 * 1024 * 1024

NEG_INF = float("-inf")
LOG2E = 1.4426950408889634
EXP2_SAFE = 60.0


def _cparams(sem):
    return pltpu.CompilerParams(dimension_semantics=sem, vmem_limit_bytes=VMEM_LIMIT)


def _full(shape):
    zeros = (0,) * len(shape)
    return pl.BlockSpec(shape, lambda *_: zeros)


def _head_norm(y, g, e):
    ss = jnp.dot((y * y).astype(BF16), e, preferred_element_type=F32)
    return y * lax.rsqrt(ss * (1.0 / HEAD_DIM) + NORM_EPS) * g


def _rope(y, rc, rsa, rsb):
    outs = []
    for c in range(y.shape[1] // LANES):
        yc = y[:, c * LANES:(c + 1) * LANES]
        outs.append(yc * rc + pltpu.roll(yc, LANES - 16, 1) * rsa + pltpu.roll(yc, 16, 1) * rsb)
    return jnp.concatenate(outs, axis=1)


def _inproj_kernel(x_ref, g1_ref, w_ref, gcq_ref, gckv_ref, wuq_ref, wuk_ref, wuv_ref, gq_ref, gk_ref, e_ref,
                   rc_ref, rsa_ref, rsb_ref, fb_ref, tril_ref, qkv_ref, cum_ref, acc_ref, carry_ref,
                   *, tiles_per_seq):
    i = pl.program_id(0)
    tm = x_ref.shape[0]

    @pl.when(i % tiles_per_seq == 0)
    def _():
        carry_ref[...] = jnp.zeros_like(carry_ref)

    x = x_ref[...]
    ms = jnp.mean(x * x, axis=-1, keepdims=True)
    xn = (x * lax.rsqrt(ms + NORM_EPS) * g1_ref[...]).astype(BF16)
    acc_ref[...] = jnp.dot(xn, w_ref[...], preferred_element_type=F32)

    e = e_ref[...]
    rc, rsa, rsb = rc_ref[...], rsa_ref[...], rsb_ref[...]

    ckv = acc_ref[:, 0:128]
    ckvn = (ckv * lax.rsqrt(jnp.mean(ckv * ckv, axis=-1, keepdims=True) + NORM_EPS) * gckv_ref[...]).astype(BF16)
    u = acc_ref[:, 128:384]
    lane256 = lax.broadcasted_iota(jnp.int32, (tm, 256), 1)
    ssq = jnp.sum(jnp.where(lane256 < MLA_Q_RANK, u * u, 0.0), axis=-1, keepdims=True) * (1.0 / MLA_Q_RANK)
    un = (u * lax.rsqrt(ssq + NORM_EPS) * gcq_ref[...]).astype(BF16)
    qa = jnp.dot(un, wuq_ref[...], preferred_element_type=F32)
    ka = jnp.dot(ckvn, wuk_ref[...], preferred_element_type=F32) + acc_ref[:, 384:640]
    va = jnp.dot(ckvn, wuv_ref[...], preferred_element_type=F32)
    qa = _rope(_head_norm(qa, gq_ref[0:1, :], e), rc, rsa, rsb)
    ka = _rope(_head_norm(ka, gk_ref[0:1, :], e), rc, rsa, rsb)
    qkv_ref[:, QA * 256:(QA + 1) * 256] = qa.astype(BF16)
    qkv_ref[:, KA * 256:(KA + 1) * 256] = ka.astype(BF16)
    qkv_ref[:, VA * 256:(VA + 1) * 256] = va.astype(BF16)

    for m, (jq, jk, jv, base) in enumerate(((QB, KB, VB, 640), (QC, KC, VC, 1408))):
        q = _head_norm(acc_ref[:, base:base + 256], gq_ref[m + 1:m + 2, :], e)
        k = _head_norm(acc_ref[:, base + 256:base + 512], gk_ref[m + 1:m + 2, :], e)
        qkv_ref[:, jq * 256:(jq + 1) * 256] = q.astype(BF16)
        qkv_ref[:, jk * 256:(jk + 1) * 256] = k.astype(BF16)
        qkv_ref[:, jv * 256:(jv + 1) * 256] = acc_ref[:, base + 512:base + 768].astype(BF16)

    qd = _head_norm(acc_ref[:, 2176:2432], gq_ref[3:4, :], e)
    kd = _head_norm(acc_ref[:, 2432:2560], gk_ref[3:4, 0:128], e[0:128, 0:128])
    qkv_ref[:, QD * 256:(QD + 1) * 256] = qd.astype(BF16)
    qkv_ref[:, KVD * 256:KVD * 256 + 128] = kd.astype(BF16)
    qkv_ref[:, KVD * 256 + 128:(KVD + 1) * 256] = acc_ref[:, 2560:2688].astype(BF16)

    z = u[:, 128:256] + fb_ref[...]
    ls = jnp.minimum(z, 0.0) - jnp.log(1.0 + jnp.exp(-jnp.abs(z)))
    lane128 = lax.broadcasted_iota(jnp.int32, (tm, LANES), 1)
    ls = jnp.where((lane128 >= FG_LANE) & (lane128 < FG_LANE + HEADS), ls, 0.0)
    hi = ls.astype(BF16)
    r1 = ls - hi.astype(F32)
    mid = r1.astype(BF16)
    lo = (r1 - mid.astype(F32)).astype(BF16)
    tril = tril_ref[...]
    local = (jnp.dot(tril, hi, preferred_element_type=F32) + jnp.dot(tril, mid, preferred_element_type=F32)
             + jnp.dot(tril, lo, preferred_element_type=F32))
    cum = local + carry_ref[0:1, :]
    cum_ref[...] = cum
    carry_ref[...] = jnp.broadcast_to(cum[tm - 1:tm, :], carry_ref.shape)


def _inproj(xt, seq_len, lw, tabs, tm=512):
    T = xt.shape[0]
    tps = seq_len // tm
    kern = functools.partial(_inproj_kernel, tiles_per_seq=tps)
    tab_spec = pl.BlockSpec((tm, LANES), lambda i: (i % tps, 0))
    return pl.pallas_call(
        kern,
        grid=(T // tm,),
        in_specs=[pl.BlockSpec((tm, D_MODEL), lambda i: (i, 0)),
                  _full((1, D_MODEL)), _full((D_MODEL, W_COLS)), _full((1, 256)), _full((1, 128)),
                  _full((256, 256)), _full((128, 256)), _full((128, 256)), _full((4, 256)), _full((4, 256)),
                  _full((256, 256)), tab_spec, tab_spec, tab_spec, _full((1, LANES)), _full((tm, tm))],
        out_specs=[pl.BlockSpec((tm, QKV_COLS), lambda i: (i, 0)), pl.BlockSpec((tm, LANES), lambda i: (i, 0))],
        out_shape=[jax.ShapeDtypeStruct((T, QKV_COLS), BF16), jax.ShapeDtypeStruct((T, LANES), F32)],
        scratch_shapes=[pltpu.VMEM((tm, W_COLS), F32), pltpu.VMEM((8, LANES), F32)],
        compiler_params=_cparams(("arbitrary",)),
        name="inproj",
    )(xt, lw["g1"], lw["wp"], lw["gcq"], lw["gckv"], lw["wuq"], lw["wuk"], lw["wuv"], lw["gq"], lw["gk"],
      tabs["e"], tabs["rc"], tabs["rsa"], tabs["rsb"], lw["fb"], tabs["tril"])


def _split_heads(x):
    half0 = lax.broadcasted_iota(jnp.int32, x.shape, 1) < HEAD_DIM
    zero = jnp.zeros_like(x)
    return jnp.where(half0, x, zero), jnp.where(half0, zero, x)


def _causal_mask(s):
    row = lax.broadcasted_iota(jnp.int32, s.shape, 0)
    col = lax.broadcasted_iota(jnp.int32, s.shape, 1)
    return jnp.where(col <= row, s, NEG_INF)


def _row_cum(cq_ref, h):
    blk = cq_ref[0] * LOG2E
    lane = lax.broadcasted_iota(jnp.int32, blk.shape, 1)
    return jnp.sum(jnp.where(lane == FG_LANE + 2 * pl.program_id(1) + h, blk, 0.0), axis=1, keepdims=True)


def _dense_bounded_kernel(*refs, tq, tk, fox):
    if fox:
        q_ref, k_ref, v_ref, cq_ref, ck_ref, o_ref, acc_scr = refs
    else:
        q_ref, k_ref, v_ref, o_ref, acc_scr = refs
    qi = pl.program_id(2)
    qh = _split_heads(q_ref[0])
    lane_k = lax.broadcasted_iota(jnp.int32, (tk, LANES), 1)
    ones0 = jnp.where(lane_k == 0, 1.0, 0.0).astype(BF16)
    ones1 = jnp.where(lane_k == 1, 1.0, 0.0).astype(BF16)
    acc_scr[...] = jnp.zeros_like(acc_scr)
    if fox:
        cq = [_row_cum(cq_ref, h) for h in range(2)]

    def step(start, r0, masked):
        k = k_ref[0, pl.ds(start, tk), :]
        v0, v1 = _split_heads(v_ref[0, pl.ds(start, tk), :])
        v2 = jnp.concatenate([jnp.concatenate([v0, ones0], axis=1), jnp.concatenate([v1, ones1], axis=1)], axis=0)
        ps = []
        for h in range(2):
            s = lax.dot_general(qh[h][r0:, :], k, (((1,), (1,)), ((), ())), preferred_element_type=F32)
            if fox:
                s = (s + cq[h][r0:, :]) - ck_ref[0, 0, h:h + 1, pl.ds(start, tk)] * LOG2E
            if masked:
                s = _causal_mask(s)
            ps.append(jnp.exp2(s).astype(BF16))
        acc_scr[r0:, :] += jnp.dot(jnp.concatenate(ps, axis=1), v2, preferred_element_type=F32)

    def body(j, carry):
        step(pl.multiple_of(j * tk, tk), 0, False)
        return carry

    lax.fori_loop(0, qi * (tq // tk), body, 0)
    for d in range(tq // tk):
        step(pl.multiple_of(qi * tq + d * tk, tk), d * tk, True)
    acc = acc_scr[...]
    half0 = lax.broadcasted_iota(jnp.int32, (tq, LANES), 1) < HEAD_DIM
    o = jnp.where(half0, acc[:, 0:LANES] / acc[:, LANES:LANES + 1], acc[:, 0:LANES] / acc[:, LANES + 1:LANES + 2])
    o_ref[0] = o.astype(o_ref.dtype)


def _dense_online_kernel(*refs, tq, tk, fox):
    if fox:
        q_ref, k_ref, v_ref, cq_ref, ck_ref, o_ref, m_scr, l_scr, acc_scr = refs
    else:
        q_ref, k_ref, v_ref, o_ref, m_scr, l_scr, acc_scr = refs
    qi = pl.program_id(2)
    qh = _split_heads(q_ref[0])
    m_scr[...] = jnp.full(m_scr.shape, NEG_INF, F32)
    l_scr[...] = jnp.zeros_like(l_scr)
    acc_scr[...] = jnp.zeros_like(acc_scr)
    if fox:
        cq = [_row_cum(cq_ref, h) for h in range(2)]

    def step(start, r0, masked):
        k = k_ref[0, pl.ds(start, tk), :]
        v = v_ref[0, pl.ds(start, tk), :]
        for h in range(2):
            s = lax.dot_general(qh[h][r0:, :], k, (((1,), (1,)), ((), ())), preferred_element_type=F32)
            if fox:
                s = (s + cq[h][r0:, :]) - ck_ref[0, 0, h:h + 1, pl.ds(start, tk)] * LOG2E
            if masked:
                s = _causal_mask(s)
            m_prev = m_scr[h, r0:, :]
            m_next = jnp.maximum(m_prev, jnp.max(s, axis=1, keepdims=True))
            p = jnp.exp2(s - jnp.concatenate([m_next] * (tk // LANES), axis=1))
            alpha = jnp.exp2(m_prev - m_next)
            l_scr[h, r0:, :] = alpha * l_scr[h, r0:, :] + jnp.sum(p, axis=1, keepdims=True)
            m_scr[h, r0:, :] = m_next
            pv = jnp.dot(p.astype(BF16), v, preferred_element_type=F32)
            acc_scr[h, r0:, :] = acc_scr[h, r0:, :] * alpha + pv

    def body(j, carry):
        step(pl.multiple_of(j * tk, tk), 0, False)
        return carry

    lax.fori_loop(0, qi * (tq // tk), body, 0)
    for d in range(tq // tk):
        step(pl.multiple_of(qi * tq + d * tk, tk), d * tk, True)
    half0 = lax.broadcasted_iota(jnp.int32, (tq, LANES), 1) < HEAD_DIM
    o = jnp.where(half0, acc_scr[0] / l_scr[0], acc_scr[1] / l_scr[1])
    o_ref[0] = o.astype(o_ref.dtype)


def _dense_attn(qkv3, cum3, cum8, jq, jk, jv, logit_bound, tq=1024, tk=512):
    B, S, _ = qkv3.shape
    tq = min(tq, S)
    assert S % tq == 0 and tq % tk == 0
    fox = cum8 is not None
    in_specs = [pl.BlockSpec((1, tq, LANES), lambda b, p, i: (b, i, 2 * jq + p)),
                pl.BlockSpec((1, S, LANES), lambda b, p, i: (b, 0, 2 * jk + p)),
                pl.BlockSpec((1, S, LANES), lambda b, p, i: (b, 0, 2 * jv + p))]
    args = [qkv3, qkv3, qkv3]
    if fox:
        in_specs.append(pl.BlockSpec((1, tq, LANES), lambda b, p, i: (b, i, 0)))
        in_specs.append(pl.BlockSpec((1, 1, 8, S), lambda b, p, i: (b, p, 0, 0)))
        args += [cum3, cum8]
    common = dict(
        grid=(B, 2, S // tq),
        in_specs=in_specs,
        out_specs=pl.BlockSpec((1, tq, LANES), lambda b, p, i: (b, i, p)),
        out_shape=jax.ShapeDtypeStruct((B, S, GROUP_W), BF16),
        compiler_params=_cparams(("parallel", "parallel", "arbitrary")),
    )
    name = "fox_attn" if fox else "mla_attn"
    bounded = pl.pallas_call(functools.partial(_dense_bounded_kernel, tq=tq, tk=tk, fox=fox), name=name + "_bounded",
                             scratch_shapes=[pltpu.VMEM((tq, 2 * LANES), F32)], **common)
    online = pl.pallas_call(functools.partial(_dense_online_kernel, tq=tq, tk=tk, fox=fox), name=name + "_online",
                            scratch_shapes=[pltpu.VMEM((2, tq, LANES), F32)] * 3, **common)
    return lax.cond(logit_bound <= EXP2_SAFE, lambda *a: bounded(*a), lambda *a: online(*a), *args)


def _banded_kernel(*refs, tq, maxdist, unit, slopes, kv_chunks, has_sink, want_lse):
    refs = list(refs)
    sink_ref = refs.pop(0) if has_sink else None
    q_ref, k_ref, v_ref, o_ref = refs[:4]
    lse_ref = refs[4] if want_lse else None
    qi = pl.program_id(2)
    q0 = qi * tq
    start = pl.multiple_of(jnp.maximum(q0 - LANES, 0), LANES)
    w = tq + LANES
    row = q0 + lax.broadcasted_iota(jnp.int32, (tq, w), 0)
    col = start + lax.broadcasted_iota(jnp.int32, (tq, w), 1)
    dist = row - col
    valid = (dist >= 0) & (dist <= maxdist)
    distf = dist.astype(F32) * float(unit)
    lane = lax.broadcasted_iota(jnp.int32, (tq, LANES), 1)
    half0 = lane < HEAD_DIM
    for c in range(2):
        qc = q_ref[0, :, c * LANES:(c + 1) * LANES]
        kc_idx = c if kv_chunks == 2 else 0
        kc = k_ref[0, pl.ds(start, w), kc_idx * LANES:(kc_idx + 1) * LANES]
        vc = v_ref[0, pl.ds(start, w), kc_idx * LANES:(kc_idx + 1) * LANES]
        zero = jnp.zeros_like(qc)
        o_h, lse_h = [], []
        for e in range(2):
            qh = jnp.where(half0, qc, zero) if e == 0 else jnp.where(half0, zero, qc)
            s = lax.dot_general(qh, kc, (((1,), (1,)), ((), ())), preferred_element_type=F32)
            s = jnp.where(valid, s - slopes[2 * c + e] * distf, NEG_INF)
            m = jnp.max(s, axis=1, keepdims=True)
            p = jnp.exp(s - m)
            lse = m + jnp.log(jnp.sum(p, axis=1, keepdims=True))
            if has_sink:
                sk = sink_ref[2 * c + e]
                mx = jnp.maximum(lse, sk)
                lse = mx + jnp.log(jnp.exp(lse - mx) + jnp.exp(sk - mx))
            pv = jnp.dot(p.astype(BF16), vc, preferred_element_type=F32)
            o_h.append(pv * jnp.exp(m - lse))
            lse_h.append(lse)
        o_ref[0, :, c * LANES:(c + 1) * LANES] = jnp.where(half0, o_h[0], o_h[1]).astype(o_ref.dtype)
        if want_lse:
            lse_ref[0, :, c * LANES:(c + 1) * LANES] = jnp.where(half0, lse_h[0], lse_h[1])


def _banded(qkv3, jq, jk, jv, kw, dil, maxdist, slopes, sink, want_lse, tq=256):
    B, S, nq = qkv3.shape
    L = S // dil
    tq = min(tq, L - LANES)
    assert tq >= LANES and L % tq == 0
    view = qkv3.reshape(B, L, dil * nq)
    has_sink = sink is not None
    kern = functools.partial(_banded_kernel, tq=tq, maxdist=maxdist, unit=dil, slopes=tuple(slopes),
                             kv_chunks=kw // LANES, has_sink=has_sink, want_lse=want_lse)
    in_specs = [pl.BlockSpec((1, tq, 256), lambda b, r, i: (b, i, r * (nq // 256) + jq)),
                pl.BlockSpec((1, L, kw), lambda b, r, i: (b, 0, r * (nq // kw) + jk)),
                pl.BlockSpec((1, L, kw), lambda b, r, i: (b, 0, r * (nq // kw) + jv))]
    args = [view, view, view]
    if has_sink:
        in_specs.insert(0, pl.BlockSpec(memory_space=pltpu.SMEM))
        args.insert(0, sink)
    out_spec = pl.BlockSpec((1, tq, 256), lambda b, r, i: (b, i, r))
    out_specs = [out_spec]
    out_shape = [jax.ShapeDtypeStruct((B, L, dil * 256), BF16)]
    if want_lse:
        out_specs.append(out_spec)
        out_shape.append(jax.ShapeDtypeStruct((B, L, dil * 256), F32))
    outs = pl.pallas_call(
        kern,
        grid=(B, dil, L // tq),
        in_specs=in_specs,
        out_specs=out_specs,
        out_shape=out_shape,
        compiler_params=_cparams(("parallel", "parallel", "parallel")),
        name=f"banded_d{dil}" if want_lse else "swa_attn",
    )(*args)
    return [o.reshape(B * S, 256) for o in outs]


def _wo_router_kernel(x_ref, oa_ref, ob0_ref, ob1_ref, ob2_ref, l0_ref, l1_ref, l2_ref, oc_ref, od_ref,
                      wo_ref, g2_ref, wr_ref, br_ref, ltri_ref, h_ref, xn_ref, route_ref, cnt_ref, carry_ref):
    i = pl.program_id(0)
    tm = x_ref.shape[0]

    @pl.when(i == 0)
    def _():
        carry_ref[...] = jnp.zeros_like(carry_ref)

    la, lb, lc = l0_ref[...], l1_ref[...], l2_ref[...]
    mx = jnp.maximum(jnp.maximum(la, lb), lc)
    ea, eb, ec = jnp.exp(la - mx), jnp.exp(lb - mx), jnp.exp(lc - mx)
    ob = (ea * ob0_ref[...].astype(F32) + eb * ob1_ref[...].astype(F32) + ec * ob2_ref[...].astype(F32)) / (ea + eb + ec)
    mix = jnp.concatenate([oa_ref[...], ob.astype(BF16), oc_ref[...], od_ref[...]], axis=1)
    h = x_ref[...] + jnp.dot(mix, wo_ref[...], preferred_element_type=F32)
    h_ref[...] = h
    xn = h * lax.rsqrt(jnp.mean(h * h, axis=-1, keepdims=True) + NORM_EPS) * g2_ref[...]
    xn_ref[...] = xn

    z = jnp.dot(xn, wr_ref[...], preferred_element_type=F32, precision=lax.Precision.HIGHEST) + br_ref[...]
    lane = lax.broadcasted_iota(jnp.int32, (tm, LANES), 1)
    lanef = lane.astype(F32)
    big = float(LANES)
    zg = jnp.where((lane >= N_EXPERTS) & (lane < N_EXPERTS + N_GROUPS), z, NEG_INF)
    mg = jnp.max(zg, axis=1, keepdims=True)
    p_g = 1.0 / jnp.sum(jnp.exp(zg - mg), axis=1, keepdims=True)
    gsel = jnp.min(jnp.where(zg == mg, lanef, big), axis=1, keepdims=True) - float(N_EXPERTS)
    lo = gsel * float(EXPERTS_PER_GROUP)
    ze = jnp.where((lanef >= lo) & (lanef < lo + float(EXPERTS_PER_GROUP)), z, NEG_INF)
    m1 = jnp.max(ze, axis=1, keepdims=True)
    i1 = jnp.min(jnp.where(ze == m1, lanef, big), axis=1, keepdims=True)
    ze2 = jnp.where(lanef == i1, NEG_INF, ze)
    m2 = jnp.max(ze2, axis=1, keepdims=True)
    i2 = jnp.min(jnp.where(ze2 == m2, lanef, big), axis=1, keepdims=True)
    e2 = jnp.exp(m2 - m1)
    gate1 = p_g / (1.0 + e2)
    gate2 = p_g * e2 / (1.0 + e2)

    oh1 = jnp.where(lanef == i1, 1.0, 0.0)
    oh2 = jnp.where(lanef == i2, 1.0, 0.0)
    oh = oh1 + oh2
    before = carry_ref[0:1, :] + jnp.dot(ltri_ref[...], oh.astype(BF16), preferred_element_type=F32)
    r1 = jnp.sum(before * oh1, axis=1, keepdims=True)
    r2 = jnp.sum(before * oh2, axis=1, keepdims=True)
    total = carry_ref[0:1, :] + jnp.sum(oh, axis=0, keepdims=True)
    carry_ref[...] = jnp.broadcast_to(total, carry_ref.shape)
    cnt_ref[...] = jnp.broadcast_to(total, cnt_ref.shape)

    route = jnp.where(lane == 0, i1, jnp.where(lane == 1, i2, jnp.where(lane == 2, r1, jnp.where(
        lane == 3, r2, jnp.where(lane == 4, gate1, jnp.where(lane == 5, gate2, 0.0))))))
    route_ref[...] = route


def _wo_router(xt, oa, obs, lses, oc, od, lw, ltri, tm=512):
    T = xt.shape[0]
    row = lambda w: pl.BlockSpec((tm, w), lambda i: (i, 0))
    return pl.pallas_call(
        _wo_router_kernel,
        grid=(T // tm,),
        in_specs=[row(D_MODEL), row(256), row(256), row(256), row(256), row(256), row(256), row(256), row(256),
                  row(256), _full((D_MODEL, D_MODEL)), _full((1, D_MODEL)), _full((D_MODEL, LANES)),
                  _full((1, LANES)), _full((tm, tm))],
        out_specs=[row(D_MODEL), row(D_MODEL), row(LANES), _full((8, LANES))],
        out_shape=[jax.ShapeDtypeStruct((T, D_MODEL), F32), jax.ShapeDtypeStruct((T, D_MODEL), F32),
                   jax.ShapeDtypeStruct((T, LANES), F32), jax.ShapeDtypeStruct((8, LANES), F32)],
        scratch_shapes=[pltpu.VMEM((8, LANES), F32)],
        compiler_params=_cparams(("arbitrary",)),
        name="wo_router",
    )(xt, oa, obs[0], obs[1], obs[2], lses[0], lses[1], lses[2], oc, od, lw["wo"], lw["g2"], lw["wr"], lw["br"], ltri)


def _row_copy(src, src_row, dst, dst_row, sem):
    return pltpu.make_async_copy(src.at[pl.ds(src_row, 1)], dst.at[pl.ds(dst_row, 1)], sem)


def _scatter_kernel(dest_ref, x_ref, buf_in, buf_out, sem, *, ts):
    del buf_in

    def issue(t, carry):
        for k in range(TOP_K):
            _row_copy(x_ref, t, buf_out, dest_ref[TOP_K * t + k], sem).start()
        return carry

    lax.fori_loop(0, ts, issue, 0, unroll=8)
    for k in range(TOP_K):
        pltpu.make_async_copy(x_ref, buf_out.at[pl.ds(0, ts)], sem).wait()


def _scatter_rows(xn, dest_flat, n_rows, ts=512):
    T, D = xn.shape
    return pl.pallas_call(
        functools.partial(_scatter_kernel, ts=ts),
        grid=(T // ts,),
        in_specs=[pl.BlockSpec((TOP_K * ts,), lambda i: (i,), memory_space=pltpu.SMEM),
                  pl.BlockSpec((ts, D), lambda i: (i, 0)), pl.BlockSpec(memory_space=pl.ANY)],
        out_specs=pl.BlockSpec(memory_space=pl.ANY),
        out_shape=jax.ShapeDtypeStruct((n_rows, D), F32),
        scratch_shapes=[pltpu.SemaphoreType.DMA(())],
        input_output_aliases={2: 0},
        compiler_params=_cparams(("arbitrary",)),
        name="moe_scatter",
    )(dest_flat, xn, jnp.zeros((n_rows, D), F32))


def _expert_kernel(blk_e_ref, nused_ref, x_ref, w1_ref, w3_ref, w2_ref, y_ref):
    del blk_e_ref
    i = pl.program_id(0)

    @pl.when(i < nused_ref[0])
    def _():
        x = x_ref[...].astype(BF16)
        a = jnp.dot(x, w1_ref[0], preferred_element_type=F32)
        b = jnp.dot(x, w3_ref[0], preferred_element_type=F32)
        hid = (a / (1.0 + jnp.exp(-a))) * b
        y_ref[...] = jnp.dot(hid.astype(BF16), w2_ref[0], preferred_element_type=F32)

    @pl.when(i >= nused_ref[0])
    def _():
        y_ref[...] = jnp.zeros_like(y_ref)


def _expert_mlp(xbuf, blk_e, nused, w1, w3, w2):
    P, D = xbuf.shape
    nblk = P // ROW_BLOCK
    x_map = lambda i, be, nu: (jnp.maximum(jnp.minimum(i, nu[0] - 1), 0), 0)
    grid_spec = pltpu.PrefetchScalarGridSpec(
        num_scalar_prefetch=2,
        grid=(nblk,),
        in_specs=[pl.BlockSpec((ROW_BLOCK, D), x_map),
                  pl.BlockSpec((1, D, D_EXPERT), lambda i, be, nu: (be[i], 0, 0)),
                  pl.BlockSpec((1, D, D_EXPERT), lambda i, be, nu: (be[i], 0, 0)),
                  pl.BlockSpec((1, D_EXPERT, D), lambda i, be, nu: (be[i], 0, 0))],
        out_specs=pl.BlockSpec((ROW_BLOCK, D), lambda i, be, nu: (i, 0)),
    )
    return pl.pallas_call(
        _expert_kernel,
        grid_spec=grid_spec,
        out_shape=jax.ShapeDtypeStruct((P, D), F32),
        compiler_params=_cparams(("arbitrary",)),
        name="moe_experts",
    )(blk_e, nused, xbuf, w1, w3, w2)


def _combine_kernel(dest_ref, h_ref, route_ref, y_hbm, o_ref, rows_scr, sem, *, tc):
    def issue(t, carry):
        for k in range(TOP_K):
            _row_copy(y_hbm, dest_ref[TOP_K * t + k], rows_scr.at[k], t, sem).start()
        return carry

    lax.fori_loop(0, tc, issue, 0, unroll=8)
    for k in range(TOP_K):
        pltpu.make_async_copy(y_hbm.at[pl.ds(0, tc)], rows_scr.at[k], sem).wait()
    route = route_ref[...]
    o_ref[...] = h_ref[...] + route[:, 4:5] * rows_scr[0] + route[:, 5:6] * rows_scr[1]


def _combine(h, route, ybuf, dest_flat, tc=256):
    T, D = h.shape
    return pl.pallas_call(
        functools.partial(_combine_kernel, tc=tc),
        grid=(T // tc,),
        in_specs=[pl.BlockSpec((TOP_K * tc,), lambda i: (i,), memory_space=pltpu.SMEM),
                  pl.BlockSpec((tc, D), lambda i: (i, 0)), pl.BlockSpec((tc, LANES), lambda i: (i, 0)),
                  pl.BlockSpec(memory_space=pl.ANY)],
        out_specs=pl.BlockSpec((tc, D), lambda i: (i, 0)),
        out_shape=jax.ShapeDtypeStruct((T, D), F32),
        scratch_shapes=[pltpu.VMEM((TOP_K, tc, D), F32), pltpu.SemaphoreType.DMA(())],
        compiler_params=_cparams(("arbitrary",)),
        name="moe_combine",
    )(dest_flat, h, route, ybuf)


def _layer_weights(l, norm1_g, w_in, mla_gcq, mla_gckv, mla_wuq, mla_wukv, qk_gq, qk_gk, fox_bf, sink, w_o,
                   norm2_g, w_rg, b_rg, w_re, b_re, w1, w3, w2):
    offs = np.concatenate([[0], np.cumsum(IN_SPLITS)])
    w = w_in[l]
    cq, ckv, kr, pb, pc, fg, pdq, pdkv = [w[:, offs[j]:offs[j + 1]] for j in range(8)]
    z = lambda n: jnp.zeros((D_MODEL, n), F32)
    perm = np.array([0, 2, 1, 3])
    ublock = jnp.concatenate([cq, fg, z(256 - MLA_Q_RANK - HEADS)], axis=1)
    kr_rep = jnp.concatenate([z(MLA_NOPE), kr] * HEADS, axis=1)
    dq = pdq.reshape(D_MODEL, HEADS, HEAD_DIM)[:, perm].reshape(D_MODEL, GROUP_W)
    wp = jnp.concatenate([ckv, ublock, kr_rep, pb, pc, dq, pdkv], axis=1).astype(BF16)

    wuq = jnp.concatenate([mla_wuq[l], jnp.zeros((256 - MLA_Q_RANK, GROUP_W), F32)], axis=0).astype(BF16)
    wukv = mla_wukv[l].reshape(MLA_KV_RANK, HEADS, MLA_NOPE + HEAD_DIM)
    wuk = jnp.concatenate([wukv[:, :, :MLA_NOPE], jnp.zeros((MLA_KV_RANK, HEADS, MLA_ROPE), F32)], axis=2)
    wuk = wuk.reshape(MLA_KV_RANK, GROUP_W).astype(BF16)
    wuv = wukv[:, :, MLA_NOPE:].reshape(MLA_KV_RANK, GROUP_W).astype(BF16)
    gcq = jnp.concatenate([mla_gcq[l], jnp.zeros((256 - MLA_Q_RANK,), F32)])[None, :]
    scale = HEAD_DIM ** -0.5
    unit = jnp.array([LOG2E, 1.0, LOG2E, 1.0], F32)[:, None]
    gq = jnp.tile(qk_gq[l], (1, HEADS)) * (scale * unit)
    gk = jnp.tile(qk_gk[l], (1, HEADS))
    bound = HEAD_DIM * jnp.max(jnp.abs(gq), axis=1) * jnp.max(jnp.abs(gk), axis=1)
    fb = jnp.zeros((1, LANES), F32).at[0, FG_LANE:FG_LANE + HEADS].set(fox_bf[l])
    wo = w_o[l]
    wo_d = wo[3 * GROUP_W:].reshape(HEADS, HEAD_DIM, D_MODEL)[perm].reshape(GROUP_W, D_MODEL)
    wo = jnp.concatenate([wo[:3 * GROUP_W], wo_d], axis=0).astype(BF16)
    wr = jnp.concatenate([w_re[l], w_rg[l], jnp.zeros((D_MODEL, LANES - N_EXPERTS - N_GROUPS), F32)], axis=1)
    br = jnp.concatenate([b_re[l], b_rg[l], jnp.zeros((LANES - N_EXPERTS - N_GROUPS,), F32)])[None, :]
    return dict(g1=norm1_g[l][None, :], wp=wp, gcq=gcq, gckv=mla_gckv[l][None, :], wuq=wuq, wuk=wuk, wuv=wuv,
                gq=gq, gk=gk, bound=bound, fb=fb, sink=sink[l][perm], wo=wo, g2=norm2_g[l][None, :], wr=wr, br=br,
                w1=w1[l].astype(BF16), w3=w3[l].astype(BF16), w2=w2[l].astype(BF16))


def _tables(seq_len, tm):
    half = MLA_ROPE // 2
    inv = ROPE_BASE ** (-jnp.arange(0, MLA_ROPE, 2, dtype=F32) / MLA_ROPE)
    ang = jnp.arange(seq_len, dtype=F32)[:, None] * inv[None, :]
    cos, sin = jnp.cos(ang), jnp.sin(ang)
    one = jnp.ones((seq_len, MLA_NOPE), F32)
    zn = jnp.zeros((seq_len, MLA_NOPE), F32)
    zh = jnp.zeros((seq_len, half), F32)
    rc = jnp.concatenate([one, cos, cos] * 2, axis=1)
    rsa = jnp.concatenate([zn, -sin, zh] * 2, axis=1)
    rsb = jnp.concatenate([zn, zh, sin] * 2, axis=1)
    seg = np.arange(256) // HEAD_DIM
    e = jnp.asarray(seg[:, None] == seg[None, :], BF16)
    idx = np.arange(tm)
    tril = jnp.asarray(idx[None, :] <= idx[:, None], BF16)
    ltri = jnp.asarray(idx[None, :] < idx[:, None], BF16)
    return dict(rc=rc, rsa=rsa, rsb=rsb, e=e, tril=tril, ltri=ltri)


def _alibi():
    n = 2 * HEADS
    s = [2.0 ** (-8.0 * i / n) for i in range(1, n + 1)]
    return s[HEADS:], s[:HEADS]


def kernel(x, norm1_g, w_in, mla_gcq, mla_gckv, mla_wuq, mla_wukv, qk_gq, qk_gk, fox_bf, sink, w_o,
           norm2_g, w_rg, b_rg, w_re, b_re, w1, w3, w2):
    B, S, D = x.shape
    T = B * S
    depth = w_in.shape[0]
    tm = 512
    tabs = _tables(S, tm)
    slopes_dil, slopes_swa = _alibi()
    slopes_swa_perm = [slopes_swa[0], slopes_swa[2], slopes_swa[1], slopes_swa[3]]
    n_assign = T * TOP_K
    nblk = n_assign // ROW_BLOCK + N_EXPERTS
    n_rows = nblk * ROW_BLOCK

    xt = x.reshape(T, D)
    for l in range(depth):
        lw = _layer_weights(l, norm1_g, w_in, mla_gcq, mla_gckv, mla_wuq, mla_wukv, qk_gq, qk_gk, fox_bf, sink,
                            w_o, norm2_g, w_rg, b_rg, w_re, b_re, w1, w3, w2)
        qkv, cum = _inproj(xt, S, lw, tabs, tm=tm)
        qkv3 = qkv.reshape(B, S, QKV_COLS)
        cum8 = jnp.pad(cum[:, FG_LANE:FG_LANE + HEADS].reshape(B, S, 2, 2).transpose(0, 2, 3, 1),
                       ((0, 0), (0, 0), (0, 6), (0, 0)))
        oa = _dense_attn(qkv3, None, None, QA, KA, VA, lw["bound"][0]).reshape(T, GROUP_W)
        oc = _dense_attn(qkv3, cum.reshape(B, S, LANES), cum8, QC, KC, VC, lw["bound"][2]).reshape(T, GROUP_W)
        obs, lses = [], []
        for window, dil in DILATED_PAIRS:
            o, lse = _banded(qkv3, QB, KB, VB, 256, dil, window // dil, slopes_dil, None, True)
            obs.append(o)
            lses.append(lse)
        (od,) = _banded(qkv3, QD, 2 * KVD, 2 * KVD + 1, 128, 1, SWA_WINDOW - 1, slopes_swa_perm, lw["sink"], False)
        h, xn, route, cnt = _wo_router(xt, oa, obs, lses, oc, od, lw, tabs["ltri"], tm=tm)

        counts = cnt[0, :N_EXPERTS].astype(jnp.int32)
        padded = (counts + ROW_BLOCK - 1) // ROW_BLOCK * ROW_BLOCK
        pend = jnp.cumsum(padded)
        pstart = pend - padded
        expert = route[:, 0:TOP_K].astype(jnp.int32)
        rank = route[:, TOP_K:2 * TOP_K].astype(jnp.int32)
        dest = (pstart[expert] + rank).reshape(-1)
        blk_start = jnp.arange(nblk, dtype=jnp.int32) * ROW_BLOCK
        blk_e = jnp.minimum(jnp.sum((pend[None, :] <= blk_start[:, None]).astype(jnp.int32), axis=1), N_EXPERTS - 1)
        nused = (pend[-1:] // ROW_BLOCK).astype(jnp.int32)

        xbuf = _scatter_rows(xn, dest, n_rows)
        ybuf = _expert_mlp(xbuf, blk_e, nused, lw["w1"], lw["w3"], lw["w2"])
        xt = _combine(h, route, ybuf, dest)
    return xt.reshape(B, S, D)
```

```python
import functools

import numpy as np
import jax
import jax.numpy as jnp
from jax import lax
from jax.experimental import pallas as pl
from jax.experimental.pallas import tpu as pltpu

F32 = jnp.float32
BF16 = jnp.bfloat16

D_MODEL = 1024
HEAD_DIM = 64
HEADS = 4
GROUP_W = HEADS * HEAD_DIM
NORM_EPS = 1e-6
MLA_Q_RANK, MLA_KV_RANK, MLA_NOPE, MLA_ROPE = 192, 128, 32, 32
ROPE_BASE = 10000.0
DILATED_PAIRS = ((128, 1), (512, 4), (2048, 16))
SWA_WINDOW = 128
N_GROUPS, EXPERTS_PER_GROUP, N_EXPERTS, TOP_K, D_EXPERT = 4, 8, 32, 2, 256
IN_SPLITS = (MLA_Q_RANK, MLA_KV_RANK, MLA_ROPE, 3 * GROUP_W, 3 * GROUP_W, HEADS, GROUP_W, 2 * 2 * HEAD_DIM)

LANES = 128
W_COLS = 2688
QKV_COLS = 2048
QA, KA, VA, QC, KC, VC, QD, KVD = range(8)
QKVB_COLS = 768
BAND = 128
BAND_TOKENS = 2048
FG_LANE = 64
ROW_BLOCK = 256
VMEM_LIMIT = 56 * 1024 * 1024

NEG_INF = float("-inf")
LOG2E = 1.4426950408889634
EXP2_SAFE = 60.0


def _cparams(sem):
    return pltpu.CompilerParams(dimension_semantics=sem, vmem_limit_bytes=VMEM_LIMIT)


def _full(shape):
    zeros = (0,) * len(shape)
    return pl.BlockSpec(shape, lambda *_: zeros)


def _head_norm(y, g, e):
    ss = jnp.dot((y * y).astype(BF16), e, preferred_element_type=F32)
    return y * lax.rsqrt(ss * (1.0 / HEAD_DIM) + NORM_EPS) * g


def _rope(y, rc, rsa, rsb):
    outs = []
    for c in range(y.shape[1] // LANES):
        yc = y[:, c * LANES:(c + 1) * LANES]
        outs.append(yc * rc + pltpu.roll(yc, LANES - 16, 1) * rsa + pltpu.roll(yc, 16, 1) * rsb)
    return jnp.concatenate(outs, axis=1)


def _inproj_kernel(x_ref, g1_ref, w_ref, gcq_ref, gckv_ref, wuq_ref, wuk_ref, wuv_ref, gq_ref, gk_ref, e_ref,
                   rc_ref, rsa_ref, rsb_ref, fb_ref, tril_ref, qkv_ref, qkvb_ref, cum_ref, acc_ref, carry_ref,
                   *, tiles_per_seq):
    i = pl.program_id(0)
    tm = x_ref.shape[0]

    @pl.when(i % tiles_per_seq == 0)
    def _():
        carry_ref[...] = jnp.zeros_like(carry_ref)

    x = x_ref[...]
    ms = jnp.mean(x * x, axis=-1, keepdims=True)
    xn = (x * lax.rsqrt(ms + NORM_EPS) * g1_ref[...]).astype(BF16)
    acc_ref[...] = jnp.dot(xn, w_ref[...], preferred_element_type=F32)

    e = e_ref[...]
    rc, rsa, rsb = rc_ref[...], rsa_ref[...], rsb_ref[...]

    ckv = acc_ref[:, 0:128]
    ckvn = (ckv * lax.rsqrt(jnp.mean(ckv * ckv, axis=-1, keepdims=True) + NORM_EPS) * gckv_ref[...]).astype(BF16)
    u = acc_ref[:, 128:384]
    lane256 = lax.broadcasted_iota(jnp.int32, (tm, 256), 1)
    ssq = jnp.sum(jnp.where(lane256 < MLA_Q_RANK, u * u, 0.0), axis=-1, keepdims=True) * (1.0 / MLA_Q_RANK)
    un = (u * lax.rsqrt(ssq + NORM_EPS) * gcq_ref[...]).astype(BF16)
    qa = jnp.dot(un, wuq_ref[...], preferred_element_type=F32)
    ka = jnp.dot(ckvn, wuk_ref[...], preferred_element_type=F32) + acc_ref[:, 384:640]
    va = jnp.dot(ckvn, wuv_ref[...], preferred_element_type=F32)
    qa = _rope(_head_norm(qa, gq_ref[0:1, :], e), rc, rsa, rsb)
    ka = _rope(_head_norm(ka, gk_ref[0:1, :], e), rc, rsa, rsb)
    qkv_ref[:, QA * 256:(QA + 1) * 256] = qa.astype(BF16)
    qkv_ref[:, KA * 256:(KA + 1) * 256] = ka.astype(BF16)
    qkv_ref[:, VA * 256:(VA + 1) * 256] = va.astype(BF16)

    qkvb_ref[:, 0:256] = _head_norm(acc_ref[:, 640:896], gq_ref[1:2, :], e)
    qkvb_ref[:, 256:512] = _head_norm(acc_ref[:, 896:1152], gk_ref[1:2, :], e)
    qkvb_ref[:, 512:768] = acc_ref[:, 1152:1408]
    qkv_ref[:, QC * 256:(QC + 1) * 256] = _head_norm(acc_ref[:, 1408:1664], gq_ref[2:3, :], e).astype(BF16)
    qkv_ref[:, KC * 256:(KC + 1) * 256] = _head_norm(acc_ref[:, 1664:1920], gk_ref[2:3, :], e).astype(BF16)
    qkv_ref[:, VC * 256:(VC + 1) * 256] = acc_ref[:, 1920:2176].astype(BF16)

    qd = _head_norm(acc_ref[:, 2176:2432], gq_ref[3:4, :], e)
    kd = _head_norm(acc_ref[:, 2432:2560], gk_ref[3:4, 0:128], e[0:128, 0:128])
    qkv_ref[:, QD * 256:(QD + 1) * 256] = qd.astype(BF16)
    qkv_ref[:, KVD * 256:KVD * 256 + 128] = kd.astype(BF16)
    qkv_ref[:, KVD * 256 + 128:(KVD + 1) * 256] = acc_ref[:, 2560:2688].astype(BF16)

    z = u[:, 128:256] + fb_ref[...]
    ls = jnp.minimum(z, 0.0) - jnp.log(1.0 + jnp.exp(-jnp.abs(z)))
    lane128 = lax.broadcasted_iota(jnp.int32, (tm, LANES), 1)
    ls = jnp.where((lane128 >= FG_LANE) & (lane128 < FG_LANE + HEADS), ls, 0.0)
    hi = ls.astype(BF16)
    r1 = ls - hi.astype(F32)
    mid = r1.astype(BF16)
    lo = (r1 - mid.astype(F32)).astype(BF16)
    tril = tril_ref[...]
    local = (jnp.dot(tril, hi, preferred_element_type=F32) + jnp.dot(tril, mid, preferred_element_type=F32)
             + jnp.dot(tril, lo, preferred_element_type=F32))
    cum = local + carry_ref[0:1, :]
    cum_ref[...] = cum
    carry_ref[...] = jnp.broadcast_to(cum[tm - 1:tm, :], carry_ref.shape)


def _inproj(xt, seq_len, lw, tabs, tm=512):
    T = xt.shape[0]
    tps = seq_len // tm
    kern = functools.partial(_inproj_kernel, tiles_per_seq=tps)
    tab_spec = pl.BlockSpec((tm, LANES), lambda i: (i % tps, 0))
    return pl.pallas_call(
        kern,
        grid=(T // tm,),
        in_specs=[pl.BlockSpec((tm, D_MODEL), lambda i: (i, 0)),
                  _full((1, D_MODEL)), _full((D_MODEL, W_COLS)), _full((1, 256)), _full((1, 128)),
                  _full((256, 256)), _full((128, 256)), _full((128, 256)), _full((4, 256)), _full((4, 256)),
                  _full((256, 256)), tab_spec, tab_spec, tab_spec, _full((1, LANES)), _full((tm, tm))],
        out_specs=[pl.BlockSpec((tm, QKV_COLS), lambda i: (i, 0)), pl.BlockSpec((tm, QKVB_COLS), lambda i: (i, 0)),
                   pl.BlockSpec((tm, LANES), lambda i: (i, 0))],
        out_shape=[jax.ShapeDtypeStruct((T, QKV_COLS), BF16), jax.ShapeDtypeStruct((T, QKVB_COLS), F32),
                   jax.ShapeDtypeStruct((T, LANES), F32)],
        scratch_shapes=[pltpu.VMEM((tm, W_COLS), F32), pltpu.VMEM((8, LANES), F32)],
        compiler_params=_cparams(("arbitrary",)),
        name="inproj",
    )(xt, lw["g1"], lw["wp"], lw["gcq"], lw["gckv"], lw["wuq"], lw["wuk"], lw["wuv"], lw["gq"], lw["gk"],
      tabs["e"], tabs["rc"], tabs["rsa"], tabs["rsb"], lw["fb"], tabs["tril"])


def _split_heads(x):
    half0 = lax.broadcasted_iota(jnp.int32, x.shape, 1) < HEAD_DIM
    zero = jnp.zeros_like(x)
    return jnp.where(half0, x, zero), jnp.where(half0, zero, x)


def _causal_mask(s):
    row = lax.broadcasted_iota(jnp.int32, s.shape, 0)
    col = lax.broadcasted_iota(jnp.int32, s.shape, 1)
    return jnp.where(col <= row, s, NEG_INF)


def _row_cum(cq_ref, h):
    blk = cq_ref[0] * LOG2E
    lane = lax.broadcasted_iota(jnp.int32, blk.shape, 1)
    return jnp.sum(jnp.where(lane == FG_LANE + 2 * pl.program_id(1) + h, blk, 0.0), axis=1, keepdims=True)


def _dense_bounded_kernel(*refs, tq, tk, fox):
    if fox:
        q_ref, k_ref, v_ref, cq_ref, ck_ref, o_ref, acc_scr = refs
    else:
        q_ref, k_ref, v_ref, o_ref, acc_scr = refs
    qi = pl.program_id(2)
    qh = _split_heads(q_ref[0])
    lane_k = lax.broadcasted_iota(jnp.int32, (tk, LANES), 1)
    ones0 = jnp.where(lane_k == 0, 1.0, 0.0).astype(BF16)
    ones1 = jnp.where(lane_k == 1, 1.0, 0.0).astype(BF16)
    acc_scr[...] = jnp.zeros_like(acc_scr)
    if fox:
        cq = [_row_cum(cq_ref, h) for h in range(2)]

    def step(start, r0, masked):
        k = k_ref[0, pl.ds(start, tk), :]
        v0, v1 = _split_heads(v_ref[0, pl.ds(start, tk), :])
        v2 = jnp.concatenate([jnp.concatenate([v0, ones0], axis=1), jnp.concatenate([v1, ones1], axis=1)], axis=0)
        ps = []
        for h in range(2):
            s = lax.dot_general(qh[h][r0:, :], k, (((1,), (1,)), ((), ())), preferred_element_type=F32)
            if fox:
                s = (s + cq[h][r0:, :]) - ck_ref[0, 0, h:h + 1, pl.ds(start, tk)] * LOG2E
            if masked:
                s = _causal_mask(s)
            ps.append(jnp.exp2(s).astype(BF16))
        acc_scr[r0:, :] += jnp.dot(jnp.concatenate(ps, axis=1), v2, preferred_element_type=F32)

    def body(j, carry):
        step(pl.multiple_of(j * tk, tk), 0, False)
        return carry

    lax.fori_loop(0, qi * (tq // tk), body, 0)
    for d in range(tq // tk):
        step(pl.multiple_of(qi * tq + d * tk, tk), d * tk, True)
    acc = acc_scr[...]
    half0 = lax.broadcasted_iota(jnp.int32, (tq, LANES), 1) < HEAD_DIM
    o = jnp.where(half0, acc[:, 0:LANES] / acc[:, LANES:LANES + 1], acc[:, 0:LANES] / acc[:, LANES + 1:LANES + 2])
    o_ref[0] = o.astype(o_ref.dtype)


def _dense_online_kernel(*refs, tq, tk, fox):
    if fox:
        q_ref, k_ref, v_ref, cq_ref, ck_ref, o_ref, m_scr, l_scr, acc_scr = refs
    else:
        q_ref, k_ref, v_ref, o_ref, m_scr, l_scr, acc_scr = refs
    qi = pl.program_id(2)
    qh = _split_heads(q_ref[0])
    m_scr[...] = jnp.full(m_scr.shape, NEG_INF, F32)
    l_scr[...] = jnp.zeros_like(l_scr)
    acc_scr[...] = jnp.zeros_like(acc_scr)
    if fox:
        cq = [_row_cum(cq_ref, h) for h in range(2)]

    def step(start, r0, masked):
        k = k_ref[0, pl.ds(start, tk), :]
        v = v_ref[0, pl.ds(start, tk), :]
        for h in range(2):
            s = lax.dot_general(qh[h][r0:, :], k, (((1,), (1,)), ((), ())), preferred_element_type=F32)
            if fox:
                s = (s + cq[h][r0:, :]) - ck_ref[0, 0, h:h + 1, pl.ds(start, tk)] * LOG2E
            if masked:
                s = _causal_mask(s)
            m_prev = m_scr[h, r0:, :]
            m_next = jnp.maximum(m_prev, jnp.max(s, axis=1, keepdims=True))
            p = jnp.exp2(s - jnp.concatenate([m_next] * (tk // LANES), axis=1))
            alpha = jnp.exp2(m_prev - m_next)
            l_scr[h, r0:, :] = alpha * l_scr[h, r0:, :] + jnp.sum(p, axis=1, keepdims=True)
            m_scr[h, r0:, :] = m_next
            pv = jnp.dot(p.astype(BF16), v, preferred_element_type=F32)
            acc_scr[h, r0:, :] = acc_scr[h, r0:, :] * alpha + pv

    def body(j, carry):
        step(pl.multiple_of(j * tk, tk), 0, False)
        return carry

    lax.fori_loop(0, qi * (tq // tk), body, 0)
    for d in range(tq // tk):
        step(pl.multiple_of(qi * tq + d * tk, tk), d * tk, True)
    half0 = lax.broadcasted_iota(jnp.int32, (tq, LANES), 1) < HEAD_DIM
    o = jnp.where(half0, acc_scr[0] / l_scr[0], acc_scr[1] / l_scr[1])
    o_ref[0] = o.astype(o_ref.dtype)


def _dense_attn(qkv3, cum3, cum8, jq, jk, jv, logit_bound, tq=2048, tk=512):
    B, S, _ = qkv3.shape
    tq = min(tq, S)
    assert S % tq == 0 and tq % tk == 0
    fox = cum8 is not None
    in_specs = [pl.BlockSpec((1, tq, LANES), lambda b, p, i: (b, i, 2 * jq + p)),
                pl.BlockSpec((1, S, LANES), lambda b, p, i: (b, 0, 2 * jk + p)),
                pl.BlockSpec((1, S, LANES), lambda b, p, i: (b, 0, 2 * jv + p))]
    args = [qkv3, qkv3, qkv3]
    if fox:
        in_specs.append(pl.BlockSpec((1, tq, LANES), lambda b, p, i: (b, i, 0)))
        in_specs.append(pl.BlockSpec((1, 1, 8, S), lambda b, p, i: (b, p, 0, 0)))
        args += [cum3, cum8]
    common = dict(
        grid=(B, 2, S // tq),
        in_specs=in_specs,
        out_specs=pl.BlockSpec((1, tq, LANES), lambda b, p, i: (b, i, p)),
        out_shape=jax.ShapeDtypeStruct((B, S, GROUP_W), BF16),
        compiler_params=_cparams(("parallel", "parallel", "arbitrary")),
    )
    name = "fox_attn" if fox else "mla_attn"
    bounded = pl.pallas_call(functools.partial(_dense_bounded_kernel, tq=tq, tk=tk, fox=fox), name=name + "_bounded",
                             scratch_shapes=[pltpu.VMEM((tq, 2 * LANES), F32)], **common)
    online = pl.pallas_call(functools.partial(_dense_online_kernel, tq=tq, tk=tk, fox=fox), name=name + "_online",
                            scratch_shapes=[pltpu.VMEM((2, tq, LANES), F32)] * 3, **common)
    return lax.cond(logit_bound <= EXP2_SAFE, lambda *a: bounded(*a), lambda *a: online(*a), *args)


def _fold_rows(r, n, dil):
    return slice(r, r + n) if dil == 1 else pl.ds(r, n, stride=dil)


def _banded_kernel(*refs, dil, maxdist, slopes, kv_chunks, has_sink, want_lse):
    refs = list(refs)
    sink_ref = refs.pop(0) if has_sink else None
    q_refs = [refs.pop(0) for _ in range(2)]
    kv_refs = [[refs.pop(0) for _ in range(4)] for _ in range(kv_chunks)]
    o_ref = refs.pop(0)
    lse_ref = refs.pop(0) if want_lse else None
    o_scr = refs.pop(0)
    lse_scr = refs.pop(0) if want_lse else None
    nq = q_refs[0].shape[1] // dil
    row = lax.broadcasted_iota(jnp.int32, (BAND, 2 * BAND), 0)
    col = lax.broadcasted_iota(jnp.int32, (BAND, 2 * BAND), 1)
    dist = row + BAND - col
    valid = (dist >= 0) & (dist <= maxdist)
    distf = dist.astype(F32) * float(dil)
    bias = [jnp.where(valid, -slopes[h] * distf, NEG_INF) for h in range(HEADS)]
    no_prev = jnp.where(col < BAND, jnp.where(pl.program_id(1) == 0, NEG_INF, 0.0), 0.0)
    bias_first = [b + no_prev for b in bias]
    lane_k = lax.broadcasted_iota(jnp.int32, (2 * BAND, LANES), 1)
    ones0 = jnp.where(lane_k == 0, 1.0, 0.0).astype(BF16)
    ones1 = jnp.where(lane_k == 1, 1.0, 0.0).astype(BF16)
    half0 = lax.broadcasted_iota(jnp.int32, (BAND, LANES), 1) < HEAD_DIM
    for c in range(2):
        kp_ref, kc_ref, vp_ref, vc_ref = kv_refs[c if kv_chunks == 2 else 0]
        for r in range(dil):
            prev_rows, cur_rows = _fold_rows(r, BAND, dil), _fold_rows(r, nq, dil)
            qh = _split_heads(q_refs[c][0, cur_rows, :].astype(BF16))
            kf = jnp.concatenate([kp_ref[0, prev_rows, :], kc_ref[0, cur_rows, :]], axis=0).astype(BF16)
            vh = _split_heads(jnp.concatenate([vp_ref[0, prev_rows, :], vc_ref[0, cur_rows, :]], axis=0).astype(BF16))
            outs, lses = [], []
            for j in range(nq // BAND):
                keys = slice(j * BAND, (j + 2) * BAND)
                v2 = jnp.concatenate([jnp.concatenate([vh[0][keys], ones0], axis=1),
                                      jnp.concatenate([vh[1][keys], ones1], axis=1)], axis=0)
                ps, ms = [], []
                for e in range(2):
                    s = lax.dot_general(qh[e][j * BAND:(j + 1) * BAND], kf[keys], (((1,), (1,)), ((), ())),
                                        preferred_element_type=F32)
                    s = s + (bias_first if j == 0 else bias)[2 * c + e]
                    m = jnp.max(s, axis=1, keepdims=True)
                    ps.append(jnp.exp(s - m).astype(BF16))
                    ms.append(m)
                acc = jnp.dot(jnp.concatenate(ps, axis=1), v2, preferred_element_type=F32)
                scale, lse = [], []
                for e in range(2):
                    denom = acc[:, LANES + e:LANES + e + 1]
                    if has_sink:
                        ls = ms[e] + jnp.log(denom)
                        sk = sink_ref[2 * c + e]
                        mx = jnp.maximum(ls, sk)
                        ls = mx + jnp.log(jnp.exp(ls - mx) + jnp.exp(sk - mx))
                        scale.append(jnp.exp(ms[e] - ls))
                    else:
                        scale.append(1.0 / denom)
                    if want_lse:
                        lse.append(ms[e] + jnp.log(denom))
                outs.append(jnp.where(half0, acc[:, 0:LANES] * scale[0], acc[:, 0:LANES] * scale[1]))
                if want_lse:
                    lses.append(jnp.where(half0, lse[0], lse[1]))
            o_scr[cur_rows, :] = jnp.concatenate(outs, axis=0)
            if want_lse:
                lse_scr[cur_rows, :] = jnp.concatenate(lses, axis=0)
        o_ref[0, :, c * LANES:(c + 1) * LANES] = o_scr[...]
        if want_lse:
            lse_ref[0, :, c * LANES:(c + 1) * LANES] = lse_scr[...]


def _banded(q_arr, jq, kv_arr, jk, jv, kw, dil, maxdist, slopes, sink, want_lse):
    B, S, _ = q_arr.shape
    tb = min(BAND_TOKENS, S)
    pb = BAND * dil
    assert S % tb == 0 and tb % pb == 0 and maxdist <= BAND
    ratio = tb // pb
    has_sink = sink is not None
    kern = functools.partial(_banded_kernel, dil=dil, maxdist=maxdist, slopes=tuple(slopes),
                             kv_chunks=kw // LANES, has_sink=has_sink, want_lse=want_lse)
    prev = lambda j: pl.BlockSpec((1, pb, LANES), lambda b, i: (b, jnp.maximum(i * ratio - 1, 0), j))
    cur = lambda j: pl.BlockSpec((1, tb, LANES), lambda b, i: (b, i, j))
    in_specs = [cur(2 * jq), cur(2 * jq + 1)]
    args = [q_arr, q_arr]
    for c in range(kw // LANES):
        jkc, jvc = jk * (kw // LANES) + c, jv * (kw // LANES) + c
        in_specs += [prev(jkc), cur(jkc), prev(jvc), cur(jvc)]
        args += [kv_arr] * 4
    if has_sink:
        in_specs.insert(0, pl.BlockSpec(memory_space=pltpu.SMEM))
        args.insert(0, sink)
    n_out = 2 if want_lse else 1
    outs = pl.pallas_call(
        kern,
        grid=(B, S // tb),
        in_specs=in_specs,
        out_specs=[pl.BlockSpec((1, tb, 256), lambda b, i: (b, i, 0))] * n_out,
        out_shape=[jax.ShapeDtypeStruct((B, S, 256), F32)] * n_out,
        scratch_shapes=[pltpu.VMEM((tb, LANES), F32)] * n_out,
        compiler_params=_cparams(("parallel", "parallel")),
        name=f"banded_d{dil}" if want_lse else "swa_attn",
    )(*args)
    return [o.reshape(B * S, 256) for o in outs]


def _wo_router_kernel(x_ref, oa_ref, ob0_ref, ob1_ref, ob2_ref, l0_ref, l1_ref, l2_ref, oc_ref, od_ref,
                      wo_ref, g2_ref, wr_ref, br_ref, ltri_ref, h_ref, xn_ref, route_ref, cnt_ref, carry_ref):
    i = pl.program_id(0)
    tm = x_ref.shape[0]

    @pl.when(i == 0)
    def _():
        carry_ref[...] = jnp.zeros_like(carry_ref)

    la, lb, lc = l0_ref[...], l1_ref[...], l2_ref[...]
    mx = jnp.maximum(jnp.maximum(la, lb), lc)
    ea, eb, ec = jnp.exp(la - mx), jnp.exp(lb - mx), jnp.exp(lc - mx)
    ob = (ea * ob0_ref[...] + eb * ob1_ref[...] + ec * ob2_ref[...]) / (ea + eb + ec)
    mix = jnp.concatenate([oa_ref[...], ob.astype(BF16), oc_ref[...], od_ref[...].astype(BF16)], axis=1)
    h = x_ref[...] + jnp.dot(mix, wo_ref[...], preferred_element_type=F32)
    h_ref[...] = h
    xn = h * lax.rsqrt(jnp.mean(h * h, axis=-1, keepdims=True) + NORM_EPS) * g2_ref[...]
    xn_ref[...] = xn

    z = jnp.dot(xn, wr_ref[...], preferred_element_type=F32, precision=lax.Precision.HIGHEST) + br_ref[...]
    lane = lax.broadcasted_iota(jnp.int32, (tm, LANES), 1)
    lanef = lane.astype(F32)
    big = float(LANES)
    zg = jnp.where((lane >= N_EXPERTS) & (lane < N_EXPERTS + N_GROUPS), z, NEG_INF)
    mg = jnp.max(zg, axis=1, keepdims=True)
    p_g = 1.0 / jnp.sum(jnp.exp(zg - mg), axis=1, keepdims=True)
    gsel = jnp.min(jnp.where(zg == mg, lanef, big), axis=1, keepdims=True) - float(N_EXPERTS)
    lo = gsel * float(EXPERTS_PER_GROUP)
    ze = jnp.where((lanef >= lo) & (lanef < lo + float(EXPERTS_PER_GROUP)), z, NEG_INF)
    m1 = jnp.max(ze, axis=1, keepdims=True)
    i1 = jnp.min(jnp.where(ze == m1, lanef, big), axis=1, keepdims=True)
    ze2 = jnp.where(lanef == i1, NEG_INF, ze)
    m2 = jnp.max(ze2, axis=1, keepdims=True)
    i2 = jnp.min(jnp.where(ze2 == m2, lanef, big), axis=1, keepdims=True)
    e2 = jnp.exp(m2 - m1)
    gate1 = p_g / (1.0 + e2)
    gate2 = p_g * e2 / (1.0 + e2)

    oh1 = jnp.where(lanef == i1, 1.0, 0.0)
    oh2 = jnp.where(lanef == i2, 1.0, 0.0)
    oh = oh1 + oh2
    before = carry_ref[0:1, :] + jnp.dot(ltri_ref[...], oh.astype(BF16), preferred_element_type=F32)
    r1 = jnp.sum(before * oh1, axis=1, keepdims=True)
    r2 = jnp.sum(before * oh2, axis=1, keepdims=True)
    total = carry_ref[0:1, :] + jnp.sum(oh, axis=0, keepdims=True)
    carry_ref[...] = jnp.broadcast_to(total, carry_ref.shape)
    cnt_ref[...] = jnp.broadcast_to(total, cnt_ref.shape)

    route = jnp.where(lane == 0, i1, jnp.where(lane == 1, i2, jnp.where(lane == 2, r1, jnp.where(
        lane == 3, r2, jnp.where(lane == 4, gate1, jnp.where(lane == 5, gate2, 0.0))))))
    route_ref[...] = route


def _wo_router(xt, oa, obs, lses, oc, od, lw, ltri, tm=512):
    T = xt.shape[0]
    row = lambda w: pl.BlockSpec((tm, w), lambda i: (i, 0))
    return pl.pallas_call(
        _wo_router_kernel,
        grid=(T // tm,),
        in_specs=[row(D_MODEL), row(256), row(256), row(256), row(256), row(256), row(256), row(256), row(256),
                  row(256), _full((D_MODEL, D_MODEL)), _full((1, D_MODEL)), _full((D_MODEL, LANES)),
                  _full((1, LANES)), _full((tm, tm))],
        out_specs=[row(D_MODEL), row(D_MODEL), row(LANES), _full((8, LANES))],
        out_shape=[jax.ShapeDtypeStruct((T, D_MODEL), F32), jax.ShapeDtypeStruct((T, D_MODEL), F32),
                   jax.ShapeDtypeStruct((T, LANES), F32), jax.ShapeDtypeStruct((8, LANES), F32)],
        scratch_shapes=[pltpu.VMEM((8, LANES), F32)],
        compiler_params=_cparams(("arbitrary",)),
        name="wo_router",
    )(xt, oa, obs[0], obs[1], obs[2], lses[0], lses[1], lses[2], oc, od, lw["wo"], lw["g2"], lw["wr"], lw["br"], ltri)


def _row_copy(src, src_row, dst, dst_row, sem):
    return pltpu.make_async_copy(src.at[pl.ds(src_row, 1)], dst.at[pl.ds(dst_row, 1)], sem)


def _scatter_kernel(dest_ref, x_ref, buf_in, buf_out, sem, *, ts):
    del buf_in

    def issue(t, carry):
        for k in range(TOP_K):
            _row_copy(x_ref, t, buf_out, dest_ref[TOP_K * t + k], sem).start()
        return carry

    lax.fori_loop(0, ts, issue, 0, unroll=8)
    for k in range(TOP_K):
        pltpu.make_async_copy(x_ref, buf_out.at[pl.ds(0, ts)], sem).wait()


def _scatter_rows(xn, dest_flat, n_rows, ts=512):
    T, D = xn.shape
    return pl.pallas_call(
        functools.partial(_scatter_kernel, ts=ts),
        grid=(T // ts,),
        in_specs=[pl.BlockSpec((TOP_K * ts,), lambda i: (i,), memory_space=pltpu.SMEM),
                  pl.BlockSpec((ts, D), lambda i: (i, 0)), pl.BlockSpec(memory_space=pl.ANY)],
        out_specs=pl.BlockSpec(memory_space=pl.ANY),
        out_shape=jax.ShapeDtypeStruct((n_rows, D), F32),
        scratch_shapes=[pltpu.SemaphoreType.DMA(())],
        input_output_aliases={2: 0},
        compiler_params=_cparams(("arbitrary",)),
        name="moe_scatter",
    )(dest_flat, xn, jnp.zeros((n_rows, D), F32))


def _expert_kernel(blk_e_ref, nused_ref, x_ref, w1_ref, w3_ref, w2_ref, y_ref):
    del blk_e_ref
    i = pl.program_id(0)

    @pl.when(i < nused_ref[0])
    def _():
        x = x_ref[...].astype(BF16)
        a = jnp.dot(x, w1_ref[0], preferred_element_type=F32)
        b = jnp.dot(x, w3_ref[0], preferred_element_type=F32)
        hid = (a / (1.0 + jnp.exp(-a))) * b
        y_ref[...] = jnp.dot(hid.astype(BF16), w2_ref[0], preferred_element_type=F32)

    @pl.when(i >= nused_ref[0])
    def _():
        y_ref[...] = jnp.zeros_like(y_ref)


def _expert_mlp(xbuf, blk_e, nused, w1, w3, w2):
    P, D = xbuf.shape
    nblk = P // ROW_BLOCK
    x_map = lambda i, be, nu: (jnp.maximum(jnp.minimum(i, nu[0] - 1), 0), 0)
    grid_spec = pltpu.PrefetchScalarGridSpec(
        num_scalar_prefetch=2,
        grid=(nblk,),
        in_specs=[pl.BlockSpec((ROW_BLOCK, D), x_map),
                  pl.BlockSpec((1, D, D_EXPERT), lambda i, be, nu: (be[i], 0, 0)),
                  pl.BlockSpec((1, D, D_EXPERT), lambda i, be, nu: (be[i], 0, 0)),
                  pl.BlockSpec((1, D_EXPERT, D), lambda i, be, nu: (be[i], 0, 0))],
        out_specs=pl.BlockSpec((ROW_BLOCK, D), lambda i, be, nu: (i, 0)),
    )
    return pl.pallas_call(
        _expert_kernel,
        grid_spec=grid_spec,
        out_shape=jax.ShapeDtypeStruct((P, D), F32),
        compiler_params=_cparams(("arbitrary",)),
        name="moe_experts",
    )(blk_e, nused, xbuf, w1, w3, w2)


def _combine_kernel(dest_ref, h_ref, route_ref, y_hbm, o_ref, rows_scr, sem, *, tc):
    def issue(t, carry):
        for k in range(TOP_K):
            _row_copy(y_hbm, dest_ref[TOP_K * t + k], rows_scr.at[k], t, sem).start()
        return carry

    lax.fori_loop(0, tc, issue, 0, unroll=8)
    for k in range(TOP_K):
        pltpu.make_async_copy(y_hbm.at[pl.ds(0, tc)], rows_scr.at[k], sem).wait()
    route = route_ref[...]
    o_ref[...] = h_ref[...] + route[:, 4:5] * rows_scr[0] + route[:, 5:6] * rows_scr[1]


def _combine(h, route, ybuf, dest_flat, tc=256):
    T, D = h.shape
    return pl.pallas_call(
        functools.partial(_combine_kernel, tc=tc),
        grid=(T // tc,),
        in_specs=[pl.BlockSpec((TOP_K * tc,), lambda i: (i,), memory_space=pltpu.SMEM),
                  pl.BlockSpec((tc, D), lambda i: (i, 0)), pl.BlockSpec((tc, LANES), lambda i: (i, 0)),
                  pl.BlockSpec(memory_space=pl.ANY)],
        out_specs=pl.BlockSpec((tc, D), lambda i: (i, 0)),
        out_shape=jax.ShapeDtypeStruct((T, D), F32),
        scratch_shapes=[pltpu.VMEM((TOP_K, tc, D), F32), pltpu.SemaphoreType.DMA(())],
        compiler_params=_cparams(("arbitrary",)),
        name="moe_combine",
    )(dest_flat, h, route, ybuf)


def _layer_weights(l, norm1_g, w_in, mla_gcq, mla_gckv, mla_wuq, mla_wukv, qk_gq, qk_gk, fox_bf, sink, w_o,
                   norm2_g, w_rg, b_rg, w_re, b_re, w1, w3, w2):
    offs = np.concatenate([[0], np.cumsum(IN_SPLITS)])
    w = w_in[l]
    cq, ckv, kr, pb, pc, fg, pdq, pdkv = [w[:, offs[j]:offs[j + 1]] for j in range(8)]
    z = lambda n: jnp.zeros((D_MODEL, n), F32)
    perm = np.array([0, 2, 1, 3])
    ublock = jnp.concatenate([cq, fg, z(256 - MLA_Q_RANK - HEADS)], axis=1)
    kr_rep = jnp.concatenate([z(MLA_NOPE), kr] * HEADS, axis=1)
    dq = pdq.reshape(D_MODEL, HEADS, HEAD_DIM)[:, perm].reshape(D_MODEL, GROUP_W)
    wp = jnp.concatenate([ckv, ublock, kr_rep, pb, pc, dq, pdkv], axis=1).astype(BF16)

    wuq = jnp.concatenate([mla_wuq[l], jnp.zeros((256 - MLA_Q_RANK, GROUP_W), F32)], axis=0).astype(BF16)
    wukv = mla_wukv[l].reshape(MLA_KV_RANK, HEADS, MLA_NOPE + HEAD_DIM)
    wuk = jnp.concatenate([wukv[:, :, :MLA_NOPE], jnp.zeros((MLA_KV_RANK, HEADS, MLA_ROPE), F32)], axis=2)
    wuk = wuk.reshape(MLA_KV_RANK, GROUP_W).astype(BF16)
    wuv = wukv[:, :, MLA_NOPE:].reshape(MLA_KV_RANK, GROUP_W).astype(BF16)
    gcq = jnp.concatenate([mla_gcq[l], jnp.zeros((256 - MLA_Q_RANK,), F32)])[None, :]
    scale = HEAD_DIM ** -0.5
    unit = jnp.array([LOG2E, 1.0, LOG2E, 1.0], F32)[:, None]
    gq = jnp.tile(qk_gq[l], (1, HEADS)) * (scale * unit)
    gk = jnp.tile(qk_gk[l], (1, HEADS))
    bound = HEAD_DIM * jnp.max(jnp.abs(gq), axis=1) * jnp.max(jnp.abs(gk), axis=1)
    fb = jnp.zeros((1, LANES), F32).at[0, FG_LANE:FG_LANE + HEADS].set(fox_bf[l])
    wo = w_o[l]
    wo_d = wo[3 * GROUP_W:].reshape(HEADS, HEAD_DIM, D_MODEL)[perm].reshape(GROUP_W, D_MODEL)
    wo = jnp.concatenate([wo[:3 * GROUP_W], wo_d], axis=0).astype(BF16)
    wr = jnp.concatenate([w_re[l], w_rg[l], jnp.zeros((D_MODEL, LANES - N_EXPERTS - N_GROUPS), F32)], axis=1)
    br = jnp.concatenate([b_re[l], b_rg[l], jnp.zeros((LANES - N_EXPERTS - N_GROUPS,), F32)])[None, :]
    return dict(g1=norm1_g[l][None, :], wp=wp, gcq=gcq, gckv=mla_gckv[l][None, :], wuq=wuq, wuk=wuk, wuv=wuv,
                gq=gq, gk=gk, bound=bound, fb=fb, sink=sink[l][perm], wo=wo, g2=norm2_g[l][None, :], wr=wr, br=br,
                w1=w1[l].astype(BF16), w3=w3[l].astype(BF16), w2=w2[l].astype(BF16))


def _tables(seq_len, tm):
    half = MLA_ROPE // 2
    inv = ROPE_BASE ** (-jnp.arange(0, MLA_ROPE, 2, dtype=F32) / MLA_ROPE)
    ang = jnp.arange(seq_len, dtype=F32)[:, None] * inv[None, :]
    cos, sin = jnp.cos(ang), jnp.sin(ang)
    one = jnp.ones((seq_len, MLA_NOPE), F32)
    zn = jnp.zeros((seq_len, MLA_NOPE), F32)
    zh = jnp.zeros((seq_len, half), F32)
    rc = jnp.concatenate([one, cos, cos] * 2, axis=1)
    rsa = jnp.concatenate([zn, -sin, zh] * 2, axis=1)
    rsb = jnp.concatenate([zn, zh, sin] * 2, axis=1)
    seg = np.arange(256) // HEAD_DIM
    e = jnp.asarray(seg[:, None] == seg[None, :], BF16)
    idx = np.arange(tm)
    tril = jnp.asarray(idx[None, :] <= idx[:, None], BF16)
    ltri = jnp.asarray(idx[None, :] < idx[:, None], BF16)
    return dict(rc=rc, rsa=rsa, rsb=rsb, e=e, tril=tril, ltri=ltri)


def _alibi():
    n = 2 * HEADS
    s = [2.0 ** (-8.0 * i / n) for i in range(1, n + 1)]
    return s[HEADS:], s[:HEADS]


def kernel(x, norm1_g, w_in, mla_gcq, mla_gckv, mla_wuq, mla_wukv, qk_gq, qk_gk, fox_bf, sink, w_o,
           norm2_g, w_rg, b_rg, w_re, b_re, w1, w3, w2):
    B, S, D = x.shape
    T = B * S
    depth = w_in.shape[0]
    tm = 512
    tabs = _tables(S, tm)
    slopes_dil, slopes_swa = _alibi()
    slopes_swa_perm = [slopes_swa[0], slopes_swa[2], slopes_swa[1], slopes_swa[3]]
    n_assign = T * TOP_K
    nblk = n_assign // ROW_BLOCK + N_EXPERTS
    n_rows = nblk * ROW_BLOCK

    xt = x.reshape(T, D)
    for l in range(depth):
        lw = _layer_weights(l, norm1_g, w_in, mla_gcq, mla_gckv, mla_wuq, mla_wukv, qk_gq, qk_gk, fox_bf, sink,
                            w_o, norm2_g, w_rg, b_rg, w_re, b_re, w1, w3, w2)
        qkv, qkvb, cum = _inproj(xt, S, lw, tabs, tm=tm)
        qkv3 = qkv.reshape(B, S, QKV_COLS)
        qkvb3 = qkvb.reshape(B, S, QKVB_COLS)
        cum8 = jnp.pad(cum[:, FG_LANE:FG_LANE + HEADS].reshape(B, S, 2, 2).transpose(0, 2, 3, 1),
                       ((0, 0), (0, 0), (0, 6), (0, 0)))
        oa = _dense_attn(qkv3, None, None, QA, KA, VA, lw["bound"][0]).reshape(T, GROUP_W)
        oc = _dense_attn(qkv3, cum.reshape(B, S, LANES), cum8, QC, KC, VC, lw["bound"][2]).reshape(T, GROUP_W)
        obs, lses = [], []
        for window, dil in DILATED_PAIRS:
            o, lse = _banded(qkvb3, 0, qkvb3, 1, 2, 256, dil, window // dil, slopes_dil, None, True)
            obs.append(o)
            lses.append(lse)
        (od,) = _banded(qkv3, QD, qkv3, 2 * KVD, 2 * KVD + 1, 128, 1, SWA_WINDOW - 1, slopes_swa_perm,
                        lw["sink"], False)
        h, xn, route, cnt = _wo_router(xt, oa, obs, lses, oc, od, lw, tabs["ltri"], tm=tm)

        counts = cnt[0, :N_EXPERTS].astype(jnp.int32)
        padded = (counts + ROW_BLOCK - 1) // ROW_BLOCK * ROW_BLOCK
        pend = jnp.cumsum(padded)
        pstart = pend - padded
        expert = route[:, 0:TOP_K].astype(jnp.int32)
        rank = route[:, TOP_K:2 * TOP_K].astype(jnp.int32)
        dest = (pstart[expert] + rank).reshape(-1)
        blk_start = jnp.arange(nblk, dtype=jnp.int32) * ROW_BLOCK
        blk_e = jnp.minimum(jnp.sum((pend[None, :] <= blk_start[:, None]).astype(jnp.int32), axis=1), N_EXPERTS - 1)
        nused = (pend[-1:] // ROW_BLOCK).astype(jnp.int32)

        xbuf = _scatter_rows(xn, dest, n_rows)
        ybuf = _expert_mlp(xbuf, blk_e, nused, lw["w1"], lw["w3"], lw["w2"])
        xt = _combine(h, route, ybuf, dest)
    return xt.reshape(B, S, D)
```

```python
import functools

import numpy as np
import jax
import jax.numpy as jnp
from jax import lax
from jax.experimental import pallas as pl
from jax.experimental.pallas import tpu as pltpu

F32 = jnp.float32
BF16 = jnp.bfloat16

D_MODEL = 1024
HEAD_DIM = 64
HEADS = 4
GROUP_W = HEADS * HEAD_DIM
NORM_EPS = 1e-6
MLA_Q_RANK, MLA_KV_RANK, MLA_NOPE, MLA_ROPE = 192, 128, 32, 32
ROPE_BASE = 10000.0
DILATED_PAIRS = ((128, 1), (512, 4), (2048, 16))
SWA_WINDOW = 128
N_GROUPS, EXPERTS_PER_GROUP, N_EXPERTS, TOP_K, D_EXPERT = 4, 8, 32, 2, 256
IN_SPLITS = (MLA_Q_RANK, MLA_KV_RANK, MLA_ROPE, 3 * GROUP_W, 3 * GROUP_W, HEADS, GROUP_W, 2 * 2 * HEAD_DIM)

LANES = 128
W_COLS = 2688
QKV_COLS = 2048
QA, KA, VA, QC, KC, VC, QD, KVD = range(8)
QKVB_COLS = 768
BAND = 128
BAND_TOKENS = 2048
FG_LANE = 64
ROW_BLOCK = 512
VMEM_LIMIT = 56 * 1024 * 1024

NEG_INF = float("-inf")
LOG2E = 1.4426950408889634
EXP2_SAFE = 60.0


def _cparams(sem):
    return pltpu.CompilerParams(dimension_semantics=sem, vmem_limit_bytes=VMEM_LIMIT)


def _full(shape):
    zeros = (0,) * len(shape)
    return pl.BlockSpec(shape, lambda *_: zeros)


def _head_norm(y, g, e):
    ss = jnp.dot((y * y).astype(BF16), e, preferred_element_type=F32)
    return y * lax.rsqrt(ss * (1.0 / HEAD_DIM) + NORM_EPS) * g


def _rope(y, rc, rsa, rsb):
    outs = []
    for c in range(y.shape[1] // LANES):
        yc = y[:, c * LANES:(c + 1) * LANES]
        outs.append(yc * rc + pltpu.roll(yc, LANES - 16, 1) * rsa + pltpu.roll(yc, 16, 1) * rsb)
    return jnp.concatenate(outs, axis=1)


def _inproj_kernel(x_ref, g1_ref, w_ref, gcq_ref, gckv_ref, wuq_ref, wuk_ref, wuv_ref, gq_ref, gk_ref, e_ref,
                   rc_ref, rsa_ref, rsb_ref, fb_ref, tril_ref, qkv_ref, qkvb_ref, cum_ref, acc_ref, carry_ref,
                   *, tiles_per_seq):
    i = pl.program_id(0)
    tm = x_ref.shape[0]

    @pl.when(i % tiles_per_seq == 0)
    def _():
        carry_ref[...] = jnp.zeros_like(carry_ref)

    x = x_ref[...]
    ms = jnp.mean(x * x, axis=-1, keepdims=True)
    xn = (x * lax.rsqrt(ms + NORM_EPS) * g1_ref[...]).astype(BF16)
    acc_ref[...] = jnp.dot(xn, w_ref[...], preferred_element_type=F32)

    e = e_ref[...]
    rc, rsa, rsb = rc_ref[...], rsa_ref[...], rsb_ref[...]

    ckv = acc_ref[:, 0:128]
    ckvn = (ckv * lax.rsqrt(jnp.mean(ckv * ckv, axis=-1, keepdims=True) + NORM_EPS) * gckv_ref[...]).astype(BF16)
    u = acc_ref[:, 128:384]
    lane256 = lax.broadcasted_iota(jnp.int32, (tm, 256), 1)
    ssq = jnp.sum(jnp.where(lane256 < MLA_Q_RANK, u * u, 0.0), axis=-1, keepdims=True) * (1.0 / MLA_Q_RANK)
    un = (u * lax.rsqrt(ssq + NORM_EPS) * gcq_ref[...]).astype(BF16)
    qa = jnp.dot(un, wuq_ref[...], preferred_element_type=F32)
    ka = jnp.dot(ckvn, wuk_ref[...], preferred_element_type=F32) + acc_ref[:, 384:640]
    va = jnp.dot(ckvn, wuv_ref[...], preferred_element_type=F32)
    qa = _rope(_head_norm(qa, gq_ref[0:1, :], e), rc, rsa, rsb)
    ka = _rope(_head_norm(ka, gk_ref[0:1, :], e), rc, rsa, rsb)
    qkv_ref[:, QA * 256:(QA + 1) * 256] = qa.astype(BF16)
    qkv_ref[:, KA * 256:(KA + 1) * 256] = ka.astype(BF16)
    qkv_ref[:, VA * 256:(VA + 1) * 256] = va.astype(BF16)

    qkvb_ref[:, 0:256] = _head_norm(acc_ref[:, 640:896], gq_ref[1:2, :], e)
    qkvb_ref[:, 256:512] = _head_norm(acc_ref[:, 896:1152], gk_ref[1:2, :], e)
    qkvb_ref[:, 512:768] = acc_ref[:, 1152:1408]
    qkv_ref[:, QC * 256:(QC + 1) * 256] = _head_norm(acc_ref[:, 1408:1664], gq_ref[2:3, :], e).astype(BF16)
    qkv_ref[:, KC * 256:(KC + 1) * 256] = _head_norm(acc_ref[:, 1664:1920], gk_ref[2:3, :], e).astype(BF16)
    qkv_ref[:, VC * 256:(VC + 1) * 256] = acc_ref[:, 1920:2176].astype(BF16)

    qd = _head_norm(acc_ref[:, 2176:2432], gq_ref[3:4, :], e)
    kd = _head_norm(acc_ref[:, 2432:2560], gk_ref[3:4, 0:128], e[0:128, 0:128])
    qkv_ref[:, QD * 256:(QD + 1) * 256] = qd.astype(BF16)
    qkv_ref[:, KVD * 256:KVD * 256 + 128] = kd.astype(BF16)
    qkv_ref[:, KVD * 256 + 128:(KVD + 1) * 256] = acc_ref[:, 2560:2688].astype(BF16)

    z = u[:, 128:256] + fb_ref[...]
    ls = jnp.minimum(z, 0.0) - jnp.log(1.0 + jnp.exp(-jnp.abs(z)))
    lane128 = lax.broadcasted_iota(jnp.int32, (tm, LANES), 1)
    ls = jnp.where((lane128 >= FG_LANE) & (lane128 < FG_LANE + HEADS), ls, 0.0)
    hi = ls.astype(BF16)
    r1 = ls - hi.astype(F32)
    mid = r1.astype(BF16)
    lo = (r1 - mid.astype(F32)).astype(BF16)
    tril = tril_ref[...]
    local = (jnp.dot(tril, hi, preferred_element_type=F32) + jnp.dot(tril, mid, preferred_element_type=F32)
             + jnp.dot(tril, lo, preferred_element_type=F32))
    cum = local + carry_ref[0:1, :]
    cum_ref[...] = cum
    carry_ref[...] = jnp.broadcast_to(cum[tm - 1:tm, :], carry_ref.shape)


def _inproj(xt, seq_len, lw, tabs, tm=512):
    T = xt.shape[0]
    tps = seq_len // tm
    kern = functools.partial(_inproj_kernel, tiles_per_seq=tps)
    tab_spec = pl.BlockSpec((tm, LANES), lambda i: (i % tps, 0))
    return pl.pallas_call(
        kern,
        grid=(T // tm,),
        in_specs=[pl.BlockSpec((tm, D_MODEL), lambda i: (i, 0)),
                  _full((1, D_MODEL)), _full((D_MODEL, W_COLS)), _full((1, 256)), _full((1, 128)),
                  _full((256, 256)), _full((128, 256)), _full((128, 256)), _full((4, 256)), _full((4, 256)),
                  _full((256, 256)), tab_spec, tab_spec, tab_spec, _full((1, LANES)), _full((tm, tm))],
        out_specs=[pl.BlockSpec((tm, QKV_COLS), lambda i: (i, 0)), pl.BlockSpec((tm, QKVB_COLS), lambda i: (i, 0)),
                   pl.BlockSpec((tm, LANES), lambda i: (i, 0))],
        out_shape=[jax.ShapeDtypeStruct((T, QKV_COLS), BF16), jax.ShapeDtypeStruct((T, QKVB_COLS), F32),
                   jax.ShapeDtypeStruct((T, LANES), F32)],
        scratch_shapes=[pltpu.VMEM((tm, W_COLS), F32), pltpu.VMEM((8, LANES), F32)],
        compiler_params=_cparams(("arbitrary",)),
        name="inproj",
    )(xt, lw["g1"], lw["wp"], lw["gcq"], lw["gckv"], lw["wuq"], lw["wuk"], lw["wuv"], lw["gq"], lw["gk"],
      tabs["e"], tabs["rc"], tabs["rsa"], tabs["rsb"], lw["fb"], tabs["tril"])


def _split_heads(x):
    half0 = lax.broadcasted_iota(jnp.int32, x.shape, 1) < HEAD_DIM
    zero = jnp.zeros_like(x)
    return jnp.where(half0, x, zero), jnp.where(half0, zero, x)


def _causal_mask(s):
    row = lax.broadcasted_iota(jnp.int32, s.shape, 0)
    col = lax.broadcasted_iota(jnp.int32, s.shape, 1)
    return jnp.where(col <= row, s, NEG_INF)


def _row_cum(cq_ref, h):
    blk = cq_ref[0] * LOG2E
    lane = lax.broadcasted_iota(jnp.int32, blk.shape, 1)
    return jnp.sum(jnp.where(lane == FG_LANE + 2 * pl.program_id(1) + h, blk, 0.0), axis=1, keepdims=True)


def _dense_bounded_kernel(*refs, tq, tk, fox):
    if fox:
        q_ref, k_ref, v_ref, cq_ref, ck_ref, o_ref, acc_scr = refs
    else:
        q_ref, k_ref, v_ref, o_ref, acc_scr = refs
    qi = pl.program_id(2)
    qh = _split_heads(q_ref[0])
    lane_k = lax.broadcasted_iota(jnp.int32, (tk, LANES), 1)
    ones0 = jnp.where(lane_k == 0, 1.0, 0.0).astype(BF16)
    ones1 = jnp.where(lane_k == 1, 1.0, 0.0).astype(BF16)
    acc_scr[...] = jnp.zeros_like(acc_scr)
    if fox:
        cq = [_row_cum(cq_ref, h) for h in range(2)]

    def step(start, r0, masked):
        k = k_ref[0, pl.ds(start, tk), :]
        v0, v1 = _split_heads(v_ref[0, pl.ds(start, tk), :])
        v2 = jnp.concatenate([jnp.concatenate([v0, ones0], axis=1), jnp.concatenate([v1, ones1], axis=1)], axis=0)
        ps = []
        for h in range(2):
            s = lax.dot_general(qh[h][r0:, :], k, (((1,), (1,)), ((), ())), preferred_element_type=F32)
            if fox:
                s = (s + cq[h][r0:, :]) - ck_ref[0, 0, h:h + 1, pl.ds(start, tk)] * LOG2E
            if masked:
                s = _causal_mask(s)
            ps.append(jnp.exp2(s).astype(BF16))
        acc_scr[r0:, :] += jnp.dot(jnp.concatenate(ps, axis=1), v2, preferred_element_type=F32)

    def body(j, carry):
        step(pl.multiple_of(j * tk, tk), 0, False)
        return carry

    lax.fori_loop(0, qi * (tq // tk), body, 0)
    for d in range(tq // tk):
        step(pl.multiple_of(qi * tq + d * tk, tk), d * tk, True)
    acc = acc_scr[...]
    half0 = lax.broadcasted_iota(jnp.int32, (tq, LANES), 1) < HEAD_DIM
    o = jnp.where(half0, acc[:, 0:LANES] / acc[:, LANES:LANES + 1], acc[:, 0:LANES] / acc[:, LANES + 1:LANES + 2])
    o_ref[0] = o.astype(o_ref.dtype)


def _dense_online_kernel(*refs, tq, tk, fox):
    if fox:
        q_ref, k_ref, v_ref, cq_ref, ck_ref, o_ref, m_scr, l_scr, acc_scr = refs
    else:
        q_ref, k_ref, v_ref, o_ref, m_scr, l_scr, acc_scr = refs
    qi = pl.program_id(2)
    qh = _split_heads(q_ref[0])
    m_scr[...] = jnp.full(m_scr.shape, NEG_INF, F32)
    l_scr[...] = jnp.zeros_like(l_scr)
    acc_scr[...] = jnp.zeros_like(acc_scr)
    if fox:
        cq = [_row_cum(cq_ref, h) for h in range(2)]

    def step(start, r0, masked):
        k = k_ref[0, pl.ds(start, tk), :]
        v = v_ref[0, pl.ds(start, tk), :]
        for h in range(2):
            s = lax.dot_general(qh[h][r0:, :], k, (((1,), (1,)), ((), ())), preferred_element_type=F32)
            if fox:
                s = (s + cq[h][r0:, :]) - ck_ref[0, 0, h:h + 1, pl.ds(start, tk)] * LOG2E
            if masked:
                s = _causal_mask(s)
            m_prev = m_scr[h, r0:, :]
            m_next = jnp.maximum(m_prev, jnp.max(s, axis=1, keepdims=True))
            p = jnp.exp2(s - jnp.concatenate([m_next] * (tk // LANES), axis=1))
            alpha = jnp.exp2(m_prev - m_next)
            l_scr[h, r0:, :] = alpha * l_scr[h, r0:, :] + jnp.sum(p, axis=1, keepdims=True)
            m_scr[h, r0:, :] = m_next
            pv = jnp.dot(p.astype(BF16), v, preferred_element_type=F32)
            acc_scr[h, r0:, :] = acc_scr[h, r0:, :] * alpha + pv

    def body(j, carry):
        step(pl.multiple_of(j * tk, tk), 0, False)
        return carry

    lax.fori_loop(0, qi * (tq // tk), body, 0)
    for d in range(tq // tk):
        step(pl.multiple_of(qi * tq + d * tk, tk), d * tk, True)
    half0 = lax.broadcasted_iota(jnp.int32, (tq, LANES), 1) < HEAD_DIM
    o = jnp.where(half0, acc_scr[0] / l_scr[0], acc_scr[1] / l_scr[1])
    o_ref[0] = o.astype(o_ref.dtype)


def _dense_attn(qkv3, cum3, cum8, jq, jk, jv, logit_bound, tq=2048, tk=512):
    B, S, _ = qkv3.shape
    tq = min(tq, S)
    assert S % tq == 0 and tq % tk == 0
    fox = cum8 is not None
    in_specs = [pl.BlockSpec((1, tq, LANES), lambda b, p, i: (b, i, 2 * jq + p)),
                pl.BlockSpec((1, S, LANES), lambda b, p, i: (b, 0, 2 * jk + p)),
                pl.BlockSpec((1, S, LANES), lambda b, p, i: (b, 0, 2 * jv + p))]
    args = [qkv3, qkv3, qkv3]
    if fox:
        in_specs.append(pl.BlockSpec((1, tq, LANES), lambda b, p, i: (b, i, 0)))
        in_specs.append(pl.BlockSpec((1, 1, 8, S), lambda b, p, i: (b, p, 0, 0)))
        args += [cum3, cum8]
    common = dict(
        grid=(B, 2, S // tq),
        in_specs=in_specs,
        out_specs=pl.BlockSpec((1, tq, LANES), lambda b, p, i: (b, i, p)),
        out_shape=jax.ShapeDtypeStruct((B, S, GROUP_W), BF16),
        compiler_params=_cparams(("parallel", "parallel", "arbitrary")),
    )
    name = "fox_attn" if fox else "mla_attn"
    bounded = pl.pallas_call(functools.partial(_dense_bounded_kernel, tq=tq, tk=tk, fox=fox), name=name + "_bounded",
                             scratch_shapes=[pltpu.VMEM((tq, 2 * LANES), F32)], **common)
    online = pl.pallas_call(functools.partial(_dense_online_kernel, tq=tq, tk=tk, fox=fox), name=name + "_online",
                            scratch_shapes=[pltpu.VMEM((2, tq, LANES), F32)] * 3, **common)
    return lax.cond(logit_bound <= EXP2_SAFE, lambda *a: bounded(*a), lambda *a: online(*a), *args)


def _fold_rows(r, n, dil):
    return slice(r, r + n) if dil == 1 else pl.ds(r, n, stride=dil)


def _banded_kernel(*refs, dil, maxdist, slopes, kv_chunks, has_sink, want_lse):
    refs = list(refs)
    sink_ref = refs.pop(0) if has_sink else None
    q_refs = [refs.pop(0) for _ in range(2)]
    kv_refs = [[refs.pop(0) for _ in range(4)] for _ in range(kv_chunks)]
    o_ref = refs.pop(0)
    lse_ref = refs.pop(0) if want_lse else None
    o_scr = refs.pop(0)
    lse_scr = refs.pop(0) if want_lse else None
    nq = q_refs[0].shape[1] // dil
    row = lax.broadcasted_iota(jnp.int32, (BAND, 2 * BAND), 0)
    col = lax.broadcasted_iota(jnp.int32, (BAND, 2 * BAND), 1)
    dist = row + BAND - col
    valid = (dist >= 0) & (dist <= maxdist)
    distf = dist.astype(F32) * float(dil)
    bias = [jnp.where(valid, -slopes[h] * distf, NEG_INF) for h in range(HEADS)]
    no_prev = jnp.where(col < BAND, jnp.where(pl.program_id(1) == 0, NEG_INF, 0.0), 0.0)
    bias_first = [b + no_prev for b in bias]
    lane_k = lax.broadcasted_iota(jnp.int32, (2 * BAND, LANES), 1)
    ones0 = jnp.where(lane_k == 0, 1.0, 0.0).astype(BF16)
    ones1 = jnp.where(lane_k == 1, 1.0, 0.0).astype(BF16)
    half0 = lax.broadcasted_iota(jnp.int32, (BAND, LANES), 1) < HEAD_DIM
    for c in range(2):
        kp_ref, kc_ref, vp_ref, vc_ref = kv_refs[c if kv_chunks == 2 else 0]
        for r in range(dil):
            prev_rows, cur_rows = _fold_rows(r, BAND, dil), _fold_rows(r, nq, dil)
            qh = _split_heads(q_refs[c][0, cur_rows, :].astype(BF16))
            kf = jnp.concatenate([kp_ref[0, prev_rows, :], kc_ref[0, cur_rows, :]], axis=0).astype(BF16)
            vh = _split_heads(jnp.concatenate([vp_ref[0, prev_rows, :], vc_ref[0, cur_rows, :]], axis=0).astype(BF16))
            outs, lses = [], []
            for j in range(nq // BAND):
                keys = slice(j * BAND, (j + 2) * BAND)
                v2 = jnp.concatenate([jnp.concatenate([vh[0][keys], ones0], axis=1),
                                      jnp.concatenate([vh[1][keys], ones1], axis=1)], axis=0)
                ps, ms = [], []
                for e in range(2):
                    s = lax.dot_general(qh[e][j * BAND:(j + 1) * BAND], kf[keys], (((1,), (1,)), ((), ())),
                                        preferred_element_type=F32)
                    s = s + (bias_first if j == 0 else bias)[2 * c + e]
                    m = jnp.max(s, axis=1, keepdims=True)
                    ps.append(jnp.exp(s - m).astype(BF16))
                    ms.append(m)
                acc = jnp.dot(jnp.concatenate(ps, axis=1), v2, preferred_element_type=F32)
                scale, lse = [], []
                for e in range(2):
                    denom = acc[:, LANES + e:LANES + e + 1]
                    if has_sink:
                        ls = ms[e] + jnp.log(denom)
                        sk = sink_ref[2 * c + e]
                        mx = jnp.maximum(ls, sk)
                        ls = mx + jnp.log(jnp.exp(ls - mx) + jnp.exp(sk - mx))
                        scale.append(jnp.exp(ms[e] - ls))
                    else:
                        scale.append(1.0 / denom)
                    if want_lse:
                        lse.append(ms[e] + jnp.log(denom))
                outs.append(jnp.where(half0, acc[:, 0:LANES] * scale[0], acc[:, 0:LANES] * scale[1]))
                if want_lse:
                    lses.append(jnp.where(half0, lse[0], lse[1]))
            o_scr[cur_rows, :] = jnp.concatenate(outs, axis=0)
            if want_lse:
                lse_scr[cur_rows, :] = jnp.concatenate(lses, axis=0)
        o_ref[0, :, c * LANES:(c + 1) * LANES] = o_scr[...]
        if want_lse:
            lse_ref[0, :, c * LANES:(c + 1) * LANES] = lse_scr[...]


def _banded(q_arr, jq, kv_arr, jk, jv, kw, dil, maxdist, slopes, sink, want_lse):
    B, S, _ = q_arr.shape
    tb = min(BAND_TOKENS, S)
    pb = BAND * dil
    assert S % tb == 0 and tb % pb == 0 and maxdist <= BAND
    ratio = tb // pb
    has_sink = sink is not None
    kern = functools.partial(_banded_kernel, dil=dil, maxdist=maxdist, slopes=tuple(slopes),
                             kv_chunks=kw // LANES, has_sink=has_sink, want_lse=want_lse)
    prev = lambda j: pl.BlockSpec((1, pb, LANES), lambda b, i: (b, jnp.maximum(i * ratio - 1, 0), j))
    cur = lambda j: pl.BlockSpec((1, tb, LANES), lambda b, i: (b, i, j))
    in_specs = [cur(2 * jq), cur(2 * jq + 1)]
    args = [q_arr, q_arr]
    for c in range(kw // LANES):
        jkc, jvc = jk * (kw // LANES) + c, jv * (kw // LANES) + c
        in_specs += [prev(jkc), cur(jkc), prev(jvc), cur(jvc)]
        args += [kv_arr] * 4
    if has_sink:
        in_specs.insert(0, pl.BlockSpec(memory_space=pltpu.SMEM))
        args.insert(0, sink)
    n_out = 2 if want_lse else 1
    outs = pl.pallas_call(
        kern,
        grid=(B, S // tb),
        in_specs=in_specs,
        out_specs=[pl.BlockSpec((1, tb, 256), lambda b, i: (b, i, 0))] * n_out,
        out_shape=[jax.ShapeDtypeStruct((B, S, 256), F32)] * n_out,
        scratch_shapes=[pltpu.VMEM((tb, LANES), F32)] * n_out,
        compiler_params=_cparams(("parallel", "parallel")),
        name=f"banded_d{dil}" if want_lse else "swa_attn",
    )(*args)
    return [o.reshape(B * S, 256) for o in outs]


def _wo_router_kernel(x_ref, oa_ref, ob0_ref, ob1_ref, ob2_ref, l0_ref, l1_ref, l2_ref, oc_ref, od_ref,
                      wo_ref, g2_ref, wr_ref, br_ref, ltri_ref, h_ref, xn_ref, route_ref, route_t_ref, cnt_ref,
                      carry_ref):
    i = pl.program_id(0)
    tm = x_ref.shape[0]

    @pl.when(i == 0)
    def _():
        carry_ref[...] = jnp.zeros_like(carry_ref)

    la, lb, lc = l0_ref[...], l1_ref[...], l2_ref[...]
    mx = jnp.maximum(jnp.maximum(la, lb), lc)
    ea, eb, ec = jnp.exp(la - mx), jnp.exp(lb - mx), jnp.exp(lc - mx)
    ob = (ea * ob0_ref[...] + eb * ob1_ref[...] + ec * ob2_ref[...]) / (ea + eb + ec)
    mix = jnp.concatenate([oa_ref[...], ob.astype(BF16), oc_ref[...], od_ref[...].astype(BF16)], axis=1)
    h = x_ref[...] + jnp.dot(mix, wo_ref[...], preferred_element_type=F32)
    h_ref[...] = h
    xn = h * lax.rsqrt(jnp.mean(h * h, axis=-1, keepdims=True) + NORM_EPS) * g2_ref[...]
    xn_ref[...] = xn

    z = jnp.dot(xn, wr_ref[...], preferred_element_type=F32, precision=lax.Precision.HIGHEST) + br_ref[...]
    lane = lax.broadcasted_iota(jnp.int32, (tm, LANES), 1)
    lanef = lane.astype(F32)
    big = float(LANES)
    zg = jnp.where((lane >= N_EXPERTS) & (lane < N_EXPERTS + N_GROUPS), z, NEG_INF)
    mg = jnp.max(zg, axis=1, keepdims=True)
    p_g = 1.0 / jnp.sum(jnp.exp(zg - mg), axis=1, keepdims=True)
    gsel = jnp.min(jnp.where(zg == mg, lanef, big), axis=1, keepdims=True) - float(N_EXPERTS)
    lo = gsel * float(EXPERTS_PER_GROUP)
    ze = jnp.where((lanef >= lo) & (lanef < lo + float(EXPERTS_PER_GROUP)), z, NEG_INF)
    m1 = jnp.max(ze, axis=1, keepdims=True)
    i1 = jnp.min(jnp.where(ze == m1, lanef, big), axis=1, keepdims=True)
    ze2 = jnp.where(lanef == i1, NEG_INF, ze)
    m2 = jnp.max(ze2, axis=1, keepdims=True)
    i2 = jnp.min(jnp.where(ze2 == m2, lanef, big), axis=1, keepdims=True)
    e2 = jnp.exp(m2 - m1)
    gate1 = p_g / (1.0 + e2)
    gate2 = p_g * e2 / (1.0 + e2)

    oh1 = jnp.where(lanef == i1, 1.0, 0.0)
    oh2 = jnp.where(lanef == i2, 1.0, 0.0)
    oh = oh1 + oh2
    before = carry_ref[0:1, :] + jnp.dot(ltri_ref[...], oh.astype(BF16), preferred_element_type=F32)
    r1 = jnp.sum(before * oh1, axis=1, keepdims=True)
    r2 = jnp.sum(before * oh2, axis=1, keepdims=True)
    total = carry_ref[0:1, :] + jnp.sum(oh, axis=0, keepdims=True)
    carry_ref[...] = jnp.broadcast_to(total, carry_ref.shape)
    cnt_ref[...] = jnp.broadcast_to(total, cnt_ref.shape)

    route = jnp.where(lane == 0, i1, jnp.where(lane == 1, i2, jnp.where(lane == 2, r1, jnp.where(
        lane == 3, r2, jnp.where(lane == 4, gate1, jnp.where(lane == 5, gate2, 0.0))))))
    route_ref[...] = route
    route_t_ref[...] = jnp.transpose(route)[0:8, :]


def _wo_router(xt, oa, obs, lses, oc, od, lw, ltri, tm=512):
    T = xt.shape[0]
    row = lambda w: pl.BlockSpec((tm, w), lambda i: (i, 0))
    return pl.pallas_call(
        _wo_router_kernel,
        grid=(T // tm,),
        in_specs=[row(D_MODEL), row(256), row(256), row(256), row(256), row(256), row(256), row(256), row(256),
                  row(256), _full((D_MODEL, D_MODEL)), _full((1, D_MODEL)), _full((D_MODEL, LANES)),
                  _full((1, LANES)), _full((tm, tm))],
        out_specs=[row(D_MODEL), row(D_MODEL), row(LANES), pl.BlockSpec((8, tm), lambda i: (0, i)), _full((8, LANES))],
        out_shape=[jax.ShapeDtypeStruct((T, D_MODEL), F32), jax.ShapeDtypeStruct((T, D_MODEL), F32),
                   jax.ShapeDtypeStruct((T, LANES), F32), jax.ShapeDtypeStruct((8, T), F32),
                   jax.ShapeDtypeStruct((8, LANES), F32)],
        scratch_shapes=[pltpu.VMEM((8, LANES), F32)],
        compiler_params=_cparams(("arbitrary",)),
        name="wo_router",
    )(xt, oa, obs[0], obs[1], obs[2], lses[0], lses[1], lses[2], oc, od, lw["wo"], lw["g2"], lw["wr"], lw["br"], ltri)


def _row_copy(src, src_row, dst, dst_row, sem):
    return pltpu.make_async_copy(src.at[pl.ds(src_row, 1)], dst.at[pl.ds(dst_row, 1)], sem)


def _tile_major(dest_t, tile):
    k, T = dest_t.shape
    return dest_t.reshape(k, T // tile, tile).transpose(1, 0, 2).reshape(-1)


def _scatter_kernel(dest_ref, x_ref, buf_in, buf_out, sem, *, ts):
    del buf_in

    def issue(t, carry):
        for k in range(TOP_K):
            _row_copy(x_ref, t, buf_out, dest_ref[k * ts + t], sem).start()
        return carry

    lax.fori_loop(0, ts, issue, 0, unroll=8)
    for k in range(TOP_K):
        pltpu.make_async_copy(x_ref, buf_out.at[pl.ds(0, ts)], sem).wait()


def _scatter_rows(xn, dest_t, n_rows, ts=1024):
    T, W = xn.shape
    dest_flat = _tile_major(dest_t, ts)
    return pl.pallas_call(
        functools.partial(_scatter_kernel, ts=ts),
        grid=(T // ts,),
        in_specs=[pl.BlockSpec((TOP_K * ts,), lambda i: (i,), memory_space=pltpu.SMEM),
                  pl.BlockSpec((ts, W), lambda i: (i, 0)), pl.BlockSpec(memory_space=pl.ANY)],
        out_specs=pl.BlockSpec(memory_space=pl.ANY),
        out_shape=jax.ShapeDtypeStruct((n_rows, W), xn.dtype),
        scratch_shapes=[pltpu.SemaphoreType.DMA(())],
        input_output_aliases={2: 0},
        compiler_params=_cparams(("arbitrary",)),
        name="moe_scatter",
    )(dest_flat, xn, jnp.zeros((n_rows, W), xn.dtype))


def _expert_kernel(blk_e_ref, nused_ref, x_ref, w1_ref, w3_ref, w2_ref, y_ref):
    del blk_e_ref
    i = pl.program_id(0)

    @pl.when(i < nused_ref[0])
    def _():
        x = x_ref[...].astype(BF16)
        a = jnp.dot(x, w1_ref[0, 0].astype(BF16), preferred_element_type=F32)
        b = jnp.dot(x, w3_ref[0, 0].astype(BF16), preferred_element_type=F32)
        hid = (a / (1.0 + jnp.exp(-a))) * b
        y_ref[...] = jnp.dot(hid.astype(BF16), w2_ref[0, 0].astype(BF16), preferred_element_type=F32)

    @pl.when(i >= nused_ref[0])
    def _():
        y_ref[...] = jnp.zeros_like(y_ref)


def _expert_mlp(xbuf, blk_e, nused, layer, w1, w3, w2):
    P, D = xbuf.shape
    W = D
    nblk = P // ROW_BLOCK
    x_map = lambda i, be, nu: (jnp.maximum(jnp.minimum(i, nu[0] - 1), 0), 0)
    w_map = lambda i, be, nu: (layer, be[i], 0, 0)
    grid_spec = pltpu.PrefetchScalarGridSpec(
        num_scalar_prefetch=2,
        grid=(nblk,),
        in_specs=[pl.BlockSpec((ROW_BLOCK, W), x_map),
                  pl.BlockSpec((1, 1, D, D_EXPERT), w_map),
                  pl.BlockSpec((1, 1, D, D_EXPERT), w_map),
                  pl.BlockSpec((1, 1, D_EXPERT, D), w_map)],
        out_specs=pl.BlockSpec((ROW_BLOCK, W), lambda i, be, nu: (i, 0)),
    )
    return pl.pallas_call(
        _expert_kernel,
        grid_spec=grid_spec,
        out_shape=jax.ShapeDtypeStruct((P, W), xbuf.dtype),
        compiler_params=_cparams(("arbitrary",)),
        name="moe_experts",
    )(blk_e, nused, xbuf, w1, w3, w2)


def _combine_kernel(dest_ref, dest_next_ref, h_ref, route_ref, y_hbm, o_ref, rows_scr, sems, *, tc):
    i = pl.program_id(0)
    n = pl.num_programs(0)
    slot = i % 2

    def gather(d_ref, s):
        def issue(t, carry):
            for k in range(TOP_K):
                _row_copy(y_hbm, d_ref[k * tc + t], rows_scr.at[s, k], t, sems.at[s]).start()
            return carry

        lax.fori_loop(0, tc, issue, 0, unroll=8)

    @pl.when(i == 0)
    def _():
        gather(dest_ref, 0)

    @pl.when(i + 1 < n)
    def _():
        gather(dest_next_ref, 1 - slot)

    for k in range(TOP_K):
        pltpu.make_async_copy(y_hbm.at[pl.ds(0, tc)], rows_scr.at[slot, k], sems.at[slot]).wait()
    route = route_ref[...]
    o_ref[...] = h_ref[...] + route[:, 4:5] * rows_scr[slot, 0] + route[:, 5:6] * rows_scr[slot, 1]


def _combine(h, route, ybuf, dest_t, tc=256):
    T, D = h.shape
    n = T // tc
    dest_flat = _tile_major(dest_t, tc)
    return pl.pallas_call(
        functools.partial(_combine_kernel, tc=tc),
        grid=(n,),
        in_specs=[pl.BlockSpec((TOP_K * tc,), lambda i: (i,), memory_space=pltpu.SMEM),
                  pl.BlockSpec((TOP_K * tc,), lambda i: (jnp.minimum(i + 1, n - 1),), memory_space=pltpu.SMEM),
                  pl.BlockSpec((tc, D), lambda i: (i, 0)), pl.BlockSpec((tc, LANES), lambda i: (i, 0)),
                  pl.BlockSpec(memory_space=pl.ANY)],
        out_specs=pl.BlockSpec((tc, D), lambda i: (i, 0)),
        out_shape=jax.ShapeDtypeStruct((T, D), F32),
        scratch_shapes=[pltpu.VMEM((2, TOP_K, tc, D), ybuf.dtype), pltpu.SemaphoreType.DMA((2,))],
        compiler_params=_cparams(("arbitrary",)),
        name="moe_combine",
    )(dest_flat, dest_flat, h, route, ybuf)


def _layer_weights(l, norm1_g, w_in, mla_gcq, mla_gckv, mla_wuq, mla_wukv, qk_gq, qk_gk, fox_bf, sink, w_o,
                   norm2_g, w_rg, b_rg, w_re, b_re, w1, w3, w2):
    offs = np.concatenate([[0], np.cumsum(IN_SPLITS)])
    w = w_in[l]
    cq, ckv, kr, pb, pc, fg, pdq, pdkv = [w[:, offs[j]:offs[j + 1]] for j in range(8)]
    z = lambda n: jnp.zeros((D_MODEL, n), F32)
    perm = np.array([0, 2, 1, 3])
    ublock = jnp.concatenate([cq, fg, z(256 - MLA_Q_RANK - HEADS)], axis=1)
    kr_rep = jnp.concatenate([z(MLA_NOPE), kr] * HEADS, axis=1)
    dq = pdq.reshape(D_MODEL, HEADS, HEAD_DIM)[:, perm].reshape(D_MODEL, GROUP_W)
    wp = jnp.concatenate([ckv, ublock, kr_rep, pb, pc, dq, pdkv], axis=1).astype(BF16)

    wuq = jnp.concatenate([mla_wuq[l], jnp.zeros((256 - MLA_Q_RANK, GROUP_W), F32)], axis=0).astype(BF16)
    wukv = mla_wukv[l].reshape(MLA_KV_RANK, HEADS, MLA_NOPE + HEAD_DIM)
    wuk = jnp.concatenate([wukv[:, :, :MLA_NOPE], jnp.zeros((MLA_KV_RANK, HEADS, MLA_ROPE), F32)], axis=2)
    wuk = wuk.reshape(MLA_KV_RANK, GROUP_W).astype(BF16)
    wuv = wukv[:, :, MLA_NOPE:].reshape(MLA_KV_RANK, GROUP_W).astype(BF16)
    gcq = jnp.concatenate([mla_gcq[l], jnp.zeros((256 - MLA_Q_RANK,), F32)])[None, :]
    scale = HEAD_DIM ** -0.5
    unit = jnp.array([LOG2E, 1.0, LOG2E, 1.0], F32)[:, None]
    gq = jnp.tile(qk_gq[l], (1, HEADS)) * (scale * unit)
    gk = jnp.tile(qk_gk[l], (1, HEADS))
    bound = HEAD_DIM * jnp.max(jnp.abs(gq), axis=1) * jnp.max(jnp.abs(gk), axis=1)
    fb = jnp.zeros((1, LANES), F32).at[0, FG_LANE:FG_LANE + HEADS].set(fox_bf[l])
    wo = w_o[l]
    wo_d = wo[3 * GROUP_W:].reshape(HEADS, HEAD_DIM, D_MODEL)[perm].reshape(GROUP_W, D_MODEL)
    wo = jnp.concatenate([wo[:3 * GROUP_W], wo_d], axis=0).astype(BF16)
    wr = jnp.concatenate([w_re[l], w_rg[l], jnp.zeros((D_MODEL, LANES - N_EXPERTS - N_GROUPS), F32)], axis=1)
    br = jnp.concatenate([b_re[l], b_rg[l], jnp.zeros((LANES - N_EXPERTS - N_GROUPS,), F32)])[None, :]
    return dict(g1=norm1_g[l][None, :], wp=wp, gcq=gcq, gckv=mla_gckv[l][None, :], wuq=wuq, wuk=wuk, wuv=wuv,
                gq=gq, gk=gk, bound=bound, fb=fb, sink=sink[l][perm], wo=wo, g2=norm2_g[l][None, :], wr=wr, br=br)


def _tables(seq_len, tm):
    half = MLA_ROPE // 2
    inv = ROPE_BASE ** (-jnp.arange(0, MLA_ROPE, 2, dtype=F32) / MLA_ROPE)
    ang = jnp.arange(seq_len, dtype=F32)[:, None] * inv[None, :]
    cos, sin = jnp.cos(ang), jnp.sin(ang)
    one = jnp.ones((seq_len, MLA_NOPE), F32)
    zn = jnp.zeros((seq_len, MLA_NOPE), F32)
    zh = jnp.zeros((seq_len, half), F32)
    rc = jnp.concatenate([one, cos, cos] * 2, axis=1)
    rsa = jnp.concatenate([zn, -sin, zh] * 2, axis=1)
    rsb = jnp.concatenate([zn, zh, sin] * 2, axis=1)
    seg = np.arange(256) // HEAD_DIM
    e = jnp.asarray(seg[:, None] == seg[None, :], BF16)
    idx = np.arange(tm)
    tril = jnp.asarray(idx[None, :] <= idx[:, None], BF16)
    ltri = jnp.asarray(idx[None, :] < idx[:, None], BF16)
    return dict(rc=rc, rsa=rsa, rsb=rsb, e=e, tril=tril, ltri=ltri)


def _alibi():
    n = 2 * HEADS
    s = [2.0 ** (-8.0 * i / n) for i in range(1, n + 1)]
    return s[HEADS:], s[:HEADS]


def kernel(x, norm1_g, w_in, mla_gcq, mla_gckv, mla_wuq, mla_wukv, qk_gq, qk_gk, fox_bf, sink, w_o,
           norm2_g, w_rg, b_rg, w_re, b_re, w1, w3, w2):
    B, S, D = x.shape
    T = B * S
    depth = w_in.shape[0]
    tm = 512
    tabs = _tables(S, tm)
    slopes_dil, slopes_swa = _alibi()
    slopes_swa_perm = [slopes_swa[0], slopes_swa[2], slopes_swa[1], slopes_swa[3]]
    n_assign = T * TOP_K
    nblk = n_assign // ROW_BLOCK + N_EXPERTS
    n_rows = nblk * ROW_BLOCK

    xt = x.reshape(T, D)
    for l in range(depth):
        lw = _layer_weights(l, norm1_g, w_in, mla_gcq, mla_gckv, mla_wuq, mla_wukv, qk_gq, qk_gk, fox_bf, sink,
                            w_o, norm2_g, w_rg, b_rg, w_re, b_re, w1, w3, w2)
        qkv, qkvb, cum = _inproj(xt, S, lw, tabs, tm=tm)
        qkv3 = qkv.reshape(B, S, QKV_COLS)
        qkvb3 = qkvb.reshape(B, S, QKVB_COLS)
        cum8 = jnp.pad(cum[:, FG_LANE:FG_LANE + HEADS].reshape(B, S, 2, 2).transpose(0, 2, 3, 1),
                       ((0, 0), (0, 0), (0, 6), (0, 0)))
        oa = _dense_attn(qkv3, None, None, QA, KA, VA, lw["bound"][0]).reshape(T, GROUP_W)
        oc = _dense_attn(qkv3, cum.reshape(B, S, LANES), cum8, QC, KC, VC, lw["bound"][2]).reshape(T, GROUP_W)
        obs, lses = [], []
        for window, dil in DILATED_PAIRS:
            o, lse = _banded(qkvb3, 0, qkvb3, 1, 2, 256, dil, window // dil, slopes_dil, None, True)
            obs.append(o)
            lses.append(lse)
        (od,) = _banded(qkv3, QD, qkv3, 2 * KVD, 2 * KVD + 1, 128, 1, SWA_WINDOW - 1, slopes_swa_perm,
                        lw["sink"], False)
        h, xn, route, route_t, cnt = _wo_router(xt, oa, obs, lses, oc, od, lw, tabs["ltri"], tm=tm)

        counts = cnt[0, :N_EXPERTS].astype(jnp.int32)
        padded = (counts + ROW_BLOCK - 1) // ROW_BLOCK * ROW_BLOCK
        pend = jnp.cumsum(padded)
        pstart = pend - padded
        expert = route_t[0:TOP_K].astype(jnp.int32)
        rank = route_t[TOP_K:2 * TOP_K].astype(jnp.int32)
        dest = pstart[expert] + rank
        blk_start = jnp.arange(nblk, dtype=jnp.int32) * ROW_BLOCK
        blk_e = jnp.minimum(jnp.sum((pend[None, :] <= blk_start[:, None]).astype(jnp.int32), axis=1), N_EXPERTS - 1)
        nused = (pend[-1:] // ROW_BLOCK).astype(jnp.int32)

        xbuf = _scatter_rows(xn, dest, n_rows)
        ybuf = _expert_mlp(xbuf, blk_e, nused, l, w1, w3, w2)
        xt = _combine(h, route, ybuf, dest)
    return xt.reshape(B, S, D)
```

```python
import functools

import numpy as np
import jax
import jax.numpy as jnp
from jax import lax
from jax.experimental import pallas as pl
from jax.experimental.pallas import tpu as pltpu

F32 = jnp.float32
BF16 = jnp.bfloat16

D_MODEL = 1024
HEAD_DIM = 64
HEADS = 4
GROUP_W = HEADS * HEAD_DIM
NORM_EPS = 1e-6
MLA_Q_RANK, MLA_KV_RANK, MLA_NOPE, MLA_ROPE = 192, 128, 32, 32
ROPE_BASE = 10000.0
DILATED_PAIRS = ((128, 1), (512, 4), (2048, 16))
SWA_WINDOW = 128
N_GROUPS, EXPERTS_PER_GROUP, N_EXPERTS, TOP_K, D_EXPERT = 4, 8, 32, 2, 256
IN_SPLITS = (MLA_Q_RANK, MLA_KV_RANK, MLA_ROPE, 3 * GROUP_W, 3 * GROUP_W, HEADS, GROUP_W, 2 * 2 * HEAD_DIM)

LANES = 128
W_COLS = 2688
QKV_COLS = 2048
QA, KA, VA, QC, KC, VC, QD, KVD = range(8)
QKVB_COLS = 768
BAND = 128
BAND_TOKENS = 2048
FG_LANE = 64
ROW_BLOCK = 512
VMEM_LIMIT = 56 * 1024 * 1024

NEG_INF = float("-inf")
LOG2E = 1.4426950408889634
EXP2_SAFE = 60.0


def _cparams(sem):
    return pltpu.CompilerParams(dimension_semantics=sem, vmem_limit_bytes=VMEM_LIMIT)


def _full(shape):
    zeros = (0,) * len(shape)
    return pl.BlockSpec(shape, lambda *_: zeros)


def _head_norm(y, g, e):
    ss = jnp.dot((y * y).astype(BF16), e, preferred_element_type=F32)
    return y * lax.rsqrt(ss * (1.0 / HEAD_DIM) + NORM_EPS) * g


def _rope(y, rc, rsa, rsb):
    outs = []
    for c in range(y.shape[1] // LANES):
        yc = y[:, c * LANES:(c + 1) * LANES]
        outs.append(yc * rc + pltpu.roll(yc, LANES - 16, 1) * rsa + pltpu.roll(yc, 16, 1) * rsb)
    return jnp.concatenate(outs, axis=1)


def _inproj_kernel(x_ref, g1_ref, w_ref, gcq_ref, gckv_ref, wuq_ref, wuk_ref, wuv_ref, gq_ref, gk_ref, e_ref,
                   rc_ref, rsa_ref, rsb_ref, fb_ref, tril_ref, qkv_ref, qkvb_ref, cum_ref, acc_ref, carry_ref,
                   *, tiles_per_seq):
    i = pl.program_id(0)
    tm = x_ref.shape[0]

    @pl.when(i % tiles_per_seq == 0)
    def _():
        carry_ref[...] = jnp.zeros_like(carry_ref)

    x = x_ref[...]
    ms = jnp.mean(x * x, axis=-1, keepdims=True)
    xn = (x * lax.rsqrt(ms + NORM_EPS) * g1_ref[...]).astype(BF16)
    acc_ref[...] = jnp.dot(xn, w_ref[...], preferred_element_type=F32)

    e = e_ref[...]
    rc, rsa, rsb = rc_ref[...], rsa_ref[...], rsb_ref[...]

    ckv = acc_ref[:, 0:128]
    ckvn = (ckv * lax.rsqrt(jnp.mean(ckv * ckv, axis=-1, keepdims=True) + NORM_EPS) * gckv_ref[...]).astype(BF16)
    u = acc_ref[:, 128:384]
    lane256 = lax.broadcasted_iota(jnp.int32, (tm, 256), 1)
    ssq = jnp.sum(jnp.where(lane256 < MLA_Q_RANK, u * u, 0.0), axis=-1, keepdims=True) * (1.0 / MLA_Q_RANK)
    un = (u * lax.rsqrt(ssq + NORM_EPS) * gcq_ref[...]).astype(BF16)
    qa = jnp.dot(un, wuq_ref[...], preferred_element_type=F32)
    ka = jnp.dot(ckvn, wuk_ref[...], preferred_element_type=F32) + acc_ref[:, 384:640]
    va = jnp.dot(ckvn, wuv_ref[...], preferred_element_type=F32)
    qa = _rope(_head_norm(qa, gq_ref[0:1, :], e), rc, rsa, rsb)
    ka = _rope(_head_norm(ka, gk_ref[0:1, :], e), rc, rsa, rsb)
    qkv_ref[:, QA * 256:(QA + 1) * 256] = qa.astype(BF16)
    qkv_ref[:, KA * 256:(KA + 1) * 256] = ka.astype(BF16)
    qkv_ref[:, VA * 256:(VA + 1) * 256] = va.astype(BF16)

    qkvb_ref[:, 0:256] = _head_norm(acc_ref[:, 640:896], gq_ref[1:2, :], e)
    qkvb_ref[:, 256:512] = _head_norm(acc_ref[:, 896:1152], gk_ref[1:2, :], e)
    qkvb_ref[:, 512:768] = acc_ref[:, 1152:1408]
    qkv_ref[:, QC * 256:(QC + 1) * 256] = _head_norm(acc_ref[:, 1408:1664], gq_ref[2:3, :], e).astype(BF16)
    qkv_ref[:, KC * 256:(KC + 1) * 256] = _head_norm(acc_ref[:, 1664:1920], gk_ref[2:3, :], e).astype(BF16)
    qkv_ref[:, VC * 256:(VC + 1) * 256] = acc_ref[:, 1920:2176].astype(BF16)

    qd = _head_norm(acc_ref[:, 2176:2432], gq_ref[3:4, :], e)
    kd = _head_norm(acc_ref[:, 2432:2560], gk_ref[3:4, 0:128], e[0:128, 0:128])
    qkv_ref[:, QD * 256:(QD + 1) * 256] = qd.astype(BF16)
    qkv_ref[:, KVD * 256:KVD * 256 + 128] = kd.astype(BF16)
    qkv_ref[:, KVD * 256 + 128:(KVD + 1) * 256] = acc_ref[:, 2560:2688].astype(BF16)

    z = u[:, 128:256] + fb_ref[...]
    ls = jnp.minimum(z, 0.0) - jnp.log(1.0 + jnp.exp(-jnp.abs(z)))
    lane128 = lax.broadcasted_iota(jnp.int32, (tm, LANES), 1)
    ls = jnp.where((lane128 >= FG_LANE) & (lane128 < FG_LANE + HEADS), ls, 0.0)
    hi = ls.astype(BF16)
    r1 = ls - hi.astype(F32)
    mid = r1.astype(BF16)
    lo = (r1 - mid.astype(F32)).astype(BF16)
    tril = tril_ref[...]
    local = (jnp.dot(tril, hi, preferred_element_type=F32) + jnp.dot(tril, mid, preferred_element_type=F32)
             + jnp.dot(tril, lo, preferred_element_type=F32))
    cum = local + carry_ref[0:1, :]
    cum_ref[...] = cum
    carry_ref[...] = jnp.broadcast_to(cum[tm - 1:tm, :], carry_ref.shape)


def _inproj(xt, seq_len, lw, tabs, tm=512):
    T = xt.shape[0]
    tps = seq_len // tm
    kern = functools.partial(_inproj_kernel, tiles_per_seq=tps)
    tab_spec = pl.BlockSpec((tm, LANES), lambda i: (i % tps, 0))
    return pl.pallas_call(
        kern,
        grid=(T // tm,),
        in_specs=[pl.BlockSpec((tm, D_MODEL), lambda i: (i, 0)),
                  _full((1, D_MODEL)), _full((D_MODEL, W_COLS)), _full((1, 256)), _full((1, 128)),
                  _full((256, 256)), _full((128, 256)), _full((128, 256)), _full((4, 256)), _full((4, 256)),
                  _full((256, 256)), tab_spec, tab_spec, tab_spec, _full((1, LANES)), _full((tm, tm))],
        out_specs=[pl.BlockSpec((tm, QKV_COLS), lambda i: (i, 0)), pl.BlockSpec((tm, QKVB_COLS), lambda i: (i, 0)),
                   pl.BlockSpec((tm, LANES), lambda i: (i, 0))],
        out_shape=[jax.ShapeDtypeStruct((T, QKV_COLS), BF16), jax.ShapeDtypeStruct((T, QKVB_COLS), F32),
                   jax.ShapeDtypeStruct((T, LANES), F32)],
        scratch_shapes=[pltpu.VMEM((tm, W_COLS), F32), pltpu.VMEM((8, LANES), F32)],
        compiler_params=_cparams(("arbitrary",)),
        name="inproj",
    )(xt, lw["g1"], lw["wp"], lw["gcq"], lw["gckv"], lw["wuq"], lw["wuk"], lw["wuv"], lw["gq"], lw["gk"],
      tabs["e"], tabs["rc"], tabs["rsa"], tabs["rsb"], lw["fb"], tabs["tril"])


def _split_heads(x):
    half0 = lax.broadcasted_iota(jnp.int32, x.shape, 1) < HEAD_DIM
    zero = jnp.zeros_like(x)
    return jnp.where(half0, x, zero), jnp.where(half0, zero, x)


def _causal_mask(s):
    row = lax.broadcasted_iota(jnp.int32, s.shape, 0)
    col = lax.broadcasted_iota(jnp.int32, s.shape, 1)
    return jnp.where(col <= row, s, NEG_INF)


def _row_cum(cq_ref, h):
    blk = cq_ref[0] * LOG2E
    lane = lax.broadcasted_iota(jnp.int32, blk.shape, 1)
    return jnp.sum(jnp.where(lane == FG_LANE + 2 * pl.program_id(1) + h, blk, 0.0), axis=1, keepdims=True)


def _dense_bounded_kernel(*refs, tq, tk, fox):
    if fox:
        q_ref, k_ref, v_ref, cq_ref, ck_ref, o_ref, acc_scr = refs
    else:
        q_ref, k_ref, v_ref, o_ref, acc_scr = refs
    qi = pl.program_id(2)
    qh = _split_heads(q_ref[0])
    lane_k = lax.broadcasted_iota(jnp.int32, (tk, LANES), 1)
    ones0 = jnp.where(lane_k == 0, 1.0, 0.0).astype(BF16)
    ones1 = jnp.where(lane_k == 1, 1.0, 0.0).astype(BF16)
    acc_scr[...] = jnp.zeros_like(acc_scr)
    if fox:
        cq = [_row_cum(cq_ref, h) for h in range(2)]

    def step(start, r0, masked):
        k = k_ref[0, pl.ds(start, tk), :]
        v0, v1 = _split_heads(v_ref[0, pl.ds(start, tk), :])
        v2 = jnp.concatenate([jnp.concatenate([v0, ones0], axis=1), jnp.concatenate([v1, ones1], axis=1)], axis=0)
        ps = []
        for h in range(2):
            s = lax.dot_general(qh[h][r0:, :], k, (((1,), (1,)), ((), ())), preferred_element_type=F32)
            if fox:
                s = (s + cq[h][r0:, :]) - ck_ref[0, 0, h:h + 1, pl.ds(start, tk)] * LOG2E
            if masked:
                s = _causal_mask(s)
            ps.append(jnp.exp2(s).astype(BF16))
        acc_scr[r0:, :] += jnp.dot(jnp.concatenate(ps, axis=1), v2, preferred_element_type=F32)

    def body(j, carry):
        step(pl.multiple_of(j * tk, tk), 0, False)
        return carry

    lax.fori_loop(0, qi * (tq // tk), body, 0)
    for d in range(tq // tk):
        step(pl.multiple_of(qi * tq + d * tk, tk), d * tk, True)
    acc = acc_scr[...]
    half0 = lax.broadcasted_iota(jnp.int32, (tq, LANES), 1) < HEAD_DIM
    o = jnp.where(half0, acc[:, 0:LANES] / acc[:, LANES:LANES + 1], acc[:, 0:LANES] / acc[:, LANES + 1:LANES + 2])
    o_ref[0] = o.astype(o_ref.dtype)


def _dense_online_kernel(*refs, tq, tk, fox):
    if fox:
        q_ref, k_ref, v_ref, cq_ref, ck_ref, o_ref, m_scr, l_scr, acc_scr = refs
    else:
        q_ref, k_ref, v_ref, o_ref, m_scr, l_scr, acc_scr = refs
    qi = pl.program_id(2)
    qh = _split_heads(q_ref[0])
    m_scr[...] = jnp.full(m_scr.shape, NEG_INF, F32)
    l_scr[...] = jnp.zeros_like(l_scr)
    acc_scr[...] = jnp.zeros_like(acc_scr)
    if fox:
        cq = [_row_cum(cq_ref, h) for h in range(2)]

    def step(start, r0, masked):
        k = k_ref[0, pl.ds(start, tk), :]
        v = v_ref[0, pl.ds(start, tk), :]
        for h in range(2):
            s = lax.dot_general(qh[h][r0:, :], k, (((1,), (1,)), ((), ())), preferred_element_type=F32)
            if fox:
                s = (s + cq[h][r0:, :]) - ck_ref[0, 0, h:h + 1, pl.ds(start, tk)] * LOG2E
            if masked:
                s = _causal_mask(s)
            m_prev = m_scr[h, r0:, :]
            m_next = jnp.maximum(m_prev, jnp.max(s, axis=1, keepdims=True))
            p = jnp.exp2(s - jnp.concatenate([m_next] * (tk // LANES), axis=1))
            alpha = jnp.exp2(m_prev - m_next)
            l_scr[h, r0:, :] = alpha * l_scr[h, r0:, :] + jnp.sum(p, axis=1, keepdims=True)
            m_scr[h, r0:, :] = m_next
            pv = jnp.dot(p.astype(BF16), v, preferred_element_type=F32)
            acc_scr[h, r0:, :] = acc_scr[h, r0:, :] * alpha + pv

    def body(j, carry):
        step(pl.multiple_of(j * tk, tk), 0, False)
        return carry

    lax.fori_loop(0, qi * (tq // tk), body, 0)
    for d in range(tq // tk):
        step(pl.multiple_of(qi * tq + d * tk, tk), d * tk, True)
    half0 = lax.broadcasted_iota(jnp.int32, (tq, LANES), 1) < HEAD_DIM
    o = jnp.where(half0, acc_scr[0] / l_scr[0], acc_scr[1] / l_scr[1])
    o_ref[0] = o.astype(o_ref.dtype)


def _dense_attn(qkv3, cum3, cum8, jq, jk, jv, logit_bound, tq=2048, tk=512):
    B, S, _ = qkv3.shape
    tq = min(tq, S)
    assert S % tq == 0 and tq % tk == 0
    fox = cum8 is not None
    in_specs = [pl.BlockSpec((1, tq, LANES), lambda b, p, i: (b, i, 2 * jq + p)),
                pl.BlockSpec((1, S, LANES), lambda b, p, i: (b, 0, 2 * jk + p)),
                pl.BlockSpec((1, S, LANES), lambda b, p, i: (b, 0, 2 * jv + p))]
    args = [qkv3, qkv3, qkv3]
    if fox:
        in_specs.append(pl.BlockSpec((1, tq, LANES), lambda b, p, i: (b, i, 0)))
        in_specs.append(pl.BlockSpec((1, 1, 8, S), lambda b, p, i: (b, p, 0, 0)))
        args += [cum3, cum8]
    common = dict(
        grid=(B, 2, S // tq),
        in_specs=in_specs,
        out_specs=pl.BlockSpec((1, tq, LANES), lambda b, p, i: (b, i, p)),
        out_shape=jax.ShapeDtypeStruct((B, S, GROUP_W), BF16),
        compiler_params=_cparams(("parallel", "parallel", "arbitrary")),
    )
    name = "fox_attn" if fox else "mla_attn"
    bounded = pl.pallas_call(functools.partial(_dense_bounded_kernel, tq=tq, tk=tk, fox=fox), name=name + "_bounded",
                             scratch_shapes=[pltpu.VMEM((tq, 2 * LANES), F32)], **common)
    online = pl.pallas_call(functools.partial(_dense_online_kernel, tq=tq, tk=tk, fox=fox), name=name + "_online",
                            scratch_shapes=[pltpu.VMEM((2, tq, LANES), F32)] * 3, **common)
    return lax.cond(logit_bound <= EXP2_SAFE, lambda *a: bounded(*a), lambda *a: online(*a), *args)


def _fold_rows(r, n, dil):
    return slice(r, r + n) if dil == 1 else pl.ds(r, n, stride=dil)


def _banded_kernel(*refs, dil, maxdist, slopes, kv_chunks, has_sink, want_lse):
    refs = list(refs)
    sink_ref = refs.pop(0) if has_sink else None
    q_refs = [refs.pop(0) for _ in range(2)]
    kv_refs = [[refs.pop(0) for _ in range(4)] for _ in range(kv_chunks)]
    o_ref = refs.pop(0)
    lse_ref = refs.pop(0) if want_lse else None
    o_scr = refs.pop(0)
    lse_scr = refs.pop(0) if want_lse else None
    nq = q_refs[0].shape[1] // dil
    row = lax.broadcasted_iota(jnp.int32, (BAND, 2 * BAND), 0)
    col = lax.broadcasted_iota(jnp.int32, (BAND, 2 * BAND), 1)
    dist = row + BAND - col
    valid = (dist >= 0) & (dist <= maxdist)
    distf = dist.astype(F32) * float(dil)
    bias = [jnp.where(valid, -slopes[h] * distf, NEG_INF) for h in range(HEADS)]
    no_prev = jnp.where(col < BAND, jnp.where(pl.program_id(1) == 0, NEG_INF, 0.0), 0.0)
    bias_first = [b + no_prev for b in bias]
    lane_k = lax.broadcasted_iota(jnp.int32, (2 * BAND, LANES), 1)
    ones0 = jnp.where(lane_k == 0, 1.0, 0.0).astype(BF16)
    ones1 = jnp.where(lane_k == 1, 1.0, 0.0).astype(BF16)
    half0 = lax.broadcasted_iota(jnp.int32, (BAND, LANES), 1) < HEAD_DIM
    for c in range(2):
        kp_ref, kc_ref, vp_ref, vc_ref = kv_refs[c if kv_chunks == 2 else 0]
        for r in range(dil):
            prev_rows, cur_rows = _fold_rows(r, BAND, dil), _fold_rows(r, nq, dil)
            qh = _split_heads(q_refs[c][0, cur_rows, :].astype(BF16))
            kf = jnp.concatenate([kp_ref[0, prev_rows, :], kc_ref[0, cur_rows, :]], axis=0).astype(BF16)
            vh = _split_heads(jnp.concatenate([vp_ref[0, prev_rows, :], vc_ref[0, cur_rows, :]], axis=0).astype(BF16))
            outs, lses = [], []
            for j in range(nq // BAND):
                keys = slice(j * BAND, (j + 2) * BAND)
                v2 = jnp.concatenate([jnp.concatenate([vh[0][keys], ones0], axis=1),
                                      jnp.concatenate([vh[1][keys], ones1], axis=1)], axis=0)
                ps, ms = [], []
                for e in range(2):
                    s = lax.dot_general(qh[e][j * BAND:(j + 1) * BAND], kf[keys], (((1,), (1,)), ((), ())),
                                        preferred_element_type=F32)
                    s = s + (bias_first if j == 0 else bias)[2 * c + e]
                    m = jnp.max(s, axis=1, keepdims=True)
                    ps.append(jnp.exp(s - m).astype(BF16))
                    ms.append(m)
                acc = jnp.dot(jnp.concatenate(ps, axis=1), v2, preferred_element_type=F32)
                scale, lse = [], []
                for e in range(2):
                    denom = acc[:, LANES + e:LANES + e + 1]
                    if has_sink:
                        ls = ms[e] + jnp.log(denom)
                        sk = sink_ref[2 * c + e]
                        mx = jnp.maximum(ls, sk)
                        ls = mx + jnp.log(jnp.exp(ls - mx) + jnp.exp(sk - mx))
                        scale.append(jnp.exp(ms[e] - ls))
                    else:
                        scale.append(1.0 / denom)
                    if want_lse:
                        lse.append(ms[e] + jnp.log(denom))
                outs.append(jnp.where(half0, acc[:, 0:LANES] * scale[0], acc[:, 0:LANES] * scale[1]))
                if want_lse:
                    lses.append(jnp.where(half0, lse[0], lse[1]))
            o_scr[cur_rows, :] = jnp.concatenate(outs, axis=0)
            if want_lse:
                lse_scr[cur_rows, :] = jnp.concatenate(lses, axis=0)
        o_ref[0, :, c * LANES:(c + 1) * LANES] = o_scr[...]
        if want_lse:
            lse_ref[0, :, c * LANES:(c + 1) * LANES] = lse_scr[...]


def _banded(q_arr, jq, kv_arr, jk, jv, kw, dil, maxdist, slopes, sink, want_lse):
    B, S, _ = q_arr.shape
    tb = min(BAND_TOKENS, S)
    pb = BAND * dil
    assert S % tb == 0 and tb % pb == 0 and maxdist <= BAND
    ratio = tb // pb
    has_sink = sink is not None
    kern = functools.partial(_banded_kernel, dil=dil, maxdist=maxdist, slopes=tuple(slopes),
                             kv_chunks=kw // LANES, has_sink=has_sink, want_lse=want_lse)
    prev = lambda j: pl.BlockSpec((1, pb, LANES), lambda b, i: (b, jnp.maximum(i * ratio - 1, 0), j))
    cur = lambda j: pl.BlockSpec((1, tb, LANES), lambda b, i: (b, i, j))
    in_specs = [cur(2 * jq), cur(2 * jq + 1)]
    args = [q_arr, q_arr]
    for c in range(kw // LANES):
        jkc, jvc = jk * (kw // LANES) + c, jv * (kw // LANES) + c
        in_specs += [prev(jkc), cur(jkc), prev(jvc), cur(jvc)]
        args += [kv_arr] * 4
    if has_sink:
        in_specs.insert(0, pl.BlockSpec(memory_space=pltpu.SMEM))
        args.insert(0, sink)
    n_out = 2 if want_lse else 1
    outs = pl.pallas_call(
        kern,
        grid=(B, S // tb),
        in_specs=in_specs,
        out_specs=[pl.BlockSpec((1, tb, 256), lambda b, i: (b, i, 0))] * n_out,
        out_shape=[jax.ShapeDtypeStruct((B, S, 256), F32)] * n_out,
        scratch_shapes=[pltpu.VMEM((tb, LANES), F32)] * n_out,
        compiler_params=_cparams(("parallel", "parallel")),
        name=f"banded_d{dil}" if want_lse else "swa_attn",
    )(*args)
    return [o.reshape(B * S, 256) for o in outs]


def _wo_router_kernel(x_ref, oa_ref, ob0_ref, ob1_ref, ob2_ref, l0_ref, l1_ref, l2_ref, oc_ref, od_ref,
                      wo_ref, g2_ref, wr_ref, br_ref, ltri_ref, h_ref, xn_ref, route_ref, route_t_ref, cnt_ref,
                      carry_ref):
    i = pl.program_id(0)
    tm = x_ref.shape[0]

    @pl.when(i == 0)
    def _():
        carry_ref[...] = jnp.zeros_like(carry_ref)

    la, lb, lc = l0_ref[...], l1_ref[...], l2_ref[...]
    mx = jnp.maximum(jnp.maximum(la, lb), lc)
    ea, eb, ec = jnp.exp(la - mx), jnp.exp(lb - mx), jnp.exp(lc - mx)
    ob = (ea * ob0_ref[...] + eb * ob1_ref[...] + ec * ob2_ref[...]) / (ea + eb + ec)
    mix = jnp.concatenate([oa_ref[...], ob.astype(BF16), oc_ref[...], od_ref[...].astype(BF16)], axis=1)
    h = x_ref[...] + jnp.dot(mix, wo_ref[...], preferred_element_type=F32)
    h_ref[...] = h
    xn = h * lax.rsqrt(jnp.mean(h * h, axis=-1, keepdims=True) + NORM_EPS) * g2_ref[...]
    xn_ref[...] = xn

    xh = xn.astype(BF16)
    xl = (xn - xh.astype(F32)).astype(BF16)
    z = (jnp.dot(xh, wr_ref[0], preferred_element_type=F32) + jnp.dot(xl, wr_ref[0], preferred_element_type=F32)
         + jnp.dot(xh, wr_ref[1], preferred_element_type=F32)) + br_ref[...]
    lane = lax.broadcasted_iota(jnp.int32, (tm, LANES), 1)
    lanef = lane.astype(F32)
    big = float(LANES)
    zg = jnp.where((lane >= N_EXPERTS) & (lane < N_EXPERTS + N_GROUPS), z, NEG_INF)
    mg = jnp.max(zg, axis=1, keepdims=True)
    p_g = 1.0 / jnp.sum(jnp.exp(zg - mg), axis=1, keepdims=True)
    gsel = jnp.min(jnp.where(zg == mg, lanef, big), axis=1, keepdims=True) - float(N_EXPERTS)
    lo = gsel * float(EXPERTS_PER_GROUP)
    ze = jnp.where((lanef >= lo) & (lanef < lo + float(EXPERTS_PER_GROUP)), z, NEG_INF)
    m1 = jnp.max(ze, axis=1, keepdims=True)
    i1 = jnp.min(jnp.where(ze == m1, lanef, big), axis=1, keepdims=True)
    ze2 = jnp.where(lanef == i1, NEG_INF, ze)
    m2 = jnp.max(ze2, axis=1, keepdims=True)
    i2 = jnp.min(jnp.where(ze2 == m2, lanef, big), axis=1, keepdims=True)
    e2 = jnp.exp(m2 - m1)
    gate1 = p_g / (1.0 + e2)
    gate2 = p_g * e2 / (1.0 + e2)

    oh1 = jnp.where(lanef == i1, 1.0, 0.0)
    oh2 = jnp.where(lanef == i2, 1.0, 0.0)
    oh = oh1 + oh2
    before = carry_ref[0:1, :] + jnp.dot(ltri_ref[...], oh.astype(BF16), preferred_element_type=F32)
    r1 = jnp.sum(before * oh1, axis=1, keepdims=True)
    r2 = jnp.sum(before * oh2, axis=1, keepdims=True)
    total = carry_ref[0:1, :] + jnp.sum(oh, axis=0, keepdims=True)
    carry_ref[...] = jnp.broadcast_to(total, carry_ref.shape)
    cnt_ref[...] = jnp.broadcast_to(total, cnt_ref.shape)

    route = jnp.where(lane == 0, i1, jnp.where(lane == 1, i2, jnp.where(lane == 2, r1, jnp.where(
        lane == 3, r2, jnp.where(lane == 4, gate1, jnp.where(lane == 5, gate2, 0.0))))))
    route_ref[...] = route
    route_t_ref[...] = jnp.transpose(route)[0:8, :]


def _wo_router(xt, oa, obs, lses, oc, od, lw, ltri, tm=512):
    T = xt.shape[0]
    row = lambda w: pl.BlockSpec((tm, w), lambda i: (i, 0))
    return pl.pallas_call(
        _wo_router_kernel,
        grid=(T // tm,),
        in_specs=[row(D_MODEL), row(256), row(256), row(256), row(256), row(256), row(256), row(256), row(256),
                  row(256), _full((D_MODEL, D_MODEL)), _full((1, D_MODEL)), _full((2, D_MODEL, LANES)),
                  _full((1, LANES)), _full((tm, tm))],
        out_specs=[row(D_MODEL), row(D_MODEL), row(LANES), pl.BlockSpec((8, tm), lambda i: (0, i)), _full((8, LANES))],
        out_shape=[jax.ShapeDtypeStruct((T, D_MODEL), F32), jax.ShapeDtypeStruct((T, D_MODEL), F32),
                   jax.ShapeDtypeStruct((T, LANES), F32), jax.ShapeDtypeStruct((8, T), F32),
                   jax.ShapeDtypeStruct((8, LANES), F32)],
        scratch_shapes=[pltpu.VMEM((8, LANES), F32)],
        compiler_params=_cparams(("arbitrary",)),
        name="wo_router",
    )(xt, oa, obs[0], obs[1], obs[2], lses[0], lses[1], lses[2], oc, od, lw["wo"], lw["g2"], lw["wr"], lw["br"], ltri)


def _row_copy(src, src_row, dst, dst_row, sem):
    return pltpu.make_async_copy(src.at[pl.ds(src_row, 1)], dst.at[pl.ds(dst_row, 1)], sem)


def _tile_major(dest_t, tile):
    k, T = dest_t.shape
    return dest_t.reshape(k, T // tile, tile).transpose(1, 0, 2).reshape(-1)


def _scatter_kernel(dest_ref, x_ref, buf_in, buf_out, sem, *, ts):
    del buf_in

    def issue(t, carry):
        for k in range(TOP_K):
            _row_copy(x_ref, t, buf_out, dest_ref[k * ts + t], sem).start()
        return carry

    lax.fori_loop(0, ts, issue, 0, unroll=8)
    for k in range(TOP_K):
        pltpu.make_async_copy(x_ref, buf_out.at[pl.ds(0, ts)], sem).wait()


def _scatter_rows(xn, dest_t, n_rows, ts=1024):
    T, W = xn.shape
    dest_flat = _tile_major(dest_t, ts)
    return pl.pallas_call(
        functools.partial(_scatter_kernel, ts=ts),
        grid=(T // ts,),
        in_specs=[pl.BlockSpec((TOP_K * ts,), lambda i: (i,), memory_space=pltpu.SMEM),
                  pl.BlockSpec((ts, W), lambda i: (i, 0)), pl.BlockSpec(memory_space=pl.ANY)],
        out_specs=pl.BlockSpec(memory_space=pl.ANY),
        out_shape=jax.ShapeDtypeStruct((n_rows, W), xn.dtype),
        scratch_shapes=[pltpu.SemaphoreType.DMA(())],
        input_output_aliases={2: 0},
        compiler_params=_cparams(("arbitrary",)),
        name="moe_scatter",
    )(dest_flat, xn, jnp.zeros((n_rows, W), xn.dtype))


def _expert_kernel(blk_e_ref, nused_ref, x_ref, w1_ref, w3_ref, w2_ref, y_ref):
    del blk_e_ref
    i = pl.program_id(0)

    @pl.when(i < nused_ref[0])
    def _():
        x = x_ref[...].astype(BF16)
        a = jnp.dot(x, w1_ref[0, 0].astype(BF16), preferred_element_type=F32)
        b = jnp.dot(x, w3_ref[0, 0].astype(BF16), preferred_element_type=F32)
        hid = (a / (1.0 + jnp.exp(-a))) * b
        y_ref[...] = jnp.dot(hid.astype(BF16), w2_ref[0, 0].astype(BF16), preferred_element_type=F32)

    @pl.when(i >= nused_ref[0])
    def _():
        y_ref[...] = jnp.zeros_like(y_ref)


def _expert_mlp(xbuf, blk_e, nused, layer, w1, w3, w2):
    P, D = xbuf.shape
    W = D
    nblk = P // ROW_BLOCK
    x_map = lambda i, be, nu: (jnp.maximum(jnp.minimum(i, nu[0] - 1), 0), 0)
    w_map = lambda i, be, nu: (layer, be[i], 0, 0)
    grid_spec = pltpu.PrefetchScalarGridSpec(
        num_scalar_prefetch=2,
        grid=(nblk,),
        in_specs=[pl.BlockSpec((ROW_BLOCK, W), x_map),
                  pl.BlockSpec((1, 1, D, D_EXPERT), w_map),
                  pl.BlockSpec((1, 1, D, D_EXPERT), w_map),
                  pl.BlockSpec((1, 1, D_EXPERT, D), w_map)],
        out_specs=pl.BlockSpec((ROW_BLOCK, W), lambda i, be, nu: (i, 0)),
    )
    return pl.pallas_call(
        _expert_kernel,
        grid_spec=grid_spec,
        out_shape=jax.ShapeDtypeStruct((P, W), xbuf.dtype),
        compiler_params=_cparams(("arbitrary",)),
        name="moe_experts",
    )(blk_e, nused, xbuf, w1, w3, w2)


def _combine_kernel(dest_ref, dest_next_ref, h_ref, route_ref, y_hbm, o_ref, rows_scr, sems, *, tc):
    i = pl.program_id(0)
    n = pl.num_programs(0)
    slot = i % 2

    def gather(d_ref, s):
        def issue(t, carry):
            for k in range(TOP_K):
                _row_copy(y_hbm, d_ref[k * tc + t], rows_scr.at[s, k], t, sems.at[s]).start()
            return carry

        lax.fori_loop(0, tc, issue, 0, unroll=8)

    @pl.when(i == 0)
    def _():
        gather(dest_ref, 0)

    @pl.when(i + 1 < n)
    def _():
        gather(dest_next_ref, 1 - slot)

    for k in range(TOP_K):
        pltpu.make_async_copy(y_hbm.at[pl.ds(0, tc)], rows_scr.at[slot, k], sems.at[slot]).wait()
    route = route_ref[...]
    o_ref[...] = h_ref[...] + route[:, 4:5] * rows_scr[slot, 0] + route[:, 5:6] * rows_scr[slot, 1]


def _combine(h, route, ybuf, dest_t, tc=256):
    T, D = h.shape
    n = T // tc
    dest_flat = _tile_major(dest_t, tc)
    return pl.pallas_call(
        functools.partial(_combine_kernel, tc=tc),
        grid=(n,),
        in_specs=[pl.BlockSpec((TOP_K * tc,), lambda i: (i,), memory_space=pltpu.SMEM),
                  pl.BlockSpec((TOP_K * tc,), lambda i: (jnp.minimum(i + 1, n - 1),), memory_space=pltpu.SMEM),
                  pl.BlockSpec((tc, D), lambda i: (i, 0)), pl.BlockSpec((tc, LANES), lambda i: (i, 0)),
                  pl.BlockSpec(memory_space=pl.ANY)],
        out_specs=pl.BlockSpec((tc, D), lambda i: (i, 0)),
        out_shape=jax.ShapeDtypeStruct((T, D), F32),
        scratch_shapes=[pltpu.VMEM((2, TOP_K, tc, D), ybuf.dtype), pltpu.SemaphoreType.DMA((2,))],
        compiler_params=_cparams(("arbitrary",)),
        name="moe_combine",
    )(dest_flat, dest_flat, h, route, ybuf)


def _layer_weights(l, norm1_g, w_in, mla_gcq, mla_gckv, mla_wuq, mla_wukv, qk_gq, qk_gk, fox_bf, sink, w_o,
                   norm2_g, w_rg, b_rg, w_re, b_re, w1, w3, w2):
    offs = np.concatenate([[0], np.cumsum(IN_SPLITS)])
    w = w_in[l]
    cq, ckv, kr, pb, pc, fg, pdq, pdkv = [w[:, offs[j]:offs[j + 1]] for j in range(8)]
    z = lambda n: jnp.zeros((D_MODEL, n), F32)
    perm = np.array([0, 2, 1, 3])
    ublock = jnp.concatenate([cq, fg, z(256 - MLA_Q_RANK - HEADS)], axis=1)
    kr_rep = jnp.concatenate([z(MLA_NOPE), kr] * HEADS, axis=1)
    dq = pdq.reshape(D_MODEL, HEADS, HEAD_DIM)[:, perm].reshape(D_MODEL, GROUP_W)
    wp = jnp.concatenate([ckv, ublock, kr_rep, pb, pc, dq, pdkv], axis=1).astype(BF16)

    wuq = jnp.concatenate([mla_wuq[l], jnp.zeros((256 - MLA_Q_RANK, GROUP_W), F32)], axis=0).astype(BF16)
    wukv = mla_wukv[l].reshape(MLA_KV_RANK, HEADS, MLA_NOPE + HEAD_DIM)
    wuk = jnp.concatenate([wukv[:, :, :MLA_NOPE], jnp.zeros((MLA_KV_RANK, HEADS, MLA_ROPE), F32)], axis=2)
    wuk = wuk.reshape(MLA_KV_RANK, GROUP_W).astype(BF16)
    wuv = wukv[:, :, MLA_NOPE:].reshape(MLA_KV_RANK, GROUP_W).astype(BF16)
    gcq = jnp.concatenate([mla_gcq[l], jnp.zeros((256 - MLA_Q_RANK,), F32)])[None, :]
    scale = HEAD_DIM ** -0.5
    unit = jnp.array([LOG2E, 1.0, LOG2E, 1.0], F32)[:, None]
    gq = jnp.tile(qk_gq[l], (1, HEADS)) * (scale * unit)
    gk = jnp.tile(qk_gk[l], (1, HEADS))
    bound = HEAD_DIM * jnp.max(jnp.abs(gq), axis=1) * jnp.max(jnp.abs(gk), axis=1)
    fb = jnp.zeros((1, LANES), F32).at[0, FG_LANE:FG_LANE + HEADS].set(fox_bf[l])
    wo = w_o[l]
    wo_d = wo[3 * GROUP_W:].reshape(HEADS, HEAD_DIM, D_MODEL)[perm].reshape(GROUP_W, D_MODEL)
    wo = jnp.concatenate([wo[:3 * GROUP_W], wo_d], axis=0).astype(BF16)
    wr = jnp.concatenate([w_re[l], w_rg[l], jnp.zeros((D_MODEL, LANES - N_EXPERTS - N_GROUPS), F32)], axis=1)
    wr_hi = wr.astype(BF16)
    wr = jnp.stack([wr_hi, (wr - wr_hi.astype(F32)).astype(BF16)])
    br =jnp.concatenate([b_re[l], b_rg[l], jnp.zeros((LANES - N_EXPERTS - N_GROUPS,), F32)])[None, :]
    return dict(g1=norm1_g[l][None, :], wp=wp, gcq=gcq, gckv=mla_gckv[l][None, :], wuq=wuq, wuk=wuk, wuv=wuv,
                gq=gq, gk=gk, bound=bound, fb=fb, sink=sink[l][perm], wo=wo, g2=norm2_g[l][None, :], wr=wr, br=br)


def _tables(seq_len, tm):
    half = MLA_ROPE // 2
    inv = ROPE_BASE ** (-jnp.arange(0, MLA_ROPE, 2, dtype=F32) / MLA_ROPE)
    ang = jnp.arange(seq_len, dtype=F32)[:, None] * inv[None, :]
    cos, sin = jnp.cos(ang), jnp.sin(ang)
    one = jnp.ones((seq_len, MLA_NOPE), F32)
    zn = jnp.zeros((seq_len, MLA_NOPE), F32)
    zh = jnp.zeros((seq_len, half), F32)
    rc = jnp.concatenate([one, cos, cos] * 2, axis=1)
    rsa = jnp.concatenate([zn, -sin, zh] * 2, axis=1)
    rsb = jnp.concatenate([zn, zh, sin] * 2, axis=1)
    seg = np.arange(256) // HEAD_DIM
    e = jnp.asarray(seg[:, None] == seg[None, :], BF16)
    idx = np.arange(tm)
    tril = jnp.asarray(idx[None, :] <= idx[:, None], BF16)
    ltri = jnp.asarray(idx[None, :] < idx[:, None], BF16)
    return dict(rc=rc, rsa=rsa, rsb=rsb, e=e, tril=tril, ltri=ltri)


def _alibi():
    n = 2 * HEADS
    s = [2.0 ** (-8.0 * i / n) for i in range(1, n + 1)]
    return s[HEADS:], s[:HEADS]


def kernel(x, norm1_g, w_in, mla_gcq, mla_gckv, mla_wuq, mla_wukv, qk_gq, qk_gk, fox_bf, sink, w_o,
           norm2_g, w_rg, b_rg, w_re, b_re, w1, w3, w2):
    B, S, D = x.shape
    T = B * S
    depth = w_in.shape[0]
    tm = 512
    tabs = _tables(S, tm)
    slopes_dil, slopes_swa = _alibi()
    slopes_swa_perm = [slopes_swa[0], slopes_swa[2], slopes_swa[1], slopes_swa[3]]
    n_assign = T * TOP_K
    nblk = n_assign // ROW_BLOCK + N_EXPERTS
    n_rows = nblk * ROW_BLOCK

    xt = x.reshape(T, D)
    for l in range(depth):
        lw = _layer_weights(l, norm1_g, w_in, mla_gcq, mla_gckv, mla_wuq, mla_wukv, qk_gq, qk_gk, fox_bf, sink,
                            w_o, norm2_g, w_rg, b_rg, w_re, b_re, w1, w3, w2)
        qkv, qkvb, cum = _inproj(xt, S, lw, tabs, tm=tm)
        qkv3 = qkv.reshape(B, S, QKV_COLS)
        qkvb3 = qkvb.reshape(B, S, QKVB_COLS)
        cum8 = jnp.pad(cum[:, FG_LANE:FG_LANE + HEADS].reshape(B, S, 2, 2).transpose(0, 2, 3, 1),
                       ((0, 0), (0, 0), (0, 6), (0, 0)))
        oa = _dense_attn(qkv3, None, None, QA, KA, VA, lw["bound"][0]).reshape(T, GROUP_W)
        oc = _dense_attn(qkv3, cum.reshape(B, S, LANES), cum8, QC, KC, VC, lw["bound"][2]).reshape(T, GROUP_W)
        obs, lses = [], []
        for window, dil in DILATED_PAIRS:
            o, lse = _banded(qkvb3, 0, qkvb3, 1, 2, 256, dil, window // dil, slopes_dil, None, True)
            obs.append(o)
            lses.append(lse)
        (od,) = _banded(qkv3, QD, qkv3, 2 * KVD, 2 * KVD + 1, 128, 1, SWA_WINDOW - 1, slopes_swa_perm,
                        lw["sink"], False)
        h, xn, route, route_t, cnt = _wo_router(xt, oa, obs, lses, oc, od, lw, tabs["ltri"], tm=tm)

        counts = cnt[0, :N_EXPERTS].astype(jnp.int32)
        padded = (counts + ROW_BLOCK - 1) // ROW_BLOCK * ROW_BLOCK
        pend = jnp.cumsum(padded)
        pstart = pend - padded
        expert = route_t[0:TOP_K].astype(jnp.int32)
        rank = route_t[TOP_K:2 * TOP_K].astype(jnp.int32)
        hit = expert[None] == jnp.arange(N_EXPERTS, dtype=jnp.int32)[:, None, None]
        dest = rank + jnp.sum(jnp.where(hit, pstart[:, None, None], 0), axis=0)
        blk_start = jnp.arange(nblk, dtype=jnp.int32) * ROW_BLOCK
        blk_e = jnp.minimum(jnp.sum((pend[None, :] <= blk_start[:, None]).astype(jnp.int32), axis=1), N_EXPERTS - 1)
        nused = (pend[-1:] // ROW_BLOCK).astype(jnp.int32)

        xbuf = _scatter_rows(xn, dest, n_rows)
        ybuf = _expert_mlp(xbuf, blk_e, nused, l, w1, w3, w2)
        xt = _combine(h, route, ybuf, dest)
    return xt.reshape(B, S, D)
```

```python
import functools

import numpy as np
import jax
import jax.numpy as jnp
from jax import lax
from jax.experimental import pallas as pl
from jax.experimental.pallas import tpu as pltpu

F32 = jnp.float32
BF16 = jnp.bfloat16

D_MODEL = 1024
HEAD_DIM = 64
HEADS = 4
GROUP_W = HEADS * HEAD_DIM
NORM_EPS = 1e-6
MLA_Q_RANK, MLA_KV_RANK, MLA_NOPE, MLA_ROPE = 192, 128, 32, 32
ROPE_BASE = 10000.0
DILATED_PAIRS = ((128, 1), (512, 4), (2048, 16))
SWA_WINDOW = 128
N_GROUPS, EXPERTS_PER_GROUP, N_EXPERTS, TOP_K, D_EXPERT = 4, 8, 32, 2, 256
IN_SPLITS = (MLA_Q_RANK, MLA_KV_RANK, MLA_ROPE, 3 * GROUP_W, 3 * GROUP_W, HEADS, GROUP_W, 2 * 2 * HEAD_DIM)

LANES = 128
W_COLS = 2688
QKV_COLS = 2048
QA, KA, VA, QC, KC, VC, QD, KVD = range(8)
QKVB_COLS = 768
BAND = 128
BAND_TOKENS = 2048
FG_LANE = 64
ROW_BLOCK = 512
VMEM_LIMIT = 56 * 1024 * 1024

NEG_INF = float("-inf")
LOG2E = 1.4426950408889634
EXP2_SAFE = 60.0


def _cparams(sem):
    return pltpu.CompilerParams(dimension_semantics=sem, vmem_limit_bytes=VMEM_LIMIT)


def _full(shape):
    zeros = (0,) * len(shape)
    return pl.BlockSpec(shape, lambda *_: zeros)


def _head_norm(y, g, e):
    ss = jnp.dot((y * y).astype(BF16), e, preferred_element_type=F32)
    return y * lax.rsqrt(ss * (1.0 / HEAD_DIM) + NORM_EPS) * g


def _rope(y, rc, rsa, rsb):
    outs = []
    for c in range(y.shape[1] // LANES):
        yc = y[:, c * LANES:(c + 1) * LANES]
        outs.append(yc * rc + pltpu.roll(yc, LANES - 16, 1) * rsa + pltpu.roll(yc, 16, 1) * rsb)
    return jnp.concatenate(outs, axis=1)


def _inproj_kernel(x_ref, g1_ref, w_ref, gcq_ref, gckv_ref, wuq_ref, wuk_ref, wuv_ref, gq_ref, gk_ref, e_ref,
                   rc_ref, rsa_ref, rsb_ref, fb_ref, tril_ref, qkv_ref, qkvb_ref, cum_ref, acc_ref, carry_ref,
                   *, tiles_per_seq):
    i = pl.program_id(0)
    tm = x_ref.shape[0]

    @pl.when(i % tiles_per_seq == 0)
    def _():
        carry_ref[...] = jnp.zeros_like(carry_ref)

    x = x_ref[...]
    ms = jnp.mean(x * x, axis=-1, keepdims=True)
    xn = (x * lax.rsqrt(ms + NORM_EPS) * g1_ref[...]).astype(BF16)
    acc_ref[...] = jnp.dot(xn, w_ref[...], preferred_element_type=F32)

    e = e_ref[...]
    rc, rsa, rsb = rc_ref[...], rsa_ref[...], rsb_ref[...]

    ckv = acc_ref[:, 0:128]
    ckvn = (ckv * lax.rsqrt(jnp.mean(ckv * ckv, axis=-1, keepdims=True) + NORM_EPS) * gckv_ref[...]).astype(BF16)
    u = acc_ref[:, 128:384]
    lane256 = lax.broadcasted_iota(jnp.int32, (tm, 256), 1)
    ssq = jnp.sum(jnp.where(lane256 < MLA_Q_RANK, u * u, 0.0), axis=-1, keepdims=True) * (1.0 / MLA_Q_RANK)
    un = (u * lax.rsqrt(ssq + NORM_EPS) * gcq_ref[...]).astype(BF16)
    qa = jnp.dot(un, wuq_ref[...], preferred_element_type=F32)
    ka = jnp.dot(ckvn, wuk_ref[...], preferred_element_type=F32) + acc_ref[:, 384:640]
    va = jnp.dot(ckvn, wuv_ref[...], preferred_element_type=F32)
    qa = _rope(_head_norm(qa, gq_ref[0:1, :], e), rc, rsa, rsb)
    ka = _rope(_head_norm(ka, gk_ref[0:1, :], e), rc, rsa, rsb)
    qkv_ref[:, QA * 256:(QA + 1) * 256] = qa.astype(BF16)
    qkv_ref[:, KA * 256:(KA + 1) * 256] = ka.astype(BF16)
    qkv_ref[:, VA * 256:(VA + 1) * 256] = va.astype(BF16)

    qkvb_ref[:, 0:256] = _head_norm(acc_ref[:, 640:896], gq_ref[1:2, :], e)
    qkvb_ref[:, 256:512] = _head_norm(acc_ref[:, 896:1152], gk_ref[1:2, :], e)
    qkvb_ref[:, 512:768] = acc_ref[:, 1152:1408]
    qkv_ref[:, QC * 256:(QC + 1) * 256] = _head_norm(acc_ref[:, 1408:1664], gq_ref[2:3, :], e).astype(BF16)
    qkv_ref[:, KC * 256:(KC + 1) * 256] = _head_norm(acc_ref[:, 1664:1920], gk_ref[2:3, :], e).astype(BF16)
    qkv_ref[:, VC * 256:(VC + 1) * 256] = acc_ref[:, 1920:2176].astype(BF16)

    qd = _head_norm(acc_ref[:, 2176:2432], gq_ref[3:4, :], e)
    kd = _head_norm(acc_ref[:, 2432:2560], gk_ref[3:4, 0:128], e[0:128, 0:128])
    qkv_ref[:, QD * 256:(QD + 1) * 256] = qd.astype(BF16)
    qkv_ref[:, KVD * 256:KVD * 256 + 128] = kd.astype(BF16)
    qkv_ref[:, KVD * 256 + 128:(KVD + 1) * 256] = acc_ref[:, 2560:2688].astype(BF16)

    z = u[:, 128:256] + fb_ref[...]
    ls = jnp.minimum(z, 0.0) - jnp.log(1.0 + jnp.exp(-jnp.abs(z)))
    lane128 = lax.broadcasted_iota(jnp.int32, (tm, LANES), 1)
    ls = jnp.where((lane128 >= FG_LANE) & (lane128 < FG_LANE + HEADS), ls, 0.0)
    hi = ls.astype(BF16)
    r1 = ls - hi.astype(F32)
    mid = r1.astype(BF16)
    lo = (r1 - mid.astype(F32)).astype(BF16)
    tril = tril_ref[...]
    local = (jnp.dot(tril, hi, preferred_element_type=F32) + jnp.dot(tril, mid, preferred_element_type=F32)
             + jnp.dot(tril, lo, preferred_element_type=F32))
    cum = local + carry_ref[0:1, :]
    cum_ref[...] = cum
    carry_ref[...] = jnp.broadcast_to(cum[tm - 1:tm, :], carry_ref.shape)


def _inproj(xt, seq_len, lw, tabs, tm=512):
    T = xt.shape[0]
    tps = seq_len // tm
    kern = functools.partial(_inproj_kernel, tiles_per_seq=tps)
    tab_spec = pl.BlockSpec((tm, LANES), lambda i: (i % tps, 0))
    return pl.pallas_call(
        kern,
        grid=(T // tm,),
        in_specs=[pl.BlockSpec((tm, D_MODEL), lambda i: (i, 0)),
                  _full((1, D_MODEL)), _full((D_MODEL, W_COLS)), _full((1, 256)), _full((1, 128)),
                  _full((256, 256)), _full((128, 256)), _full((128, 256)), _full((4, 256)), _full((4, 256)),
                  _full((256, 256)), tab_spec, tab_spec, tab_spec, _full((1, LANES)), _full((tm, tm))],
        out_specs=[pl.BlockSpec((tm, QKV_COLS), lambda i: (i, 0)), pl.BlockSpec((tm, QKVB_COLS), lambda i: (i, 0)),
                   pl.BlockSpec((tm, LANES), lambda i: (i, 0))],
        out_shape=[jax.ShapeDtypeStruct((T, QKV_COLS), BF16), jax.ShapeDtypeStruct((T, QKVB_COLS), F32),
                   jax.ShapeDtypeStruct((T, LANES), F32)],
        scratch_shapes=[pltpu.VMEM((tm, W_COLS), F32), pltpu.VMEM((8, LANES), F32)],
        compiler_params=_cparams(("arbitrary",)),
        name="inproj",
    )(xt, lw["g1"], lw["wp"], lw["gcq"], lw["gckv"], lw["wuq"], lw["wuk"], lw["wuv"], lw["gq"], lw["gk"],
      tabs["e"], tabs["rc"], tabs["rsa"], tabs["rsb"], lw["fb"], tabs["tril"])


def _split_heads(x):
    half0 = lax.broadcasted_iota(jnp.int32, x.shape, 1) < HEAD_DIM
    zero = jnp.zeros_like(x)
    return jnp.where(half0, x, zero), jnp.where(half0, zero, x)


def _causal_mask(s):
    row = lax.broadcasted_iota(jnp.int32, s.shape, 0)
    col = lax.broadcasted_iota(jnp.int32, s.shape, 1)
    return jnp.where(col <= row, s, NEG_INF)


def _row_cum(cq_ref, h):
    blk = cq_ref[0] * LOG2E
    lane = lax.broadcasted_iota(jnp.int32, blk.shape, 1)
    return jnp.sum(jnp.where(lane == FG_LANE + 2 * pl.program_id(1) + h, blk, 0.0), axis=1, keepdims=True)


def _dense_bounded_kernel(*refs, tq, tk, fox):
    if fox:
        q_ref, k_ref, v_ref, cq_ref, ck_ref, o_ref, acc_scr = refs
    else:
        q_ref, k_ref, v_ref, o_ref, acc_scr = refs
    qi = pl.program_id(2)
    qh = _split_heads(q_ref[0])
    lane_k = lax.broadcasted_iota(jnp.int32, (tk, LANES), 1)
    ones0 = jnp.where(lane_k == 0, 1.0, 0.0).astype(BF16)
    ones1 = jnp.where(lane_k == 1, 1.0, 0.0).astype(BF16)
    acc_scr[...] = jnp.zeros_like(acc_scr)
    if fox:
        cq = [_row_cum(cq_ref, h) for h in range(2)]

    def step(start, r0, masked):
        k = k_ref[0, pl.ds(start, tk), :]
        v0, v1 = _split_heads(v_ref[0, pl.ds(start, tk), :])
        v2 = jnp.concatenate([jnp.concatenate([v0, ones0], axis=1), jnp.concatenate([v1, ones1], axis=1)], axis=0)
        ps = []
        for h in range(2):
            s = lax.dot_general(qh[h][r0:, :], k, (((1,), (1,)), ((), ())), preferred_element_type=F32)
            if fox:
                s = (s + cq[h][r0:, :]) - ck_ref[0, 0, h:h + 1, pl.ds(start, tk)] * LOG2E
            if masked:
                s = _causal_mask(s)
            ps.append(jnp.exp2(s).astype(BF16))
        acc_scr[r0:, :] += jnp.dot(jnp.concatenate(ps, axis=1), v2, preferred_element_type=F32)

    def body(j, carry):
        step(pl.multiple_of(j * tk, tk), 0, False)
        return carry

    lax.fori_loop(0, qi * (tq // tk), body, 0)
    for d in range(tq // tk):
        step(pl.multiple_of(qi * tq + d * tk, tk), d * tk, True)
    acc = acc_scr[...]
    half0 = lax.broadcasted_iota(jnp.int32, (tq, LANES), 1) < HEAD_DIM
    o = jnp.where(half0, acc[:, 0:LANES] / acc[:, LANES:LANES + 1], acc[:, 0:LANES] / acc[:, LANES + 1:LANES + 2])
    o_ref[0] = o.astype(o_ref.dtype)


def _dense_online_kernel(*refs, tq, tk, fox):
    if fox:
        q_ref, k_ref, v_ref, cq_ref, ck_ref, o_ref, m_scr, l_scr, acc_scr = refs
    else:
        q_ref, k_ref, v_ref, o_ref, m_scr, l_scr, acc_scr = refs
    qi = pl.program_id(2)
    qh = _split_heads(q_ref[0])
    m_scr[...] = jnp.full(m_scr.shape, NEG_INF, F32)
    l_scr[...] = jnp.zeros_like(l_scr)
    acc_scr[...] = jnp.zeros_like(acc_scr)
    if fox:
        cq = [_row_cum(cq_ref, h) for h in range(2)]

    def step(start, r0, masked):
        k = k_ref[0, pl.ds(start, tk), :]
        v = v_ref[0, pl.ds(start, tk), :]
        for h in range(2):
            s = lax.dot_general(qh[h][r0:, :], k, (((1,), (1,)), ((), ())), preferred_element_type=F32)
            if fox:
                s = (s + cq[h][r0:, :]) - ck_ref[0, 0, h:h + 1, pl.ds(start, tk)] * LOG2E
            if masked:
                s = _causal_mask(s)
            m_prev = m_scr[h, r0:, :]
            m_next = jnp.maximum(m_prev, jnp.max(s, axis=1, keepdims=True))
            p = jnp.exp2(s - jnp.concatenate([m_next] * (tk // LANES), axis=1))
            alpha = jnp.exp2(m_prev - m_next)
            l_scr[h, r0:, :] = alpha * l_scr[h, r0:, :] + jnp.sum(p, axis=1, keepdims=True)
            m_scr[h, r0:, :] = m_next
            pv = jnp.dot(p.astype(BF16), v, preferred_element_type=F32)
            acc_scr[h, r0:, :] = acc_scr[h, r0:, :] * alpha + pv

    def body(j, carry):
        step(pl.multiple_of(j * tk, tk), 0, False)
        return carry

    lax.fori_loop(0, qi * (tq // tk), body, 0)
    for d in range(tq // tk):
        step(pl.multiple_of(qi * tq + d * tk, tk), d * tk, True)
    half0 = lax.broadcasted_iota(jnp.int32, (tq, LANES), 1) < HEAD_DIM
    o = jnp.where(half0, acc_scr[0] / l_scr[0], acc_scr[1] / l_scr[1])
    o_ref[0] = o.astype(o_ref.dtype)


def _dense_attn(qkv3, cum3, cum8, jq, jk, jv, logit_bound, tq=2048, tk=512):
    B, S, _ = qkv3.shape
    tq = min(tq, S)
    assert S % tq == 0 and tq % tk == 0
    fox = cum8 is not None
    in_specs = [pl.BlockSpec((1, tq, LANES), lambda b, p, i: (b, i, 2 * jq + p)),
                pl.BlockSpec((1, S, LANES), lambda b, p, i: (b, 0, 2 * jk + p)),
                pl.BlockSpec((1, S, LANES), lambda b, p, i: (b, 0, 2 * jv + p))]
    args = [qkv3, qkv3, qkv3]
    if fox:
        in_specs.append(pl.BlockSpec((1, tq, LANES), lambda b, p, i: (b, i, 0)))
        in_specs.append(pl.BlockSpec((1, 1, 8, S), lambda b, p, i: (b, p, 0, 0)))
        args += [cum3, cum8]
    common = dict(
        grid=(B, 2, S // tq),
        in_specs=in_specs,
        out_specs=pl.BlockSpec((1, tq, LANES), lambda b, p, i: (b, i, p)),
        out_shape=jax.ShapeDtypeStruct((B, S, GROUP_W), BF16),
        compiler_params=_cparams(("parallel", "parallel", "arbitrary")),
    )
    name = "fox_attn" if fox else "mla_attn"
    bounded = pl.pallas_call(functools.partial(_dense_bounded_kernel, tq=tq, tk=tk, fox=fox), name=name + "_bounded",
                             scratch_shapes=[pltpu.VMEM((tq, 2 * LANES), F32)], **common)
    online = pl.pallas_call(functools.partial(_dense_online_kernel, tq=tq, tk=tk, fox=fox), name=name + "_online",
                            scratch_shapes=[pltpu.VMEM((2, tq, LANES), F32)] * 3, **common)
    return lax.cond(logit_bound <= EXP2_SAFE, lambda *a: bounded(*a), lambda *a: online(*a), *args)


def _fold_rows(r, n, dil):
    return slice(r, r + n) if dil == 1 else pl.ds(r, n, stride=dil)


def _banded_kernel(*refs, dil, maxdist, slopes, kv_chunks, has_sink, want_lse):
    refs = list(refs)
    sink_ref = refs.pop(0) if has_sink else None
    q_refs = [refs.pop(0) for _ in range(2)]
    kv_refs = [[refs.pop(0) for _ in range(4)] for _ in range(kv_chunks)]
    o_ref = refs.pop(0)
    lse_ref = refs.pop(0) if want_lse else None
    o_scr = refs.pop(0)
    lse_scr = refs.pop(0) if want_lse else None
    nq = q_refs[0].shape[1] // dil
    row = lax.broadcasted_iota(jnp.int32, (BAND, 2 * BAND), 0)
    col = lax.broadcasted_iota(jnp.int32, (BAND, 2 * BAND), 1)
    dist = row + BAND - col
    valid = (dist >= 0) & (dist <= maxdist)
    distf = dist.astype(F32) * float(dil)
    bias = [jnp.where(valid, -slopes[h] * distf, NEG_INF) for h in range(HEADS)]
    no_prev = jnp.where(col < BAND, jnp.where(pl.program_id(1) == 0, NEG_INF, 0.0), 0.0)
    bias_first = [b + no_prev for b in bias]
    lane_k = lax.broadcasted_iota(jnp.int32, (2 * BAND, LANES), 1)
    ones0 = jnp.where(lane_k == 0, 1.0, 0.0).astype(BF16)
    ones1 = jnp.where(lane_k == 1, 1.0, 0.0).astype(BF16)
    half0 = lax.broadcasted_iota(jnp.int32, (BAND, LANES), 1) < HEAD_DIM
    for c in range(2):
        kp_ref, kc_ref, vp_ref, vc_ref = kv_refs[c if kv_chunks == 2 else 0]
        for r in range(dil):
            prev_rows, cur_rows = _fold_rows(r, BAND, dil), _fold_rows(r, nq, dil)
            qh = _split_heads(q_refs[c][0, cur_rows, :].astype(BF16))
            kf = jnp.concatenate([kp_ref[0, prev_rows, :], kc_ref[0, cur_rows, :]], axis=0).astype(BF16)
            vh = _split_heads(jnp.concatenate([vp_ref[0, prev_rows, :], vc_ref[0, cur_rows, :]], axis=0).astype(BF16))
            outs, lses = [], []
            for j in range(nq // BAND):
                keys = slice(j * BAND, (j + 2) * BAND)
                v2 = jnp.concatenate([jnp.concatenate([vh[0][keys], ones0], axis=1),
                                      jnp.concatenate([vh[1][keys], ones1], axis=1)], axis=0)
                ps, ms = [], []
                for e in range(2):
                    s = lax.dot_general(qh[e][j * BAND:(j + 1) * BAND], kf[keys], (((1,), (1,)), ((), ())),
                                        preferred_element_type=F32)
                    s = s + (bias_first if j == 0 else bias)[2 * c + e]
                    m = jnp.max(s, axis=1, keepdims=True)
                    ps.append(jnp.exp(s - m).astype(BF16))
                    ms.append(m)
                acc = jnp.dot(jnp.concatenate(ps, axis=1), v2, preferred_element_type=F32)
                scale, lse = [], []
                for e in range(2):
                    denom = acc[:, LANES + e:LANES + e + 1]
                    if has_sink:
                        ls = ms[e] + jnp.log(denom)
                        sk = sink_ref[2 * c + e]
                        mx = jnp.maximum(ls, sk)
                        ls = mx + jnp.log(jnp.exp(ls - mx) + jnp.exp(sk - mx))
                        scale.append(jnp.exp(ms[e] - ls))
                    else:
                        scale.append(1.0 / denom)
                    if want_lse:
                        lse.append(ms[e] + jnp.log(denom))
                outs.append(jnp.where(half0, acc[:, 0:LANES] * scale[0], acc[:, 0:LANES] * scale[1]))
                if want_lse:
                    lses.append(jnp.where(half0, lse[0], lse[1]))
            o_scr[cur_rows, :] = jnp.concatenate(outs, axis=0)
            if want_lse:
                lse_scr[cur_rows, :] = jnp.concatenate(lses, axis=0)
        o_ref[0, :, c * LANES:(c + 1) * LANES] = o_scr[...]
        if want_lse:
            lse_ref[0, :, c * LANES:(c + 1) * LANES] = lse_scr[...]


def _banded(q_arr, jq, kv_arr, jk, jv, kw, dil, maxdist, slopes, sink, want_lse):
    B, S, _ = q_arr.shape
    tb = min(BAND_TOKENS, S)
    pb = BAND * dil
    assert S % tb == 0 and tb % pb == 0 and maxdist <= BAND
    ratio = tb // pb
    has_sink = sink is not None
    kern = functools.partial(_banded_kernel, dil=dil, maxdist=maxdist, slopes=tuple(slopes),
                             kv_chunks=kw // LANES, has_sink=has_sink, want_lse=want_lse)
    prev = lambda j: pl.BlockSpec((1, pb, LANES), lambda b, i: (b, jnp.maximum(i * ratio - 1, 0), j))
    cur = lambda j: pl.BlockSpec((1, tb, LANES), lambda b, i: (b, i, j))
    in_specs = [cur(2 * jq), cur(2 * jq + 1)]
    args = [q_arr, q_arr]
    for c in range(kw // LANES):
        jkc, jvc = jk * (kw // LANES) + c, jv * (kw // LANES) + c
        in_specs += [prev(jkc), cur(jkc), prev(jvc), cur(jvc)]
        args += [kv_arr] * 4
    if has_sink:
        in_specs.insert(0, pl.BlockSpec(memory_space=pltpu.SMEM))
        args.insert(0, sink)
    n_out = 2 if want_lse else 1
    outs = pl.pallas_call(
        kern,
        grid=(B, S // tb),
        in_specs=in_specs,
        out_specs=[pl.BlockSpec((1, tb, 256), lambda b, i: (b, i, 0))] * n_out,
        out_shape=[jax.ShapeDtypeStruct((B, S, 256), F32)] * n_out,
        scratch_shapes=[pltpu.VMEM((tb, LANES), F32)] * n_out,
        compiler_params=_cparams(("parallel", "parallel")),
        name=f"banded_d{dil}" if want_lse else "swa_attn",
    )(*args)
    return [o.reshape(B * S, 256) for o in outs]


def _wo_router_kernel(x_ref, oa_ref, ob0_ref, ob1_ref, ob2_ref, l0_ref, l1_ref, l2_ref, oc_ref, od_ref,
                      wo_ref, g2_ref, wr_ref, br_ref, ltri_ref, h_ref, xn_ref, route_ref, route_t_ref, cnt_ref,
                      carry_ref):
    i = pl.program_id(0)
    tm = x_ref.shape[0]

    @pl.when(i == 0)
    def _():
        carry_ref[...] = jnp.zeros_like(carry_ref)

    la, lb, lc = l0_ref[...], l1_ref[...], l2_ref[...]
    mx = jnp.maximum(jnp.maximum(la, lb), lc)
    ea, eb, ec = jnp.exp(la - mx), jnp.exp(lb - mx), jnp.exp(lc - mx)
    ob = (ea * ob0_ref[...] + eb * ob1_ref[...] + ec * ob2_ref[...]) / (ea + eb + ec)
    mix = jnp.concatenate([oa_ref[...], ob.astype(BF16), oc_ref[...], od_ref[...].astype(BF16)], axis=1)
    h = x_ref[...] + jnp.dot(mix, wo_ref[...], preferred_element_type=F32)
    h_ref[...] = h
    xn = h * lax.rsqrt(jnp.mean(h * h, axis=-1, keepdims=True) + NORM_EPS) * g2_ref[...]
    xn_ref[...] = xn

    xh = xn.astype(BF16)
    xl = (xn - xh.astype(F32)).astype(BF16)
    z = (jnp.dot(xh, wr_ref[0], preferred_element_type=F32) + jnp.dot(xl, wr_ref[0], preferred_element_type=F32)
         + jnp.dot(xh, wr_ref[1], preferred_element_type=F32)) + br_ref[...]
    lane = lax.broadcasted_iota(jnp.int32, (tm, LANES), 1)
    lanef = lane.astype(F32)
    big = float(LANES)
    zg = jnp.where((lane >= N_EXPERTS) & (lane < N_EXPERTS + N_GROUPS), z, NEG_INF)
    mg = jnp.max(zg, axis=1, keepdims=True)
    p_g = 1.0 / jnp.sum(jnp.exp(zg - mg), axis=1, keepdims=True)
    gsel = jnp.min(jnp.where(zg == mg, lanef, big), axis=1, keepdims=True) - float(N_EXPERTS)
    lo = gsel * float(EXPERTS_PER_GROUP)
    ze = jnp.where((lanef >= lo) & (lanef < lo + float(EXPERTS_PER_GROUP)), z, NEG_INF)
    m1 = jnp.max(ze, axis=1, keepdims=True)
    i1 = jnp.min(jnp.where(ze == m1, lanef, big), axis=1, keepdims=True)
    ze2 = jnp.where(lanef == i1, NEG_INF, ze)
    m2 = jnp.max(ze2, axis=1, keepdims=True)
    i2 = jnp.min(jnp.where(ze2 == m2, lanef, big), axis=1, keepdims=True)
    e2 = jnp.exp(m2 - m1)
    gate1 = p_g / (1.0 + e2)
    gate2 = p_g * e2 / (1.0 + e2)

    oh1 = jnp.where(lanef == i1, 1.0, 0.0)
    oh2 = jnp.where(lanef == i2, 1.0, 0.0)
    oh = oh1 + oh2
    before = carry_ref[0:1, :] + jnp.dot(ltri_ref[...], oh.astype(BF16), preferred_element_type=F32)
    r1 = jnp.sum(before * oh1, axis=1, keepdims=True)
    r2 = jnp.sum(before * oh2, axis=1, keepdims=True)
    total = carry_ref[0:1, :] + jnp.sum(oh, axis=0, keepdims=True)
    carry_ref[...] = jnp.broadcast_to(total, carry_ref.shape)
    cnt_ref[...] = jnp.broadcast_to(total, cnt_ref.shape)

    route = jnp.where(lane == 0, i1, jnp.where(lane == 1, i2, jnp.where(lane == 2, r1, jnp.where(
        lane == 3, r2, jnp.where(lane == 4, gate1, jnp.where(lane == 5, gate2, 0.0))))))
    route_ref[...] = route
    route_t_ref[...] = jnp.transpose(route)[0:8, :]


def _wo_router(xt, oa, obs, lses, oc, od, lw, ltri, tm=512):
    T = xt.shape[0]
    row = lambda w: pl.BlockSpec((tm, w), lambda i: (i, 0))
    return pl.pallas_call(
        _wo_router_kernel,
        grid=(T // tm,),
        in_specs=[row(D_MODEL), row(256), row(256), row(256), row(256), row(256), row(256), row(256), row(256),
                  row(256), _full((D_MODEL, D_MODEL)), _full((1, D_MODEL)), _full((2, D_MODEL, LANES)),
                  _full((1, LANES)), _full((tm, tm))],
        out_specs=[row(D_MODEL), row(D_MODEL), row(LANES), pl.BlockSpec((8, tm), lambda i: (0, i)), _full((8, LANES))],
        out_shape=[jax.ShapeDtypeStruct((T, D_MODEL), F32), jax.ShapeDtypeStruct((T, D_MODEL), F32),
                   jax.ShapeDtypeStruct((T, LANES), F32), jax.ShapeDtypeStruct((8, T), F32),
                   jax.ShapeDtypeStruct((8, LANES), F32)],
        scratch_shapes=[pltpu.VMEM((8, LANES), F32)],
        compiler_params=_cparams(("arbitrary",)),
        name="wo_router",
    )(xt, oa, obs[0], obs[1], obs[2], lses[0], lses[1], lses[2], oc, od, lw["wo"], lw["g2"], lw["wr"], lw["br"], ltri)


def _row_copy(src, src_row, dst, dst_row, sem):
    return pltpu.make_async_copy(src.at[pl.ds(src_row, 1)], dst.at[pl.ds(dst_row, 1)], sem)


def _tile_major(dest_t, tile):
    k, T = dest_t.shape
    return dest_t.reshape(k, T // tile, tile).transpose(1, 0, 2).reshape(-1)


def _scatter_kernel(dest_ref, x_ref, buf_out, sem, *, ts):
    def issue(t, carry):
        for k in range(TOP_K):
            _row_copy(x_ref, t, buf_out, dest_ref[k * ts + t], sem).start()
        return carry

    lax.fori_loop(0, ts, issue, 0, unroll=8)
    for k in range(TOP_K):
        pltpu.make_async_copy(x_ref, buf_out.at[pl.ds(0, ts)], sem).wait()


def _scatter_rows(xn, dest_t, ts=1024):
    T, W = xn.shape
    dest_flat = _tile_major(dest_t, ts)
    return pl.pallas_call(
        functools.partial(_scatter_kernel, ts=ts),
        grid=(T // ts,),
        in_specs=[pl.BlockSpec((TOP_K * ts,), lambda i: (i,), memory_space=pltpu.SMEM),
                  pl.BlockSpec((ts, W), lambda i: (i, 0))],
        out_specs=pl.BlockSpec(memory_space=pl.ANY),
        out_shape=jax.ShapeDtypeStruct((TOP_K * T, W), xn.dtype),
        scratch_shapes=[pltpu.SemaphoreType.DMA(())],
        compiler_params=_cparams(("arbitrary",)),
        name="moe_scatter",
    )(dest_flat, xn)


def _expert_items(counts, n_blocks):
    n_items_max = n_blocks + N_EXPERTS
    end = jnp.cumsum(counts)
    start = end - counts
    first_b = start // ROW_BLOCK
    per_expert = jnp.where(counts > 0, (end - 1) // ROW_BLOCK - first_b + 1, 0)
    item_end = jnp.cumsum(per_expert)
    item_start = item_end - per_expert
    n_items = item_end[-1:]
    idx = jnp.minimum(jnp.arange(n_items_max, dtype=jnp.int32), n_items - 1)
    e = jnp.minimum(jnp.sum((item_end[None, :] <= idx[:, None]).astype(jnp.int32), axis=1), N_EXPERTS - 1)
    onehot = (e[:, None] == jnp.arange(N_EXPERTS, dtype=jnp.int32)[None, :]).astype(jnp.int32)
    pick = lambda table: jnp.sum(onehot * table[None, :], axis=1)
    b = pick(first_b) + idx - pick(item_start)
    lo = jnp.maximum(pick(start) - b * ROW_BLOCK, 0)
    hi = jnp.minimum(pick(end) - b * ROW_BLOCK, ROW_BLOCK)
    return e, b, lo, hi, n_items.astype(jnp.int32), start


def _expert_kernel(e_ref, b_ref, lo_ref, hi_ref, n_ref, x_ref, w1_ref, w3_ref, w2_ref, y_ref):
    del e_ref
    i = pl.program_id(0)

    @pl.when(i < n_ref[0])
    def _():
        x = x_ref[...].astype(BF16)
        a = jnp.dot(x, w1_ref[0, 0].astype(BF16), preferred_element_type=F32)
        b = jnp.dot(x, w3_ref[0, 0].astype(BF16), preferred_element_type=F32)
        hid = (a / (1.0 + jnp.exp(-a))) * b
        y = jnp.dot(hid.astype(BF16), w2_ref[0, 0].astype(BF16), preferred_element_type=F32)
        row = lax.broadcasted_iota(jnp.int32, y.shape, 0)
        y = jnp.where((row >= lo_ref[i]) & (row < hi_ref[i]), y, 0.0)
        first_visit = jnp.logical_or(i == 0, b_ref[i] != b_ref[jnp.maximum(i - 1, 0)])

        @pl.when(first_visit)
        def _():
            y_ref[...] = y

        @pl.when(jnp.logical_not(first_visit))
        def _():
            y_ref[...] += y


def _expert_mlp(xbuf, items, layer, w1, w3, w2):
    P, D = xbuf.shape
    e, b, lo, hi, n_items = items
    x_map = lambda i, e_, b_, lo_, hi_, n_: (b_[i], 0)
    w_map = lambda i, e_, b_, lo_, hi_, n_: (layer, e_[i], 0, 0)
    grid_spec = pltpu.PrefetchScalarGridSpec(
        num_scalar_prefetch=5,
        grid=(e.shape[0],),
        in_specs=[pl.BlockSpec((ROW_BLOCK, D), x_map),
                  pl.BlockSpec((1, 1, D, D_EXPERT), w_map),
                  pl.BlockSpec((1, 1, D, D_EXPERT), w_map),
                  pl.BlockSpec((1, 1, D_EXPERT, D), w_map)],
        out_specs=pl.BlockSpec((ROW_BLOCK, D), x_map),
    )
    return pl.pallas_call(
        _expert_kernel,
        grid_spec=grid_spec,
        out_shape=jax.ShapeDtypeStruct((P, D), xbuf.dtype),
        compiler_params=_cparams(("arbitrary",)),
        name="moe_experts",
    )(e, b, lo, hi, n_items, xbuf, w1, w3, w2)


def _combine_kernel(dest_ref, dest_next_ref, h_ref, route_ref, y_hbm, o_ref, rows_scr, sems, *, tc):
    i = pl.program_id(0)
    n = pl.num_programs(0)
    slot = i % 2

    def gather(d_ref, s):
        def issue(t, carry):
            for k in range(TOP_K):
                _row_copy(y_hbm, d_ref[k * tc + t], rows_scr.at[s, k], t, sems.at[s]).start()
            return carry

        lax.fori_loop(0, tc, issue, 0, unroll=8)

    @pl.when(i == 0)
    def _():
        gather(dest_ref, 0)

    @pl.when(i + 1 < n)
    def _():
        gather(dest_next_ref, 1 - slot)

    for k in range(TOP_K):
        pltpu.make_async_copy(y_hbm.at[pl.ds(0, tc)], rows_scr.at[slot, k], sems.at[slot]).wait()
    route = route_ref[...]
    o_ref[...] = h_ref[...] + route[:, 4:5] * rows_scr[slot, 0] + route[:, 5:6] * rows_scr[slot, 1]


def _combine(h, route, ybuf, dest_t, tc=256):
    T, D = h.shape
    n = T // tc
    dest_flat = _tile_major(dest_t, tc)
    return pl.pallas_call(
        functools.partial(_combine_kernel, tc=tc),
        grid=(n,),
        in_specs=[pl.BlockSpec((TOP_K * tc,), lambda i: (i,), memory_space=pltpu.SMEM),
                  pl.BlockSpec((TOP_K * tc,), lambda i: (jnp.minimum(i + 1, n - 1),), memory_space=pltpu.SMEM),
                  pl.BlockSpec((tc, D), lambda i: (i, 0)), pl.BlockSpec((tc, LANES), lambda i: (i, 0)),
                  pl.BlockSpec(memory_space=pl.ANY)],
        out_specs=pl.BlockSpec((tc, D), lambda i: (i, 0)),
        out_shape=jax.ShapeDtypeStruct((T, D), F32),
        scratch_shapes=[pltpu.VMEM((2, TOP_K, tc, D), ybuf.dtype), pltpu.SemaphoreType.DMA((2,))],
        compiler_params=_cparams(("arbitrary",)),
        name="moe_combine",
    )(dest_flat, dest_flat, h, route, ybuf)


def _prep_weights(norm1_g, w_in, mla_gcq, mla_gckv, mla_wuq, mla_wukv, qk_gq, qk_gk, fox_bf, sink, w_o,
                  norm2_g, w_rg, b_rg, w_re, b_re):
    L = w_in.shape[0]
    offs = np.concatenate([[0], np.cumsum(IN_SPLITS)])
    cq, ckv, kr, pb, pc, fg, pdq, pdkv = [w_in[:, :, offs[j]:offs[j + 1]] for j in range(8)]
    z = lambda *shape: jnp.zeros((L,) + shape, F32)
    perm = np.array([0, 2, 1, 3])
    ublock = jnp.concatenate([cq, fg, z(D_MODEL, 256 - MLA_Q_RANK - HEADS)], axis=2)
    kr_rep = jnp.concatenate([z(D_MODEL, MLA_NOPE), kr] * HEADS, axis=2)
    dq = pdq.reshape(L, D_MODEL, HEADS, HEAD_DIM)[:, :, perm].reshape(L, D_MODEL, GROUP_W)
    wp = jnp.concatenate([ckv, ublock, kr_rep, pb, pc, dq, pdkv], axis=2).astype(BF16)

    wuq = jnp.concatenate([mla_wuq, z(256 - MLA_Q_RANK, GROUP_W)], axis=1).astype(BF16)
    wukv = mla_wukv.reshape(L, MLA_KV_RANK, HEADS, MLA_NOPE + HEAD_DIM)
    wuk = jnp.concatenate([wukv[..., :MLA_NOPE], z(MLA_KV_RANK, HEADS, MLA_ROPE)], axis=3)
    wuk = wuk.reshape(L, MLA_KV_RANK, GROUP_W).astype(BF16)
    wuv = wukv[..., MLA_NOPE:].reshape(L, MLA_KV_RANK, GROUP_W).astype(BF16)
    gcq = jnp.concatenate([mla_gcq, z(256 - MLA_Q_RANK)], axis=1)[:, None, :]
    scale = HEAD_DIM ** -0.5
    unit = jnp.array([LOG2E, 1.0, LOG2E, 1.0], F32)[None, :, None]
    gq = jnp.tile(qk_gq, (1, 1, HEADS)) * (scale * unit)
    gk = jnp.tile(qk_gk, (1, 1, HEADS))
    bound = HEAD_DIM * jnp.max(jnp.abs(gq), axis=2) * jnp.max(jnp.abs(gk), axis=2)
    fb = z(1, LANES).at[:, 0, FG_LANE:FG_LANE + HEADS].set(fox_bf)
    wo_d = w_o[:, 3 * GROUP_W:].reshape(L, HEADS, HEAD_DIM, D_MODEL)[:, perm].reshape(L, GROUP_W, D_MODEL)
    wo = jnp.concatenate([w_o[:, :3 * GROUP_W], wo_d], axis=1).astype(BF16)
    wr = jnp.concatenate([w_re, w_rg, z(D_MODEL, LANES - N_EXPERTS - N_GROUPS)], axis=2)
    wr_hi = wr.astype(BF16)
    wr = jnp.stack([wr_hi, (wr - wr_hi.astype(F32)).astype(BF16)], axis=1)
    br = jnp.concatenate([b_re, b_rg, z(LANES - N_EXPERTS - N_GROUPS)], axis=1)[:, None, :]
    return dict(g1=norm1_g[:, None, :], wp=wp, gcq=gcq, gckv=mla_gckv[:, None, :], wuq=wuq, wuk=wuk, wuv=wuv,
                gq=gq, gk=gk, bound=bound, fb=fb, sink=sink[:, perm], wo=wo, g2=norm2_g[:, None, :], wr=wr, br=br)


def _tables(seq_len, tm):
    half = MLA_ROPE // 2
    inv = ROPE_BASE ** (-jnp.arange(0, MLA_ROPE, 2, dtype=F32) / MLA_ROPE)
    ang = jnp.arange(seq_len, dtype=F32)[:, None] * inv[None, :]
    cos, sin = jnp.cos(ang), jnp.sin(ang)
    one = jnp.ones((seq_len, MLA_NOPE), F32)
    zn = jnp.zeros((seq_len, MLA_NOPE), F32)
    zh = jnp.zeros((seq_len, half), F32)
    rc = jnp.concatenate([one, cos, cos] * 2, axis=1)
    rsa = jnp.concatenate([zn, -sin, zh] * 2, axis=1)
    rsb = jnp.concatenate([zn, zh, sin] * 2, axis=1)
    seg = np.arange(256) // HEAD_DIM
    e = jnp.asarray(seg[:, None] == seg[None, :], BF16)
    idx = np.arange(tm)
    tril = jnp.asarray(idx[None, :] <= idx[:, None], BF16)
    ltri = jnp.asarray(idx[None, :] < idx[:, None], BF16)
    return dict(rc=rc, rsa=rsa, rsb=rsb, e=e, tril=tril, ltri=ltri)


def _alibi():
    n = 2 * HEADS
    s = [2.0 ** (-8.0 * i / n) for i in range(1, n + 1)]
    return s[HEADS:], s[:HEADS]


def kernel(x, norm1_g, w_in, mla_gcq, mla_gckv, mla_wuq, mla_wukv, qk_gq, qk_gk, fox_bf, sink, w_o,
           norm2_g, w_rg, b_rg, w_re, b_re, w1, w3, w2):
    B, S, D = x.shape
    T = B * S
    depth = w_in.shape[0]
    tm = 512
    tabs = _tables(S, tm)
    slopes_dil, slopes_swa = _alibi()
    slopes_swa_perm = [slopes_swa[0], slopes_swa[2], slopes_swa[1], slopes_swa[3]]
    n_assign = T * TOP_K
    assert n_assign % ROW_BLOCK == 0

    all_w = _prep_weights(norm1_g, w_in, mla_gcq, mla_gckv, mla_wuq, mla_wukv, qk_gq, qk_gk, fox_bf, sink,
                          w_o, norm2_g, w_rg, b_rg, w_re, b_re)
    xt = x.reshape(T, D)
    for l in range(depth):
        lw = {name: arr[l] for name, arr in all_w.items()}
        qkv, qkvb, cum = _inproj(xt, S, lw, tabs, tm=tm)
        qkv3 = qkv.reshape(B, S, QKV_COLS)
        qkvb3 = qkvb.reshape(B, S, QKVB_COLS)
        cum8 = jnp.pad(cum[:, FG_LANE:FG_LANE + HEADS].reshape(B, S, 2, 2).transpose(0, 2, 3, 1),
                       ((0, 0), (0, 0), (0, 6), (0, 0)))
        oa = _dense_attn(qkv3, None, None, QA, KA, VA, lw["bound"][0]).reshape(T, GROUP_W)
        oc = _dense_attn(qkv3, cum.reshape(B, S, LANES), cum8, QC, KC, VC, lw["bound"][2]).reshape(T, GROUP_W)
        obs, lses = [], []
        for window, dil in DILATED_PAIRS:
            o, lse = _banded(qkvb3, 0, qkvb3, 1, 2, 256, dil, window // dil, slopes_dil, None, True)
            obs.append(o)
            lses.append(lse)
        (od,) = _banded(qkv3, QD, qkv3, 2 * KVD, 2 * KVD + 1, 128, 1, SWA_WINDOW - 1, slopes_swa_perm,
                        lw["sink"], False)
        h, xn, route, route_t, cnt = _wo_router(xt, oa, obs, lses, oc, od, lw, tabs["ltri"], tm=tm)

        counts = cnt[0, :N_EXPERTS].astype(jnp.int32)
        *items, start = _expert_items(counts, n_assign // ROW_BLOCK)
        expert = route_t[0:TOP_K].astype(jnp.int32)
        rank = route_t[TOP_K:2 * TOP_K].astype(jnp.int32)
        hit = expert[None] == jnp.arange(N_EXPERTS, dtype=jnp.int32)[:, None, None]
        dest = rank + jnp.sum(jnp.where(hit, start[:, None, None], 0), axis=0)

        xbuf = _scatter_rows(xn, dest)
        ybuf = _expert_mlp(xbuf, items, l, w1, w3, w2)
        xt = _combine(h, route, ybuf, dest)
    return xt.reshape(B, S, D)
```

```python
import functools

import numpy as np
import jax
import jax.numpy as jnp
from jax import lax
from jax.experimental import pallas as pl
from jax.experimental.pallas import tpu as pltpu

F32 = jnp.float32
BF16 = jnp.bfloat16

D_MODEL = 1024
HEAD_DIM = 64
HEADS = 4
GROUP_W = HEADS * HEAD_DIM
NORM_EPS = 1e-6
MLA_Q_RANK, MLA_KV_RANK, MLA_NOPE, MLA_ROPE = 192, 128, 32, 32
ROPE_BASE = 10000.0
DILATED_PAIRS = ((128, 1), (512, 4), (2048, 16))
SWA_WINDOW = 128
N_GROUPS, EXPERTS_PER_GROUP, N_EXPERTS, TOP_K, D_EXPERT = 4, 8, 32, 2, 256
IN_SPLITS = (MLA_Q_RANK, MLA_KV_RANK, MLA_ROPE, 3 * GROUP_W, 3 * GROUP_W, HEADS, GROUP_W, 2 * 2 * HEAD_DIM)

LANES = 128
W_COLS = 2688
QKV_COLS = 2048
QA, KA, VA, QC, KC, VC, QD, KVD = range(8)
QKVB_COLS = 768
BAND = 128
BAND_TOKENS = 2048
FG_LANE = 64
ROW_BLOCK = 512
VMEM_LIMIT = 56 * 1024 * 1024

NEG_INF = float("-inf")
LOG2E = 1.4426950408889634
LN2 = 0.6931471805599453
EXP2_SAFE = 60.0


def _cparams(sem):
    return pltpu.CompilerParams(dimension_semantics=sem, vmem_limit_bytes=VMEM_LIMIT)


def _full(shape):
    zeros = (0,) * len(shape)
    return pl.BlockSpec(shape, lambda *_: zeros)


def _head_norm(y, g, e):
    ss = jnp.dot((y * y).astype(BF16), e, preferred_element_type=F32)
    return y * lax.rsqrt(ss * (1.0 / HEAD_DIM) + NORM_EPS) * g


def _rope(y, rc, rsa, rsb):
    outs = []
    for c in range(y.shape[1] // LANES):
        yc = y[:, c * LANES:(c + 1) * LANES]
        outs.append(yc * rc + pltpu.roll(yc, LANES - 16, 1) * rsa + pltpu.roll(yc, 16, 1) * rsb)
    return jnp.concatenate(outs, axis=1)


def _inproj_kernel(x_ref, g1_ref, w_ref, gcq_ref, gckv_ref, wuq_ref, wuk_ref, wuv_ref, gq_ref, gk_ref, e_ref,
                   rc_ref, rsa_ref, rsb_ref, fb_ref, tril_ref, qkv_ref, qkvb_ref, cum_ref, acc_ref, carry_ref,
                   *, tiles_per_seq):
    i = pl.program_id(0)
    tm = x_ref.shape[0]

    @pl.when(i % tiles_per_seq == 0)
    def _():
        carry_ref[...] = jnp.zeros_like(carry_ref)

    x = x_ref[...]
    ms = jnp.mean(x * x, axis=-1, keepdims=True)
    xn = (x * lax.rsqrt(ms + NORM_EPS) * g1_ref[...]).astype(BF16)
    acc_ref[...] = jnp.dot(xn, w_ref[...], preferred_element_type=F32)

    e = e_ref[...]
    rc, rsa, rsb = rc_ref[...], rsa_ref[...], rsb_ref[...]

    ckv = acc_ref[:, 0:128]
    ckvn = (ckv * lax.rsqrt(jnp.mean(ckv * ckv, axis=-1, keepdims=True) + NORM_EPS) * gckv_ref[...]).astype(BF16)
    u = acc_ref[:, 128:384]
    lane256 = lax.broadcasted_iota(jnp.int32, (tm, 256), 1)
    ssq = jnp.sum(jnp.where(lane256 < MLA_Q_RANK, u * u, 0.0), axis=-1, keepdims=True) * (1.0 / MLA_Q_RANK)
    un = (u * lax.rsqrt(ssq + NORM_EPS) * gcq_ref[...]).astype(BF16)
    qa = jnp.dot(un, wuq_ref[...], preferred_element_type=F32)
    ka = jnp.dot(ckvn, wuk_ref[...], preferred_element_type=F32) + acc_ref[:, 384:640]
    va = jnp.dot(ckvn, wuv_ref[...], preferred_element_type=F32)
    qa = _rope(_head_norm(qa, gq_ref[0:1, :], e), rc, rsa, rsb)
    ka = _rope(_head_norm(ka, gk_ref[0:1, :], e), rc, rsa, rsb)
    qkv_ref[:, QA * 256:(QA + 1) * 256] = qa.astype(BF16)
    qkv_ref[:, KA * 256:(KA + 1) * 256] = ka.astype(BF16)
    qkv_ref[:, VA * 256:(VA + 1) * 256] = va.astype(BF16)

    qkvb_ref[:, 0:256] = _head_norm(acc_ref[:, 640:896], gq_ref[1:2, :], e)
    qkvb_ref[:, 256:512] = _head_norm(acc_ref[:, 896:1152], gk_ref[1:2, :], e)
    qkvb_ref[:, 512:768] = acc_ref[:, 1152:1408]
    qkv_ref[:, QC * 256:(QC + 1) * 256] = _head_norm(acc_ref[:, 1408:1664], gq_ref[2:3, :], e).astype(BF16)
    qkv_ref[:, KC * 256:(KC + 1) * 256] = _head_norm(acc_ref[:, 1664:1920], gk_ref[2:3, :], e).astype(BF16)
    qkv_ref[:, VC * 256:(VC + 1) * 256] = acc_ref[:, 1920:2176].astype(BF16)

    qd = _head_norm(acc_ref[:, 2176:2432], gq_ref[3:4, :], e)
    kd = _head_norm(acc_ref[:, 2432:2560], gk_ref[3:4, 0:128], e[0:128, 0:128])
    qkv_ref[:, QD * 256:(QD + 1) * 256] = qd.astype(BF16)
    qkv_ref[:, KVD * 256:KVD * 256 + 128] = kd.astype(BF16)
    qkv_ref[:, KVD * 256 + 128:(KVD + 1) * 256] = acc_ref[:, 2560:2688].astype(BF16)

    z = u[:, 128:256] + fb_ref[...]
    ls = jnp.minimum(z, 0.0) - jnp.log(1.0 + jnp.exp(-jnp.abs(z)))
    lane128 = lax.broadcasted_iota(jnp.int32, (tm, LANES), 1)
    ls = jnp.where((lane128 >= FG_LANE) & (lane128 < FG_LANE + HEADS), ls, 0.0)
    hi = ls.astype(BF16)
    r1 = ls - hi.astype(F32)
    mid = r1.astype(BF16)
    lo = (r1 - mid.astype(F32)).astype(BF16)
    tril = tril_ref[...]
    local = (jnp.dot(tril, hi, preferred_element_type=F32) + jnp.dot(tril, mid, preferred_element_type=F32)
             + jnp.dot(tril, lo, preferred_element_type=F32))
    cum = local + carry_ref[0:1, :]
    cum_ref[...] = cum
    carry_ref[...] = jnp.broadcast_to(cum[tm - 1:tm, :], carry_ref.shape)


def _inproj(xt, seq_len, lw, tabs, tm=512):
    T = xt.shape[0]
    tps = seq_len // tm
    kern = functools.partial(_inproj_kernel, tiles_per_seq=tps)
    tab_spec = pl.BlockSpec((tm, LANES), lambda i: (i % tps, 0))
    return pl.pallas_call(
        kern,
        grid=(T // tm,),
        in_specs=[pl.BlockSpec((tm, D_MODEL), lambda i: (i, 0)),
                  _full((1, D_MODEL)), _full((D_MODEL, W_COLS)), _full((1, 256)), _full((1, 128)),
                  _full((256, 256)), _full((128, 256)), _full((128, 256)), _full((4, 256)), _full((4, 256)),
                  _full((256, 256)), tab_spec, tab_spec, tab_spec, _full((1, LANES)), _full((tm, tm))],
        out_specs=[pl.BlockSpec((tm, QKV_COLS), lambda i: (i, 0)), pl.BlockSpec((tm, QKVB_COLS), lambda i: (i, 0)),
                   pl.BlockSpec((tm, LANES), lambda i: (i, 0))],
        out_shape=[jax.ShapeDtypeStruct((T, QKV_COLS), BF16), jax.ShapeDtypeStruct((T, QKVB_COLS), F32),
                   jax.ShapeDtypeStruct((T, LANES), F32)],
        scratch_shapes=[pltpu.VMEM((tm, W_COLS), F32), pltpu.VMEM((8, LANES), F32)],
        compiler_params=_cparams(("arbitrary",)),
        name="inproj",
    )(xt, lw["g1"], lw["wp"], lw["gcq"], lw["gckv"], lw["wuq"], lw["wuk"], lw["wuv"], lw["gq"], lw["gk"],
      tabs["e"], tabs["rc"], tabs["rsa"], tabs["rsb"], lw["fb"], tabs["tril"])


def _split_heads(x):
    half0 = lax.broadcasted_iota(jnp.int32, x.shape, 1) < HEAD_DIM
    zero = jnp.zeros_like(x)
    return jnp.where(half0, x, zero), jnp.where(half0, zero, x)


def _causal_mask(s):
    row = lax.broadcasted_iota(jnp.int32, s.shape, 0)
    col = lax.broadcasted_iota(jnp.int32, s.shape, 1)
    return jnp.where(col <= row, s, NEG_INF)


def _row_cum(cq_ref, h):
    blk = cq_ref[0] * LOG2E
    lane = lax.broadcasted_iota(jnp.int32, blk.shape, 1)
    return jnp.sum(jnp.where(lane == FG_LANE + 2 * pl.program_id(1) + h, blk, 0.0), axis=1, keepdims=True)


def _dense_bounded_kernel(*refs, tq, tk, fox):
    if fox:
        q_ref, k_ref, v_ref, cq_ref, ck_ref, o_ref, acc_scr = refs
    else:
        q_ref, k_ref, v_ref, o_ref, acc_scr = refs
    qi = pl.program_id(2)
    qh = _split_heads(q_ref[0])
    lane_k = lax.broadcasted_iota(jnp.int32, (tk, LANES), 1)
    ones0 = jnp.where(lane_k == 0, 1.0, 0.0).astype(BF16)
    ones1 = jnp.where(lane_k == 1, 1.0, 0.0).astype(BF16)
    acc_scr[...] = jnp.zeros_like(acc_scr)
    if fox:
        cq = [_row_cum(cq_ref, h) for h in range(2)]

    def step(start, r0, masked):
        k = k_ref[0, pl.ds(start, tk), :]
        v0, v1 = _split_heads(v_ref[0, pl.ds(start, tk), :])
        v2 = jnp.concatenate([jnp.concatenate([v0, ones0], axis=1), jnp.concatenate([v1, ones1], axis=1)], axis=0)
        ps = []
        for h in range(2):
            s = lax.dot_general(qh[h][r0:, :], k, (((1,), (1,)), ((), ())), preferred_element_type=F32)
            if fox:
                s = (s + cq[h][r0:, :]) - ck_ref[0, 0, h:h + 1, pl.ds(start, tk)] * LOG2E
            if masked:
                s = _causal_mask(s)
            ps.append(jnp.exp2(s).astype(BF16))
        acc_scr[r0:, :] += jnp.dot(jnp.concatenate(ps, axis=1), v2, preferred_element_type=F32)

    def body(j, carry):
        step(pl.multiple_of(j * tk, tk), 0, False)
        return carry

    lax.fori_loop(0, qi * (tq // tk), body, 0)
    for d in range(tq // tk):
        step(pl.multiple_of(qi * tq + d * tk, tk), d * tk, True)
    acc = acc_scr[...]
    half0 = lax.broadcasted_iota(jnp.int32, (tq, LANES), 1) < HEAD_DIM
    o = jnp.where(half0, acc[:, 0:LANES] / acc[:, LANES:LANES + 1], acc[:, 0:LANES] / acc[:, LANES + 1:LANES + 2])
    o_ref[0] = o.astype(o_ref.dtype)


def _dense_online_kernel(*refs, tq, tk, fox):
    if fox:
        q_ref, k_ref, v_ref, cq_ref, ck_ref, o_ref, m_scr, l_scr, acc_scr = refs
    else:
        q_ref, k_ref, v_ref, o_ref, m_scr, l_scr, acc_scr = refs
    qi = pl.program_id(2)
    qh = _split_heads(q_ref[0])
    m_scr[...] = jnp.full(m_scr.shape, NEG_INF, F32)
    l_scr[...] = jnp.zeros_like(l_scr)
    acc_scr[...] = jnp.zeros_like(acc_scr)
    if fox:
        cq = [_row_cum(cq_ref, h) for h in range(2)]

    def step(start, r0, masked):
        k = k_ref[0, pl.ds(start, tk), :]
        v = v_ref[0, pl.ds(start, tk), :]
        for h in range(2):
            s = lax.dot_general(qh[h][r0:, :], k, (((1,), (1,)), ((), ())), preferred_element_type=F32)
            if fox:
                s = (s + cq[h][r0:, :]) - ck_ref[0, 0, h:h + 1, pl.ds(start, tk)] * LOG2E
            if masked:
                s = _causal_mask(s)
            m_prev = m_scr[h, r0:, :]
            m_next = jnp.maximum(m_prev, jnp.max(s, axis=1, keepdims=True))
            p = jnp.exp2(s - jnp.concatenate([m_next] * (tk // LANES), axis=1))
            alpha = jnp.exp2(m_prev - m_next)
            l_scr[h, r0:, :] = alpha * l_scr[h, r0:, :] + jnp.sum(p, axis=1, keepdims=True)
            m_scr[h, r0:, :] = m_next
            pv = jnp.dot(p.astype(BF16), v, preferred_element_type=F32)
            acc_scr[h, r0:, :] = acc_scr[h, r0:, :] * alpha + pv

    def body(j, carry):
        step(pl.multiple_of(j * tk, tk), 0, False)
        return carry

    lax.fori_loop(0, qi * (tq // tk), body, 0)
    for d in range(tq // tk):
        step(pl.multiple_of(qi * tq + d * tk, tk), d * tk, True)
    half0 = lax.broadcasted_iota(jnp.int32, (tq, LANES), 1) < HEAD_DIM
    o = jnp.where(half0, acc_scr[0] / l_scr[0], acc_scr[1] / l_scr[1])
    o_ref[0] = o.astype(o_ref.dtype)


def _dense_attn(qkv3, cum3, cum8, jq, jk, jv, logit_bound, tq=2048, tk=512):
    B, S, _ = qkv3.shape
    tq = min(tq, S)
    assert S % tq == 0 and tq % tk == 0
    fox = cum8 is not None
    in_specs = [pl.BlockSpec((1, tq, LANES), lambda b, p, i: (b, i, 2 * jq + p)),
                pl.BlockSpec((1, S, LANES), lambda b, p, i: (b, 0, 2 * jk + p)),
                pl.BlockSpec((1, S, LANES), lambda b, p, i: (b, 0, 2 * jv + p))]
    args = [qkv3, qkv3, qkv3]
    if fox:
        in_specs.append(pl.BlockSpec((1, tq, LANES), lambda b, p, i: (b, i, 0)))
        in_specs.append(pl.BlockSpec((1, 1, 8, S), lambda b, p, i: (b, p, 0, 0)))
        args += [cum3, cum8]
    common = dict(
        grid=(B, 2, S // tq),
        in_specs=in_specs,
        out_specs=pl.BlockSpec((1, tq, LANES), lambda b, p, i: (b, i, p)),
        out_shape=jax.ShapeDtypeStruct((B, S, GROUP_W), BF16),
        compiler_params=_cparams(("parallel", "parallel", "arbitrary")),
    )
    name = "fox_attn" if fox else "mla_attn"
    bounded = pl.pallas_call(functools.partial(_dense_bounded_kernel, tq=tq, tk=tk, fox=fox), name=name + "_bounded",
                             scratch_shapes=[pltpu.VMEM((tq, 2 * LANES), F32)], **common)
    online = pl.pallas_call(functools.partial(_dense_online_kernel, tq=tq, tk=tk, fox=fox), name=name + "_online",
                            scratch_shapes=[pltpu.VMEM((2, tq, LANES), F32)] * 3, **common)
    return lax.cond(logit_bound <= EXP2_SAFE, lambda *a: bounded(*a), lambda *a: online(*a), *args)


def _fold_rows(r, n, dil):
    return slice(r, r + n) if dil == 1 else pl.ds(r, n, stride=dil)


def _banded_kernel(*refs, dil, maxdist, slopes, kv_chunks, has_sink, want_lse, bounded):
    refs = list(refs)
    sink_ref = refs.pop(0) if has_sink else None
    q_refs = [refs.pop(0) for _ in range(2)]
    kv_refs = [[refs.pop(0) for _ in range(4)] for _ in range(kv_chunks)]
    o_ref = refs.pop(0)
    lse_ref = refs.pop(0) if want_lse else None
    o_scr = refs.pop(0)
    lse_scr = refs.pop(0) if want_lse else None
    nq = q_refs[0].shape[1] // dil
    row = lax.broadcasted_iota(jnp.int32, (BAND, 2 * BAND), 0)
    col = lax.broadcasted_iota(jnp.int32, (BAND, 2 * BAND), 1)
    dist = row + BAND - col
    valid = (dist >= 0) & (dist <= maxdist)
    distf = dist.astype(F32) * (float(dil) * LOG2E)
    bias = [jnp.where(valid, -slopes[h] * distf, NEG_INF) for h in range(HEADS)]
    no_prev = jnp.where(col < BAND, jnp.where(pl.program_id(1) == 0, NEG_INF, 0.0), 0.0)
    bias_first = [b + no_prev for b in bias]
    half0_k = lax.broadcasted_iota(jnp.int32, (2 * BAND, LANES), 1) < HEAD_DIM
    ones0 = jnp.where(half0_k, 1.0, 0.0).astype(BF16)
    ones1 = jnp.where(half0_k, 0.0, 1.0).astype(BF16)
    half0 = lax.broadcasted_iota(jnp.int32, (BAND, LANES), 1) < HEAD_DIM
    for c in range(2):
        kp_ref, kc_ref, vp_ref, vc_ref = kv_refs[c if kv_chunks == 2 else 0]
        if has_sink:
            sink_lanes = jnp.where(half0[0:1, :], sink_ref[2 * c], sink_ref[2 * c + 1])
        for r in range(dil):
            prev_rows, cur_rows = _fold_rows(r, BAND, dil), _fold_rows(r, nq, dil)
            qh = _split_heads(q_refs[c][0, cur_rows, :].astype(BF16))
            kf = jnp.concatenate([kp_ref[0, prev_rows, :], kc_ref[0, cur_rows, :]], axis=0).astype(BF16)
            vh = _split_heads(jnp.concatenate([vp_ref[0, prev_rows, :], vc_ref[0, cur_rows, :]], axis=0).astype(BF16))
            outs, lses = [], []
            for j in range(nq // BAND):
                keys = slice(j * BAND, (j + 2) * BAND)
                v2 = jnp.concatenate([jnp.concatenate([vh[0][keys], ones0], axis=1),
                                      jnp.concatenate([vh[1][keys], ones1], axis=1)], axis=0)
                ps, ms = [], []
                for e in range(2):
                    s = lax.dot_general(qh[e][j * BAND:(j + 1) * BAND], kf[keys], (((1,), (1,)), ((), ())),
                                        preferred_element_type=F32)
                    s = s + (bias_first if j == 0 else bias)[2 * c + e]
                    if bounded:
                        ps.append(jnp.exp2(s).astype(BF16))
                    else:
                        m = jnp.max(s, axis=1, keepdims=True)
                        ps.append(jnp.exp2(s - m).astype(BF16))
                        ms.append(m)
                acc = jnp.dot(jnp.concatenate(ps, axis=1), v2, preferred_element_type=F32)
                unnorm, den = acc[:, 0:LANES], acc[:, LANES:2 * LANES]
                if bounded:
                    lse = jnp.log(den)
                    outs.append(unnorm / (den + jnp.exp(sink_lanes)) if has_sink else unnorm / den)
                else:
                    shift = jnp.where(half0, ms[0], ms[1]) * LN2
                    lse = shift + jnp.log(den)
                    if has_sink:
                        mx = jnp.maximum(lse, sink_lanes)
                        total = mx + jnp.log(jnp.exp(lse - mx) + jnp.exp(sink_lanes - mx))
                        outs.append(unnorm * jnp.exp(shift - total))
                    else:
                        outs.append(unnorm / den)
                if want_lse:
                    lses.append(lse)
            o_scr[cur_rows, :] = jnp.concatenate(outs, axis=0)
            if want_lse:
                lse_scr[cur_rows, :] = jnp.concatenate(lses, axis=0)
        o_ref[0, :, c * LANES:(c + 1) * LANES] = o_scr[...]
        if want_lse:
            lse_ref[0, :, c * LANES:(c + 1) * LANES] = lse_scr[...]


def _banded(q_arr, jq, kv_arr, jk, jv, kw, dil, maxdist, slopes, sink, want_lse, logit_bound):
    B, S, _ = q_arr.shape
    tb = min(BAND_TOKENS, S)
    pb = BAND * dil
    assert S % tb == 0 and tb % pb == 0 and maxdist <= BAND
    ratio = tb // pb
    has_sink = sink is not None
    kern = functools.partial(_banded_kernel, dil=dil, maxdist=maxdist, slopes=tuple(slopes),
                             kv_chunks=kw // LANES, has_sink=has_sink, want_lse=want_lse)
    prev = lambda j: pl.BlockSpec((1, pb, LANES), lambda b, i: (b, jnp.maximum(i * ratio - 1, 0), j))
    cur = lambda j: pl.BlockSpec((1, tb, LANES), lambda b, i: (b, i, j))
    in_specs = [cur(2 * jq), cur(2 * jq + 1)]
    args = [q_arr, q_arr]
    for c in range(kw // LANES):
        jkc, jvc = jk * (kw // LANES) + c, jv * (kw // LANES) + c
        in_specs += [prev(jkc), cur(jkc), prev(jvc), cur(jvc)]
        args += [kv_arr] * 4
    if has_sink:
        in_specs.insert(0, pl.BlockSpec(memory_space=pltpu.SMEM))
        args.insert(0, sink)
    n_out = 2 if want_lse else 1
    name = f"banded_d{dil}" if want_lse else "swa_attn"
    call = lambda bounded: pl.pallas_call(
        functools.partial(kern, bounded=bounded),
        grid=(B, S // tb),
        in_specs=in_specs,
        out_specs=[pl.BlockSpec((1, tb, 256), lambda b, i: (b, i, 0))] * n_out,
        out_shape=[jax.ShapeDtypeStruct((B, S, 256), F32)] * n_out,
        scratch_shapes=[pltpu.VMEM((tb, LANES), F32)] * n_out,
        compiler_params=_cparams(("parallel", "parallel")),
        name=name + ("_bounded" if bounded else "_rowmax"),
    )
    outs = lax.cond(logit_bound <= EXP2_SAFE, lambda *a: call(True)(*a), lambda *a: call(False)(*a), *args)
    return [o.reshape(B * S, 256) for o in outs]


def _wo_router_kernel(x_ref, oa_ref, ob0_ref, ob1_ref, ob2_ref, l0_ref, l1_ref, l2_ref, oc_ref, od_ref,
                      wo_ref, g2_ref, wr_ref, br_ref, ltri_ref, h_ref, xn_ref, route_ref, route_t_ref, cnt_ref,
                      carry_ref):
    i = pl.program_id(0)
    tm = x_ref.shape[0]

    @pl.when(i == 0)
    def _():
        carry_ref[...] = jnp.zeros_like(carry_ref)

    la, lb, lc = l0_ref[...], l1_ref[...], l2_ref[...]
    mx = jnp.maximum(jnp.maximum(la, lb), lc)
    ea, eb, ec = jnp.exp(la - mx), jnp.exp(lb - mx), jnp.exp(lc - mx)
    ob = (ea * ob0_ref[...] + eb * ob1_ref[...] + ec * ob2_ref[...]) / (ea + eb + ec)
    mix = jnp.concatenate([oa_ref[...], ob.astype(BF16), oc_ref[...], od_ref[...].astype(BF16)], axis=1)
    h = x_ref[...] + jnp.dot(mix, wo_ref[...], preferred_element_type=F32)
    h_ref[...] = h
    xn = h * lax.rsqrt(jnp.mean(h * h, axis=-1, keepdims=True) + NORM_EPS) * g2_ref[...]
    xn_ref[...] = xn

    xh = xn.astype(BF16)
    xl = (xn - xh.astype(F32)).astype(BF16)
    z = (jnp.dot(xh, wr_ref[0], preferred_element_type=F32) + jnp.dot(xl, wr_ref[0], preferred_element_type=F32)
         + jnp.dot(xh, wr_ref[1], preferred_element_type=F32)) + br_ref[...]
    lane = lax.broadcasted_iota(jnp.int32, (tm, LANES), 1)
    lanef = lane.astype(F32)
    big = float(LANES)
    zg = jnp.where((lane >= N_EXPERTS) & (lane < N_EXPERTS + N_GROUPS), z, NEG_INF)
    mg = jnp.max(zg, axis=1, keepdims=True)
    p_g = 1.0 / jnp.sum(jnp.exp(zg - mg), axis=1, keepdims=True)
    gsel = jnp.min(jnp.where(zg == mg, lanef, big), axis=1, keepdims=True) - float(N_EXPERTS)
    lo = gsel * float(EXPERTS_PER_GROUP)
    ze = jnp.where((lanef >= lo) & (lanef < lo + float(EXPERTS_PER_GROUP)), z, NEG_INF)
    m1 = jnp.max(ze, axis=1, keepdims=True)
    i1 = jnp.min(jnp.where(ze == m1, lanef, big), axis=1, keepdims=True)
    ze2 = jnp.where(lanef == i1, NEG_INF, ze)
    m2 = jnp.max(ze2, axis=1, keepdims=True)
    i2 = jnp.min(jnp.where(ze2 == m2, lanef, big), axis=1, keepdims=True)
    e2 = jnp.exp(m2 - m1)
    gate1 = p_g / (1.0 + e2)
    gate2 = p_g * e2 / (1.0 + e2)

    oh1 = jnp.where(lanef == i1, 1.0, 0.0)
    oh2 = jnp.where(lanef == i2, 1.0, 0.0)
    oh = oh1 + oh2
    before = carry_ref[0:1, :] + jnp.dot(ltri_ref[...], oh.astype(BF16), preferred_element_type=F32)
    r1 = jnp.sum(before * oh1, axis=1, keepdims=True)
    r2 = jnp.sum(before * oh2, axis=1, keepdims=True)
    total = carry_ref[0:1, :] + jnp.sum(oh, axis=0, keepdims=True)
    carry_ref[...] = jnp.broadcast_to(total, carry_ref.shape)
    cnt_ref[...] = jnp.broadcast_to(total, cnt_ref.shape)

    route = jnp.where(lane == 0, i1, jnp.where(lane == 1, i2, jnp.where(lane == 2, r1, jnp.where(
        lane == 3, r2, jnp.where(lane == 4, gate1, jnp.where(lane == 5, gate2, 0.0))))))
    route_ref[...] = route
    route_t_ref[...] = jnp.transpose(route)[0:8, :]


def _wo_router(xt, oa, obs, lses, oc, od, lw, ltri, tm=512):
    T = xt.shape[0]
    row = lambda w: pl.BlockSpec((tm, w), lambda i: (i, 0))
    return pl.pallas_call(
        _wo_router_kernel,
        grid=(T // tm,),
        in_specs=[row(D_MODEL), row(256), row(256), row(256), row(256), row(256), row(256), row(256), row(256),
                  row(256), _full((D_MODEL, D_MODEL)), _full((1, D_MODEL)), _full((2, D_MODEL, LANES)),
                  _full((1, LANES)), _full((tm, tm))],
        out_specs=[row(D_MODEL), row(D_MODEL), row(LANES), pl.BlockSpec((8, tm), lambda i: (0, i)), _full((8, LANES))],
        out_shape=[jax.ShapeDtypeStruct((T, D_MODEL), F32), jax.ShapeDtypeStruct((T, D_MODEL), F32),
                   jax.ShapeDtypeStruct((T, LANES), F32), jax.ShapeDtypeStruct((8, T), F32),
                   jax.ShapeDtypeStruct((8, LANES), F32)],
        scratch_shapes=[pltpu.VMEM((8, LANES), F32)],
        compiler_params=_cparams(("arbitrary",)),
        name="wo_router",
    )(xt, oa, obs[0], obs[1], obs[2], lses[0], lses[1], lses[2], oc, od, lw["wo"], lw["g2"], lw["wr"], lw["br"], ltri)


def _row_copy(src, src_row, dst, dst_row, sem):
    return pltpu.make_async_copy(src.at[pl.ds(src_row, 1)], dst.at[pl.ds(dst_row, 1)], sem)


def _tile_major(dest_t, tile):
    k, T = dest_t.shape
    return dest_t.reshape(k, T // tile, tile).transpose(1, 0, 2).reshape(-1)


def _scatter_kernel(dest_ref, x_ref, buf_out, sem, *, ts):
    def issue(t, carry):
        for k in range(TOP_K):
            _row_copy(x_ref, t, buf_out, dest_ref[k * ts + t], sem).start()
        return carry

    lax.fori_loop(0, ts, issue, 0, unroll=8)
    for k in range(TOP_K):
        pltpu.make_async_copy(x_ref, buf_out.at[pl.ds(0, ts)], sem).wait()


def _scatter_rows(xn, dest_t, ts=1024):
    T, W = xn.shape
    dest_flat = _tile_major(dest_t, ts)
    return pl.pallas_call(
        functools.partial(_scatter_kernel, ts=ts),
        grid=(T // ts,),
        in_specs=[pl.BlockSpec((TOP_K * ts,), lambda i: (i,), memory_space=pltpu.SMEM),
                  pl.BlockSpec((ts, W), lambda i: (i, 0))],
        out_specs=pl.BlockSpec(memory_space=pl.ANY),
        out_shape=jax.ShapeDtypeStruct((TOP_K * T, W), xn.dtype),
        scratch_shapes=[pltpu.SemaphoreType.DMA(())],
        compiler_params=_cparams(("arbitrary",)),
        name="moe_scatter",
    )(dest_flat, xn)


def _expert_items(counts, n_blocks):
    n_items_max = n_blocks + N_EXPERTS
    end = jnp.cumsum(counts)
    start = end - counts
    first_b = start // ROW_BLOCK
    per_expert = jnp.where(counts > 0, (end - 1) // ROW_BLOCK - first_b + 1, 0)
    item_end = jnp.cumsum(per_expert)
    item_start = item_end - per_expert
    n_items = item_end[-1:]
    idx = jnp.minimum(jnp.arange(n_items_max, dtype=jnp.int32), n_items - 1)
    e = jnp.minimum(jnp.sum((item_end[None, :] <= idx[:, None]).astype(jnp.int32), axis=1), N_EXPERTS - 1)
    onehot = (e[:, None] == jnp.arange(N_EXPERTS, dtype=jnp.int32)[None, :]).astype(jnp.int32)
    pick = lambda table: jnp.sum(onehot * table[None, :], axis=1)
    b = pick(first_b) + idx - pick(item_start)
    lo = jnp.maximum(pick(start) - b * ROW_BLOCK, 0)
    hi = jnp.minimum(pick(end) - b * ROW_BLOCK, ROW_BLOCK)
    return e, b, lo, hi, n_items.astype(jnp.int32), start


def _expert_kernel(e_ref, b_ref, lo_ref, hi_ref, n_ref, x_ref, w1_ref, w3_ref, w2_ref, y_ref):
    del e_ref
    i = pl.program_id(0)

    @pl.when(i < n_ref[0])
    def _():
        x = x_ref[...].astype(BF16)
        a = jnp.dot(x, w1_ref[0, 0].astype(BF16), preferred_element_type=F32)
        b = jnp.dot(x, w3_ref[0, 0].astype(BF16), preferred_element_type=F32)
        hid = (a / (1.0 + jnp.exp(-a))) * b
        y = jnp.dot(hid.astype(BF16), w2_ref[0, 0].astype(BF16), preferred_element_type=F32)
        row = lax.broadcasted_iota(jnp.int32, y.shape, 0)
        y = jnp.where((row >= lo_ref[i]) & (row < hi_ref[i]), y, 0.0)
        first_visit = jnp.logical_or(i == 0, b_ref[i] != b_ref[jnp.maximum(i - 1, 0)])

        @pl.when(first_visit)
        def _():
            y_ref[...] = y

        @pl.when(jnp.logical_not(first_visit))
        def _():
            y_ref[...] += y


def _expert_mlp(xbuf, items, layer, w1, w3, w2):
    P, D = xbuf.shape
    e, b, lo, hi, n_items = items
    x_map = lambda i, e_, b_, lo_, hi_, n_: (b_[i], 0)
    w_map = lambda i, e_, b_, lo_, hi_, n_: (layer, e_[i], 0, 0)
    grid_spec = pltpu.PrefetchScalarGridSpec(
        num_scalar_prefetch=5,
        grid=(e.shape[0],),
        in_specs=[pl.BlockSpec((ROW_BLOCK, D), x_map),
                  pl.BlockSpec((1, 1, D, D_EXPERT), w_map),
                  pl.BlockSpec((1, 1, D, D_EXPERT), w_map),
                  pl.BlockSpec((1, 1, D_EXPERT, D), w_map)],
        out_specs=pl.BlockSpec((ROW_BLOCK, D), x_map),
    )
    return pl.pallas_call(
        _expert_kernel,
        grid_spec=grid_spec,
        out_shape=jax.ShapeDtypeStruct((P, D), xbuf.dtype),
        compiler_params=_cparams(("arbitrary",)),
        name="moe_experts",
    )(e, b, lo, hi, n_items, xbuf, w1, w3, w2)


def _combine_kernel(dest_ref, dest_next_ref, h_ref, route_ref, y_hbm, o_ref, rows_scr, sems, *, tc):
    i = pl.program_id(0)
    n = pl.num_programs(0)
    slot = i % 2

    def gather(d_ref, s):
        def issue(t, carry):
            for k in range(TOP_K):
                _row_copy(y_hbm, d_ref[k * tc + t], rows_scr.at[s, k], t, sems.at[s]).start()
            return carry

        lax.fori_loop(0, tc, issue, 0, unroll=8)

    @pl.when(i == 0)
    def _():
        gather(dest_ref, 0)

    @pl.when(i + 1 < n)
    def _():
        gather(dest_next_ref, 1 - slot)

    for k in range(TOP_K):
        pltpu.make_async_copy(y_hbm.at[pl.ds(0, tc)], rows_scr.at[slot, k], sems.at[slot]).wait()
    route = route_ref[...]
    o_ref[...] = h_ref[...] + route[:, 4:5] * rows_scr[slot, 0] + route[:, 5:6] * rows_scr[slot, 1]


def _combine(h, route, ybuf, dest_t, tc=256):
    T, D = h.shape
    n = T // tc
    dest_flat = _tile_major(dest_t, tc)
    return pl.pallas_call(
        functools.partial(_combine_kernel, tc=tc),
        grid=(n,),
        in_specs=[pl.BlockSpec((TOP_K * tc,), lambda i: (i,), memory_space=pltpu.SMEM),
                  pl.BlockSpec((TOP_K * tc,), lambda i: (jnp.minimum(i + 1, n - 1),), memory_space=pltpu.SMEM),
                  pl.BlockSpec((tc, D), lambda i: (i, 0)), pl.BlockSpec((tc, LANES), lambda i: (i, 0)),
                  pl.BlockSpec(memory_space=pl.ANY)],
        out_specs=pl.BlockSpec((tc, D), lambda i: (i, 0)),
        out_shape=jax.ShapeDtypeStruct((T, D), F32),
        scratch_shapes=[pltpu.VMEM((2, TOP_K, tc, D), ybuf.dtype), pltpu.SemaphoreType.DMA((2,))],
        compiler_params=_cparams(("arbitrary",)),
        name="moe_combine",
    )(dest_flat, dest_flat, h, route, ybuf)


def _prep_weights(norm1_g, w_in, mla_gcq, mla_gckv, mla_wuq, mla_wukv, qk_gq, qk_gk, fox_bf, sink, w_o,
                  norm2_g, w_rg, b_rg, w_re, b_re):
    L = w_in.shape[0]
    offs = np.concatenate([[0], np.cumsum(IN_SPLITS)])
    cq, ckv, kr, pb, pc, fg, pdq, pdkv = [w_in[:, :, offs[j]:offs[j + 1]] for j in range(8)]
    z = lambda *shape: jnp.zeros((L,) + shape, F32)
    perm = np.array([0, 2, 1, 3])
    ublock = jnp.concatenate([cq, fg, z(D_MODEL, 256 - MLA_Q_RANK - HEADS)], axis=2)
    kr_rep = jnp.concatenate([z(D_MODEL, MLA_NOPE), kr] * HEADS, axis=2)
    dq = pdq.reshape(L, D_MODEL, HEADS, HEAD_DIM)[:, :, perm].reshape(L, D_MODEL, GROUP_W)
    wp = jnp.concatenate([ckv, ublock, kr_rep, pb, pc, dq, pdkv], axis=2).astype(BF16)

    wuq = jnp.concatenate([mla_wuq, z(256 - MLA_Q_RANK, GROUP_W)], axis=1).astype(BF16)
    wukv = mla_wukv.reshape(L, MLA_KV_RANK, HEADS, MLA_NOPE + HEAD_DIM)
    wuk = jnp.concatenate([wukv[..., :MLA_NOPE], z(MLA_KV_RANK, HEADS, MLA_ROPE)], axis=3)
    wuk = wuk.reshape(L, MLA_KV_RANK, GROUP_W).astype(BF16)
    wuv = wukv[..., MLA_NOPE:].reshape(L, MLA_KV_RANK, GROUP_W).astype(BF16)
    gcq = jnp.concatenate([mla_gcq, z(256 - MLA_Q_RANK)], axis=1)[:, None, :]
    scale = HEAD_DIM ** -0.5
    gq = jnp.tile(qk_gq, (1, 1, HEADS)) * (scale * LOG2E)
    gk = jnp.tile(qk_gk, (1, 1, HEADS))
    bound = HEAD_DIM * jnp.max(jnp.abs(gq), axis=2) * jnp.max(jnp.abs(gk), axis=2)
    fb = z(1, LANES).at[:, 0, FG_LANE:FG_LANE + HEADS].set(fox_bf)
    wo_d = w_o[:, 3 * GROUP_W:].reshape(L, HEADS, HEAD_DIM, D_MODEL)[:, perm].reshape(L, GROUP_W, D_MODEL)
    wo = jnp.concatenate([w_o[:, :3 * GROUP_W], wo_d], axis=1).astype(BF16)
    wr = jnp.concatenate([w_re, w_rg, z(D_MODEL, LANES - N_EXPERTS - N_GROUPS)], axis=2)
    wr_hi = wr.astype(BF16)
    wr = jnp.stack([wr_hi, (wr - wr_hi.astype(F32)).astype(BF16)], axis=1)
    br = jnp.concatenate([b_re, b_rg, z(LANES - N_EXPERTS - N_GROUPS)], axis=1)[:, None, :]
    return dict(g1=norm1_g[:, None, :], wp=wp, gcq=gcq, gckv=mla_gckv[:, None, :], wuq=wuq, wuk=wuk, wuv=wuv,
                gq=gq, gk=gk, bound=bound, fb=fb, sink=sink[:, perm], wo=wo, g2=norm2_g[:, None, :], wr=wr, br=br)


def _tables(seq_len, tm):
    half = MLA_ROPE // 2
    inv = ROPE_BASE ** (-jnp.arange(0, MLA_ROPE, 2, dtype=F32) / MLA_ROPE)
    ang = jnp.arange(seq_len, dtype=F32)[:, None] * inv[None, :]
    cos, sin = jnp.cos(ang), jnp.sin(ang)
    one = jnp.ones((seq_len, MLA_NOPE), F32)
    zn = jnp.zeros((seq_len, MLA_NOPE), F32)
    zh = jnp.zeros((seq_len, half), F32)
    rc = jnp.concatenate([one, cos, cos] * 2, axis=1)
    rsa = jnp.concatenate([zn, -sin, zh] * 2, axis=1)
    rsb = jnp.concatenate([zn, zh, sin] * 2, axis=1)
    seg = np.arange(256) // HEAD_DIM
    e = jnp.asarray(seg[:, None] == seg[None, :], BF16)
    idx = np.arange(tm)
    tril = jnp.asarray(idx[None, :] <= idx[:, None], BF16)
    ltri = jnp.asarray(idx[None, :] < idx[:, None], BF16)
    return dict(rc=rc, rsa=rsa, rsb=rsb, e=e, tril=tril, ltri=ltri)


def _alibi():
    n = 2 * HEADS
    s = [2.0 ** (-8.0 * i / n) for i in range(1, n + 1)]
    return s[HEADS:], s[:HEADS]


def kernel(x, norm1_g, w_in, mla_gcq, mla_gckv, mla_wuq, mla_wukv, qk_gq, qk_gk, fox_bf, sink, w_o,
           norm2_g, w_rg, b_rg, w_re, b_re, w1, w3, w2):
    B, S, D = x.shape
    T = B * S
    depth = w_in.shape[0]
    tm = 512
    tabs = _tables(S, tm)
    slopes_dil, slopes_swa = _alibi()
    slopes_swa_perm = [slopes_swa[0], slopes_swa[2], slopes_swa[1], slopes_swa[3]]
    n_assign = T * TOP_K
    assert n_assign % ROW_BLOCK == 0

    all_w = _prep_weights(norm1_g, w_in, mla_gcq, mla_gckv, mla_wuq, mla_wukv, qk_gq, qk_gk, fox_bf, sink,
                          w_o, norm2_g, w_rg, b_rg, w_re, b_re)
    xt = x.reshape(T, D)
    for l in range(depth):
        lw = {name: arr[l] for name, arr in all_w.items()}
        qkv, qkvb, cum = _inproj(xt, S, lw, tabs, tm=tm)
        qkv3 = qkv.reshape(B, S, QKV_COLS)
        qkvb3 = qkvb.reshape(B, S, QKVB_COLS)
        cum8 = jnp.pad(cum[:, FG_LANE:FG_LANE + HEADS].reshape(B, S, 2, 2).transpose(0, 2, 3, 1),
                       ((0, 0), (0, 0), (0, 6), (0, 0)))
        oa = _dense_attn(qkv3, None, None, QA, KA, VA, lw["bound"][0]).reshape(T, GROUP_W)
        oc = _dense_attn(qkv3, cum.reshape(B, S, LANES), cum8, QC, KC, VC, lw["bound"][2]).reshape(T, GROUP_W)
        obs, lses = [], []
        for window, dil in DILATED_PAIRS:
            o, lse = _banded(qkvb3, 0, qkvb3, 1, 2, 256, dil, window // dil, slopes_dil, None, True, lw["bound"][1])
            obs.append(o)
            lses.append(lse)
        swa_bound = jnp.maximum(lw["bound"][3], jnp.max(jnp.abs(lw["sink"])) * LOG2E)
        (od,) = _banded(qkv3, QD, qkv3, 2 * KVD, 2 * KVD + 1, 128, 1, SWA_WINDOW - 1, slopes_swa_perm,
                        lw["sink"], False, swa_bound)
        h, xn, route, route_t, cnt = _wo_router(xt, oa, obs, lses, oc, od, lw, tabs["ltri"], tm=tm)

        counts = cnt[0, :N_EXPERTS].astype(jnp.int32)
        *items, start = _expert_items(counts, n_assign // ROW_BLOCK)
        expert = route_t[0:TOP_K].astype(jnp.int32)
        rank = route_t[TOP_K:2 * TOP_K].astype(jnp.int32)
        hit = expert[None] == jnp.arange(N_EXPERTS, dtype=jnp.int32)[:, None, None]
        dest = rank + jnp.sum(jnp.where(hit, start[:, None, None], 0), axis=0)

        xbuf = _scatter_rows(xn, dest)
        ybuf = _expert_mlp(xbuf, items, l, w1, w3, w2)
        xt = _combine(h, route, ybuf, dest)
    return xt.reshape(B, S, D)
```

```python
import functools

import numpy as np
import jax
import jax.numpy as jnp
from jax import lax
from jax.experimental import pallas as pl
from jax.experimental.pallas import tpu as pltpu

F32 = jnp.float32
BF16 = jnp.bfloat16

D_MODEL = 1024
HEAD_DIM = 64
HEADS = 4
GROUP_W = HEADS * HEAD_DIM
NORM_EPS = 1e-6
MLA_Q_RANK, MLA_KV_RANK, MLA_NOPE, MLA_ROPE = 192, 128, 32, 32
ROPE_BASE = 10000.0
DILATED_PAIRS = ((128, 1), (512, 4), (2048, 16))
SWA_WINDOW = 128
N_GROUPS, EXPERTS_PER_GROUP, N_EXPERTS, TOP_K, D_EXPERT = 4, 8, 32, 2, 256
IN_SPLITS = (MLA_Q_RANK, MLA_KV_RANK, MLA_ROPE, 3 * GROUP_W, 3 * GROUP_W, HEADS, GROUP_W, 2 * 2 * HEAD_DIM)

LANES = 128
SUBLANES = 8
W_COLS = 2688
QKV_COLS = 2048
QA, KA, VA, QC, KC, VC, QD, KVD = range(8)
QKVB_COLS = 768
ONES_ROWS = 16
BAND = 128
BAND_TOKENS = 2048
FG_LANE = 64
ROW_BLOCK = 512
VMEM_LIMIT = 56 * 1024 * 1024

NEG_INF = float("-inf")
LOG2E = 1.4426950408889634
LN2 = 0.6931471805599453
EXP2_SAFE = 60.0


def _cparams(sem):
    return pltpu.CompilerParams(dimension_semantics=sem, vmem_limit_bytes=VMEM_LIMIT)


def _full(shape):
    zeros = (0,) * len(shape)
    return pl.BlockSpec(shape, lambda *_: zeros)


def _head_norm(y, g, e):
    ss = jnp.dot((y * y).astype(BF16), e, preferred_element_type=F32)
    return y * lax.rsqrt(ss * (1.0 / HEAD_DIM) + NORM_EPS) * g


def _rope(y, rc, rsa, rsb):
    outs = []
    for c in range(y.shape[1] // LANES):
        yc = y[:, c * LANES:(c + 1) * LANES]
        outs.append(yc * rc + pltpu.roll(yc, LANES - 16, 1) * rsa + pltpu.roll(yc, 16, 1) * rsb)
    return jnp.concatenate(outs, axis=1)


def _inproj_kernel(x_ref, g1_ref, w_ref, gcq_ref, gckv_ref, wuq_ref, wuk_ref, wuv_ref, gq_ref, gk_ref, e_ref,
                   rc_ref, rsa_ref, rsb_ref, fb_ref, tril_ref, qkv_ref, qkvb_ref, vt_ref, cum_ref, acc_ref, carry_ref,
                   *, tiles_per_seq):
    i = pl.program_id(0)
    tm = x_ref.shape[0]

    @pl.when(i % tiles_per_seq == 0)
    def _():
        carry_ref[...] = jnp.zeros_like(carry_ref)

    x = x_ref[...]
    ms = jnp.mean(x * x, axis=-1, keepdims=True)
    xn = (x * lax.rsqrt(ms + NORM_EPS) * g1_ref[...]).astype(BF16)
    acc_ref[...] = jnp.dot(xn, w_ref[...], preferred_element_type=F32)

    e = e_ref[...]
    rc, rsa, rsb = rc_ref[...], rsa_ref[...], rsb_ref[...]

    ckv = acc_ref[:, 0:128]
    ckvn = (ckv * lax.rsqrt(jnp.mean(ckv * ckv, axis=-1, keepdims=True) + NORM_EPS) * gckv_ref[...]).astype(BF16)
    u = acc_ref[:, 128:384]
    lane256 = lax.broadcasted_iota(jnp.int32, (tm, 256), 1)
    ssq = jnp.sum(jnp.where(lane256 < MLA_Q_RANK, u * u, 0.0), axis=-1, keepdims=True) * (1.0 / MLA_Q_RANK)
    un = (u * lax.rsqrt(ssq + NORM_EPS) * gcq_ref[...]).astype(BF16)
    qa = jnp.dot(un, wuq_ref[...], preferred_element_type=F32)
    ka = jnp.dot(ckvn, wuk_ref[...], preferred_element_type=F32) + acc_ref[:, 384:640]
    va = jnp.dot(ckvn, wuv_ref[...], preferred_element_type=F32)
    qa = _rope(_head_norm(qa, gq_ref[0:1, :], e), rc, rsa, rsb)
    ka = _rope(_head_norm(ka, gk_ref[0:1, :], e), rc, rsa, rsb)
    qkv_ref[:, QA * 256:(QA + 1) * 256] = qa.astype(BF16)
    qkv_ref[:, KA * 256:(KA + 1) * 256] = ka.astype(BF16)
    qkv_ref[:, VA * 256:(VA + 1) * 256] = va.astype(BF16)
    vt_ref[0:2] = jnp.transpose(va).astype(BF16).reshape(2, LANES, tm)
    vt_ref[2:4] = jnp.transpose(acc_ref[:, 1920:2176]).astype(BF16).reshape(2, LANES, tm)

    qkvb_ref[:, 0:256] = _head_norm(acc_ref[:, 640:896], gq_ref[1:2, :], e)
    qkvb_ref[:, 256:512] = _head_norm(acc_ref[:, 896:1152], gk_ref[1:2, :], e)
    qkvb_ref[:, 512:768] = acc_ref[:, 1152:1408]
    qkv_ref[:, QC * 256:(QC + 1) * 256] = _head_norm(acc_ref[:, 1408:1664], gq_ref[2:3, :], e).astype(BF16)
    qkv_ref[:, KC * 256:(KC + 1) * 256] = _head_norm(acc_ref[:, 1664:1920], gk_ref[2:3, :], e).astype(BF16)
    qkv_ref[:, VC * 256:(VC + 1) * 256] = acc_ref[:, 1920:2176].astype(BF16)

    qd = _head_norm(acc_ref[:, 2176:2432], gq_ref[3:4, :], e)
    kd = _head_norm(acc_ref[:, 2432:2560], gk_ref[3:4, 0:128], e[0:128, 0:128])
    qkv_ref[:, QD * 256:(QD + 1) * 256] = qd.astype(BF16)
    qkv_ref[:, KVD * 256:KVD * 256 + 128] = kd.astype(BF16)
    qkv_ref[:, KVD * 256 + 128:(KVD + 1) * 256] = acc_ref[:, 2560:2688].astype(BF16)

    z = u[:, 128:256] + fb_ref[...]
    ls = jnp.minimum(z, 0.0) - jnp.log(1.0 + jnp.exp(-jnp.abs(z)))
    lane128 = lax.broadcasted_iota(jnp.int32, (tm, LANES), 1)
    ls = jnp.where((lane128 >= FG_LANE) & (lane128 < FG_LANE + HEADS), ls, 0.0)
    hi = ls.astype(BF16)
    r1 = ls - hi.astype(F32)
    mid = r1.astype(BF16)
    lo = (r1 - mid.astype(F32)).astype(BF16)
    tril = tril_ref[...]
    local = (jnp.dot(tril, hi, preferred_element_type=F32) + jnp.dot(tril, mid, preferred_element_type=F32)
             + jnp.dot(tril, lo, preferred_element_type=F32))
    cum = local + carry_ref[0:1, :]
    cum_ref[...] = cum
    carry_ref[...] = jnp.broadcast_to(cum[tm - 1:tm, :], carry_ref.shape)


def _inproj(xt, seq_len, lw, tabs, tm=512):
    T = xt.shape[0]
    tps = seq_len // tm
    kern = functools.partial(_inproj_kernel, tiles_per_seq=tps)
    tab_spec = pl.BlockSpec((tm, LANES), lambda i: (i % tps, 0))
    return pl.pallas_call(
        kern,
        grid=(T // tm,),
        in_specs=[pl.BlockSpec((tm, D_MODEL), lambda i: (i, 0)),
                  _full((1, D_MODEL)), _full((D_MODEL, W_COLS)), _full((1, 256)), _full((1, 128)),
                  _full((256, 256)), _full((128, 256)), _full((128, 256)), _full((4, 256)), _full((4, 256)),
                  _full((256, 256)), tab_spec, tab_spec, tab_spec, _full((1, LANES)), _full((tm, tm))],
        out_specs=[pl.BlockSpec((tm, QKV_COLS), lambda i: (i, 0)), pl.BlockSpec((tm, QKVB_COLS), lambda i: (i, 0)),
                   pl.BlockSpec((4, LANES, tm), lambda i: (0, 0, i)), pl.BlockSpec((tm, LANES), lambda i: (i, 0))],
        out_shape=[jax.ShapeDtypeStruct((T, QKV_COLS), BF16), jax.ShapeDtypeStruct((T, QKVB_COLS), F32),
                   jax.ShapeDtypeStruct((4, LANES, T), BF16), jax.ShapeDtypeStruct((T, LANES), F32)],
        scratch_shapes=[pltpu.VMEM((tm, W_COLS), F32), pltpu.VMEM((8, LANES), F32)],
        compiler_params=_cparams(("arbitrary",)),
        name="inproj",
    )(xt, lw["g1"], lw["wp"], lw["gcq"], lw["gckv"], lw["wuq"], lw["wuk"], lw["wuv"], lw["gq"], lw["gk"],
      tabs["e"], tabs["rc"], tabs["rsa"], tabs["rsb"], lw["fb"], tabs["tril"])


def _split_heads(x):
    half0 = lax.broadcasted_iota(jnp.int32, x.shape, 1) < HEAD_DIM
    zero = jnp.zeros_like(x)
    return jnp.where(half0, x, zero), jnp.where(half0, zero, x)


def _causal_mask(s):
    row = lax.broadcasted_iota(jnp.int32, s.shape, 0)
    col = lax.broadcasted_iota(jnp.int32, s.shape, 1)
    return jnp.where(col <= row, s, NEG_INF)


def _cum_column(blk, h):
    lane = lax.broadcasted_iota(jnp.int32, blk.shape, 1)
    return jnp.sum(jnp.where(lane == FG_LANE + 2 * pl.program_id(1) + h, blk * LOG2E, 0.0), axis=1, keepdims=True)


def _row_cum(cq_ref, h):
    return _cum_column(cq_ref[0], h)


def _dense_bounded_kernel(*refs, tq, tk, fox):
    if fox:
        q_ref, k_ref, vt_ref, ccol_ref, crow_ref, o_ref, acc_scr = refs
    else:
        q_ref, k_ref, vt_ref, o_ref, acc_scr = refs
    qi = pl.program_id(2)
    qh = _split_heads(q_ref[0])
    ones = jnp.ones((ONES_ROWS, tk), BF16)
    acc_scr[...] = jnp.zeros_like(acc_scr)

    def step(start, c0, masked):
        k = k_ref[0, pl.ds(start, tk), :]
        for h in range(2):
            s = lax.dot_general(k, qh[h][c0:, :], (((1,), (1,)), ((), ())), preferred_element_type=F32)
            if fox:
                crow = crow_ref[0, 0, h:h + 1, pl.ds(pl.multiple_of(qi * tq + c0, tk), tq - c0)] * LOG2E
                s = (s + crow) - _cum_column(ccol_ref[0, pl.ds(start, tk), :], h)
            if masked:
                key = lax.broadcasted_iota(jnp.int32, s.shape, 0)
                query = lax.broadcasted_iota(jnp.int32, s.shape, 1)
                s = jnp.where(key <= query, s, NEG_INF)
            lhs = jnp.concatenate([vt_ref[0, h * HEAD_DIM:(h + 1) * HEAD_DIM, pl.ds(start, tk)], ones], axis=0)
            acc_scr[h, :, c0:] += jnp.dot(lhs, jnp.exp2(s).astype(BF16), preferred_element_type=F32)

    def body(j, carry):
        step(pl.multiple_of(j * tk, tk), 0, False)
        return carry

    lax.fori_loop(0, qi * (tq // tk), body, 0)
    for d in range(tq // tk):
        step(pl.multiple_of(qi * tq + d * tk, tk), d * tk, True)
    o_t = jnp.concatenate([acc_scr[h, 0:HEAD_DIM, :] / acc_scr[h, HEAD_DIM:HEAD_DIM + 1, :] for h in range(2)], axis=0)
    o_ref[0] = jnp.transpose(o_t).astype(o_ref.dtype)


def _dense_online_kernel(*refs, tq, tk, fox):
    if fox:
        q_ref, k_ref, v_ref, cq_ref, ck_ref, o_ref, m_scr, l_scr, acc_scr = refs
    else:
        q_ref, k_ref, v_ref, o_ref, m_scr, l_scr, acc_scr = refs
    qi = pl.program_id(2)
    qh = _split_heads(q_ref[0])
    m_scr[...] = jnp.full(m_scr.shape, NEG_INF, F32)
    l_scr[...] = jnp.zeros_like(l_scr)
    acc_scr[...] = jnp.zeros_like(acc_scr)
    if fox:
        cq = [_row_cum(cq_ref, h) for h in range(2)]

    def step(start, r0, masked):
        k = k_ref[0, pl.ds(start, tk), :]
        v = v_ref[0, pl.ds(start, tk), :]
        for h in range(2):
            s = lax.dot_general(qh[h][r0:, :], k, (((1,), (1,)), ((), ())), preferred_element_type=F32)
            if fox:
                s = (s + cq[h][r0:, :]) - ck_ref[0, 0, h:h + 1, pl.ds(start, tk)] * LOG2E
            if masked:
                s = _causal_mask(s)
            m_prev = m_scr[h, r0:, :]
            m_next = jnp.maximum(m_prev, jnp.max(s, axis=1, keepdims=True))
            p = jnp.exp2(s - jnp.concatenate([m_next] * (tk // LANES), axis=1))
            alpha = jnp.exp2(m_prev - m_next)
            l_scr[h, r0:, :] = alpha * l_scr[h, r0:, :] + jnp.sum(p, axis=1, keepdims=True)
            m_scr[h, r0:, :] = m_next
            pv = jnp.dot(p.astype(BF16), v, preferred_element_type=F32)
            acc_scr[h, r0:, :] = acc_scr[h, r0:, :] * alpha + pv

    def body(j, carry):
        step(pl.multiple_of(j * tk, tk), 0, False)
        return carry

    lax.fori_loop(0, qi * (tq // tk), body, 0)
    for d in range(tq // tk):
        step(pl.multiple_of(qi * tq + d * tk, tk), d * tk, True)
    half0 = lax.broadcasted_iota(jnp.int32, (tq, LANES), 1) < HEAD_DIM
    o = jnp.where(half0, acc_scr[0] / l_scr[0], acc_scr[1] / l_scr[1])
    o_ref[0] = o.astype(o_ref.dtype)


def _dense_attn(qkv3, vt, jvt, cum3, cum8, jq, jk, jv, logit_bound, tq=2048, tk=512):
    B, S, _ = qkv3.shape
    tq = min(tq, S)
    assert S % tq == 0 and tq % tk == 0
    fox = cum8 is not None
    q_spec = pl.BlockSpec((1, tq, LANES), lambda b, p, i: (b, i, 2 * jq + p))
    k_spec = pl.BlockSpec((1, S, LANES), lambda b, p, i: (b, 0, 2 * jk + p))
    v_spec = pl.BlockSpec((1, S, LANES), lambda b, p, i: (b, 0, 2 * jv + p))
    vt_spec = pl.BlockSpec((1, LANES, S), lambda b, p, i: (jvt + p, 0, b))
    cum_row_spec = pl.BlockSpec((1, 1, 8, S), lambda b, p, i: (b, p, 0, 0))
    common = dict(
        grid=(B, 2, S // tq),
        out_specs=pl.BlockSpec((1, tq, LANES), lambda b, p, i: (b, i, p)),
        out_shape=jax.ShapeDtypeStruct((B, S, GROUP_W), BF16),
        compiler_params=_cparams(("parallel", "parallel", "arbitrary")),
    )
    bounded_specs, online_specs = [q_spec, k_spec, vt_spec], [q_spec, k_spec, v_spec]
    if fox:
        bounded_specs += [pl.BlockSpec((1, S, LANES), lambda b, p, i: (b, 0, 0)), cum_row_spec]
        online_specs += [pl.BlockSpec((1, tq, LANES), lambda b, p, i: (b, i, 0)), cum_row_spec]
    name = "fox_attn" if fox else "mla_attn"
    bounded = pl.pallas_call(functools.partial(_dense_bounded_kernel, tq=tq, tk=tk, fox=fox), name=name + "_bounded",
                             in_specs=bounded_specs,
                             scratch_shapes=[pltpu.VMEM((2, HEAD_DIM + ONES_ROWS, tq), F32)], **common)
    online = pl.pallas_call(functools.partial(_dense_online_kernel, tq=tq, tk=tk, fox=fox), name=name + "_online",
                            in_specs=online_specs,
                            scratch_shapes=[pltpu.VMEM((2, tq, LANES), F32)] * 3, **common)
    cums = (cum3, cum8) if fox else ()
    return lax.cond(logit_bound <= EXP2_SAFE,
                    lambda vt_, *c: bounded(qkv3, qkv3, vt_, *c),
                    lambda vt_, *c: online(qkv3, qkv3, qkv3, *c), vt, *cums)


def _fold_rows(r, n, dil):
    return slice(r, r + n) if dil == 1 else pl.ds(r, n, stride=dil)


def _banded_kernel(*refs, dil, maxdist, slopes, kv_chunks, has_sink, want_lse, bounded):
    refs = list(refs)
    sink_ref = refs.pop(0) if has_sink else None
    q_refs = [refs.pop(0) for _ in range(2)]
    kv_refs = [[refs.pop(0) for _ in range(4)] for _ in range(kv_chunks)]
    o_ref = refs.pop(0)
    lse_ref = refs.pop(0) if want_lse else None
    o_scr = refs.pop(0)
    lse_scr = refs.pop(0) if want_lse else None
    nq = q_refs[0].shape[1] // dil
    row = lax.broadcasted_iota(jnp.int32, (BAND, 2 * BAND), 0)
    col = lax.broadcasted_iota(jnp.int32, (BAND, 2 * BAND), 1)
    dist = row + BAND - col
    valid = (dist >= 0) & (dist <= maxdist)
    distf = dist.astype(F32) * (float(dil) * LOG2E)
    bias = [jnp.where(valid, -slopes[h] * distf, NEG_INF) for h in range(HEADS)]
    no_prev = jnp.where(col < BAND, jnp.where(pl.program_id(1) == 0, NEG_INF, 0.0), 0.0)
    bias_first = [b + no_prev for b in bias]
    half0_k = lax.broadcasted_iota(jnp.int32, (2 * BAND, LANES), 1) < HEAD_DIM
    ones0 = jnp.where(half0_k, 1.0, 0.0).astype(BF16)
    ones1 = jnp.where(half0_k, 0.0, 1.0).astype(BF16)
    half0 = lax.broadcasted_iota(jnp.int32, (BAND, LANES), 1) < HEAD_DIM
    for c in range(2):
        kp_ref, kc_ref, vp_ref, vc_ref = kv_refs[c if kv_chunks == 2 else 0]
        if has_sink:
            sink_lanes = jnp.where(half0[0:1, :], sink_ref[2 * c], sink_ref[2 * c + 1])
        for r in range(dil):
            prev_rows, cur_rows = _fold_rows(r, BAND, dil), _fold_rows(r, nq, dil)
            qh = _split_heads(q_refs[c][0, cur_rows, :].astype(BF16))
            kf = jnp.concatenate([kp_ref[0, prev_rows, :], kc_ref[0, cur_rows, :]], axis=0).astype(BF16)
            vh = _split_heads(jnp.concatenate([vp_ref[0, prev_rows, :], vc_ref[0, cur_rows, :]], axis=0).astype(BF16))
            outs, lses = [], []
            for j in range(nq // BAND):
                keys = slice(j * BAND, (j + 2) * BAND)
                v2 = jnp.concatenate([jnp.concatenate([vh[0][keys], ones0], axis=1),
                                      jnp.concatenate([vh[1][keys], ones1], axis=1)], axis=0)
                ps, ms = [], []
                for e in range(2):
                    s = lax.dot_general(qh[e][j * BAND:(j + 1) * BAND], kf[keys], (((1,), (1,)), ((), ())),
                                        preferred_element_type=F32)
                    s = s + (bias_first if j == 0 else bias)[2 * c + e]
                    if bounded:
                        ps.append(jnp.exp2(s).astype(BF16))
                    else:
                        m = jnp.max(s, axis=1, keepdims=True)
                        ps.append(jnp.exp2(s - m).astype(BF16))
                        ms.append(m)
                acc = jnp.dot(jnp.concatenate(ps, axis=1), v2, preferred_element_type=F32)
                unnorm, den = acc[:, 0:LANES], acc[:, LANES:2 * LANES]
                if bounded:
                    lse = jnp.log(den)
                    outs.append(unnorm / (den + jnp.exp(sink_lanes)) if has_sink else unnorm / den)
                else:
                    shift = jnp.where(half0, ms[0], ms[1]) * LN2
                    lse = shift + jnp.log(den)
                    if has_sink:
                        mx = jnp.maximum(lse, sink_lanes)
                        total = mx + jnp.log(jnp.exp(lse - mx) + jnp.exp(sink_lanes - mx))
                        outs.append(unnorm * jnp.exp(shift - total))
                    else:
                        outs.append(unnorm / den)
                if want_lse:
                    lses.append(lse)
            o_scr[cur_rows, :] = jnp.concatenate(outs, axis=0)
            if want_lse:
                lse_scr[cur_rows, :] = jnp.concatenate(lses, axis=0)
        o_ref[0, :, c * LANES:(c + 1) * LANES] = o_scr[...]
        if want_lse:
            lse_ref[0, :, c * LANES:(c + 1) * LANES] = lse_scr[...]


def _banded(q_arr, jq, kv_arr, jk, jv, kw, dil, maxdist, slopes, sink, want_lse, logit_bound):
    B, S, _ = q_arr.shape
    tb = min(BAND_TOKENS, S)
    pb = BAND * dil
    assert S % tb == 0 and tb % pb == 0 and maxdist <= BAND
    ratio = tb // pb
    has_sink = sink is not None
    kern = functools.partial(_banded_kernel, dil=dil, maxdist=maxdist, slopes=tuple(slopes),
                             kv_chunks=kw // LANES, has_sink=has_sink, want_lse=want_lse)
    prev = lambda j: pl.BlockSpec((1, pb, LANES), lambda b, i: (b, jnp.maximum(i * ratio - 1, 0), j))
    cur = lambda j: pl.BlockSpec((1, tb, LANES), lambda b, i: (b, i, j))
    in_specs = [cur(2 * jq), cur(2 * jq + 1)]
    args = [q_arr, q_arr]
    for c in range(kw // LANES):
        jkc, jvc = jk * (kw // LANES) + c, jv * (kw // LANES) + c
        in_specs += [prev(jkc), cur(jkc), prev(jvc), cur(jvc)]
        args += [kv_arr] * 4
    if has_sink:
        in_specs.insert(0, pl.BlockSpec(memory_space=pltpu.SMEM))
        args.insert(0, sink)
    n_out = 2 if want_lse else 1
    name = f"banded_d{dil}" if want_lse else "swa_attn"
    call = lambda bounded: pl.pallas_call(
        functools.partial(kern, bounded=bounded),
        grid=(B, S // tb),
        in_specs=in_specs,
        out_specs=[pl.BlockSpec((1, tb, 256), lambda b, i: (b, i, 0))] * n_out,
        out_shape=[jax.ShapeDtypeStruct((B, S, 256), F32)] * n_out,
        scratch_shapes=[pltpu.VMEM((tb, LANES), F32)] * n_out,
        compiler_params=_cparams(("parallel", "parallel")),
        name=name + ("_bounded" if bounded else "_rowmax"),
    )
    outs = lax.cond(logit_bound <= EXP2_SAFE, lambda *a: call(True)(*a), lambda *a: call(False)(*a), *args)
    return [o.reshape(B * S, 256) for o in outs]


def _wo_router_kernel(x_ref, oa_ref, ob0_ref, ob1_ref, ob2_ref, l0_ref, l1_ref, l2_ref, oc_ref, od_ref,
                      wo_ref, g2_ref, wr_ref, br_ref, ltri_ref, h_ref, xn_ref, route_ref, route_t_ref, cnt_ref,
                      carry_ref):
    i = pl.program_id(0)
    tm = x_ref.shape[0]

    @pl.when(i == 0)
    def _():
        carry_ref[...] = jnp.zeros_like(carry_ref)

    la, lb, lc = l0_ref[...], l1_ref[...], l2_ref[...]
    mx = jnp.maximum(jnp.maximum(la, lb), lc)
    ea, eb, ec = jnp.exp(la - mx), jnp.exp(lb - mx), jnp.exp(lc - mx)
    ob = (ea * ob0_ref[...] + eb * ob1_ref[...] + ec * ob2_ref[...]) / (ea + eb + ec)
    mix = jnp.concatenate([oa_ref[...], ob.astype(BF16), oc_ref[...], od_ref[...].astype(BF16)], axis=1)
    h = x_ref[...] + jnp.dot(mix, wo_ref[...], preferred_element_type=F32)
    h_ref[...] = h
    xn = h * lax.rsqrt(jnp.mean(h * h, axis=-1, keepdims=True) + NORM_EPS) * g2_ref[...]
    xn_ref[...] = xn

    xh = xn.astype(BF16)
    xl = (xn - xh.astype(F32)).astype(BF16)
    z = (jnp.dot(xh, wr_ref[0], preferred_element_type=F32) + jnp.dot(xl, wr_ref[0], preferred_element_type=F32)
         + jnp.dot(xh, wr_ref[1], preferred_element_type=F32)) + br_ref[...]
    lane = lax.broadcasted_iota(jnp.int32, (tm, LANES), 1)
    lanef = lane.astype(F32)
    big = float(LANES)
    zg = jnp.where((lane >= N_EXPERTS) & (lane < N_EXPERTS + N_GROUPS), z, NEG_INF)
    mg = jnp.max(zg, axis=1, keepdims=True)
    p_g = 1.0 / jnp.sum(jnp.exp(zg - mg), axis=1, keepdims=True)
    gsel = jnp.min(jnp.where(zg == mg, lanef, big), axis=1, keepdims=True) - float(N_EXPERTS)
    lo = gsel * float(EXPERTS_PER_GROUP)
    ze = jnp.where((lanef >= lo) & (lanef < lo + float(EXPERTS_PER_GROUP)), z, NEG_INF)
    m1 = jnp.max(ze, axis=1, keepdims=True)
    i1 = jnp.min(jnp.where(ze == m1, lanef, big), axis=1, keepdims=True)
    ze2 = jnp.where(lanef == i1, NEG_INF, ze)
    m2 = jnp.max(ze2, axis=1, keepdims=True)
    i2 = jnp.min(jnp.where(ze2 == m2, lanef, big), axis=1, keepdims=True)
    e2 = jnp.exp(m2 - m1)
    gate1 = p_g / (1.0 + e2)
    gate2 = p_g * e2 / (1.0 + e2)

    oh1 = jnp.where(lanef == i1, 1.0, 0.0)
    oh2 = jnp.where(lanef == i2, 1.0, 0.0)
    oh = oh1 + oh2
    before = carry_ref[0:1, :] + jnp.dot(ltri_ref[...], oh.astype(BF16), preferred_element_type=F32)
    r1 = jnp.sum(before * oh1, axis=1, keepdims=True)
    r2 = jnp.sum(before * oh2, axis=1, keepdims=True)
    total = carry_ref[0:1, :] + jnp.sum(oh, axis=0, keepdims=True)
    carry_ref[...] = jnp.broadcast_to(total, carry_ref.shape)
    cnt_ref[...] = jnp.broadcast_to(total, cnt_ref.shape)

    route = jnp.where(lane == 0, i1, jnp.where(lane == 1, i2, jnp.where(lane == 2, r1, jnp.where(
        lane == 3, r2, jnp.where(lane == 4, gate1, jnp.where(lane == 5, gate2, 0.0))))))
    route_ref[...] = route
    route_t_ref[...] = jnp.transpose(route)[0:8, :]


def _wo_router(xt, oa, obs, lses, oc, od, lw, ltri, tm=512):
    T = xt.shape[0]
    row = lambda w: pl.BlockSpec((tm, w), lambda i: (i, 0))
    return pl.pallas_call(
        _wo_router_kernel,
        grid=(T // tm,),
        in_specs=[row(D_MODEL), row(256), row(256), row(256), row(256), row(256), row(256), row(256), row(256),
                  row(256), _full((D_MODEL, D_MODEL)), _full((1, D_MODEL)), _full((2, D_MODEL, LANES)),
                  _full((1, LANES)), _full((tm, tm))],
        out_specs=[row(D_MODEL), row(D_MODEL), row(LANES), pl.BlockSpec((8, tm), lambda i: (0, i)), _full((8, LANES))],
        out_shape=[jax.ShapeDtypeStruct((T, D_MODEL), F32), jax.ShapeDtypeStruct((T, D_MODEL), F32),
                   jax.ShapeDtypeStruct((T, LANES), F32), jax.ShapeDtypeStruct((8, T), F32),
                   jax.ShapeDtypeStruct((8, LANES), F32)],
        scratch_shapes=[pltpu.VMEM((8, LANES), F32)],
        compiler_params=_cparams(("arbitrary",)),
        name="wo_router",
    )(xt, oa, obs[0], obs[1], obs[2], lses[0], lses[1], lses[2], oc, od, lw["wo"], lw["g2"], lw["wr"], lw["br"], ltri)


def _row_copy(src, src_row, dst, dst_row, sem):
    return pltpu.make_async_copy(src.at[pl.ds(src_row, 1)], dst.at[pl.ds(dst_row, 1)], sem)


def _tile_major(dest_t, tile):
    k, T = dest_t.shape
    return dest_t.reshape(k, T // tile, tile).transpose(1, 0, 2).reshape(-1)


def _scatter_kernel(dest_ref, x_ref, buf_out, sem, *, ts):
    def issue(g, carry):
        base = pl.multiple_of(g * SUBLANES, SUBLANES)
        for r in range(SUBLANES):
            for k in range(TOP_K):
                _row_copy(x_ref, base + r, buf_out, dest_ref[k * ts + base + r], sem).start()
        return carry

    lax.fori_loop(0, ts // SUBLANES, issue, 0)
    for k in range(TOP_K):
        pltpu.make_async_copy(x_ref, buf_out.at[pl.ds(0, ts)], sem).wait()


def _scatter_rows(xn, dest_t, ts=1024):
    T, W = xn.shape
    dest_flat = _tile_major(dest_t, ts)
    return pl.pallas_call(
        functools.partial(_scatter_kernel, ts=ts),
        grid=(T // ts,),
        in_specs=[pl.BlockSpec((TOP_K * ts,), lambda i: (i,), memory_space=pltpu.SMEM),
                  pl.BlockSpec((ts, W), lambda i: (i, 0))],
        out_specs=pl.BlockSpec(memory_space=pl.ANY),
        out_shape=jax.ShapeDtypeStruct((TOP_K * T, W), xn.dtype),
        scratch_shapes=[pltpu.SemaphoreType.DMA(())],
        compiler_params=_cparams(("arbitrary",)),
        name="moe_scatter",
    )(dest_flat, xn)


def _expert_items(counts, n_blocks):
    n_items_max = n_blocks + N_EXPERTS
    end = jnp.cumsum(counts)
    start = end - counts
    first_b = start // ROW_BLOCK
    per_expert = jnp.where(counts > 0, (end - 1) // ROW_BLOCK - first_b + 1, 0)
    item_end = jnp.cumsum(per_expert)
    item_start = item_end - per_expert
    n_items = item_end[-1:]
    idx = jnp.minimum(jnp.arange(n_items_max, dtype=jnp.int32), n_items - 1)
    e = jnp.minimum(jnp.sum((item_end[None, :] <= idx[:, None]).astype(jnp.int32), axis=1), N_EXPERTS - 1)
    onehot = (e[:, None] == jnp.arange(N_EXPERTS, dtype=jnp.int32)[None, :]).astype(jnp.int32)
    pick = lambda table: jnp.sum(onehot * table[None, :], axis=1)
    b = pick(first_b) + idx - pick(item_start)
    lo = jnp.maximum(pick(start) - b * ROW_BLOCK, 0)
    hi = jnp.minimum(pick(end) - b * ROW_BLOCK, ROW_BLOCK)
    return e, b, lo, hi, n_items.astype(jnp.int32), start


def _expert_kernel(e_ref, b_ref, lo_ref, hi_ref, n_ref, x_ref, w1_ref, w3_ref, w2_ref, y_ref):
    del e_ref
    i = pl.program_id(0)

    @pl.when(i < n_ref[0])
    def _():
        x = x_ref[...].astype(BF16)
        a = jnp.dot(x, w1_ref[0, 0].astype(BF16), preferred_element_type=F32)
        b = jnp.dot(x, w3_ref[0, 0].astype(BF16), preferred_element_type=F32)
        hid = (a / (1.0 + jnp.exp(-a))) * b
        y = jnp.dot(hid.astype(BF16), w2_ref[0, 0].astype(BF16), preferred_element_type=F32)
        row = lax.broadcasted_iota(jnp.int32, y.shape, 0)
        y = jnp.where((row >= lo_ref[i]) & (row < hi_ref[i]), y, 0.0)
        first_visit = jnp.logical_or(i == 0, b_ref[i] != b_ref[jnp.maximum(i - 1, 0)])

        @pl.when(first_visit)
        def _():
            y_ref[...] = y

        @pl.when(jnp.logical_not(first_visit))
        def _():
            y_ref[...] += y


def _expert_mlp(xbuf, items, layer, w1, w3, w2):
    P, D = xbuf.shape
    e, b, lo, hi, n_items = items
    x_map = lambda i, e_, b_, lo_, hi_, n_: (b_[i], 0)
    w_map = lambda i, e_, b_, lo_, hi_, n_: (layer, e_[i], 0, 0)
    grid_spec = pltpu.PrefetchScalarGridSpec(
        num_scalar_prefetch=5,
        grid=(e.shape[0],),
        in_specs=[pl.BlockSpec((ROW_BLOCK, D), x_map),
                  pl.BlockSpec((1, 1, D, D_EXPERT), w_map),
                  pl.BlockSpec((1, 1, D, D_EXPERT), w_map),
                  pl.BlockSpec((1, 1, D_EXPERT, D), w_map)],
        out_specs=pl.BlockSpec((ROW_BLOCK, D), x_map),
    )
    return pl.pallas_call(
        _expert_kernel,
        grid_spec=grid_spec,
        out_shape=jax.ShapeDtypeStruct((P, D), xbuf.dtype),
        compiler_params=_cparams(("arbitrary",)),
        name="moe_experts",
    )(e, b, lo, hi, n_items, xbuf, w1, w3, w2)


def _combine_kernel(dest_ref, dest_next_ref, h_ref, route_ref, y_hbm, o_ref, rows_scr, sems, *, tc):
    i = pl.program_id(0)
    n = pl.num_programs(0)
    slot = i % 2

    def gather(d_ref, s):
        def issue(g, carry):
            base = pl.multiple_of(g * SUBLANES, SUBLANES)
            for r in range(SUBLANES):
                for k in range(TOP_K):
                    _row_copy(y_hbm, d_ref[k * tc + base + r], rows_scr.at[s, k], base + r, sems.at[s]).start()
            return carry

        lax.fori_loop(0, tc // SUBLANES, issue, 0)

    @pl.when(i == 0)
    def _():
        gather(dest_ref, 0)

    @pl.when(i + 1 < n)
    def _():
        gather(dest_next_ref, 1 - slot)

    for k in range(TOP_K):
        pltpu.make_async_copy(y_hbm.at[pl.ds(0, tc)], rows_scr.at[slot, k], sems.at[slot]).wait()
    route = route_ref[...]
    o_ref[...] = h_ref[...] + route[:, 4:5] * rows_scr[slot, 0] + route[:, 5:6] * rows_scr[slot, 1]


def _combine(h, route, ybuf, dest_t, tc=256):
    T, D = h.shape
    n = T // tc
    dest_flat = _tile_major(dest_t, tc)
    return pl.pallas_call(
        functools.partial(_combine_kernel, tc=tc),
        grid=(n,),
        in_specs=[pl.BlockSpec((TOP_K * tc,), lambda i: (i,), memory_space=pltpu.SMEM),
                  pl.BlockSpec((TOP_K * tc,), lambda i: (jnp.minimum(i + 1, n - 1),), memory_space=pltpu.SMEM),
                  pl.BlockSpec((tc, D), lambda i: (i, 0)), pl.BlockSpec((tc, LANES), lambda i: (i, 0)),
                  pl.BlockSpec(memory_space=pl.ANY)],
        out_specs=pl.BlockSpec((tc, D), lambda i: (i, 0)),
        out_shape=jax.ShapeDtypeStruct((T, D), F32),
        scratch_shapes=[pltpu.VMEM((2, TOP_K, tc, D), ybuf.dtype), pltpu.SemaphoreType.DMA((2,))],
        compiler_params=_cparams(("arbitrary",)),
        name="moe_combine",
    )(dest_flat, dest_flat, h, route, ybuf)


def _prep_weights(norm1_g, w_in, mla_gcq, mla_gckv, mla_wuq, mla_wukv, qk_gq, qk_gk, fox_bf, sink, w_o,
                  norm2_g, w_rg, b_rg, w_re, b_re):
    L = w_in.shape[0]
    offs = np.concatenate([[0], np.cumsum(IN_SPLITS)])
    cq, ckv, kr, pb, pc, fg, pdq, pdkv = [w_in[:, :, offs[j]:offs[j + 1]] for j in range(8)]
    z = lambda *shape: jnp.zeros((L,) + shape, F32)
    perm = np.array([0, 2, 1, 3])
    ublock = jnp.concatenate([cq, fg, z(D_MODEL, 256 - MLA_Q_RANK - HEADS)], axis=2)
    kr_rep = jnp.concatenate([z(D_MODEL, MLA_NOPE), kr] * HEADS, axis=2)
    dq = pdq.reshape(L, D_MODEL, HEADS, HEAD_DIM)[:, :, perm].reshape(L, D_MODEL, GROUP_W)
    wp = jnp.concatenate([ckv, ublock, kr_rep, pb, pc, dq, pdkv], axis=2).astype(BF16)

    wuq = jnp.concatenate([mla_wuq, z(256 - MLA_Q_RANK, GROUP_W)], axis=1).astype(BF16)
    wukv = mla_wukv.reshape(L, MLA_KV_RANK, HEADS, MLA_NOPE + HEAD_DIM)
    wuk = jnp.concatenate([wukv[..., :MLA_NOPE], z(MLA_KV_RANK, HEADS, MLA_ROPE)], axis=3)
    wuk = wuk.reshape(L, MLA_KV_RANK, GROUP_W).astype(BF16)
    wuv = wukv[..., MLA_NOPE:].reshape(L, MLA_KV_RANK, GROUP_W).astype(BF16)
    gcq = jnp.concatenate([mla_gcq, z(256 - MLA_Q_RANK)], axis=1)[:, None, :]
    scale = HEAD_DIM ** -0.5
    gq = jnp.tile(qk_gq, (1, 1, HEADS)) * (scale * LOG2E)
    gk = jnp.tile(qk_gk, (1, 1, HEADS))
    bound = HEAD_DIM * jnp.max(jnp.abs(gq), axis=2) * jnp.max(jnp.abs(gk), axis=2)
    fb = z(1, LANES).at[:, 0, FG_LANE:FG_LANE + HEADS].set(fox_bf)
    wo_d = w_o[:, 3 * GROUP_W:].reshape(L, HEADS, HEAD_DIM, D_MODEL)[:, perm].reshape(L, GROUP_W, D_MODEL)
    wo = jnp.concatenate([w_o[:, :3 * GROUP_W], wo_d], axis=1).astype(BF16)
    wr = jnp.concatenate([w_re, w_rg, z(D_MODEL, LANES - N_EXPERTS - N_GROUPS)], axis=2)
    wr_hi = wr.astype(BF16)
    wr = jnp.stack([wr_hi, (wr - wr_hi.astype(F32)).astype(BF16)], axis=1)
    br = jnp.concatenate([b_re, b_rg, z(LANES - N_EXPERTS - N_GROUPS)], axis=1)[:, None, :]
    return dict(g1=norm1_g[:, None, :], wp=wp, gcq=gcq, gckv=mla_gckv[:, None, :], wuq=wuq, wuk=wuk, wuv=wuv,
                gq=gq, gk=gk, bound=bound, fb=fb, sink=sink[:, perm], wo=wo, g2=norm2_g[:, None, :], wr=wr, br=br)


def _tables(seq_len, tm):
    half = MLA_ROPE // 2
    inv = ROPE_BASE ** (-jnp.arange(0, MLA_ROPE, 2, dtype=F32) / MLA_ROPE)
    ang = jnp.arange(seq_len, dtype=F32)[:, None] * inv[None, :]
    cos, sin = jnp.cos(ang), jnp.sin(ang)
    one = jnp.ones((seq_len, MLA_NOPE), F32)
    zn = jnp.zeros((seq_len, MLA_NOPE), F32)
    zh = jnp.zeros((seq_len, half), F32)
    rc = jnp.concatenate([one, cos, cos] * 2, axis=1)
    rsa = jnp.concatenate([zn, -sin, zh] * 2, axis=1)
    rsb = jnp.concatenate([zn, zh, sin] * 2, axis=1)
    seg = np.arange(256) // HEAD_DIM
    e = jnp.asarray(seg[:, None] == seg[None, :], BF16)
    idx = np.arange(tm)
    tril = jnp.asarray(idx[None, :] <= idx[:, None], BF16)
    ltri = jnp.asarray(idx[None, :] < idx[:, None], BF16)
    return dict(rc=rc, rsa=rsa, rsb=rsb, e=e, tril=tril, ltri=ltri)


def _alibi():
    n = 2 * HEADS
    s = [2.0 ** (-8.0 * i / n) for i in range(1, n + 1)]
    return s[HEADS:], s[:HEADS]


def kernel(x, norm1_g, w_in, mla_gcq, mla_gckv, mla_wuq, mla_wukv, qk_gq, qk_gk, fox_bf, sink, w_o,
           norm2_g, w_rg, b_rg, w_re, b_re, w1, w3, w2):
    B, S, D = x.shape
    T = B * S
    depth = w_in.shape[0]
    tm = 512
    tabs = _tables(S, tm)
    slopes_dil, slopes_swa = _alibi()
    slopes_swa_perm = [slopes_swa[0], slopes_swa[2], slopes_swa[1], slopes_swa[3]]
    n_assign = T * TOP_K
    assert n_assign % ROW_BLOCK == 0

    all_w = _prep_weights(norm1_g, w_in, mla_gcq, mla_gckv, mla_wuq, mla_wukv, qk_gq, qk_gk, fox_bf, sink,
                          w_o, norm2_g, w_rg, b_rg, w_re, b_re)
    xt = x.reshape(T, D)
    for l in range(depth):
        lw = {name: arr[l] for name, arr in all_w.items()}
        qkv, qkvb, vt, cum = _inproj(xt, S, lw, tabs, tm=tm)
        qkv3 = qkv.reshape(B, S, QKV_COLS)
        qkvb3 = qkvb.reshape(B, S, QKVB_COLS)
        cum8 = jnp.pad(cum[:, FG_LANE:FG_LANE + HEADS].reshape(B, S, 2, 2).transpose(0, 2, 3, 1),
                       ((0, 0), (0, 0), (0, 6), (0, 0)))
        oa = _dense_attn(qkv3, vt, 0, None, None, QA, KA, VA, lw["bound"][0]).reshape(T, GROUP_W)
        oc = _dense_attn(qkv3, vt, 2, cum.reshape(B, S, LANES), cum8, QC, KC, VC, lw["bound"][2]).reshape(T, GROUP_W)
        obs, lses = [], []
        for window, dil in DILATED_PAIRS:
            o, lse = _banded(qkvb3, 0, qkvb3, 1, 2, 256, dil, window // dil, slopes_dil, None, True, lw["bound"][1])
            obs.append(o)
            lses.append(lse)
        swa_bound = jnp.maximum(lw["bound"][3], jnp.max(jnp.abs(lw["sink"])) * LOG2E)
        (od,) = _banded(qkv3, QD, qkv3, 2 * KVD, 2 * KVD + 1, 128, 1, SWA_WINDOW - 1, slopes_swa_perm,
                        lw["sink"], False, swa_bound)
        h, xn, route, route_t, cnt = _wo_router(xt, oa, obs, lses, oc, od, lw, tabs["ltri"], tm=tm)

        counts = cnt[0, :N_EXPERTS].astype(jnp.int32)
        *items, start = _expert_items(counts, n_assign // ROW_BLOCK)
        expert = route_t[0:TOP_K].astype(jnp.int32)
        rank = route_t[TOP_K:2 * TOP_K].astype(jnp.int32)
        hit = expert[None] == jnp.arange(N_EXPERTS, dtype=jnp.int32)[:, None, None]
        dest = rank + jnp.sum(jnp.where(hit, start[:, None, None], 0), axis=0)

        xbuf = _scatter_rows(xn, dest)
        ybuf = _expert_mlp(xbuf, items, l, w1, w3, w2)
        xt = _combine(h, route, ybuf, dest)
    return xt.reshape(B, S, D)
```

```python
import functools

import numpy as np
import jax
import jax.numpy as jnp
from jax import lax
from jax.experimental import pallas as pl
from jax.experimental.pallas import tpu as pltpu

F32 = jnp.float32
BF16 = jnp.bfloat16

D_MODEL = 1024
HEAD_DIM = 64
HEADS = 4
GROUP_W = HEADS * HEAD_DIM
NORM_EPS = 1e-6
MLA_Q_RANK, MLA_KV_RANK, MLA_NOPE, MLA_ROPE = 192, 128, 32, 32
ROPE_BASE = 10000.0
DILATED_PAIRS = ((128, 1), (512, 4), (2048, 16))
SWA_WINDOW = 128
N_GROUPS, EXPERTS_PER_GROUP, N_EXPERTS, TOP_K, D_EXPERT = 4, 8, 32, 2, 256
IN_SPLITS = (MLA_Q_RANK, MLA_KV_RANK, MLA_ROPE, 3 * GROUP_W, 3 * GROUP_W, HEADS, GROUP_W, 2 * 2 * HEAD_DIM)

LANES = 128
SUBLANES = 8
W_COLS = 2688
QKV_COLS = 2048
QA, KA, VA, QC, KC, VC, QD, KVD = range(8)
QKVB_COLS = 768
ONES_ROWS = 16
BAND = 128
BAND_TOKENS = 2048
FG_LANE = 64
ROW_BLOCK = 512
VMEM_LIMIT = 56 * 1024 * 1024

NEG_INF = float("-inf")
LOG2E = 1.4426950408889634
LN2 = 0.6931471805599453
EXP2_SAFE = 60.0


def _cparams(sem):
    return pltpu.CompilerParams(dimension_semantics=sem, vmem_limit_bytes=VMEM_LIMIT)


def _full(shape):
    zeros = (0,) * len(shape)
    return pl.BlockSpec(shape, lambda *_: zeros)


def _head_norm(y, g, e):
    ss = jnp.dot((y * y).astype(BF16), e, preferred_element_type=F32)
    return y * lax.rsqrt(ss * (1.0 / HEAD_DIM) + NORM_EPS) * g


def _rope(y, rc, rsa, rsb):
    outs = []
    for c in range(y.shape[1] // LANES):
        yc = y[:, c * LANES:(c + 1) * LANES]
        outs.append(yc * rc + pltpu.roll(yc, LANES - 16, 1) * rsa + pltpu.roll(yc, 16, 1) * rsb)
    return jnp.concatenate(outs, axis=1)


def _inproj_kernel(x_ref, g1_ref, w_ref, gcq_ref, gckv_ref, wuq_ref, wuk_ref, wuv_ref, gq_ref, gk_ref, e_ref,
                   rc_ref, rsa_ref, rsb_ref, fb_ref, tril_ref, qkv_ref, qkvb_ref, vt_ref, cum_ref, cumt_ref, acc_ref,
                   carry_ref,
                   *, tiles_per_seq):
    i = pl.program_id(0)
    tm = x_ref.shape[0]

    @pl.when(i % tiles_per_seq == 0)
    def _():
        carry_ref[...] = jnp.zeros_like(carry_ref)

    x = x_ref[...]
    ms = jnp.mean(x * x, axis=-1, keepdims=True)
    xn = (x * lax.rsqrt(ms + NORM_EPS) * g1_ref[...]).astype(BF16)
    acc_ref[...] = jnp.dot(xn, w_ref[...], preferred_element_type=F32)

    e = e_ref[...]
    rc, rsa, rsb = rc_ref[...], rsa_ref[...], rsb_ref[...]

    ckv = acc_ref[:, 0:128]
    ckvn = (ckv * lax.rsqrt(jnp.mean(ckv * ckv, axis=-1, keepdims=True) + NORM_EPS) * gckv_ref[...]).astype(BF16)
    u = acc_ref[:, 128:384]
    lane256 = lax.broadcasted_iota(jnp.int32, (tm, 256), 1)
    ssq = jnp.sum(jnp.where(lane256 < MLA_Q_RANK, u * u, 0.0), axis=-1, keepdims=True) * (1.0 / MLA_Q_RANK)
    un = (u * lax.rsqrt(ssq + NORM_EPS) * gcq_ref[...]).astype(BF16)
    qa = jnp.dot(un, wuq_ref[...], preferred_element_type=F32)
    ka = jnp.dot(ckvn, wuk_ref[...], preferred_element_type=F32) + acc_ref[:, 384:640]
    va = jnp.dot(ckvn, wuv_ref[...], preferred_element_type=F32)
    qa = _rope(_head_norm(qa, gq_ref[0:1, :], e), rc, rsa, rsb)
    ka = _rope(_head_norm(ka, gk_ref[0:1, :], e), rc, rsa, rsb)
    qkv_ref[:, QA * 256:(QA + 1) * 256] = qa.astype(BF16)
    qkv_ref[:, KA * 256:(KA + 1) * 256] = ka.astype(BF16)
    qkv_ref[:, VA * 256:(VA + 1) * 256] = va.astype(BF16)
    vt_ref[0:2] = jnp.transpose(va).astype(BF16).reshape(2, LANES, tm)
    vt_ref[2:4] = jnp.transpose(acc_ref[:, 1920:2176]).astype(BF16).reshape(2, LANES, tm)

    qkvb_ref[:, 0:256] = _head_norm(acc_ref[:, 640:896], gq_ref[1:2, :], e)
    qkvb_ref[:, 256:512] = _head_norm(acc_ref[:, 896:1152], gk_ref[1:2, :], e)
    qkvb_ref[:, 512:768] = acc_ref[:, 1152:1408]
    qkv_ref[:, QC * 256:(QC + 1) * 256] = _head_norm(acc_ref[:, 1408:1664], gq_ref[2:3, :], e).astype(BF16)
    qkv_ref[:, KC * 256:(KC + 1) * 256] = _head_norm(acc_ref[:, 1664:1920], gk_ref[2:3, :], e).astype(BF16)
    qkv_ref[:, VC * 256:(VC + 1) * 256] = acc_ref[:, 1920:2176].astype(BF16)

    qd = _head_norm(acc_ref[:, 2176:2432], gq_ref[3:4, :], e)
    kd = _head_norm(acc_ref[:, 2432:2560], gk_ref[3:4, 0:128], e[0:128, 0:128])
    qkv_ref[:, QD * 256:(QD + 1) * 256] = qd.astype(BF16)
    qkv_ref[:, KVD * 256:KVD * 256 + 128] = kd.astype(BF16)
    qkv_ref[:, KVD * 256 + 128:(KVD + 1) * 256] = acc_ref[:, 2560:2688].astype(BF16)

    z = u[:, 128:256] + fb_ref[...]
    ls = jnp.minimum(z, 0.0) - jnp.log(1.0 + jnp.exp(-jnp.abs(z)))
    lane128 = lax.broadcasted_iota(jnp.int32, (tm, LANES), 1)
    ls = jnp.where((lane128 >= FG_LANE) & (lane128 < FG_LANE + HEADS), ls, 0.0)
    hi = ls.astype(BF16)
    r1 = ls - hi.astype(F32)
    mid = r1.astype(BF16)
    lo = (r1 - mid.astype(F32)).astype(BF16)
    tril = tril_ref[...]
    local = (jnp.dot(tril, hi, preferred_element_type=F32) + jnp.dot(tril, mid, preferred_element_type=F32)
             + jnp.dot(tril, lo, preferred_element_type=F32))
    cum = local + carry_ref[0:1, :]
    cum_ref[...] = cum
    cum_t = jnp.transpose(cum)
    pad = jnp.zeros((SUBLANES - 2, tm), F32)
    for p in range(2):
        cumt_ref[p] = jnp.concatenate([cum_t[FG_LANE + 2 * p:FG_LANE + 2 * p + 2, :], pad], axis=0)
    carry_ref[...] = jnp.broadcast_to(cum[tm - 1:tm, :], carry_ref.shape)


def _inproj(xt, seq_len, lw, tabs, tm=512):
    T = xt.shape[0]
    tps = seq_len // tm
    kern = functools.partial(_inproj_kernel, tiles_per_seq=tps)
    tab_spec = pl.BlockSpec((tm, LANES), lambda i: (i % tps, 0))
    return pl.pallas_call(
        kern,
        grid=(T // tm,),
        in_specs=[pl.BlockSpec((tm, D_MODEL), lambda i: (i, 0)),
                  _full((1, D_MODEL)), _full((D_MODEL, W_COLS)), _full((1, 256)), _full((1, 128)),
                  _full((256, 256)), _full((128, 256)), _full((128, 256)), _full((4, 256)), _full((4, 256)),
                  _full((256, 256)), tab_spec, tab_spec, tab_spec, _full((1, LANES)), _full((tm, tm))],
        out_specs=[pl.BlockSpec((tm, QKV_COLS), lambda i: (i, 0)), pl.BlockSpec((tm, QKVB_COLS), lambda i: (i, 0)),
                   pl.BlockSpec((4, LANES, tm), lambda i: (0, 0, i)), pl.BlockSpec((tm, LANES), lambda i: (i, 0)),
                   pl.BlockSpec((2, SUBLANES, tm), lambda i: (0, 0, i))],
        out_shape=[jax.ShapeDtypeStruct((T, QKV_COLS), BF16), jax.ShapeDtypeStruct((T, QKVB_COLS), F32),
                   jax.ShapeDtypeStruct((4, LANES, T), BF16), jax.ShapeDtypeStruct((T, LANES), F32),
                   jax.ShapeDtypeStruct((2, SUBLANES, T), F32)],
        scratch_shapes=[pltpu.VMEM((tm, W_COLS), F32), pltpu.VMEM((8, LANES), F32)],
        compiler_params=_cparams(("arbitrary",)),
        name="inproj",
    )(xt, lw["g1"], lw["wp"], lw["gcq"], lw["gckv"], lw["wuq"], lw["wuk"], lw["wuv"], lw["gq"], lw["gk"],
      tabs["e"], tabs["rc"], tabs["rsa"], tabs["rsb"], lw["fb"], tabs["tril"])


def _split_heads(x):
    half0 = lax.broadcasted_iota(jnp.int32, x.shape, 1) < HEAD_DIM
    zero = jnp.zeros_like(x)
    return jnp.where(half0, x, zero), jnp.where(half0, zero, x)


def _causal_mask(s):
    row = lax.broadcasted_iota(jnp.int32, s.shape, 0)
    col = lax.broadcasted_iota(jnp.int32, s.shape, 1)
    return jnp.where(col <= row, s, NEG_INF)


def _cum_column(blk, h):
    lane = lax.broadcasted_iota(jnp.int32, blk.shape, 1)
    return jnp.sum(jnp.where(lane == FG_LANE + 2 * pl.program_id(1) + h, blk * LOG2E, 0.0), axis=1, keepdims=True)


def _row_cum(cq_ref, h):
    return _cum_column(cq_ref[0], h)


def _dense_bounded_kernel(*refs, tq, tk, fox):
    if fox:
        q_ref, k_ref, vt_ref, ccol_ref, crow_ref, o_ref, acc_scr = refs
    else:
        q_ref, k_ref, vt_ref, o_ref, acc_scr = refs
    qi = pl.program_id(2)
    qh = _split_heads(q_ref[0])
    ones = jnp.ones((ONES_ROWS, tk), BF16)
    acc_scr[...] = jnp.zeros_like(acc_scr)

    def step(start, c0, masked):
        k = k_ref[0, pl.ds(start, tk), :]
        for h in range(2):
            s = lax.dot_general(k, qh[h][c0:, :], (((1,), (1,)), ((), ())), preferred_element_type=F32)
            if fox:
                crow = crow_ref[0, h:h + 1, pl.ds(pl.multiple_of(qi * tq + c0, tk), tq - c0)] * LOG2E
                s = (s + crow) - _cum_column(ccol_ref[0, pl.ds(start, tk), :], h)
            if masked:
                key = lax.broadcasted_iota(jnp.int32, s.shape, 0)
                query = lax.broadcasted_iota(jnp.int32, s.shape, 1)
                s = jnp.where(key <= query, s, NEG_INF)
            lhs = jnp.concatenate([vt_ref[0, h * HEAD_DIM:(h + 1) * HEAD_DIM, pl.ds(start, tk)], ones], axis=0)
            acc_scr[h, :, c0:] += jnp.dot(lhs, jnp.exp2(s).astype(BF16), preferred_element_type=F32)

    def body(j, carry):
        step(pl.multiple_of(j * tk, tk), 0, False)
        return carry

    lax.fori_loop(0, qi * (tq // tk), body, 0)
    for d in range(tq // tk):
        step(pl.multiple_of(qi * tq + d * tk, tk), d * tk, True)
    o_t = jnp.concatenate([acc_scr[h, 0:HEAD_DIM, :] / acc_scr[h, HEAD_DIM:HEAD_DIM + 1, :] for h in range(2)], axis=0)
    o_ref[0] = jnp.transpose(o_t).astype(o_ref.dtype)


def _dense_online_kernel(*refs, tq, tk, fox):
    if fox:
        q_ref, k_ref, v_ref, cq_ref, ck_ref, o_ref, m_scr, l_scr, acc_scr = refs
    else:
        q_ref, k_ref, v_ref, o_ref, m_scr, l_scr, acc_scr = refs
    qi = pl.program_id(2)
    qh = _split_heads(q_ref[0])
    m_scr[...] = jnp.full(m_scr.shape, NEG_INF, F32)
    l_scr[...] = jnp.zeros_like(l_scr)
    acc_scr[...] = jnp.zeros_like(acc_scr)
    if fox:
        cq = [_row_cum(cq_ref, h) for h in range(2)]

    def step(start, r0, masked):
        k = k_ref[0, pl.ds(start, tk), :]
        v = v_ref[0, pl.ds(start, tk), :]
        for h in range(2):
            s = lax.dot_general(qh[h][r0:, :], k, (((1,), (1,)), ((), ())), preferred_element_type=F32)
            if fox:
                s = (s + cq[h][r0:, :]) - ck_ref[0, h:h + 1, pl.ds(start, tk)] * LOG2E
            if masked:
                s = _causal_mask(s)
            m_prev = m_scr[h, r0:, :]
            m_next = jnp.maximum(m_prev, jnp.max(s, axis=1, keepdims=True))
            p = jnp.exp2(s - jnp.concatenate([m_next] * (tk // LANES), axis=1))
            alpha = jnp.exp2(m_prev - m_next)
            l_scr[h, r0:, :] = alpha * l_scr[h, r0:, :] + jnp.sum(p, axis=1, keepdims=True)
            m_scr[h, r0:, :] = m_next
            pv = jnp.dot(p.astype(BF16), v, preferred_element_type=F32)
            acc_scr[h, r0:, :] = acc_scr[h, r0:, :] * alpha + pv

    def body(j, carry):
        step(pl.multiple_of(j * tk, tk), 0, False)
        return carry

    lax.fori_loop(0, qi * (tq // tk), body, 0)
    for d in range(tq // tk):
        step(pl.multiple_of(qi * tq + d * tk, tk), d * tk, True)
    half0 = lax.broadcasted_iota(jnp.int32, (tq, LANES), 1) < HEAD_DIM
    o = jnp.where(half0, acc_scr[0] / l_scr[0], acc_scr[1] / l_scr[1])
    o_ref[0] = o.astype(o_ref.dtype)


def _dense_attn(qkv3, vt, jvt, cum3, cum8, jq, jk, jv, logit_bound, tq_bounded=4096, tq_online=2048, tk=512):
    B, S, _ = qkv3.shape
    fox = cum8 is not None
    k_spec = pl.BlockSpec((1, S, LANES), lambda b, p, i: (b, 0, 2 * jk + p))
    v_spec = pl.BlockSpec((1, S, LANES), lambda b, p, i: (b, 0, 2 * jv + p))
    vt_spec = pl.BlockSpec((1, LANES, S), lambda b, p, i: (jvt + p, 0, b))
    cum_row_spec = pl.BlockSpec((1, SUBLANES, S), lambda b, p, i: (p, 0, b))
    name = "fox_attn" if fox else "mla_attn"

    def build(kern, tq, suffix, kv_specs, cum_specs, scratch):
        tq = min(tq, S)
        assert S % tq == 0 and tq % tk == 0
        q_spec = pl.BlockSpec((1, tq, LANES), lambda b, p, i: (b, i, 2 * jq + p))
        return pl.pallas_call(
            functools.partial(kern, tq=tq, tk=tk, fox=fox),
            grid=(B, 2, S // tq),
            in_specs=[q_spec] + kv_specs + (cum_specs(tq) if fox else []),
            out_specs=pl.BlockSpec((1, tq, LANES), lambda b, p, i: (b, i, p)),
            out_shape=jax.ShapeDtypeStruct((B, S, GROUP_W), BF16),
            scratch_shapes=scratch(tq),
            compiler_params=_cparams(("parallel", "parallel", "arbitrary")),
            name=name + suffix)

    bounded = build(_dense_bounded_kernel, tq_bounded, "_bounded", [k_spec, vt_spec],
                    lambda tq: [pl.BlockSpec((1, S, LANES), lambda b, p, i: (b, 0, 0)), cum_row_spec],
                    lambda tq: [pltpu.VMEM((2, HEAD_DIM + ONES_ROWS, tq), F32)])
    online = build(_dense_online_kernel, tq_online, "_online", [k_spec, v_spec],
                   lambda tq: [pl.BlockSpec((1, tq, LANES), lambda b, p, i: (b, i, 0)), cum_row_spec],
                   lambda tq: [pltpu.VMEM((2, tq, LANES), F32)] * 3)
    cums = (cum3, cum8) if fox else ()
    return lax.cond(logit_bound <= EXP2_SAFE,
                    lambda vt_, *c: bounded(qkv3, qkv3, vt_, *c),
                    lambda vt_, *c: online(qkv3, qkv3, qkv3, *c), vt, *cums)


def _fold_rows(r, n, dil):
    return slice(r, r + n) if dil == 1 else pl.ds(r, n, stride=dil)


def _banded_kernel(*refs, dil, maxdist, slopes, kv_chunks, has_sink, want_lse, bounded):
    refs = list(refs)
    sink_ref = refs.pop(0) if has_sink else None
    q_refs = [refs.pop(0) for _ in range(2)]
    kv_refs = [[refs.pop(0) for _ in range(4)] for _ in range(kv_chunks)]
    o_ref = refs.pop(0)
    lse_ref = refs.pop(0) if want_lse else None
    o_scr = refs.pop(0)
    lse_scr = refs.pop(0) if want_lse else None
    nq = q_refs[0].shape[1] // dil
    row = lax.broadcasted_iota(jnp.int32, (BAND, 2 * BAND), 0)
    col = lax.broadcasted_iota(jnp.int32, (BAND, 2 * BAND), 1)
    dist = row + BAND - col
    valid = (dist >= 0) & (dist <= maxdist)
    distf = dist.astype(F32) * (float(dil) * LOG2E)
    bias = [jnp.where(valid, -slopes[h] * distf, NEG_INF) for h in range(HEADS)]
    no_prev = jnp.where(col < BAND, jnp.where(pl.program_id(1) == 0, NEG_INF, 0.0), 0.0)
    bias_first = [b + no_prev for b in bias]
    half0_k = lax.broadcasted_iota(jnp.int32, (2 * BAND, LANES), 1) < HEAD_DIM
    ones0 = jnp.where(half0_k, 1.0, 0.0).astype(BF16)
    ones1 = jnp.where(half0_k, 0.0, 1.0).astype(BF16)
    half0 = lax.broadcasted_iota(jnp.int32, (BAND, LANES), 1) < HEAD_DIM
    for c in range(2):
        kp_ref, kc_ref, vp_ref, vc_ref = kv_refs[c if kv_chunks == 2 else 0]
        if has_sink:
            sink_lanes = jnp.where(half0[0:1, :], sink_ref[2 * c], sink_ref[2 * c + 1])
        for r in range(dil):
            prev_rows, cur_rows = _fold_rows(r, BAND, dil), _fold_rows(r, nq, dil)
            qh = _split_heads(q_refs[c][0, cur_rows, :].astype(BF16))
            kf = jnp.concatenate([kp_ref[0, prev_rows, :], kc_ref[0, cur_rows, :]], axis=0).astype(BF16)
            vh = _split_heads(jnp.concatenate([vp_ref[0, prev_rows, :], vc_ref[0, cur_rows, :]], axis=0).astype(BF16))
            outs, lses = [], []
            for j in range(nq // BAND):
                keys = slice(j * BAND, (j + 2) * BAND)
                v2 = jnp.concatenate([jnp.concatenate([vh[0][keys], ones0], axis=1),
                                      jnp.concatenate([vh[1][keys], ones1], axis=1)], axis=0)
                ps, ms = [], []
                for e in range(2):
                    s = lax.dot_general(qh[e][j * BAND:(j + 1) * BAND], kf[keys], (((1,), (1,)), ((), ())),
                                        preferred_element_type=F32)
                    s = s + (bias_first if j == 0 else bias)[2 * c + e]
                    if bounded:
                        ps.append(jnp.exp2(s).astype(BF16))
                    else:
                        m = jnp.max(s, axis=1, keepdims=True)
                        ps.append(jnp.exp2(s - m).astype(BF16))
                        ms.append(m)
                acc = jnp.dot(jnp.concatenate(ps, axis=1), v2, preferred_element_type=F32)
                unnorm, den = acc[:, 0:LANES], acc[:, LANES:2 * LANES]
                if bounded:
                    lse = jnp.log(den)
                    outs.append(unnorm / (den + jnp.exp(sink_lanes)) if has_sink else unnorm / den)
                else:
                    shift = jnp.where(half0, ms[0], ms[1]) * LN2
                    lse = shift + jnp.log(den)
                    if has_sink:
                        mx = jnp.maximum(lse, sink_lanes)
                        total = mx + jnp.log(jnp.exp(lse - mx) + jnp.exp(sink_lanes - mx))
                        outs.append(unnorm * jnp.exp(shift - total))
                    else:
                        outs.append(unnorm / den)
                if want_lse:
                    lses.append(lse)
            o_scr[cur_rows, :] = jnp.concatenate(outs, axis=0)
            if want_lse:
                lse_scr[cur_rows, :] = jnp.concatenate(lses, axis=0)
        o_ref[0, :, c * LANES:(c + 1) * LANES] = o_scr[...]
        if want_lse:
            lse_ref[0, :, c * LANES:(c + 1) * LANES] = lse_scr[...]


def _banded(q_arr, jq, kv_arr, jk, jv, kw, dil, maxdist, slopes, sink, want_lse, logit_bound):
    B, S, _ = q_arr.shape
    tb = min(BAND_TOKENS, S)
    pb = BAND * dil
    assert S % tb == 0 and tb % pb == 0 and maxdist <= BAND
    ratio = tb // pb
    has_sink = sink is not None
    kern = functools.partial(_banded_kernel, dil=dil, maxdist=maxdist, slopes=tuple(slopes),
                             kv_chunks=kw // LANES, has_sink=has_sink, want_lse=want_lse)
    prev = lambda j: pl.BlockSpec((1, pb, LANES), lambda b, i: (b, jnp.maximum(i * ratio - 1, 0), j))
    cur = lambda j: pl.BlockSpec((1, tb, LANES), lambda b, i: (b, i, j))
    in_specs = [cur(2 * jq), cur(2 * jq + 1)]
    args = [q_arr, q_arr]
    for c in range(kw // LANES):
        jkc, jvc = jk * (kw // LANES) + c, jv * (kw // LANES) + c
        in_specs += [prev(jkc), cur(jkc), prev(jvc), cur(jvc)]
        args += [kv_arr] * 4
    if has_sink:
        in_specs.insert(0, pl.BlockSpec(memory_space=pltpu.SMEM))
        args.insert(0, sink)
    n_out = 2 if want_lse else 1
    name = f"banded_d{dil}" if want_lse else "swa_attn"
    call = lambda bounded: pl.pallas_call(
        functools.partial(kern, bounded=bounded),
        grid=(B, S // tb),
        in_specs=in_specs,
        out_specs=[pl.BlockSpec((1, tb, 256), lambda b, i: (b, i, 0))] * n_out,
        out_shape=[jax.ShapeDtypeStruct((B, S, 256), F32)] * n_out,
        scratch_shapes=[pltpu.VMEM((tb, LANES), F32)] * n_out,
        compiler_params=_cparams(("parallel", "parallel")),
        name=name + ("_bounded" if bounded else "_rowmax"),
    )
    outs = lax.cond(logit_bound <= EXP2_SAFE, lambda *a: call(True)(*a), lambda *a: call(False)(*a), *args)
    return [o.reshape(B * S, 256) for o in outs]


def _wo_router_kernel(x_ref, oa_ref, ob0_ref, ob1_ref, ob2_ref, l0_ref, l1_ref, l2_ref, oc_ref, od_ref,
                      wo_ref, g2_ref, wr_ref, br_ref, ltri_ref, h_ref, xn_ref, route_ref, route_t_ref, cnt_ref,
                      carry_ref):
    i = pl.program_id(0)
    tm = x_ref.shape[0]

    @pl.when(i == 0)
    def _():
        carry_ref[...] = jnp.zeros_like(carry_ref)

    la, lb, lc = l0_ref[...], l1_ref[...], l2_ref[...]
    mx = jnp.maximum(jnp.maximum(la, lb), lc)
    ea, eb, ec = jnp.exp(la - mx), jnp.exp(lb - mx), jnp.exp(lc - mx)
    ob = (ea * ob0_ref[...] + eb * ob1_ref[...] + ec * ob2_ref[...]) / (ea + eb + ec)
    mix = jnp.concatenate([oa_ref[...], ob.astype(BF16), oc_ref[...], od_ref[...].astype(BF16)], axis=1)
    h = x_ref[...] + jnp.dot(mix, wo_ref[...], preferred_element_type=F32)
    h_ref[...] = h
    xn = h * lax.rsqrt(jnp.mean(h * h, axis=-1, keepdims=True) + NORM_EPS) * g2_ref[...]
    xn_ref[...] = xn

    xh = xn.astype(BF16)
    xl = (xn - xh.astype(F32)).astype(BF16)
    z = (jnp.dot(xh, wr_ref[0], preferred_element_type=F32) + jnp.dot(xl, wr_ref[0], preferred_element_type=F32)
         + jnp.dot(xh, wr_ref[1], preferred_element_type=F32)) + br_ref[...]
    lane = lax.broadcasted_iota(jnp.int32, (tm, LANES), 1)
    lanef = lane.astype(F32)
    big = float(LANES)
    zg = jnp.where((lane >= N_EXPERTS) & (lane < N_EXPERTS + N_GROUPS), z, NEG_INF)
    mg = jnp.max(zg, axis=1, keepdims=True)
    p_g = 1.0 / jnp.sum(jnp.exp(zg - mg), axis=1, keepdims=True)
    gsel = jnp.min(jnp.where(zg == mg, lanef, big), axis=1, keepdims=True) - float(N_EXPERTS)
    lo = gsel * float(EXPERTS_PER_GROUP)
    ze = jnp.where((lanef >= lo) & (lanef < lo + float(EXPERTS_PER_GROUP)), z, NEG_INF)
    m1 = jnp.max(ze, axis=1, keepdims=True)
    i1 = jnp.min(jnp.where(ze == m1, lanef, big), axis=1, keepdims=True)
    ze2 = jnp.where(lanef == i1, NEG_INF, ze)
    m2 = jnp.max(ze2, axis=1, keepdims=True)
    i2 = jnp.min(jnp.where(ze2 == m2, lanef, big), axis=1, keepdims=True)
    e2 = jnp.exp(m2 - m1)
    gate1 = p_g / (1.0 + e2)
    gate2 = p_g * e2 / (1.0 + e2)

    oh1 = jnp.where(lanef == i1, 1.0, 0.0)
    oh2 = jnp.where(lanef == i2, 1.0, 0.0)
    oh = oh1 + oh2
    before = carry_ref[0:1, :] + jnp.dot(ltri_ref[...], oh.astype(BF16), preferred_element_type=F32)
    r1 = jnp.sum(before * oh1, axis=1, keepdims=True)
    r2 = jnp.sum(before * oh2, axis=1, keepdims=True)
    total = carry_ref[0:1, :] + jnp.sum(oh, axis=0, keepdims=True)
    carry_ref[...] = jnp.broadcast_to(total, carry_ref.shape)
    cnt_ref[...] = jnp.broadcast_to(total, cnt_ref.shape)

    route = jnp.where(lane == 0, i1, jnp.where(lane == 1, i2, jnp.where(lane == 2, r1, jnp.where(
        lane == 3, r2, jnp.where(lane == 4, gate1, jnp.where(lane == 5, gate2, 0.0))))))
    route_ref[...] = route
    route_t_ref[...] = jnp.transpose(route)[0:8, :]


def _wo_router(xt, oa, obs, lses, oc, od, lw, ltri, tm=512):
    T = xt.shape[0]
    row = lambda w: pl.BlockSpec((tm, w), lambda i: (i, 0))
    return pl.pallas_call(
        _wo_router_kernel,
        grid=(T // tm,),
        in_specs=[row(D_MODEL), row(256), row(256), row(256), row(256), row(256), row(256), row(256), row(256),
                  row(256), _full((D_MODEL, D_MODEL)), _full((1, D_MODEL)), _full((2, D_MODEL, LANES)),
                  _full((1, LANES)), _full((tm, tm))],
        out_specs=[row(D_MODEL), row(D_MODEL), row(LANES), pl.BlockSpec((8, tm), lambda i: (0, i)), _full((8, LANES))],
        out_shape=[jax.ShapeDtypeStruct((T, D_MODEL), F32), jax.ShapeDtypeStruct((T, D_MODEL), F32),
                   jax.ShapeDtypeStruct((T, LANES), F32), jax.ShapeDtypeStruct((8, T), F32),
                   jax.ShapeDtypeStruct((8, LANES), F32)],
        scratch_shapes=[pltpu.VMEM((8, LANES), F32)],
        compiler_params=_cparams(("arbitrary",)),
        name="wo_router",
    )(xt, oa, obs[0], obs[1], obs[2], lses[0], lses[1], lses[2], oc, od, lw["wo"], lw["g2"], lw["wr"], lw["br"], ltri)


def _row_copy(src, src_row, dst, dst_row, sem):
    return pltpu.make_async_copy(src.at[pl.ds(src_row, 1)], dst.at[pl.ds(dst_row, 1)], sem)


def _tile_major(dest_t, tile):
    k, T = dest_t.shape
    return dest_t.reshape(k, T // tile, tile).transpose(1, 0, 2).reshape(-1)


def _scatter_kernel(dest_ref, x_ref, buf_out, sem, *, ts):
    def issue(g, carry):
        base = pl.multiple_of(g * SUBLANES, SUBLANES)
        for r in range(SUBLANES):
            for k in range(TOP_K):
                _row_copy(x_ref, base + r, buf_out, dest_ref[k * ts + base + r], sem).start()
        return carry

    lax.fori_loop(0, ts // SUBLANES, issue, 0)
    for k in range(TOP_K):
        pltpu.make_async_copy(x_ref, buf_out.at[pl.ds(0, ts)], sem).wait()


def _scatter_rows(xn, dest_t, ts=1024):
    T, W = xn.shape
    dest_flat = _tile_major(dest_t, ts)
    return pl.pallas_call(
        functools.partial(_scatter_kernel, ts=ts),
        grid=(T // ts,),
        in_specs=[pl.BlockSpec((TOP_K * ts,), lambda i: (i,), memory_space=pltpu.SMEM),
                  pl.BlockSpec((ts, W), lambda i: (i, 0))],
        out_specs=pl.BlockSpec(memory_space=pl.ANY),
        out_shape=jax.ShapeDtypeStruct((TOP_K * T, W), xn.dtype),
        scratch_shapes=[pltpu.SemaphoreType.DMA(())],
        compiler_params=_cparams(("arbitrary",)),
        name="moe_scatter",
    )(dest_flat, xn)


def _expert_items(counts, n_blocks):
    n_items_max = n_blocks + N_EXPERTS
    end = jnp.cumsum(counts)
    start = end - counts
    first_b = start // ROW_BLOCK
    per_expert = jnp.where(counts > 0, (end - 1) // ROW_BLOCK - first_b + 1, 0)
    item_end = jnp.cumsum(per_expert)
    item_start = item_end - per_expert
    n_items = item_end[-1:]
    idx = jnp.minimum(jnp.arange(n_items_max, dtype=jnp.int32), n_items - 1)
    e = jnp.minimum(jnp.sum((item_end[None, :] <= idx[:, None]).astype(jnp.int32), axis=1), N_EXPERTS - 1)
    onehot = (e[:, None] == jnp.arange(N_EXPERTS, dtype=jnp.int32)[None, :]).astype(jnp.int32)
    pick = lambda table: jnp.sum(onehot * table[None, :], axis=1)
    b = pick(first_b) + idx - pick(item_start)
    lo = jnp.maximum(pick(start) - b * ROW_BLOCK, 0)
    hi = jnp.minimum(pick(end) - b * ROW_BLOCK, ROW_BLOCK)
    return e, b, lo, hi, n_items.astype(jnp.int32), start


def _expert_kernel(e_ref, b_ref, lo_ref, hi_ref, n_ref, x_ref, w1_ref, w3_ref, w2_ref, y_ref):
    del e_ref
    i = pl.program_id(0)

    @pl.when(i < n_ref[0])
    def _():
        x = x_ref[...].astype(BF16)
        a = jnp.dot(x, w1_ref[0, 0].astype(BF16), preferred_element_type=F32)
        b = jnp.dot(x, w3_ref[0, 0].astype(BF16), preferred_element_type=F32)
        hid = (a / (1.0 + jnp.exp(-a))) * b
        y = jnp.dot(hid.astype(BF16), w2_ref[0, 0].astype(BF16), preferred_element_type=F32)
        row = lax.broadcasted_iota(jnp.int32, y.shape, 0)
        y = jnp.where((row >= lo_ref[i]) & (row < hi_ref[i]), y, 0.0)
        first_visit = jnp.logical_or(i == 0, b_ref[i] != b_ref[jnp.maximum(i - 1, 0)])

        @pl.when(first_visit)
        def _():
            y_ref[...] = y

        @pl.when(jnp.logical_not(first_visit))
        def _():
            y_ref[...] += y


def _expert_mlp(xbuf, items, layer, w1, w3, w2):
    P, D = xbuf.shape
    e, b, lo, hi, n_items = items
    x_map = lambda i, e_, b_, lo_, hi_, n_: (b_[i], 0)
    w_map = lambda i, e_, b_, lo_, hi_, n_: (layer, e_[i], 0, 0)
    grid_spec = pltpu.PrefetchScalarGridSpec(
        num_scalar_prefetch=5,
        grid=(e.shape[0],),
        in_specs=[pl.BlockSpec((ROW_BLOCK, D), x_map),
                  pl.BlockSpec((1, 1, D, D_EXPERT), w_map),
                  pl.BlockSpec((1, 1, D, D_EXPERT), w_map),
                  pl.BlockSpec((1, 1, D_EXPERT, D), w_map)],
        out_specs=pl.BlockSpec((ROW_BLOCK, D), x_map),
    )
    return pl.pallas_call(
        _expert_kernel,
        grid_spec=grid_spec,
        out_shape=jax.ShapeDtypeStruct((P, D), xbuf.dtype),
        compiler_params=_cparams(("arbitrary",)),
        name="moe_experts",
    )(e, b, lo, hi, n_items, xbuf, w1, w3, w2)


def _combine_kernel(dest_ref, dest_next_ref, h_ref, route_ref, y_hbm, o_ref, rows_scr, sems, *, tc):
    i = pl.program_id(0)
    n = pl.num_programs(0)
    slot = i % 2

    def gather(d_ref, s):
        def issue(g, carry):
            base = pl.multiple_of(g * SUBLANES, SUBLANES)
            for r in range(SUBLANES):
                for k in range(TOP_K):
                    _row_copy(y_hbm, d_ref[k * tc + base + r], rows_scr.at[s, k], base + r, sems.at[s]).start()
            return carry

        lax.fori_loop(0, tc // SUBLANES, issue, 0)

    @pl.when(i == 0)
    def _():
        gather(dest_ref, 0)

    @pl.when(i + 1 < n)
    def _():
        gather(dest_next_ref, 1 - slot)

    for k in range(TOP_K):
        pltpu.make_async_copy(y_hbm.at[pl.ds(0, tc)], rows_scr.at[slot, k], sems.at[slot]).wait()
    route = route_ref[...]
    o_ref[...] = h_ref[...] + route[:, 4:5] * rows_scr[slot, 0] + route[:, 5:6] * rows_scr[slot, 1]


def _combine(h, route, ybuf, dest_t, tc=256):
    T, D = h.shape
    n = T // tc
    dest_flat = _tile_major(dest_t, tc)
    return pl.pallas_call(
        functools.partial(_combine_kernel, tc=tc),
        grid=(n,),
        in_specs=[pl.BlockSpec((TOP_K * tc,), lambda i: (i,), memory_space=pltpu.SMEM),
                  pl.BlockSpec((TOP_K * tc,), lambda i: (jnp.minimum(i + 1, n - 1),), memory_space=pltpu.SMEM),
                  pl.BlockSpec((tc, D), lambda i: (i, 0)), pl.BlockSpec((tc, LANES), lambda i: (i, 0)),
                  pl.BlockSpec(memory_space=pl.ANY)],
        out_specs=pl.BlockSpec((tc, D), lambda i: (i, 0)),
        out_shape=jax.ShapeDtypeStruct((T, D), F32),
        scratch_shapes=[pltpu.VMEM((2, TOP_K, tc, D), ybuf.dtype), pltpu.SemaphoreType.DMA((2,))],
        compiler_params=_cparams(("arbitrary",)),
        name="moe_combine",
    )(dest_flat, dest_flat, h, route, ybuf)


def _prep_weights(norm1_g, w_in, mla_gcq, mla_gckv, mla_wuq, mla_wukv, qk_gq, qk_gk, fox_bf, sink, w_o,
                  norm2_g, w_rg, b_rg, w_re, b_re):
    L = w_in.shape[0]
    offs = np.concatenate([[0], np.cumsum(IN_SPLITS)])
    cq, ckv, kr, pb, pc, fg, pdq, pdkv = [w_in[:, :, offs[j]:offs[j + 1]] for j in range(8)]
    z = lambda *shape: jnp.zeros((L,) + shape, F32)
    perm = np.array([0, 2, 1, 3])
    ublock = jnp.concatenate([cq, fg, z(D_MODEL, 256 - MLA_Q_RANK - HEADS)], axis=2)
    kr_rep = jnp.concatenate([z(D_MODEL, MLA_NOPE), kr] * HEADS, axis=2)
    dq = pdq.reshape(L, D_MODEL, HEADS, HEAD_DIM)[:, :, perm].reshape(L, D_MODEL, GROUP_W)
    wp = jnp.concatenate([ckv, ublock, kr_rep, pb, pc, dq, pdkv], axis=2).astype(BF16)

    wuq = jnp.concatenate([mla_wuq, z(256 - MLA_Q_RANK, GROUP_W)], axis=1).astype(BF16)
    wukv = mla_wukv.reshape(L, MLA_KV_RANK, HEADS, MLA_NOPE + HEAD_DIM)
    wuk = jnp.concatenate([wukv[..., :MLA_NOPE], z(MLA_KV_RANK, HEADS, MLA_ROPE)], axis=3)
    wuk = wuk.reshape(L, MLA_KV_RANK, GROUP_W).astype(BF16)
    wuv = wukv[..., MLA_NOPE:].reshape(L, MLA_KV_RANK, GROUP_W).astype(BF16)
    gcq = jnp.concatenate([mla_gcq, z(256 - MLA_Q_RANK)], axis=1)[:, None, :]
    scale = HEAD_DIM ** -0.5
    gq = jnp.tile(qk_gq, (1, 1, HEADS)) * (scale * LOG2E)
    gk = jnp.tile(qk_gk, (1, 1, HEADS))
    bound = HEAD_DIM * jnp.max(jnp.abs(gq), axis=2) * jnp.max(jnp.abs(gk), axis=2)
    fb = z(1, LANES).at[:, 0, FG_LANE:FG_LANE + HEADS].set(fox_bf)
    wo_d = w_o[:, 3 * GROUP_W:].reshape(L, HEADS, HEAD_DIM, D_MODEL)[:, perm].reshape(L, GROUP_W, D_MODEL)
    wo = jnp.concatenate([w_o[:, :3 * GROUP_W], wo_d], axis=1).astype(BF16)
    wr = jnp.concatenate([w_re, w_rg, z(D_MODEL, LANES - N_EXPERTS - N_GROUPS)], axis=2)
    wr_hi = wr.astype(BF16)
    wr = jnp.stack([wr_hi, (wr - wr_hi.astype(F32)).astype(BF16)], axis=1)
    br = jnp.concatenate([b_re, b_rg, z(LANES - N_EXPERTS - N_GROUPS)], axis=1)[:, None, :]
    return dict(g1=norm1_g[:, None, :], wp=wp, gcq=gcq, gckv=mla_gckv[:, None, :], wuq=wuq, wuk=wuk, wuv=wuv,
                gq=gq, gk=gk, bound=bound, fb=fb, sink=sink[:, perm], wo=wo, g2=norm2_g[:, None, :], wr=wr, br=br)


def _tables(seq_len, tm):
    half = MLA_ROPE // 2
    inv = ROPE_BASE ** (-jnp.arange(0, MLA_ROPE, 2, dtype=F32) / MLA_ROPE)
    ang = jnp.arange(seq_len, dtype=F32)[:, None] * inv[None, :]
    cos, sin = jnp.cos(ang), jnp.sin(ang)
    one = jnp.ones((seq_len, MLA_NOPE), F32)
    zn = jnp.zeros((seq_len, MLA_NOPE), F32)
    zh = jnp.zeros((seq_len, half), F32)
    rc = jnp.concatenate([one, cos, cos] * 2, axis=1)
    rsa = jnp.concatenate([zn, -sin, zh] * 2, axis=1)
    rsb = jnp.concatenate([zn, zh, sin] * 2, axis=1)
    seg = np.arange(256) // HEAD_DIM
    e = jnp.asarray(seg[:, None] == seg[None, :], BF16)
    idx = np.arange(tm)
    tril = jnp.asarray(idx[None, :] <= idx[:, None], BF16)
    ltri = jnp.asarray(idx[None, :] < idx[:, None], BF16)
    return dict(rc=rc, rsa=rsa, rsb=rsb, e=e, tril=tril, ltri=ltri)


def _alibi():
    n = 2 * HEADS
    s = [2.0 ** (-8.0 * i / n) for i in range(1, n + 1)]
    return s[HEADS:], s[:HEADS]


def kernel(x, norm1_g, w_in, mla_gcq, mla_gckv, mla_wuq, mla_wukv, qk_gq, qk_gk, fox_bf, sink, w_o,
           norm2_g, w_rg, b_rg, w_re, b_re, w1, w3, w2):
    B, S, D = x.shape
    T = B * S
    depth = w_in.shape[0]
    tm = 512
    tabs = _tables(S, tm)
    slopes_dil, slopes_swa = _alibi()
    slopes_swa_perm = [slopes_swa[0], slopes_swa[2], slopes_swa[1], slopes_swa[3]]
    n_assign = T * TOP_K
    assert n_assign % ROW_BLOCK == 0

    all_w = _prep_weights(norm1_g, w_in, mla_gcq, mla_gckv, mla_wuq, mla_wukv, qk_gq, qk_gk, fox_bf, sink,
                          w_o, norm2_g, w_rg, b_rg, w_re, b_re)
    xt = x.reshape(T, D)
    for l in range(depth):
        lw = {name: arr[l] for name, arr in all_w.items()}
        qkv, qkvb, vt, cum, cum8 = _inproj(xt, S, lw, tabs, tm=tm)
        qkv3 = qkv.reshape(B, S, QKV_COLS)
        qkvb3 = qkvb.reshape(B, S, QKVB_COLS)
        oa = _dense_attn(qkv3, vt, 0, None, None, QA, KA, VA, lw["bound"][0]).reshape(T, GROUP_W)
        oc = _dense_attn(qkv3, vt, 2, cum.reshape(B, S, LANES), cum8, QC, KC, VC, lw["bound"][2]).reshape(T, GROUP_W)
        obs, lses = [], []
        for window, dil in DILATED_PAIRS:
            o, lse = _banded(qkvb3, 0, qkvb3, 1, 2, 256, dil, window // dil, slopes_dil, None, True, lw["bound"][1])
            obs.append(o)
            lses.append(lse)
        swa_bound = jnp.maximum(lw["bound"][3], jnp.max(jnp.abs(lw["sink"])) * LOG2E)
        (od,) = _banded(qkv3, QD, qkv3, 2 * KVD, 2 * KVD + 1, 128, 1, SWA_WINDOW - 1, slopes_swa_perm,
                        lw["sink"], False, swa_bound)
        h, xn, route, route_t, cnt = _wo_router(xt, oa, obs, lses, oc, od, lw, tabs["ltri"], tm=tm)

        counts = cnt[0, :N_EXPERTS].astype(jnp.int32)
        *items, start = _expert_items(counts, n_assign // ROW_BLOCK)
        expert = route_t[0:TOP_K].astype(jnp.int32)
        rank = route_t[TOP_K:2 * TOP_K].astype(jnp.int32)
        hit = expert[None] == jnp.arange(N_EXPERTS, dtype=jnp.int32)[:, None, None]
        dest = rank + jnp.sum(jnp.where(hit, start[:, None, None], 0), axis=0)

        xbuf = _scatter_rows(xn, dest)
        ybuf = _expert_mlp(xbuf, items, l, w1, w3, w2)
        xt = _combine(h, route, ybuf, dest)
    return xt.reshape(B, S, D)
```

```python
import functools

import numpy as np
import jax
import jax.numpy as jnp
from jax import lax
from jax.experimental import pallas as pl
from jax.experimental.pallas import tpu as pltpu

F32 = jnp.float32
BF16 = jnp.bfloat16

D_MODEL = 1024
HEAD_DIM = 64
HEADS = 4
GROUP_W = HEADS * HEAD_DIM
NORM_EPS = 1e-6
MLA_Q_RANK, MLA_KV_RANK, MLA_NOPE, MLA_ROPE = 192, 128, 32, 32
ROPE_BASE = 10000.0
DILATED_PAIRS = ((128, 1), (512, 4), (2048, 16))
SWA_WINDOW = 128
N_GROUPS, EXPERTS_PER_GROUP, N_EXPERTS, TOP_K, D_EXPERT = 4, 8, 32, 2, 256
IN_SPLITS = (MLA_Q_RANK, MLA_KV_RANK, MLA_ROPE, 3 * GROUP_W, 3 * GROUP_W, HEADS, GROUP_W, 2 * 2 * HEAD_DIM)

LANES = 128
SUBLANES = 8
ROW_CHUNKS = D_MODEL // LANES
W_COLS = 2688
QKV_COLS = 2048
QA, KA, VA, QC, KC, VC, QD, KVD = range(8)
QKVB_COLS = 768
ONES_ROWS = 16
BAND = 128
BAND_TOKENS = 2048
FG_LANE = 64
ROW_BLOCK = 512
VMEM_LIMIT = 56 * 1024 * 1024

NEG_INF = float("-inf")
LOG2E = 1.4426950408889634
LN2 = 0.6931471805599453
EXP2_SAFE = 60.0


def _cparams(sem):
    return pltpu.CompilerParams(dimension_semantics=sem, vmem_limit_bytes=VMEM_LIMIT)


def _full(shape):
    zeros = (0,) * len(shape)
    return pl.BlockSpec(shape, lambda *_: zeros)


def _head_norm(y, g, e):
    ss = jnp.dot((y * y).astype(BF16), e, preferred_element_type=F32)
    return y * lax.rsqrt(ss * (1.0 / HEAD_DIM) + NORM_EPS) * g


def _rope(y, rc, rsa, rsb):
    outs = []
    for c in range(y.shape[1] // LANES):
        yc = y[:, c * LANES:(c + 1) * LANES]
        outs.append(yc * rc + pltpu.roll(yc, LANES - 16, 1) * rsa + pltpu.roll(yc, 16, 1) * rsb)
    return jnp.concatenate(outs, axis=1)


def _inproj_kernel(x_ref, g1_ref, w_ref, gcq_ref, gckv_ref, wuq_ref, wuk_ref, wuv_ref, gq_ref, gk_ref, e_ref,
                   rc_ref, rsa_ref, rsb_ref, fb_ref, tril_ref, qkv_ref, qkvb_ref, vt_ref, cum_ref, cumt_ref, acc_ref,
                   carry_ref,
                   *, tiles_per_seq):
    i = pl.program_id(0)
    tm = x_ref.shape[0]

    @pl.when(i % tiles_per_seq == 0)
    def _():
        carry_ref[...] = jnp.zeros_like(carry_ref)

    x = x_ref[...]
    ms = jnp.mean(x * x, axis=-1, keepdims=True)
    xn = (x * lax.rsqrt(ms + NORM_EPS) * g1_ref[...]).astype(BF16)
    acc_ref[...] = jnp.dot(xn, w_ref[...], preferred_element_type=F32)

    e = e_ref[...]
    rc, rsa, rsb = rc_ref[...], rsa_ref[...], rsb_ref[...]

    ckv = acc_ref[:, 0:128]
    ckvn = (ckv * lax.rsqrt(jnp.mean(ckv * ckv, axis=-1, keepdims=True) + NORM_EPS) * gckv_ref[...]).astype(BF16)
    u = acc_ref[:, 128:384]
    lane256 = lax.broadcasted_iota(jnp.int32, (tm, 256), 1)
    ssq = jnp.sum(jnp.where(lane256 < MLA_Q_RANK, u * u, 0.0), axis=-1, keepdims=True) * (1.0 / MLA_Q_RANK)
    un = (u * lax.rsqrt(ssq + NORM_EPS) * gcq_ref[...]).astype(BF16)
    qa = jnp.dot(un, wuq_ref[...], preferred_element_type=F32)
    ka = jnp.dot(ckvn, wuk_ref[...], preferred_element_type=F32) + acc_ref[:, 384:640]
    va = jnp.dot(ckvn, wuv_ref[...], preferred_element_type=F32)
    qa = _rope(_head_norm(qa, gq_ref[0:1, :], e), rc, rsa, rsb)
    ka = _rope(_head_norm(ka, gk_ref[0:1, :], e), rc, rsa, rsb)
    qkv_ref[:, QA * 256:(QA + 1) * 256] = qa.astype(BF16)
    qkv_ref[:, KA * 256:(KA + 1) * 256] = ka.astype(BF16)
    qkv_ref[:, VA * 256:(VA + 1) * 256] = va.astype(BF16)
    vt_ref[0:2] = jnp.transpose(va).astype(BF16).reshape(2, LANES, tm)
    vt_ref[2:4] = jnp.transpose(acc_ref[:, 1920:2176]).astype(BF16).reshape(2, LANES, tm)

    qkvb_ref[:, 0:256] = _head_norm(acc_ref[:, 640:896], gq_ref[1:2, :], e)
    qkvb_ref[:, 256:512] = _head_norm(acc_ref[:, 896:1152], gk_ref[1:2, :], e)
    qkvb_ref[:, 512:768] = acc_ref[:, 1152:1408]
    qkv_ref[:, QC * 256:(QC + 1) * 256] = _head_norm(acc_ref[:, 1408:1664], gq_ref[2:3, :], e).astype(BF16)
    qkv_ref[:, KC * 256:(KC + 1) * 256] = _head_norm(acc_ref[:, 1664:1920], gk_ref[2:3, :], e).astype(BF16)
    qkv_ref[:, VC * 256:(VC + 1) * 256] = acc_ref[:, 1920:2176].astype(BF16)

    qd = _head_norm(acc_ref[:, 2176:2432], gq_ref[3:4, :], e)
    kd = _head_norm(acc_ref[:, 2432:2560], gk_ref[3:4, 0:128], e[0:128, 0:128])
    qkv_ref[:, QD * 256:(QD + 1) * 256] = qd.astype(BF16)
    qkv_ref[:, KVD * 256:KVD * 256 + 128] = kd.astype(BF16)
    qkv_ref[:, KVD * 256 + 128:(KVD + 1) * 256] = acc_ref[:, 2560:2688].astype(BF16)

    z = u[:, 128:256] + fb_ref[...]
    ls = jnp.minimum(z, 0.0) - jnp.log(1.0 + jnp.exp(-jnp.abs(z)))
    lane128 = lax.broadcasted_iota(jnp.int32, (tm, LANES), 1)
    ls = jnp.where((lane128 >= FG_LANE) & (lane128 < FG_LANE + HEADS), ls, 0.0)
    hi = ls.astype(BF16)
    r1 = ls - hi.astype(F32)
    mid = r1.astype(BF16)
    lo = (r1 - mid.astype(F32)).astype(BF16)
    tril = tril_ref[...]
    local = (jnp.dot(tril, hi, preferred_element_type=F32) + jnp.dot(tril, mid, preferred_element_type=F32)
             + jnp.dot(tril, lo, preferred_element_type=F32))
    cum = local + carry_ref[0:1, :]
    cum_ref[...] = cum
    cum_t = jnp.transpose(cum)
    pad = jnp.zeros((SUBLANES - 2, tm), F32)
    for p in range(2):
        cumt_ref[p] = jnp.concatenate([cum_t[FG_LANE + 2 * p:FG_LANE + 2 * p + 2, :], pad], axis=0)
    carry_ref[...] = jnp.broadcast_to(cum[tm - 1:tm, :], carry_ref.shape)


def _inproj(xt, seq_len, lw, tabs, tm=512):
    T = xt.shape[0]
    tps = seq_len // tm
    kern = functools.partial(_inproj_kernel, tiles_per_seq=tps)
    tab_spec = pl.BlockSpec((tm, LANES), lambda i: (i % tps, 0))
    return pl.pallas_call(
        kern,
        grid=(T // tm,),
        in_specs=[pl.BlockSpec((tm, D_MODEL), lambda i: (i, 0)),
                  _full((1, D_MODEL)), _full((D_MODEL, W_COLS)), _full((1, 256)), _full((1, 128)),
                  _full((256, 256)), _full((128, 256)), _full((128, 256)), _full((4, 256)), _full((4, 256)),
                  _full((256, 256)), tab_spec, tab_spec, tab_spec, _full((1, LANES)), _full((tm, tm))],
        out_specs=[pl.BlockSpec((tm, QKV_COLS), lambda i: (i, 0)), pl.BlockSpec((tm, QKVB_COLS), lambda i: (i, 0)),
                   pl.BlockSpec((4, LANES, tm), lambda i: (0, 0, i)), pl.BlockSpec((tm, LANES), lambda i: (i, 0)),
                   pl.BlockSpec((2, SUBLANES, tm), lambda i: (0, 0, i))],
        out_shape=[jax.ShapeDtypeStruct((T, QKV_COLS), BF16), jax.ShapeDtypeStruct((T, QKVB_COLS), F32),
                   jax.ShapeDtypeStruct((4, LANES, T), BF16), jax.ShapeDtypeStruct((T, LANES), F32),
                   jax.ShapeDtypeStruct((2, SUBLANES, T), F32)],
        scratch_shapes=[pltpu.VMEM((tm, W_COLS), F32), pltpu.VMEM((8, LANES), F32)],
        compiler_params=_cparams(("arbitrary",)),
        name="inproj",
    )(xt, lw["g1"], lw["wp"], lw["gcq"], lw["gckv"], lw["wuq"], lw["wuk"], lw["wuv"], lw["gq"], lw["gk"],
      tabs["e"], tabs["rc"], tabs["rsa"], tabs["rsb"], lw["fb"], tabs["tril"])


def _split_heads(x):
    half0 = lax.broadcasted_iota(jnp.int32, x.shape, 1) < HEAD_DIM
    zero = jnp.zeros_like(x)
    return jnp.where(half0, x, zero), jnp.where(half0, zero, x)


def _causal_mask(s):
    row = lax.broadcasted_iota(jnp.int32, s.shape, 0)
    col = lax.broadcasted_iota(jnp.int32, s.shape, 1)
    return jnp.where(col <= row, s, NEG_INF)


def _cum_column(blk, h):
    lane = lax.broadcasted_iota(jnp.int32, blk.shape, 1)
    return jnp.sum(jnp.where(lane == FG_LANE + 2 * pl.program_id(1) + h, blk * LOG2E, 0.0), axis=1, keepdims=True)


def _row_cum(cq_ref, h):
    return _cum_column(cq_ref[0], h)


def _dense_bounded_kernel(*refs, tq, tk, fox):
    if fox:
        q_ref, k_ref, vt_ref, ccol_ref, crow_ref, o_ref, acc_scr = refs
    else:
        q_ref, k_ref, vt_ref, o_ref, acc_scr = refs
    qi = pl.program_id(2)
    qh = _split_heads(q_ref[0])
    ones = jnp.ones((ONES_ROWS, tk), BF16)
    acc_scr[...] = jnp.zeros_like(acc_scr)

    def step(start, c0, masked):
        k = k_ref[0, pl.ds(start, tk), :]
        for h in range(2):
            s = lax.dot_general(k, qh[h][c0:, :], (((1,), (1,)), ((), ())), preferred_element_type=F32)
            if fox:
                crow = crow_ref[0, h:h + 1, pl.ds(pl.multiple_of(qi * tq + c0, tk), tq - c0)] * LOG2E
                s = (s + crow) - _cum_column(ccol_ref[0, pl.ds(start, tk), :], h)
            if masked:
                key = lax.broadcasted_iota(jnp.int32, s.shape, 0)
                query = lax.broadcasted_iota(jnp.int32, s.shape, 1)
                s = jnp.where(key <= query, s, NEG_INF)
            lhs = jnp.concatenate([vt_ref[0, h * HEAD_DIM:(h + 1) * HEAD_DIM, pl.ds(start, tk)], ones], axis=0)
            acc_scr[h, :, c0:] += jnp.dot(lhs, jnp.exp2(s).astype(BF16), preferred_element_type=F32)

    def body(j, carry):
        step(pl.multiple_of(j * tk, tk), 0, False)
        return carry

    lax.fori_loop(0, qi * (tq // tk), body, 0)
    for d in range(tq // tk):
        step(pl.multiple_of(qi * tq + d * tk, tk), d * tk, True)
    o_t = jnp.concatenate([acc_scr[h, 0:HEAD_DIM, :] / acc_scr[h, HEAD_DIM:HEAD_DIM + 1, :] for h in range(2)], axis=0)
    o_ref[0] = jnp.transpose(o_t).astype(o_ref.dtype)


def _dense_online_kernel(*refs, tq, tk, fox):
    if fox:
        q_ref, k_ref, v_ref, cq_ref, ck_ref, o_ref, m_scr, l_scr, acc_scr = refs
    else:
        q_ref, k_ref, v_ref, o_ref, m_scr, l_scr, acc_scr = refs
    qi = pl.program_id(2)
    qh = _split_heads(q_ref[0])
    m_scr[...] = jnp.full(m_scr.shape, NEG_INF, F32)
    l_scr[...] = jnp.zeros_like(l_scr)
    acc_scr[...] = jnp.zeros_like(acc_scr)
    if fox:
        cq = [_row_cum(cq_ref, h) for h in range(2)]

    def step(start, r0, masked):
        k = k_ref[0, pl.ds(start, tk), :]
        v = v_ref[0, pl.ds(start, tk), :]
        for h in range(2):
            s = lax.dot_general(qh[h][r0:, :], k, (((1,), (1,)), ((), ())), preferred_element_type=F32)
            if fox:
                s = (s + cq[h][r0:, :]) - ck_ref[0, h:h + 1, pl.ds(start, tk)] * LOG2E
            if masked:
                s = _causal_mask(s)
            m_prev = m_scr[h, r0:, :]
            m_next = jnp.maximum(m_prev, jnp.max(s, axis=1, keepdims=True))
            p = jnp.exp2(s - jnp.concatenate([m_next] * (tk // LANES), axis=1))
            alpha = jnp.exp2(m_prev - m_next)
            l_scr[h, r0:, :] = alpha * l_scr[h, r0:, :] + jnp.sum(p, axis=1, keepdims=True)
            m_scr[h, r0:, :] = m_next
            pv = jnp.dot(p.astype(BF16), v, preferred_element_type=F32)
            acc_scr[h, r0:, :] = acc_scr[h, r0:, :] * alpha + pv

    def body(j, carry):
        step(pl.multiple_of(j * tk, tk), 0, False)
        return carry

    lax.fori_loop(0, qi * (tq // tk), body, 0)
    for d in range(tq // tk):
        step(pl.multiple_of(qi * tq + d * tk, tk), d * tk, True)
    half0 = lax.broadcasted_iota(jnp.int32, (tq, LANES), 1) < HEAD_DIM
    o = jnp.where(half0, acc_scr[0] / l_scr[0], acc_scr[1] / l_scr[1])
    o_ref[0] = o.astype(o_ref.dtype)


def _dense_attn(qkv3, vt, jvt, cum3, cum8, jq, jk, jv, logit_bound, tq_bounded=4096, tq_online=2048, tk=512):
    B, S, _ = qkv3.shape
    fox = cum8 is not None
    k_spec = pl.BlockSpec((1, S, LANES), lambda b, p, i: (b, 0, 2 * jk + p))
    v_spec = pl.BlockSpec((1, S, LANES), lambda b, p, i: (b, 0, 2 * jv + p))
    vt_spec = pl.BlockSpec((1, LANES, S), lambda b, p, i: (jvt + p, 0, b))
    cum_row_spec = pl.BlockSpec((1, SUBLANES, S), lambda b, p, i: (p, 0, b))
    name = "fox_attn" if fox else "mla_attn"

    def build(kern, tq, suffix, kv_specs, cum_specs, scratch):
        tq = min(tq, S)
        assert S % tq == 0 and tq % tk == 0
        q_spec = pl.BlockSpec((1, tq, LANES), lambda b, p, i: (b, i, 2 * jq + p))
        return pl.pallas_call(
            functools.partial(kern, tq=tq, tk=tk, fox=fox),
            grid=(B, 2, S // tq),
            in_specs=[q_spec] + kv_specs + (cum_specs(tq) if fox else []),
            out_specs=pl.BlockSpec((1, tq, LANES), lambda b, p, i: (b, i, p)),
            out_shape=jax.ShapeDtypeStruct((B, S, GROUP_W), BF16),
            scratch_shapes=scratch(tq),
            compiler_params=_cparams(("parallel", "parallel", "arbitrary")),
            name=name + suffix)

    bounded = build(_dense_bounded_kernel, tq_bounded, "_bounded", [k_spec, vt_spec],
                    lambda tq: [pl.BlockSpec((1, S, LANES), lambda b, p, i: (b, 0, 0)), cum_row_spec],
                    lambda tq: [pltpu.VMEM((2, HEAD_DIM + ONES_ROWS, tq), F32)])
    online = build(_dense_online_kernel, tq_online, "_online", [k_spec, v_spec],
                   lambda tq: [pl.BlockSpec((1, tq, LANES), lambda b, p, i: (b, i, 0)), cum_row_spec],
                   lambda tq: [pltpu.VMEM((2, tq, LANES), F32)] * 3)
    cums = (cum3, cum8) if fox else ()
    return lax.cond(logit_bound <= EXP2_SAFE,
                    lambda vt_, *c: bounded(qkv3, qkv3, vt_, *c),
                    lambda vt_, *c: online(qkv3, qkv3, qkv3, *c), vt, *cums)


def _fold_rows(r, n, dil):
    return slice(r, r + n) if dil == 1 else pl.ds(r, n, stride=dil)


def _banded_kernel(*refs, dil, maxdist, slopes, kv_chunks, has_sink, want_lse, bounded):
    refs = list(refs)
    sink_ref = refs.pop(0) if has_sink else None
    q_refs = [refs.pop(0) for _ in range(2)]
    kv_refs = [[refs.pop(0) for _ in range(4)] for _ in range(kv_chunks)]
    o_ref = refs.pop(0)
    lse_ref = refs.pop(0) if want_lse else None
    o_scr = refs.pop(0)
    lse_scr = refs.pop(0) if want_lse else None
    nq = q_refs[0].shape[1] // dil
    row = lax.broadcasted_iota(jnp.int32, (BAND, 2 * BAND), 0)
    col = lax.broadcasted_iota(jnp.int32, (BAND, 2 * BAND), 1)
    dist = row + BAND - col
    valid = (dist >= 0) & (dist <= maxdist)
    distf = dist.astype(F32) * (float(dil) * LOG2E)
    bias = [jnp.where(valid, -slopes[h] * distf, NEG_INF) for h in range(HEADS)]
    no_prev = jnp.where(col < BAND, jnp.where(pl.program_id(1) == 0, NEG_INF, 0.0), 0.0)
    bias_first = [b + no_prev for b in bias]
    half0_k = lax.broadcasted_iota(jnp.int32, (2 * BAND, LANES), 1) < HEAD_DIM
    ones0 = jnp.where(half0_k, 1.0, 0.0).astype(BF16)
    ones1 = jnp.where(half0_k, 0.0, 1.0).astype(BF16)
    half0 = lax.broadcasted_iota(jnp.int32, (BAND, LANES), 1) < HEAD_DIM
    for c in range(2):
        kp_ref, kc_ref, vp_ref, vc_ref = kv_refs[c if kv_chunks == 2 else 0]
        if has_sink:
            sink_lanes = jnp.where(half0[0:1, :], sink_ref[2 * c], sink_ref[2 * c + 1])
        for r in range(dil):
            prev_rows, cur_rows = _fold_rows(r, BAND, dil), _fold_rows(r, nq, dil)
            qh = _split_heads(q_refs[c][0, cur_rows, :].astype(BF16))
            kf = jnp.concatenate([kp_ref[0, prev_rows, :], kc_ref[0, cur_rows, :]], axis=0).astype(BF16)
            vh = _split_heads(jnp.concatenate([vp_ref[0, prev_rows, :], vc_ref[0, cur_rows, :]], axis=0).astype(BF16))
            outs, lses = [], []
            for j in range(nq // BAND):
                keys = slice(j * BAND, (j + 2) * BAND)
                v2 = jnp.concatenate([jnp.concatenate([vh[0][keys], ones0], axis=1),
                                      jnp.concatenate([vh[1][keys], ones1], axis=1)], axis=0)
                ps, ms = [], []
                for e in range(2):
                    s = lax.dot_general(qh[e][j * BAND:(j + 1) * BAND], kf[keys], (((1,), (1,)), ((), ())),
                                        preferred_element_type=F32)
                    s = s + (bias_first if j == 0 else bias)[2 * c + e]
                    if bounded:
                        ps.append(jnp.exp2(s).astype(BF16))
                    else:
                        m = jnp.max(s, axis=1, keepdims=True)
                        ps.append(jnp.exp2(s - m).astype(BF16))
                        ms.append(m)
                acc = jnp.dot(jnp.concatenate(ps, axis=1), v2, preferred_element_type=F32)
                unnorm, den = acc[:, 0:LANES], acc[:, LANES:2 * LANES]
                if bounded:
                    lse = jnp.log(den)
                    outs.append(unnorm / (den + jnp.exp(sink_lanes)) if has_sink else unnorm / den)
                else:
                    shift = jnp.where(half0, ms[0], ms[1]) * LN2
                    lse = shift + jnp.log(den)
                    if has_sink:
                        mx = jnp.maximum(lse, sink_lanes)
                        total = mx + jnp.log(jnp.exp(lse - mx) + jnp.exp(sink_lanes - mx))
                        outs.append(unnorm * jnp.exp(shift - total))
                    else:
                        outs.append(unnorm / den)
                if want_lse:
                    lses.append(lse)
            o_scr[cur_rows, :] = jnp.concatenate(outs, axis=0)
            if want_lse:
                lse_scr[cur_rows, :] = jnp.concatenate(lses, axis=0)
        o_ref[0, :, c * LANES:(c + 1) * LANES] = o_scr[...]
        if want_lse:
            lse_ref[0, :, c * LANES:(c + 1) * LANES] = lse_scr[...]


def _banded(q_arr, jq, kv_arr, jk, jv, kw, dil, maxdist, slopes, sink, want_lse, logit_bound):
    B, S, _ = q_arr.shape
    tb = min(BAND_TOKENS, S)
    pb = BAND * dil
    assert S % tb == 0 and tb % pb == 0 and maxdist <= BAND
    ratio = tb // pb
    has_sink = sink is not None
    kern = functools.partial(_banded_kernel, dil=dil, maxdist=maxdist, slopes=tuple(slopes),
                             kv_chunks=kw // LANES, has_sink=has_sink, want_lse=want_lse)
    prev = lambda j: pl.BlockSpec((1, pb, LANES), lambda b, i: (b, jnp.maximum(i * ratio - 1, 0), j))
    cur = lambda j: pl.BlockSpec((1, tb, LANES), lambda b, i: (b, i, j))
    in_specs = [cur(2 * jq), cur(2 * jq + 1)]
    args = [q_arr, q_arr]
    for c in range(kw // LANES):
        jkc, jvc = jk * (kw // LANES) + c, jv * (kw // LANES) + c
        in_specs += [prev(jkc), cur(jkc), prev(jvc), cur(jvc)]
        args += [kv_arr] * 4
    if has_sink:
        in_specs.insert(0, pl.BlockSpec(memory_space=pltpu.SMEM))
        args.insert(0, sink)
    n_out = 2 if want_lse else 1
    name = f"banded_d{dil}" if want_lse else "swa_attn"
    call = lambda bounded: pl.pallas_call(
        functools.partial(kern, bounded=bounded),
        grid=(B, S // tb),
        in_specs=in_specs,
        out_specs=[pl.BlockSpec((1, tb, 256), lambda b, i: (b, i, 0))] * n_out,
        out_shape=[jax.ShapeDtypeStruct((B, S, 256), F32)] * n_out,
        scratch_shapes=[pltpu.VMEM((tb, LANES), F32)] * n_out,
        compiler_params=_cparams(("parallel", "parallel")),
        name=name + ("_bounded" if bounded else "_rowmax"),
    )
    outs = lax.cond(logit_bound <= EXP2_SAFE, lambda *a: call(True)(*a), lambda *a: call(False)(*a), *args)
    return [o.reshape(B * S, 256) for o in outs]


def _wo_router_kernel(x_ref, oa_ref, ob0_ref, ob1_ref, ob2_ref, l0_ref, l1_ref, l2_ref, oc_ref, od_ref,
                      wo_ref, g2_ref, wr_ref, br_ref, ltri_ref, h_ref, xn_ref, route_ref, route_t_ref, cnt_ref,
                      carry_ref):
    i = pl.program_id(0)
    tm = x_ref.shape[0]

    @pl.when(i == 0)
    def _():
        carry_ref[...] = jnp.zeros_like(carry_ref)

    la, lb, lc = l0_ref[...], l1_ref[...], l2_ref[...]
    mx = jnp.maximum(jnp.maximum(la, lb), lc)
    ea, eb, ec = jnp.exp(la - mx), jnp.exp(lb - mx), jnp.exp(lc - mx)
    ob = (ea * ob0_ref[...] + eb * ob1_ref[...] + ec * ob2_ref[...]) / (ea + eb + ec)
    mix = jnp.concatenate([oa_ref[...], ob.astype(BF16), oc_ref[...], od_ref[...].astype(BF16)], axis=1)
    h = x_ref[...] + jnp.dot(mix, wo_ref[...], preferred_element_type=F32)
    h_ref[...] = h
    xn = h * lax.rsqrt(jnp.mean(h * h, axis=-1, keepdims=True) + NORM_EPS) * g2_ref[...]
    _store_row_tiles(xn_ref, xn)

    xh = xn.astype(BF16)
    xl = (xn - xh.astype(F32)).astype(BF16)
    z = (jnp.dot(xh, wr_ref[0], preferred_element_type=F32) + jnp.dot(xl, wr_ref[0], preferred_element_type=F32)
         + jnp.dot(xh, wr_ref[1], preferred_element_type=F32)) + br_ref[...]
    lane = lax.broadcasted_iota(jnp.int32, (tm, LANES), 1)
    lanef = lane.astype(F32)
    big = float(LANES)
    zg = jnp.where((lane >= N_EXPERTS) & (lane < N_EXPERTS + N_GROUPS), z, NEG_INF)
    mg = jnp.max(zg, axis=1, keepdims=True)
    p_g = 1.0 / jnp.sum(jnp.exp(zg - mg), axis=1, keepdims=True)
    gsel = jnp.min(jnp.where(zg == mg, lanef, big), axis=1, keepdims=True) - float(N_EXPERTS)
    lo = gsel * float(EXPERTS_PER_GROUP)
    ze = jnp.where((lanef >= lo) & (lanef < lo + float(EXPERTS_PER_GROUP)), z, NEG_INF)
    m1 = jnp.max(ze, axis=1, keepdims=True)
    i1 = jnp.min(jnp.where(ze == m1, lanef, big), axis=1, keepdims=True)
    ze2 = jnp.where(lanef == i1, NEG_INF, ze)
    m2 = jnp.max(ze2, axis=1, keepdims=True)
    i2 = jnp.min(jnp.where(ze2 == m2, lanef, big), axis=1, keepdims=True)
    e2 = jnp.exp(m2 - m1)
    gate1 = p_g / (1.0 + e2)
    gate2 = p_g * e2 / (1.0 + e2)

    oh1 = jnp.where(lanef == i1, 1.0, 0.0)
    oh2 = jnp.where(lanef == i2, 1.0, 0.0)
    oh = oh1 + oh2
    before = carry_ref[0:1, :] + jnp.dot(ltri_ref[...], oh.astype(BF16), preferred_element_type=F32)
    r1 = jnp.sum(before * oh1, axis=1, keepdims=True)
    r2 = jnp.sum(before * oh2, axis=1, keepdims=True)
    total = carry_ref[0:1, :] + jnp.sum(oh, axis=0, keepdims=True)
    carry_ref[...] = jnp.broadcast_to(total, carry_ref.shape)
    cnt_ref[...] = jnp.broadcast_to(total, cnt_ref.shape)

    route = jnp.where(lane == 0, i1, jnp.where(lane == 1, i2, jnp.where(lane == 2, r1, jnp.where(
        lane == 3, r2, jnp.where(lane == 4, gate1, jnp.where(lane == 5, gate2, 0.0))))))
    route_ref[...] = route
    route_t_ref[...] = jnp.transpose(route)[0:8, :]


def _wo_router(xt, oa, obs, lses, oc, od, lw, ltri, tm=512):
    T = xt.shape[0]
    row = lambda w: pl.BlockSpec((tm, w), lambda i: (i, 0))
    return pl.pallas_call(
        _wo_router_kernel,
        grid=(T // tm,),
        in_specs=[row(D_MODEL), row(256), row(256), row(256), row(256), row(256), row(256), row(256), row(256),
                  row(256), _full((D_MODEL, D_MODEL)), _full((1, D_MODEL)), _full((2, D_MODEL, LANES)),
                  _full((1, LANES)), _full((tm, tm))],
        out_specs=[row(D_MODEL), pl.BlockSpec((tm * ROW_CHUNKS, LANES), lambda i: (i, 0)), row(LANES),
                   pl.BlockSpec((8, tm), lambda i: (0, i)), _full((8, LANES))],
        out_shape=[jax.ShapeDtypeStruct((T, D_MODEL), F32), jax.ShapeDtypeStruct((T * ROW_CHUNKS, LANES), F32),
                   jax.ShapeDtypeStruct((T, LANES), F32), jax.ShapeDtypeStruct((8, T), F32),
                   jax.ShapeDtypeStruct((8, LANES), F32)],
        scratch_shapes=[pltpu.VMEM((8, LANES), F32)],
        compiler_params=_cparams(("arbitrary",)),
        name="wo_router",
    )(xt, oa, obs[0], obs[1], obs[2], lses[0], lses[1], lses[2], oc, od, lw["wo"], lw["g2"], lw["wr"], lw["br"], ltri)


def _store_row_tiles(ref, x, accumulate=False):
    rows = x.shape[0]
    for c in range(ROW_CHUNKS):
        idx = pl.ds(c, rows, stride=ROW_CHUNKS)
        chunk = x[:, c * LANES:(c + 1) * LANES]
        ref[idx, :] = ref[idx, :] + chunk if accumulate else chunk


def _load_row_tiles(ref, rows, lead=()):
    return jnp.concatenate([ref[lead + (pl.ds(c, rows, stride=ROW_CHUNKS), slice(None))] for c in range(ROW_CHUNKS)],
                           axis=1)


def _tile_copy(src, dst, sem):
    return pltpu.make_async_copy(src, dst, sem)


def _tile_major(dest_t, tile):
    k, T = dest_t.shape
    return dest_t.reshape(k, T // tile, tile).transpose(1, 0, 2).reshape(-1)


def _scatter_kernel(dest_ref, x_ref, buf_out, sem, *, ts):
    def issue(t, carry):
        for k in range(TOP_K):
            _tile_copy(x_ref.at[t], buf_out.at[dest_ref[k * ts + t]], sem).start()
        return carry

    lax.fori_loop(0, ts, issue, 0, unroll=8)
    for k in range(TOP_K):
        _tile_copy(x_ref, buf_out.at[pl.ds(0, ts)], sem).wait()


def _scatter_rows(xn_tiles, dest_t, ts=1024):
    T = xn_tiles.shape[0]
    dest_flat = _tile_major(dest_t, ts)
    return pl.pallas_call(
        functools.partial(_scatter_kernel, ts=ts),
        grid=(T // ts,),
        in_specs=[pl.BlockSpec((TOP_K * ts,), lambda i: (i,), memory_space=pltpu.SMEM),
                  pl.BlockSpec((ts, ROW_CHUNKS, LANES), lambda i: (i, 0, 0))],
        out_specs=pl.BlockSpec(memory_space=pl.ANY),
        out_shape=jax.ShapeDtypeStruct((TOP_K * T, ROW_CHUNKS, LANES), xn_tiles.dtype),
        scratch_shapes=[pltpu.SemaphoreType.DMA(())],
        compiler_params=_cparams(("arbitrary",)),
        name="moe_scatter",
    )(dest_flat, xn_tiles)


def _expert_items(counts, n_blocks):
    n_items_max = n_blocks + N_EXPERTS
    end = jnp.cumsum(counts)
    start = end - counts
    first_b = start // ROW_BLOCK
    per_expert = jnp.where(counts > 0, (end - 1) // ROW_BLOCK - first_b + 1, 0)
    item_end = jnp.cumsum(per_expert)
    item_start = item_end - per_expert
    n_items = item_end[-1:]
    idx = jnp.minimum(jnp.arange(n_items_max, dtype=jnp.int32), n_items - 1)
    e = jnp.minimum(jnp.sum((item_end[None, :] <= idx[:, None]).astype(jnp.int32), axis=1), N_EXPERTS - 1)
    onehot = (e[:, None] == jnp.arange(N_EXPERTS, dtype=jnp.int32)[None, :]).astype(jnp.int32)
    pick = lambda table: jnp.sum(onehot * table[None, :], axis=1)
    b = pick(first_b) + idx - pick(item_start)
    lo = jnp.maximum(pick(start) - b * ROW_BLOCK, 0)
    hi = jnp.minimum(pick(end) - b * ROW_BLOCK, ROW_BLOCK)
    return e, b, lo, hi, n_items.astype(jnp.int32), start


def _expert_kernel(e_ref, b_ref, lo_ref, hi_ref, n_ref, x_ref, w1_ref, w3_ref, w2_ref, y_ref):
    del e_ref
    i = pl.program_id(0)

    @pl.when(i < n_ref[0])
    def _():
        x = _load_row_tiles(x_ref, ROW_BLOCK).astype(BF16)
        a = jnp.dot(x, w1_ref[0, 0].astype(BF16), preferred_element_type=F32)
        b = jnp.dot(x, w3_ref[0, 0].astype(BF16), preferred_element_type=F32)
        hid = (a / (1.0 + jnp.exp(-a))) * b
        y = jnp.dot(hid.astype(BF16), w2_ref[0, 0].astype(BF16), preferred_element_type=F32)
        row = lax.broadcasted_iota(jnp.int32, y.shape, 0)
        y = jnp.where((row >= lo_ref[i]) & (row < hi_ref[i]), y, 0.0)
        first_visit = jnp.logical_or(i == 0, b_ref[i] != b_ref[jnp.maximum(i - 1, 0)])

        @pl.when(first_visit)
        def _():
            _store_row_tiles(y_ref, y)

        @pl.when(jnp.logical_not(first_visit))
        def _():
            _store_row_tiles(y_ref, y, accumulate=True)


def _expert_mlp(xbuf, items, layer, w1, w3, w2):
    D = D_MODEL
    e, b, lo, hi, n_items = items
    x_map = lambda i, e_, b_, lo_, hi_, n_: (b_[i], 0)
    w_map = lambda i, e_, b_, lo_, hi_, n_: (layer, e_[i], 0, 0)
    grid_spec = pltpu.PrefetchScalarGridSpec(
        num_scalar_prefetch=5,
        grid=(e.shape[0],),
        in_specs=[pl.BlockSpec((ROW_BLOCK * ROW_CHUNKS, LANES), x_map),
                  pl.BlockSpec((1, 1, D, D_EXPERT), w_map),
                  pl.BlockSpec((1, 1, D, D_EXPERT), w_map),
                  pl.BlockSpec((1, 1, D_EXPERT, D), w_map)],
        out_specs=pl.BlockSpec((ROW_BLOCK * ROW_CHUNKS, LANES), x_map),
    )
    return pl.pallas_call(
        _expert_kernel,
        grid_spec=grid_spec,
        out_shape=jax.ShapeDtypeStruct(xbuf.shape, xbuf.dtype),
        compiler_params=_cparams(("arbitrary",)),
        name="moe_experts",
    )(e, b, lo, hi, n_items, xbuf, w1, w3, w2)


def _combine_kernel(dest_ref, dest_next_ref, h_ref, route_ref, y_hbm, o_ref, rows_scr, sems, *, tc):
    i = pl.program_id(0)
    n = pl.num_programs(0)
    slot = i % 2

    def gather(d_ref, s):
        def issue(t, carry):
            for k in range(TOP_K):
                dst = rows_scr.at[s, k, pl.ds(pl.multiple_of(t * ROW_CHUNKS, ROW_CHUNKS), ROW_CHUNKS)]
                _tile_copy(y_hbm.at[d_ref[k * tc + t]], dst, sems.at[s]).start()
            return carry

        lax.fori_loop(0, tc, issue, 0, unroll=8)

    @pl.when(i == 0)
    def _():
        gather(dest_ref, 0)

    @pl.when(i + 1 < n)
    def _():
        gather(dest_next_ref, 1 - slot)

    for k in range(TOP_K):
        _tile_copy(rows_scr.at[slot, k], rows_scr.at[slot, k], sems.at[slot]).wait()
    route = route_ref[...]
    gates = [jnp.broadcast_to(route[:, 4 + k:5 + k], (tc, LANES)) for k in range(TOP_K)]
    rows = [_load_row_tiles(rows_scr, tc, lead=(slot, k)) for k in range(TOP_K)]
    for c in range(ROW_CHUNKS):
        cols = slice(c * LANES, (c + 1) * LANES)
        o_ref[:, cols] = h_ref[:, cols] + gates[0] * rows[0][:, cols] + gates[1] * rows[1][:, cols]


def _combine(h, route, ybuf, dest_t, tc=256):
    T, D = h.shape
    n = T // tc
    dest_flat = _tile_major(dest_t, tc)
    return pl.pallas_call(
        functools.partial(_combine_kernel, tc=tc),
        grid=(n,),
        in_specs=[pl.BlockSpec((TOP_K * tc,), lambda i: (i,), memory_space=pltpu.SMEM),
                  pl.BlockSpec((TOP_K * tc,), lambda i: (jnp.minimum(i + 1, n - 1),), memory_space=pltpu.SMEM),
                  pl.BlockSpec((tc, D), lambda i: (i, 0)), pl.BlockSpec((tc, LANES), lambda i: (i, 0)),
                  pl.BlockSpec(memory_space=pl.ANY)],
        out_specs=pl.BlockSpec((tc, D), lambda i: (i, 0)),
        out_shape=jax.ShapeDtypeStruct((T, D), F32),
        scratch_shapes=[pltpu.VMEM((2, TOP_K, tc * ROW_CHUNKS, LANES), ybuf.dtype), pltpu.SemaphoreType.DMA((2,))],
        compiler_params=_cparams(("arbitrary",)),
        name="moe_combine",
    )(dest_flat, dest_flat, h, route, ybuf)


def _prep_weights(norm1_g, w_in, mla_gcq, mla_gckv, mla_wuq, mla_wukv, qk_gq, qk_gk, fox_bf, sink, w_o,
                  norm2_g, w_rg, b_rg, w_re, b_re):
    L = w_in.shape[0]
    offs = np.concatenate([[0], np.cumsum(IN_SPLITS)])
    cq, ckv, kr, pb, pc, fg, pdq, pdkv = [w_in[:, :, offs[j]:offs[j + 1]] for j in range(8)]
    z = lambda *shape: jnp.zeros((L,) + shape, F32)
    perm = np.array([0, 2, 1, 3])
    ublock = jnp.concatenate([cq, fg, z(D_MODEL, 256 - MLA_Q_RANK - HEADS)], axis=2)
    kr_rep = jnp.concatenate([z(D_MODEL, MLA_NOPE), kr] * HEADS, axis=2)
    dq = pdq.reshape(L, D_MODEL, HEADS, HEAD_DIM)[:, :, perm].reshape(L, D_MODEL, GROUP_W)
    wp = jnp.concatenate([ckv, ublock, kr_rep, pb, pc, dq, pdkv], axis=2).astype(BF16)

    wuq = jnp.concatenate([mla_wuq, z(256 - MLA_Q_RANK, GROUP_W)], axis=1).astype(BF16)
    wukv = mla_wukv.reshape(L, MLA_KV_RANK, HEADS, MLA_NOPE + HEAD_DIM)
    wuk = jnp.concatenate([wukv[..., :MLA_NOPE], z(MLA_KV_RANK, HEADS, MLA_ROPE)], axis=3)
    wuk = wuk.reshape(L, MLA_KV_RANK, GROUP_W).astype(BF16)
    wuv = wukv[..., MLA_NOPE:].reshape(L, MLA_KV_RANK, GROUP_W).astype(BF16)
    gcq = jnp.concatenate([mla_gcq, z(256 - MLA_Q_RANK)], axis=1)[:, None, :]
    scale = HEAD_DIM ** -0.5
    gq = jnp.tile(qk_gq, (1, 1, HEADS)) * (scale * LOG2E)
    gk = jnp.tile(qk_gk, (1, 1, HEADS))
    bound = HEAD_DIM * jnp.max(jnp.abs(gq), axis=2) * jnp.max(jnp.abs(gk), axis=2)
    fb = z(1, LANES).at[:, 0, FG_LANE:FG_LANE + HEADS].set(fox_bf)
    wo_d = w_o[:, 3 * GROUP_W:].reshape(L, HEADS, HEAD_DIM, D_MODEL)[:, perm].reshape(L, GROUP_W, D_MODEL)
    wo = jnp.concatenate([w_o[:, :3 * GROUP_W], wo_d], axis=1).astype(BF16)
    wr = jnp.concatenate([w_re, w_rg, z(D_MODEL, LANES - N_EXPERTS - N_GROUPS)], axis=2)
    wr_hi = wr.astype(BF16)
    wr = jnp.stack([wr_hi, (wr - wr_hi.astype(F32)).astype(BF16)], axis=1)
    br = jnp.concatenate([b_re, b_rg, z(LANES - N_EXPERTS - N_GROUPS)], axis=1)[:, None, :]
    return dict(g1=norm1_g[:, None, :], wp=wp, gcq=gcq, gckv=mla_gckv[:, None, :], wuq=wuq, wuk=wuk, wuv=wuv,
                gq=gq, gk=gk, bound=bound, fb=fb, sink=sink[:, perm], wo=wo, g2=norm2_g[:, None, :], wr=wr, br=br)


def _tables(seq_len, tm):
    half = MLA_ROPE // 2
    inv = ROPE_BASE ** (-jnp.arange(0, MLA_ROPE, 2, dtype=F32) / MLA_ROPE)
    ang = jnp.arange(seq_len, dtype=F32)[:, None] * inv[None, :]
    cos, sin = jnp.cos(ang), jnp.sin(ang)
    one = jnp.ones((seq_len, MLA_NOPE), F32)
    zn = jnp.zeros((seq_len, MLA_NOPE), F32)
    zh = jnp.zeros((seq_len, half), F32)
    rc = jnp.concatenate([one, cos, cos] * 2, axis=1)
    rsa = jnp.concatenate([zn, -sin, zh] * 2, axis=1)
    rsb = jnp.concatenate([zn, zh, sin] * 2, axis=1)
    seg = np.arange(256) // HEAD_DIM
    e = jnp.asarray(seg[:, None] == seg[None, :], BF16)
    idx = np.arange(tm)
    tril = jnp.asarray(idx[None, :] <= idx[:, None], BF16)
    ltri = jnp.asarray(idx[None, :] < idx[:, None], BF16)
    return dict(rc=rc, rsa=rsa, rsb=rsb, e=e, tril=tril, ltri=ltri)


def _alibi():
    n = 2 * HEADS
    s = [2.0 ** (-8.0 * i / n) for i in range(1, n + 1)]
    return s[HEADS:], s[:HEADS]


def kernel(x, norm1_g, w_in, mla_gcq, mla_gckv, mla_wuq, mla_wukv, qk_gq, qk_gk, fox_bf, sink, w_o,
           norm2_g, w_rg, b_rg, w_re, b_re, w1, w3, w2):
    B, S, D = x.shape
    T = B * S
    depth = w_in.shape[0]
    tm = 512
    tabs = _tables(S, tm)
    slopes_dil, slopes_swa = _alibi()
    slopes_swa_perm = [slopes_swa[0], slopes_swa[2], slopes_swa[1], slopes_swa[3]]
    n_assign = T * TOP_K
    assert n_assign % ROW_BLOCK == 0

    all_w = _prep_weights(norm1_g, w_in, mla_gcq, mla_gckv, mla_wuq, mla_wukv, qk_gq, qk_gk, fox_bf, sink,
                          w_o, norm2_g, w_rg, b_rg, w_re, b_re)
    xt = x.reshape(T, D)
    for l in range(depth):
        lw = {name: arr[l] for name, arr in all_w.items()}
        qkv, qkvb, vt, cum, cum8 = _inproj(xt, S, lw, tabs, tm=tm)
        qkv3 = qkv.reshape(B, S, QKV_COLS)
        qkvb3 = qkvb.reshape(B, S, QKVB_COLS)
        oa = _dense_attn(qkv3, vt, 0, None, None, QA, KA, VA, lw["bound"][0]).reshape(T, GROUP_W)
        oc = _dense_attn(qkv3, vt, 2, cum.reshape(B, S, LANES), cum8, QC, KC, VC, lw["bound"][2]).reshape(T, GROUP_W)
        obs, lses = [], []
        for window, dil in DILATED_PAIRS:
            o, lse = _banded(qkvb3, 0, qkvb3, 1, 2, 256, dil, window // dil, slopes_dil, None, True, lw["bound"][1])
            obs.append(o)
            lses.append(lse)
        swa_bound = jnp.maximum(lw["bound"][3], jnp.max(jnp.abs(lw["sink"])) * LOG2E)
        (od,) = _banded(qkv3, QD, qkv3, 2 * KVD, 2 * KVD + 1, 128, 1, SWA_WINDOW - 1, slopes_swa_perm,
                        lw["sink"], False, swa_bound)
        h, xn, route, route_t, cnt = _wo_router(xt, oa, obs, lses, oc, od, lw, tabs["ltri"], tm=tm)

        counts = cnt[0, :N_EXPERTS].astype(jnp.int32)
        *items, start = _expert_items(counts, n_assign // ROW_BLOCK)
        expert = route_t[0:TOP_K].astype(jnp.int32)
        rank = route_t[TOP_K:2 * TOP_K].astype(jnp.int32)
        hit = expert[None] == jnp.arange(N_EXPERTS, dtype=jnp.int32)[:, None, None]
        dest = rank + jnp.sum(jnp.where(hit, start[:, None, None], 0), axis=0)

        xbuf = _scatter_rows(xn.reshape(T, ROW_CHUNKS, LANES), dest)
        ybuf = _expert_mlp(xbuf.reshape(n_assign * ROW_CHUNKS, LANES), items, l, w1, w3, w2)
        xt = _combine(h, route, ybuf.reshape(n_assign, ROW_CHUNKS, LANES), dest)
    return xt.reshape(B, S, D)
```

```python
import functools

import numpy as np
import jax
import jax.numpy as jnp
from jax import lax
from jax.experimental import pallas as pl
from jax.experimental.pallas import tpu as pltpu

F32 = jnp.float32
BF16 = jnp.bfloat16

D_MODEL = 1024
HEAD_DIM = 64
HEADS = 4
GROUP_W = HEADS * HEAD_DIM
NORM_EPS = 1e-6
MLA_Q_RANK, MLA_KV_RANK, MLA_NOPE, MLA_ROPE = 192, 128, 32, 32
ROPE_BASE = 10000.0
DILATED_PAIRS = ((128, 1), (512, 4), (2048, 16))
SWA_WINDOW = 128
N_GROUPS, EXPERTS_PER_GROUP, N_EXPERTS, TOP_K, D_EXPERT = 4, 8, 32, 2, 256
IN_SPLITS = (MLA_Q_RANK, MLA_KV_RANK, MLA_ROPE, 3 * GROUP_W, 3 * GROUP_W, HEADS, GROUP_W, 2 * 2 * HEAD_DIM)

LANES = 128
SUBLANES = 8
ROW_CHUNKS = D_MODEL // LANES
W_COLS = 2688
QKV_COLS = 2048
QA, KA, VA, QC, KC, VC, QD, KVD = range(8)
QKVB_COLS = 768
ONES_ROWS = 16
BAND = 128
BAND_TOKENS = 2048
FG_LANE = 64
ROW_BLOCK = 512
VMEM_LIMIT = 56 * 1024 * 1024

NEG_INF = float("-inf")
LOG2E = 1.4426950408889634
LN2 = 0.6931471805599453
EXP2_SAFE = 60.0


def _cparams(sem):
    return pltpu.CompilerParams(dimension_semantics=sem, vmem_limit_bytes=VMEM_LIMIT)


def _full(shape):
    zeros = (0,) * len(shape)
    return pl.BlockSpec(shape, lambda *_: zeros)


def _head_norm(y, g, e):
    ss = jnp.dot((y * y).astype(BF16), e, preferred_element_type=F32)
    return y * lax.rsqrt(ss * (1.0 / HEAD_DIM) + NORM_EPS) * g


def _rope(y, rc, rsa, rsb):
    outs = []
    for c in range(y.shape[1] // LANES):
        yc = y[:, c * LANES:(c + 1) * LANES]
        outs.append(yc * rc + pltpu.roll(yc, LANES - 16, 1) * rsa + pltpu.roll(yc, 16, 1) * rsb)
    return jnp.concatenate(outs, axis=1)


def _inproj_kernel(x_ref, g1_ref, w_ref, gcq_ref, gckv_ref, wuq_ref, wuk_ref, wuv_ref, gq_ref, gk_ref, e_ref,
                   rc_ref, rsa_ref, rsb_ref, fb_ref, tril_ref, qkv_ref, qkvb_ref, vt_ref, cum_ref, cumt_ref, acc_ref,
                   carry_ref,
                   *, tiles_per_seq):
    i = pl.program_id(0)
    tm = x_ref.shape[0]

    @pl.when(i % tiles_per_seq == 0)
    def _():
        carry_ref[...] = jnp.zeros_like(carry_ref)

    x = x_ref[...]
    ms = jnp.mean(x * x, axis=-1, keepdims=True)
    xn = (x * lax.rsqrt(ms + NORM_EPS) * g1_ref[...]).astype(BF16)
    acc_ref[...] = jnp.dot(xn, w_ref[...], preferred_element_type=F32)

    e = e_ref[...]
    rc, rsa, rsb = rc_ref[...], rsa_ref[...], rsb_ref[...]

    ckv = acc_ref[:, 0:128]
    ckvn = (ckv * lax.rsqrt(jnp.mean(ckv * ckv, axis=-1, keepdims=True) + NORM_EPS) * gckv_ref[...]).astype(BF16)
    u = acc_ref[:, 128:384]
    lane256 = lax.broadcasted_iota(jnp.int32, (tm, 256), 1)
    ssq = jnp.sum(jnp.where(lane256 < MLA_Q_RANK, u * u, 0.0), axis=-1, keepdims=True) * (1.0 / MLA_Q_RANK)
    un = (u * lax.rsqrt(ssq + NORM_EPS) * gcq_ref[...]).astype(BF16)
    qa = jnp.dot(un, wuq_ref[...], preferred_element_type=F32)
    ka = jnp.dot(ckvn, wuk_ref[...], preferred_element_type=F32) + acc_ref[:, 384:640]
    va = jnp.dot(ckvn, wuv_ref[...], preferred_element_type=F32)
    qa = _rope(_head_norm(qa, gq_ref[0:1, :], e), rc, rsa, rsb)
    ka = _rope(_head_norm(ka, gk_ref[0:1, :], e), rc, rsa, rsb)
    qkv_ref[:, QA * 256:(QA + 1) * 256] = qa.astype(BF16)
    qkv_ref[:, KA * 256:(KA + 1) * 256] = ka.astype(BF16)
    qkv_ref[:, VA * 256:(VA + 1) * 256] = va.astype(BF16)
    vt_ref[0:2] = jnp.transpose(va).astype(BF16).reshape(2, LANES, tm)
    vt_ref[2:4] = jnp.transpose(acc_ref[:, 1920:2176]).astype(BF16).reshape(2, LANES, tm)

    qkvb_ref[:, 0:256] = _head_norm(acc_ref[:, 640:896], gq_ref[1:2, :], e)
    qkvb_ref[:, 256:512] = _head_norm(acc_ref[:, 896:1152], gk_ref[1:2, :], e)
    qkvb_ref[:, 512:768] = acc_ref[:, 1152:1408]
    qkv_ref[:, QC * 256:(QC + 1) * 256] = _head_norm(acc_ref[:, 1408:1664], gq_ref[2:3, :], e).astype(BF16)
    qkv_ref[:, KC * 256:(KC + 1) * 256] = _head_norm(acc_ref[:, 1664:1920], gk_ref[2:3, :], e).astype(BF16)
    qkv_ref[:, VC * 256:(VC + 1) * 256] = acc_ref[:, 1920:2176].astype(BF16)

    qd = _head_norm(acc_ref[:, 2176:2432], gq_ref[3:4, :], e)
    kd = _head_norm(acc_ref[:, 2432:2560], gk_ref[3:4, 0:128], e[0:128, 0:128])
    qkv_ref[:, QD * 256:(QD + 1) * 256] = qd.astype(BF16)
    qkv_ref[:, KVD * 256:KVD * 256 + 128] = kd.astype(BF16)
    qkv_ref[:, KVD * 256 + 128:(KVD + 1) * 256] = acc_ref[:, 2560:2688].astype(BF16)

    z = u[:, 128:256] + fb_ref[...]
    ls = jnp.minimum(z, 0.0) - jnp.log(1.0 + jnp.exp(-jnp.abs(z)))
    lane128 = lax.broadcasted_iota(jnp.int32, (tm, LANES), 1)
    ls = jnp.where((lane128 >= FG_LANE) & (lane128 < FG_LANE + HEADS), ls, 0.0)
    hi = ls.astype(BF16)
    r1 = ls - hi.astype(F32)
    mid = r1.astype(BF16)
    lo = (r1 - mid.astype(F32)).astype(BF16)
    tril = tril_ref[...]
    local = (jnp.dot(tril, hi, preferred_element_type=F32) + jnp.dot(tril, mid, preferred_element_type=F32)
             + jnp.dot(tril, lo, preferred_element_type=F32))
    cum = local + carry_ref[0:1, :]
    cum_ref[...] = cum
    cum_t = jnp.transpose(cum)
    pad = jnp.zeros((SUBLANES - 2, tm), F32)
    for p in range(2):
        cumt_ref[p] = jnp.concatenate([cum_t[FG_LANE + 2 * p:FG_LANE + 2 * p + 2, :], pad], axis=0)
    carry_ref[...] = jnp.broadcast_to(cum[tm - 1:tm, :], carry_ref.shape)


def _inproj(xt, seq_len, lw, tabs, tm=512):
    T = xt.shape[0]
    tps = seq_len // tm
    kern = functools.partial(_inproj_kernel, tiles_per_seq=tps)
    tab_spec = pl.BlockSpec((tm, LANES), lambda i: (i % tps, 0))
    return pl.pallas_call(
        kern,
        grid=(T // tm,),
        in_specs=[pl.BlockSpec((tm, D_MODEL), lambda i: (i, 0)),
                  _full((1, D_MODEL)), _full((D_MODEL, W_COLS)), _full((1, 256)), _full((1, 128)),
                  _full((256, 256)), _full((128, 256)), _full((128, 256)), _full((4, 256)), _full((4, 256)),
                  _full((256, 256)), tab_spec, tab_spec, tab_spec, _full((1, LANES)), _full((tm, tm))],
        out_specs=[pl.BlockSpec((tm, QKV_COLS), lambda i: (i, 0)), pl.BlockSpec((tm, QKVB_COLS), lambda i: (i, 0)),
                   pl.BlockSpec((4, LANES, tm), lambda i: (0, 0, i)), pl.BlockSpec((tm, LANES), lambda i: (i, 0)),
                   pl.BlockSpec((2, SUBLANES, tm), lambda i: (0, 0, i))],
        out_shape=[jax.ShapeDtypeStruct((T, QKV_COLS), BF16), jax.ShapeDtypeStruct((T, QKVB_COLS), F32),
                   jax.ShapeDtypeStruct((4, LANES, T), BF16), jax.ShapeDtypeStruct((T, LANES), F32),
                   jax.ShapeDtypeStruct((2, SUBLANES, T), F32)],
        scratch_shapes=[pltpu.VMEM((tm, W_COLS), F32), pltpu.VMEM((8, LANES), F32)],
        compiler_params=_cparams(("arbitrary",)),
        name="inproj",
    )(xt, lw["g1"], lw["wp"], lw["gcq"], lw["gckv"], lw["wuq"], lw["wuk"], lw["wuv"], lw["gq"], lw["gk"],
      tabs["e"], tabs["rc"], tabs["rsa"], tabs["rsb"], lw["fb"], tabs["tril"])


def _split_heads(x):
    half0 = lax.broadcasted_iota(jnp.int32, x.shape, 1) < HEAD_DIM
    zero = jnp.zeros_like(x)
    return jnp.where(half0, x, zero), jnp.where(half0, zero, x)


def _causal_mask(s):
    row = lax.broadcasted_iota(jnp.int32, s.shape, 0)
    col = lax.broadcasted_iota(jnp.int32, s.shape, 1)
    return jnp.where(col <= row, s, NEG_INF)


def _cum_column(blk, h):
    lane = lax.broadcasted_iota(jnp.int32, blk.shape, 1)
    return jnp.sum(jnp.where(lane == FG_LANE + 2 * pl.program_id(1) + h, blk * LOG2E, 0.0), axis=1, keepdims=True)


def _row_cum(cq_ref, h):
    return _cum_column(cq_ref[0], h)


def _dense_bounded_kernel(*refs, tq, tk, fox):
    if fox:
        q_ref, k_ref, vt_ref, ccol_ref, crow_ref, o_ref, acc_scr = refs
    else:
        q_ref, k_ref, vt_ref, o_ref, acc_scr = refs
    qi = pl.program_id(2)
    qh = _split_heads(q_ref[0])
    ones = jnp.ones((ONES_ROWS, tk), BF16)
    acc_scr[...] = jnp.zeros_like(acc_scr)

    def step(start, c0, masked):
        k = k_ref[0, pl.ds(start, tk), :]
        for h in range(2):
            s = lax.dot_general(k, qh[h][c0:, :], (((1,), (1,)), ((), ())), preferred_element_type=F32)
            if fox:
                crow = crow_ref[0, h:h + 1, pl.ds(pl.multiple_of(qi * tq + c0, tk), tq - c0)] * LOG2E
                s = (s + crow) - _cum_column(ccol_ref[0, pl.ds(start, tk), :], h)
            if masked:
                key = lax.broadcasted_iota(jnp.int32, s.shape, 0)
                query = lax.broadcasted_iota(jnp.int32, s.shape, 1)
                s = jnp.where(key <= query, s, NEG_INF)
            lhs = jnp.concatenate([vt_ref[0, h * HEAD_DIM:(h + 1) * HEAD_DIM, pl.ds(start, tk)], ones], axis=0)
            acc_scr[h, :, c0:] += jnp.dot(lhs, jnp.exp2(s).astype(BF16), preferred_element_type=F32)

    def body(j, carry):
        step(pl.multiple_of(j * tk, tk), 0, False)
        return carry

    lax.fori_loop(0, qi * (tq // tk), body, 0)
    for d in range(tq // tk):
        step(pl.multiple_of(qi * tq + d * tk, tk), d * tk, True)
    o_t = jnp.concatenate([acc_scr[h, 0:HEAD_DIM, :] / acc_scr[h, HEAD_DIM:HEAD_DIM + 1, :] for h in range(2)], axis=0)
    o_ref[0] = jnp.transpose(o_t).astype(o_ref.dtype)


def _dense_online_kernel(*refs, tq, tk, fox):
    if fox:
        q_ref, k_ref, v_ref, cq_ref, ck_ref, o_ref, m_scr, l_scr, acc_scr = refs
    else:
        q_ref, k_ref, v_ref, o_ref, m_scr, l_scr, acc_scr = refs
    qi = pl.program_id(2)
    qh = _split_heads(q_ref[0])
    m_scr[...] = jnp.full(m_scr.shape, NEG_INF, F32)
    l_scr[...] = jnp.zeros_like(l_scr)
    acc_scr[...] = jnp.zeros_like(acc_scr)
    if fox:
        cq = [_row_cum(cq_ref, h) for h in range(2)]

    def step(start, r0, masked):
        k = k_ref[0, pl.ds(start, tk), :]
        v = v_ref[0, pl.ds(start, tk), :]
        for h in range(2):
            s = lax.dot_general(qh[h][r0:, :], k, (((1,), (1,)), ((), ())), preferred_element_type=F32)
            if fox:
                s = (s + cq[h][r0:, :]) - ck_ref[0, h:h + 1, pl.ds(start, tk)] * LOG2E
            if masked:
                s = _causal_mask(s)
            m_prev = m_scr[h, r0:, :]
            m_next = jnp.maximum(m_prev, jnp.max(s, axis=1, keepdims=True))
            p = jnp.exp2(s - jnp.concatenate([m_next] * (tk // LANES), axis=1))
            alpha = jnp.exp2(m_prev - m_next)
            l_scr[h, r0:, :] = alpha * l_scr[h, r0:, :] + jnp.sum(p, axis=1, keepdims=True)
            m_scr[h, r0:, :] = m_next
            pv = jnp.dot(p.astype(BF16), v, preferred_element_type=F32)
            acc_scr[h, r0:, :] = acc_scr[h, r0:, :] * alpha + pv

    def body(j, carry):
        step(pl.multiple_of(j * tk, tk), 0, False)
        return carry

    lax.fori_loop(0, qi * (tq // tk), body, 0)
    for d in range(tq // tk):
        step(pl.multiple_of(qi * tq + d * tk, tk), d * tk, True)
    half0 = lax.broadcasted_iota(jnp.int32, (tq, LANES), 1) < HEAD_DIM
    o = jnp.where(half0, acc_scr[0] / l_scr[0], acc_scr[1] / l_scr[1])
    o_ref[0] = o.astype(o_ref.dtype)


def _dense_attn(qkv3, vt, jvt, cum3, cum8, jq, jk, jv, logit_bound, tq_bounded=4096, tq_online=2048, tk=512):
    B, S, _ = qkv3.shape
    fox = cum8 is not None
    k_spec = pl.BlockSpec((1, S, LANES), lambda b, p, i: (b, 0, 2 * jk + p))
    v_spec = pl.BlockSpec((1, S, LANES), lambda b, p, i: (b, 0, 2 * jv + p))
    vt_spec = pl.BlockSpec((1, LANES, S), lambda b, p, i: (jvt + p, 0, b))
    cum_row_spec = pl.BlockSpec((1, SUBLANES, S), lambda b, p, i: (p, 0, b))
    name = "fox_attn" if fox else "mla_attn"

    def build(kern, tq, suffix, kv_specs, cum_specs, scratch):
        tq = min(tq, S)
        assert S % tq == 0 and tq % tk == 0
        q_spec = pl.BlockSpec((1, tq, LANES), lambda b, p, i: (b, i, 2 * jq + p))
        return pl.pallas_call(
            functools.partial(kern, tq=tq, tk=tk, fox=fox),
            grid=(B, 2, S // tq),
            in_specs=[q_spec] + kv_specs + (cum_specs(tq) if fox else []),
            out_specs=pl.BlockSpec((1, tq, LANES), lambda b, p, i: (b, i, p)),
            out_shape=jax.ShapeDtypeStruct((B, S, GROUP_W), BF16),
            scratch_shapes=scratch(tq),
            compiler_params=_cparams(("parallel", "parallel", "arbitrary")),
            name=name + suffix)

    bounded = build(_dense_bounded_kernel, tq_bounded, "_bounded", [k_spec, vt_spec],
                    lambda tq: [pl.BlockSpec((1, S, LANES), lambda b, p, i: (b, 0, 0)), cum_row_spec],
                    lambda tq: [pltpu.VMEM((2, HEAD_DIM + ONES_ROWS, tq), F32)])
    online = build(_dense_online_kernel, tq_online, "_online", [k_spec, v_spec],
                   lambda tq: [pl.BlockSpec((1, tq, LANES), lambda b, p, i: (b, i, 0)), cum_row_spec],
                   lambda tq: [pltpu.VMEM((2, tq, LANES), F32)] * 3)
    cums = (cum3, cum8) if fox else ()
    return lax.cond(logit_bound <= EXP2_SAFE,
                    lambda vt_, *c: bounded(qkv3, qkv3, vt_, *c),
                    lambda vt_, *c: online(qkv3, qkv3, qkv3, *c), vt, *cums)


def _fold_rows(r, n, dil):
    return slice(r, r + n) if dil == 1 else pl.ds(r, n, stride=dil)


def _banded_kernel(*refs, dil, maxdist, slopes, kv_chunks, has_sink, want_lse, bounded):
    refs = list(refs)
    sink_ref = refs.pop(0) if has_sink else None
    q_refs = [refs.pop(0) for _ in range(2)]
    kv_refs = [[refs.pop(0) for _ in range(4)] for _ in range(kv_chunks)]
    o_ref = refs.pop(0)
    lse_ref = refs.pop(0) if want_lse else None
    o_scr = refs.pop(0)
    lse_scr = refs.pop(0) if want_lse else None
    nq = q_refs[0].shape[1] // dil
    row = lax.broadcasted_iota(jnp.int32, (BAND, 2 * BAND), 0)
    col = lax.broadcasted_iota(jnp.int32, (BAND, 2 * BAND), 1)
    dist = row + BAND - col
    valid = (dist >= 0) & (dist <= maxdist)
    distf = dist.astype(F32) * (float(dil) * LOG2E)
    bias = [jnp.where(valid, -slopes[h] * distf, NEG_INF) for h in range(HEADS)]
    no_prev = jnp.where(col < BAND, jnp.where(pl.program_id(1) == 0, NEG_INF, 0.0), 0.0)
    bias_first = [b + no_prev for b in bias]
    half0_k = lax.broadcasted_iota(jnp.int32, (2 * BAND, LANES), 1) < HEAD_DIM
    ones0 = jnp.where(half0_k, 1.0, 0.0).astype(BF16)
    ones1 = jnp.where(half0_k, 0.0, 1.0).astype(BF16)
    half0 = lax.broadcasted_iota(jnp.int32, (BAND, LANES), 1) < HEAD_DIM
    for c in range(2):
        kp_ref, kc_ref, vp_ref, vc_ref = kv_refs[c if kv_chunks == 2 else 0]
        if has_sink:
            sink_lanes = jnp.where(half0[0:1, :], sink_ref[2 * c], sink_ref[2 * c + 1])
        for r in range(dil):
            prev_rows, cur_rows = _fold_rows(r, BAND, dil), _fold_rows(r, nq, dil)
            qh = _split_heads(q_refs[c][0, cur_rows, :].astype(BF16))
            kf = jnp.concatenate([kp_ref[0, prev_rows, :], kc_ref[0, cur_rows, :]], axis=0).astype(BF16)
            vh = _split_heads(jnp.concatenate([vp_ref[0, prev_rows, :], vc_ref[0, cur_rows, :]], axis=0).astype(BF16))
            outs, lses = [], []
            for j in range(nq // BAND):
                keys = slice(j * BAND, (j + 2) * BAND)
                v2 = jnp.concatenate([jnp.concatenate([vh[0][keys], ones0], axis=1),
                                      jnp.concatenate([vh[1][keys], ones1], axis=1)], axis=0)
                ps, ms = [], []
                for e in range(2):
                    s = lax.dot_general(qh[e][j * BAND:(j + 1) * BAND], kf[keys], (((1,), (1,)), ((), ())),
                                        preferred_element_type=F32)
                    s = s + (bias_first if j == 0 else bias)[2 * c + e]
                    if bounded:
                        ps.append(jnp.exp2(s).astype(BF16))
                    else:
                        m = jnp.max(s, axis=1, keepdims=True)
                        ps.append(jnp.exp2(s - m).astype(BF16))
                        ms.append(m)
                acc = jnp.dot(jnp.concatenate(ps, axis=1), v2, preferred_element_type=F32)
                unnorm, den = acc[:, 0:LANES], acc[:, LANES:2 * LANES]
                if bounded:
                    lse = jnp.log(den)
                    outs.append(unnorm / (den + jnp.exp(sink_lanes)) if has_sink else unnorm / den)
                else:
                    shift = jnp.where(half0, ms[0], ms[1]) * LN2
                    lse = shift + jnp.log(den)
                    if has_sink:
                        mx = jnp.maximum(lse, sink_lanes)
                        total = mx + jnp.log(jnp.exp(lse - mx) + jnp.exp(sink_lanes - mx))
                        outs.append(unnorm * jnp.exp(shift - total))
                    else:
                        outs.append(unnorm / den)
                if want_lse:
                    lses.append(lse)
            o_scr[cur_rows, :] = jnp.concatenate(outs, axis=0)
            if want_lse:
                lse_scr[cur_rows, :] = jnp.concatenate(lses, axis=0)
        o_ref[0, :, c * LANES:(c + 1) * LANES] = o_scr[...]
        if want_lse:
            lse_ref[0, :, c * LANES:(c + 1) * LANES] = lse_scr[...]


def _banded(q_arr, jq, kv_arr, jk, jv, kw, dil, maxdist, slopes, sink, want_lse, logit_bound):
    B, S, _ = q_arr.shape
    tb = min(BAND_TOKENS, S)
    pb = BAND * dil
    assert S % tb == 0 and tb % pb == 0 and maxdist <= BAND
    ratio = tb // pb
    has_sink = sink is not None
    kern = functools.partial(_banded_kernel, dil=dil, maxdist=maxdist, slopes=tuple(slopes),
                             kv_chunks=kw // LANES, has_sink=has_sink, want_lse=want_lse)
    prev = lambda j: pl.BlockSpec((1, pb, LANES), lambda b, i: (b, jnp.maximum(i * ratio - 1, 0), j))
    cur = lambda j: pl.BlockSpec((1, tb, LANES), lambda b, i: (b, i, j))
    in_specs = [cur(2 * jq), cur(2 * jq + 1)]
    args = [q_arr, q_arr]
    for c in range(kw // LANES):
        jkc, jvc = jk * (kw // LANES) + c, jv * (kw // LANES) + c
        in_specs += [prev(jkc), cur(jkc), prev(jvc), cur(jvc)]
        args += [kv_arr] * 4
    if has_sink:
        in_specs.insert(0, pl.BlockSpec(memory_space=pltpu.SMEM))
        args.insert(0, sink)
    n_out = 2 if want_lse else 1
    name = f"banded_d{dil}" if want_lse else "swa_attn"
    call = lambda bounded: pl.pallas_call(
        functools.partial(kern, bounded=bounded),
        grid=(B, S // tb),
        in_specs=in_specs,
        out_specs=[pl.BlockSpec((1, tb, 256), lambda b, i: (b, i, 0))] * n_out,
        out_shape=[jax.ShapeDtypeStruct((B, S, 256), F32)] * n_out,
        scratch_shapes=[pltpu.VMEM((tb, LANES), F32)] * n_out,
        compiler_params=_cparams(("parallel", "parallel")),
        name=name + ("_bounded" if bounded else "_rowmax"),
    )
    outs = lax.cond(logit_bound <= EXP2_SAFE, lambda *a: call(True)(*a), lambda *a: call(False)(*a), *args)
    return [o.reshape(B * S, 256) for o in outs]


def _wo_router_kernel(x_ref, oa_ref, ob0_ref, ob1_ref, ob2_ref, l0_ref, l1_ref, l2_ref, oc_ref, od_ref,
                      wo_ref, g2_ref, wr_ref, br_ref, ltri_ref, h_ref, xn_ref, route_ref, route_t_ref, cnt_ref,
                      carry_ref):
    i = pl.program_id(0)
    tm = x_ref.shape[0]

    @pl.when(i == 0)
    def _():
        carry_ref[...] = jnp.zeros_like(carry_ref)

    la, lb, lc = l0_ref[...], l1_ref[...], l2_ref[...]
    mx = jnp.maximum(jnp.maximum(la, lb), lc)
    ea, eb, ec = jnp.exp(la - mx), jnp.exp(lb - mx), jnp.exp(lc - mx)
    ob = (ea * ob0_ref[...] + eb * ob1_ref[...] + ec * ob2_ref[...]) / (ea + eb + ec)
    mix = jnp.concatenate([oa_ref[...], ob.astype(BF16), oc_ref[...], od_ref[...].astype(BF16)], axis=1)
    h = x_ref[...] + jnp.dot(mix, wo_ref[...], preferred_element_type=F32)
    h_ref[...] = h
    xn = h * lax.rsqrt(jnp.mean(h * h, axis=-1, keepdims=True) + NORM_EPS) * g2_ref[...]
    _store_row_tiles(xn_ref, xn)

    xh = xn.astype(BF16)
    xl = (xn - xh.astype(F32)).astype(BF16)
    z = (jnp.dot(xh, wr_ref[0], preferred_element_type=F32) + jnp.dot(xl, wr_ref[0], preferred_element_type=F32)
         + jnp.dot(xh, wr_ref[1], preferred_element_type=F32)) + br_ref[...]
    lane = lax.broadcasted_iota(jnp.int32, (tm, LANES), 1)
    lanef = lane.astype(F32)
    big = float(LANES)
    zg = jnp.where((lane >= N_EXPERTS) & (lane < N_EXPERTS + N_GROUPS), z, NEG_INF)
    mg = jnp.max(zg, axis=1, keepdims=True)
    p_g = 1.0 / jnp.sum(jnp.exp(zg - mg), axis=1, keepdims=True)
    gsel = jnp.min(jnp.where(zg == mg, lanef, big), axis=1, keepdims=True) - float(N_EXPERTS)
    lo = gsel * float(EXPERTS_PER_GROUP)
    ze = jnp.where((lanef >= lo) & (lanef < lo + float(EXPERTS_PER_GROUP)), z, NEG_INF)
    m1 = jnp.max(ze, axis=1, keepdims=True)
    i1 = jnp.min(jnp.where(ze == m1, lanef, big), axis=1, keepdims=True)
    ze2 = jnp.where(lanef == i1, NEG_INF, ze)
    m2 = jnp.max(ze2, axis=1, keepdims=True)
    i2 = jnp.min(jnp.where(ze2 == m2, lanef, big), axis=1, keepdims=True)
    e2 = jnp.exp(m2 - m1)
    gate1 = p_g / (1.0 + e2)
    gate2 = p_g * e2 / (1.0 + e2)

    oh1 = jnp.where(lanef == i1, 1.0, 0.0)
    oh2 = jnp.where(lanef == i2, 1.0, 0.0)
    oh = oh1 + oh2
    before = carry_ref[0:1, :] + jnp.dot(ltri_ref[...], oh.astype(BF16), preferred_element_type=F32)
    r1 = jnp.sum(before * oh1, axis=1, keepdims=True)
    r2 = jnp.sum(before * oh2, axis=1, keepdims=True)
    total = carry_ref[0:1, :] + jnp.sum(oh, axis=0, keepdims=True)
    carry_ref[...] = jnp.broadcast_to(total, carry_ref.shape)
    cnt_ref[...] = jnp.broadcast_to(total, cnt_ref.shape)

    route = jnp.where(lane == 0, i1, jnp.where(lane == 1, i2, jnp.where(lane == 2, r1, jnp.where(
        lane == 3, r2, jnp.where(lane == 4, gate1, jnp.where(lane == 5, gate2, 0.0))))))
    route_ref[...] = route
    route_t_ref[...] = jnp.transpose(route)[0:8, :]


def _wo_router(xt, oa, obs, lses, oc, od, lw, ltri, tm=512):
    T = xt.shape[0]
    row = lambda w: pl.BlockSpec((tm, w), lambda i: (i, 0))
    return pl.pallas_call(
        _wo_router_kernel,
        grid=(T // tm,),
        in_specs=[row(D_MODEL), row(256), row(256), row(256), row(256), row(256), row(256), row(256), row(256),
                  row(256), _full((D_MODEL, D_MODEL)), _full((1, D_MODEL)), _full((2, D_MODEL, LANES)),
                  _full((1, LANES)), _full((tm, tm))],
        out_specs=[row(D_MODEL), pl.BlockSpec((tm * ROW_CHUNKS, LANES), lambda i: (i, 0)), row(LANES),
                   pl.BlockSpec((8, tm), lambda i: (0, i)), _full((8, LANES))],
        out_shape=[jax.ShapeDtypeStruct((T, D_MODEL), F32), jax.ShapeDtypeStruct((T * ROW_CHUNKS, LANES), F32),
                   jax.ShapeDtypeStruct((T, LANES), F32), jax.ShapeDtypeStruct((8, T), F32),
                   jax.ShapeDtypeStruct((8, LANES), F32)],
        scratch_shapes=[pltpu.VMEM((8, LANES), F32)],
        compiler_params=_cparams(("arbitrary",)),
        name="wo_router",
    )(xt, oa, obs[0], obs[1], obs[2], lses[0], lses[1], lses[2], oc, od, lw["wo"], lw["g2"], lw["wr"], lw["br"], ltri)


def _store_row_tiles(ref, x, accumulate=False):
    rows = x.shape[0]
    for c in range(ROW_CHUNKS):
        idx = pl.ds(c, rows, stride=ROW_CHUNKS)
        chunk = x[:, c * LANES:(c + 1) * LANES]
        ref[idx, :] = ref[idx, :] + chunk if accumulate else chunk


def _load_row_tiles(ref, rows, lead=()):
    return jnp.concatenate([ref[lead + (pl.ds(c, rows, stride=ROW_CHUNKS), slice(None))] for c in range(ROW_CHUNKS)],
                           axis=1)


def _tile_copy(src, dst, sem):
    return pltpu.make_async_copy(src, dst, sem)


def _tile_major(dest_t, tile):
    k, T = dest_t.shape
    return dest_t.reshape(k, T // tile, tile).transpose(1, 0, 2).reshape(-1)


def _scatter_kernel(dest_ref, x_ref, buf_out, sem, *, ts):
    def issue(t, carry):
        for k in range(TOP_K):
            _tile_copy(x_ref.at[t], buf_out.at[dest_ref[k * ts + t]], sem).start(priority=k)
        return carry

    lax.fori_loop(0, ts, issue, 0, unroll=8)
    for k in range(TOP_K):
        _tile_copy(x_ref, buf_out.at[pl.ds(0, ts)], sem).wait()


def _scatter_rows(xn_tiles, dest_t, ts=1024):
    T = xn_tiles.shape[0]
    dest_flat = _tile_major(dest_t, ts)
    return pl.pallas_call(
        functools.partial(_scatter_kernel, ts=ts),
        grid=(T // ts,),
        in_specs=[pl.BlockSpec((TOP_K * ts,), lambda i: (i,), memory_space=pltpu.SMEM),
                  pl.BlockSpec((ts, ROW_CHUNKS, LANES), lambda i: (i, 0, 0))],
        out_specs=pl.BlockSpec(memory_space=pl.ANY),
        out_shape=jax.ShapeDtypeStruct((TOP_K * T, ROW_CHUNKS, LANES), xn_tiles.dtype),
        scratch_shapes=[pltpu.SemaphoreType.DMA(())],
        compiler_params=_cparams(("arbitrary",)),
        name="moe_scatter",
    )(dest_flat, xn_tiles)


def _expert_items(counts, n_blocks):
    n_items_max = n_blocks + N_EXPERTS
    end = jnp.cumsum(counts)
    start = end - counts
    first_b = start // ROW_BLOCK
    per_expert = jnp.where(counts > 0, (end - 1) // ROW_BLOCK - first_b + 1, 0)
    item_end = jnp.cumsum(per_expert)
    item_start = item_end - per_expert
    n_items = item_end[-1:]
    idx = jnp.minimum(jnp.arange(n_items_max, dtype=jnp.int32), n_items - 1)
    e = jnp.minimum(jnp.sum((item_end[None, :] <= idx[:, None]).astype(jnp.int32), axis=1), N_EXPERTS - 1)
    onehot = (e[:, None] == jnp.arange(N_EXPERTS, dtype=jnp.int32)[None, :]).astype(jnp.int32)
    pick = lambda table: jnp.sum(onehot * table[None, :], axis=1)
    b = pick(first_b) + idx - pick(item_start)
    lo = jnp.maximum(pick(start) - b * ROW_BLOCK, 0)
    hi = jnp.minimum(pick(end) - b * ROW_BLOCK, ROW_BLOCK)
    return e, b, lo, hi, n_items.astype(jnp.int32), start


def _expert_kernel(e_ref, b_ref, lo_ref, hi_ref, n_ref, x_ref, w1_ref, w3_ref, w2_ref, y_ref):
    del e_ref
    i = pl.program_id(0)

    @pl.when(i < n_ref[0])
    def _():
        x = _load_row_tiles(x_ref, ROW_BLOCK).astype(BF16)
        a = jnp.dot(x, w1_ref[0, 0].astype(BF16), preferred_element_type=F32)
        b = jnp.dot(x, w3_ref[0, 0].astype(BF16), preferred_element_type=F32)
        hid = (a / (1.0 + jnp.exp(-a))) * b
        y = jnp.dot(hid.astype(BF16), w2_ref[0, 0].astype(BF16), preferred_element_type=F32)
        row = lax.broadcasted_iota(jnp.int32, y.shape, 0)
        y = jnp.where((row >= lo_ref[i]) & (row < hi_ref[i]), y, 0.0)
        first_visit = jnp.logical_or(i == 0, b_ref[i] != b_ref[jnp.maximum(i - 1, 0)])

        @pl.when(first_visit)
        def _():
            _store_row_tiles(y_ref, y)

        @pl.when(jnp.logical_not(first_visit))
        def _():
            _store_row_tiles(y_ref, y, accumulate=True)


def _expert_mlp(xbuf, items, layer, w1, w3, w2):
    D = D_MODEL
    e, b, lo, hi, n_items = items
    x_map = lambda i, e_, b_, lo_, hi_, n_: (b_[i], 0)
    w_map = lambda i, e_, b_, lo_, hi_, n_: (layer, e_[i], 0, 0)
    grid_spec = pltpu.PrefetchScalarGridSpec(
        num_scalar_prefetch=5,
        grid=(e.shape[0],),
        in_specs=[pl.BlockSpec((ROW_BLOCK * ROW_CHUNKS, LANES), x_map),
                  pl.BlockSpec((1, 1, D, D_EXPERT), w_map),
                  pl.BlockSpec((1, 1, D, D_EXPERT), w_map),
                  pl.BlockSpec((1, 1, D_EXPERT, D), w_map)],
        out_specs=pl.BlockSpec((ROW_BLOCK * ROW_CHUNKS, LANES), x_map),
    )
    return pl.pallas_call(
        _expert_kernel,
        grid_spec=grid_spec,
        out_shape=jax.ShapeDtypeStruct(xbuf.shape, xbuf.dtype),
        compiler_params=_cparams(("arbitrary",)),
        name="moe_experts",
    )(e, b, lo, hi, n_items, xbuf, w1, w3, w2)


def _combine_kernel(dest_ref, dest_next_ref, h_ref, route_ref, y_hbm, o_ref, rows_scr, sems, *, tc):
    i = pl.program_id(0)
    n = pl.num_programs(0)
    slot = i % 2

    def gather(d_ref, s):
        def issue(t, carry):
            for k in range(TOP_K):
                dst = rows_scr.at[s, k, pl.ds(pl.multiple_of(t * ROW_CHUNKS, ROW_CHUNKS), ROW_CHUNKS)]
                _tile_copy(y_hbm.at[d_ref[k * tc + t]], dst, sems.at[s]).start(priority=k)
            return carry

        lax.fori_loop(0, tc, issue, 0, unroll=8)

    @pl.when(i == 0)
    def _():
        gather(dest_ref, 0)

    @pl.when(i + 1 < n)
    def _():
        gather(dest_next_ref, 1 - slot)

    for k in range(TOP_K):
        _tile_copy(rows_scr.at[slot, k], rows_scr.at[slot, k], sems.at[slot]).wait()
    route = route_ref[...]
    gates = [jnp.broadcast_to(route[:, 4 + k:5 + k], (tc, LANES)) for k in range(TOP_K)]
    rows = [_load_row_tiles(rows_scr, tc, lead=(slot, k)) for k in range(TOP_K)]
    for c in range(ROW_CHUNKS):
        cols = slice(c * LANES, (c + 1) * LANES)
        o_ref[:, cols] = h_ref[:, cols] + gates[0] * rows[0][:, cols] + gates[1] * rows[1][:, cols]


def _combine(h, route, ybuf, dest_t, tc=256):
    T, D = h.shape
    n = T // tc
    dest_flat = _tile_major(dest_t, tc)
    return pl.pallas_call(
        functools.partial(_combine_kernel, tc=tc),
        grid=(n,),
        in_specs=[pl.BlockSpec((TOP_K * tc,), lambda i: (i,), memory_space=pltpu.SMEM),
                  pl.BlockSpec((TOP_K * tc,), lambda i: (jnp.minimum(i + 1, n - 1),), memory_space=pltpu.SMEM),
                  pl.BlockSpec((tc, D), lambda i: (i, 0)), pl.BlockSpec((tc, LANES), lambda i: (i, 0)),
                  pl.BlockSpec(memory_space=pl.ANY)],
        out_specs=pl.BlockSpec((tc, D), lambda i: (i, 0)),
        out_shape=jax.ShapeDtypeStruct((T, D), F32),
        scratch_shapes=[pltpu.VMEM((2, TOP_K, tc * ROW_CHUNKS, LANES), ybuf.dtype), pltpu.SemaphoreType.DMA((2,))],
        compiler_params=_cparams(("arbitrary",)),
        name="moe_combine",
    )(dest_flat, dest_flat, h, route, ybuf)


def _prep_weights(norm1_g, w_in, mla_gcq, mla_gckv, mla_wuq, mla_wukv, qk_gq, qk_gk, fox_bf, sink, w_o,
                  norm2_g, w_rg, b_rg, w_re, b_re):
    L = w_in.shape[0]
    offs = np.concatenate([[0], np.cumsum(IN_SPLITS)])
    cq, ckv, kr, pb, pc, fg, pdq, pdkv = [w_in[:, :, offs[j]:offs[j + 1]] for j in range(8)]
    z = lambda *shape: jnp.zeros((L,) + shape, F32)
    perm = np.array([0, 2, 1, 3])
    ublock = jnp.concatenate([cq, fg, z(D_MODEL, 256 - MLA_Q_RANK - HEADS)], axis=2)
    kr_rep = jnp.concatenate([z(D_MODEL, MLA_NOPE), kr] * HEADS, axis=2)
    dq = pdq.reshape(L, D_MODEL, HEADS, HEAD_DIM)[:, :, perm].reshape(L, D_MODEL, GROUP_W)
    wp = jnp.concatenate([ckv, ublock, kr_rep, pb, pc, dq, pdkv], axis=2).astype(BF16)

    wuq = jnp.concatenate([mla_wuq, z(256 - MLA_Q_RANK, GROUP_W)], axis=1).astype(BF16)
    wukv = mla_wukv.reshape(L, MLA_KV_RANK, HEADS, MLA_NOPE + HEAD_DIM)
    wuk = jnp.concatenate([wukv[..., :MLA_NOPE], z(MLA_KV_RANK, HEADS, MLA_ROPE)], axis=3)
    wuk = wuk.reshape(L, MLA_KV_RANK, GROUP_W).astype(BF16)
    wuv = wukv[..., MLA_NOPE:].reshape(L, MLA_KV_RANK, GROUP_W).astype(BF16)
    gcq = jnp.concatenate([mla_gcq, z(256 - MLA_Q_RANK)], axis=1)[:, None, :]
    scale = HEAD_DIM ** -0.5
    gq = jnp.tile(qk_gq, (1, 1, HEADS)) * (scale * LOG2E)
    gk = jnp.tile(qk_gk, (1, 1, HEADS))
    bound = HEAD_DIM * jnp.max(jnp.abs(gq), axis=2) * jnp.max(jnp.abs(gk), axis=2)
    fb = z(1, LANES).at[:, 0, FG_LANE:FG_LANE + HEADS].set(fox_bf)
    wo_d = w_o[:, 3 * GROUP_W:].reshape(L, HEADS, HEAD_DIM, D_MODEL)[:, perm].reshape(L, GROUP_W, D_MODEL)
    wo = jnp.concatenate([w_o[:, :3 * GROUP_W], wo_d], axis=1).astype(BF16)
    wr = jnp.concatenate([w_re, w_rg, z(D_MODEL, LANES - N_EXPERTS - N_GROUPS)], axis=2)
    wr_hi = wr.astype(BF16)
    wr = jnp.stack([wr_hi, (wr - wr_hi.astype(F32)).astype(BF16)], axis=1)
    br = jnp.concatenate([b_re, b_rg, z(LANES - N_EXPERTS - N_GROUPS)], axis=1)[:, None, :]
    return dict(g1=norm1_g[:, None, :], wp=wp, gcq=gcq, gckv=mla_gckv[:, None, :], wuq=wuq, wuk=wuk, wuv=wuv,
                gq=gq, gk=gk, bound=bound, fb=fb, sink=sink[:, perm], wo=wo, g2=norm2_g[:, None, :], wr=wr, br=br)


def _tables(seq_len, tm):
    half = MLA_ROPE // 2
    inv = ROPE_BASE ** (-jnp.arange(0, MLA_ROPE, 2, dtype=F32) / MLA_ROPE)
    ang = jnp.arange(seq_len, dtype=F32)[:, None] * inv[None, :]
    cos, sin = jnp.cos(ang), jnp.sin(ang)
    one = jnp.ones((seq_len, MLA_NOPE), F32)
    zn = jnp.zeros((seq_len, MLA_NOPE), F32)
    zh = jnp.zeros((seq_len, half), F32)
    rc = jnp.concatenate([one, cos, cos] * 2, axis=1)
    rsa = jnp.concatenate([zn, -sin, zh] * 2, axis=1)
    rsb = jnp.concatenate([zn, zh, sin] * 2, axis=1)
    seg = np.arange(256) // HEAD_DIM
    e = jnp.asarray(seg[:, None] == seg[None, :], BF16)
    idx = np.arange(tm)
    tril = jnp.asarray(idx[None, :] <= idx[:, None], BF16)
    ltri = jnp.asarray(idx[None, :] < idx[:, None], BF16)
    return dict(rc=rc, rsa=rsa, rsb=rsb, e=e, tril=tril, ltri=ltri)


def _alibi():
    n = 2 * HEADS
    s = [2.0 ** (-8.0 * i / n) for i in range(1, n + 1)]
    return s[HEADS:], s[:HEADS]


def kernel(x, norm1_g, w_in, mla_gcq, mla_gckv, mla_wuq, mla_wukv, qk_gq, qk_gk, fox_bf, sink, w_o,
           norm2_g, w_rg, b_rg, w_re, b_re, w1, w3, w2):
    B, S, D = x.shape
    T = B * S
    depth = w_in.shape[0]
    tm = 512
    tabs = _tables(S, tm)
    slopes_dil, slopes_swa = _alibi()
    slopes_swa_perm = [slopes_swa[0], slopes_swa[2], slopes_swa[1], slopes_swa[3]]
    n_assign = T * TOP_K
    assert n_assign % ROW_BLOCK == 0

    all_w = _prep_weights(norm1_g, w_in, mla_gcq, mla_gckv, mla_wuq, mla_wukv, qk_gq, qk_gk, fox_bf, sink,
                          w_o, norm2_g, w_rg, b_rg, w_re, b_re)
    xt = x.reshape(T, D)
    for l in range(depth):
        lw = {name: arr[l] for name, arr in all_w.items()}
        qkv, qkvb, vt, cum, cum8 = _inproj(xt, S, lw, tabs, tm=tm)
        qkv3 = qkv.reshape(B, S, QKV_COLS)
        qkvb3 = qkvb.reshape(B, S, QKVB_COLS)
        oa = _dense_attn(qkv3, vt, 0, None, None, QA, KA, VA, lw["bound"][0]).reshape(T, GROUP_W)
        oc = _dense_attn(qkv3, vt, 2, cum.reshape(B, S, LANES), cum8, QC, KC, VC, lw["bound"][2]).reshape(T, GROUP_W)
        obs, lses = [], []
        for window, dil in DILATED_PAIRS:
            o, lse = _banded(qkvb3, 0, qkvb3, 1, 2, 256, dil, window // dil, slopes_dil, None, True, lw["bound"][1])
            obs.append(o)
            lses.append(lse)
        swa_bound = jnp.maximum(lw["bound"][3], jnp.max(jnp.abs(lw["sink"])) * LOG2E)
        (od,) = _banded(qkv3, QD, qkv3, 2 * KVD, 2 * KVD + 1, 128, 1, SWA_WINDOW - 1, slopes_swa_perm,
                        lw["sink"], False, swa_bound)
        h, xn, route, route_t, cnt = _wo_router(xt, oa, obs, lses, oc, od, lw, tabs["ltri"], tm=tm)

        counts = cnt[0, :N_EXPERTS].astype(jnp.int32)
        *items, start = _expert_items(counts, n_assign // ROW_BLOCK)
        expert = route_t[0:TOP_K].astype(jnp.int32)
        rank = route_t[TOP_K:2 * TOP_K].astype(jnp.int32)
        hit = expert[None] == jnp.arange(N_EXPERTS, dtype=jnp.int32)[:, None, None]
        dest = rank + jnp.sum(jnp.where(hit, start[:, None, None], 0), axis=0)

        xbuf = _scatter_rows(xn.reshape(T, ROW_CHUNKS, LANES), dest)
        ybuf = _expert_mlp(xbuf.reshape(n_assign * ROW_CHUNKS, LANES), items, l, w1, w3, w2)
        xt = _combine(h, route, ybuf.reshape(n_assign, ROW_CHUNKS, LANES), dest)
    return xt.reshape(B, S, D)
```

```python
import functools

import numpy as np
import jax
import jax.numpy as jnp
from jax import lax
from jax.experimental import pallas as pl
from jax.experimental.pallas import tpu as pltpu

F32 = jnp.float32
BF16 = jnp.bfloat16

D_MODEL = 1024
HEAD_DIM = 64
HEADS = 4
GROUP_W = HEADS * HEAD_DIM
NORM_EPS = 1e-6
MLA_Q_RANK, MLA_KV_RANK, MLA_NOPE, MLA_ROPE = 192, 128, 32, 32
ROPE_BASE = 10000.0
DILATED_PAIRS = ((128, 1), (512, 4), (2048, 16))
SWA_WINDOW = 128
N_GROUPS, EXPERTS_PER_GROUP, N_EXPERTS, TOP_K, D_EXPERT = 4, 8, 32, 2, 256
IN_SPLITS = (MLA_Q_RANK, MLA_KV_RANK, MLA_ROPE, 3 * GROUP_W, 3 * GROUP_W, HEADS, GROUP_W, 2 * 2 * HEAD_DIM)

LANES = 128
SUBLANES = 8
ROW_CHUNKS = D_MODEL // LANES
W_COLS = 2688
QKV_COLS = 2048
QA, KA, VA, QC, KC, VC, QD, KVD = range(8)
QKVB_COLS = 768
ONES_ROWS = 16
BAND = 128
BAND_TOKENS = 2048
FG_LANE = 64
ROW_BLOCK = 512
VMEM_LIMIT = 56 * 1024 * 1024

NEG_INF = float("-inf")
LOG2E = 1.4426950408889634
LN2 = 0.6931471805599453
EXP2_SAFE = 60.0


def _cparams(sem):
    return pltpu.CompilerParams(dimension_semantics=sem, vmem_limit_bytes=VMEM_LIMIT)


def _full(shape):
    zeros = (0,) * len(shape)
    return pl.BlockSpec(shape, lambda *_: zeros)


def _head_norm(y, g, e):
    ss = jnp.dot((y * y).astype(BF16), e, preferred_element_type=F32)
    return y * lax.rsqrt(ss * (1.0 / HEAD_DIM) + NORM_EPS) * g


def _rope(y, rc, rsa, rsb):
    outs = []
    for c in range(y.shape[1] // LANES):
        yc = y[:, c * LANES:(c + 1) * LANES]
        outs.append(yc * rc + pltpu.roll(yc, LANES - 16, 1) * rsa + pltpu.roll(yc, 16, 1) * rsb)
    return jnp.concatenate(outs, axis=1)


def _inproj_kernel(x_ref, g1_ref, w_ref, gcq_ref, gckv_ref, wuq_ref, wuk_ref, wuv_ref, gq_ref, gk_ref, e_ref,
                   rc_ref, rsa_ref, rsb_ref, fb_ref, tril_ref, qkv_ref, qkvb_ref, vt_ref, cum_ref, cumt_ref, acc_ref,
                   carry_ref,
                   *, tiles_per_seq):
    i = pl.program_id(0)
    tm = x_ref.shape[0]

    @pl.when(i % tiles_per_seq == 0)
    def _():
        carry_ref[...] = jnp.zeros_like(carry_ref)

    x = x_ref[...]
    ms = jnp.mean(x * x, axis=-1, keepdims=True)
    xn = (x * lax.rsqrt(ms + NORM_EPS) * g1_ref[...]).astype(BF16)
    acc_ref[...] = jnp.dot(xn, w_ref[...], preferred_element_type=F32)

    e = e_ref[...]
    rc, rsa, rsb = rc_ref[...], rsa_ref[...], rsb_ref[...]

    ckv = acc_ref[:, 0:128]
    ckvn = (ckv * lax.rsqrt(jnp.mean(ckv * ckv, axis=-1, keepdims=True) + NORM_EPS) * gckv_ref[...]).astype(BF16)
    u = acc_ref[:, 128:384]
    lane256 = lax.broadcasted_iota(jnp.int32, (tm, 256), 1)
    ssq = jnp.sum(jnp.where(lane256 < MLA_Q_RANK, u * u, 0.0), axis=-1, keepdims=True) * (1.0 / MLA_Q_RANK)
    un = (u * lax.rsqrt(ssq + NORM_EPS) * gcq_ref[...]).astype(BF16)
    qa = jnp.dot(un, wuq_ref[...], preferred_element_type=F32)
    ka = jnp.dot(ckvn, wuk_ref[...], preferred_element_type=F32) + acc_ref[:, 384:640]
    va = jnp.dot(ckvn, wuv_ref[...], preferred_element_type=F32)
    qa = _rope(_head_norm(qa, gq_ref[0:1, :], e), rc, rsa, rsb)
    ka = _rope(_head_norm(ka, gk_ref[0:1, :], e), rc, rsa, rsb)
    qkv_ref[:, QA * 256:(QA + 1) * 256] = qa.astype(BF16)
    qkv_ref[:, KA * 256:(KA + 1) * 256] = ka.astype(BF16)
    qkv_ref[:, VA * 256:(VA + 1) * 256] = va.astype(BF16)
    vt_ref[0:2] = jnp.transpose(va).astype(BF16).reshape(2, LANES, tm)
    vt_ref[2:4] = jnp.transpose(acc_ref[:, 1920:2176]).astype(BF16).reshape(2, LANES, tm)

    qkvb_ref[:, 0:256] = _head_norm(acc_ref[:, 640:896], gq_ref[1:2, :], e)
    qkvb_ref[:, 256:512] = _head_norm(acc_ref[:, 896:1152], gk_ref[1:2, :], e)
    qkvb_ref[:, 512:768] = acc_ref[:, 1152:1408]
    qkv_ref[:, QC * 256:(QC + 1) * 256] = _head_norm(acc_ref[:, 1408:1664], gq_ref[2:3, :], e).astype(BF16)
    qkv_ref[:, KC * 256:(KC + 1) * 256] = _head_norm(acc_ref[:, 1664:1920], gk_ref[2:3, :], e).astype(BF16)
    qkv_ref[:, VC * 256:(VC + 1) * 256] = acc_ref[:, 1920:2176].astype(BF16)

    qd = _head_norm(acc_ref[:, 2176:2432], gq_ref[3:4, :], e)
    kd = _head_norm(acc_ref[:, 2432:2560], gk_ref[3:4, 0:128], e[0:128, 0:128])
    qkv_ref[:, QD * 256:(QD + 1) * 256] = qd.astype(BF16)
    qkv_ref[:, KVD * 256:KVD * 256 + 128] = kd.astype(BF16)
    qkv_ref[:, KVD * 256 + 128:(KVD + 1) * 256] = acc_ref[:, 2560:2688].astype(BF16)

    z = u[:, 128:256] + fb_ref[...]
    ls = jnp.minimum(z, 0.0) - jnp.log(1.0 + jnp.exp(-jnp.abs(z)))
    lane128 = lax.broadcasted_iota(jnp.int32, (tm, LANES), 1)
    ls = jnp.where((lane128 >= FG_LANE) & (lane128 < FG_LANE + HEADS), ls, 0.0)
    hi = ls.astype(BF16)
    r1 = ls - hi.astype(F32)
    mid = r1.astype(BF16)
    lo = (r1 - mid.astype(F32)).astype(BF16)
    tril = tril_ref[...]
    local = (jnp.dot(tril, hi, preferred_element_type=F32) + jnp.dot(tril, mid, preferred_element_type=F32)
             + jnp.dot(tril, lo, preferred_element_type=F32))
    cum = local + carry_ref[0:1, :]
    cum_ref[...] = cum
    cum_t = jnp.transpose(cum)
    pad = jnp.zeros((SUBLANES - 2, tm), F32)
    for p in range(2):
        cumt_ref[p] = jnp.concatenate([cum_t[FG_LANE + 2 * p:FG_LANE + 2 * p + 2, :], pad], axis=0)
    carry_ref[...] = jnp.broadcast_to(cum[tm - 1:tm, :], carry_ref.shape)


def _inproj(xt, seq_len, lw, tabs, tm=512):
    T = xt.shape[0]
    tps = seq_len // tm
    kern = functools.partial(_inproj_kernel, tiles_per_seq=tps)
    tab_spec = pl.BlockSpec((tm, LANES), lambda i: (i % tps, 0))
    return pl.pallas_call(
        kern,
        grid=(T // tm,),
        in_specs=[pl.BlockSpec((tm, D_MODEL), lambda i: (i, 0)),
                  _full((1, D_MODEL)), _full((D_MODEL, W_COLS)), _full((1, 256)), _full((1, 128)),
                  _full((256, 256)), _full((128, 256)), _full((128, 256)), _full((4, 256)), _full((4, 256)),
                  _full((256, 256)), tab_spec, tab_spec, tab_spec, _full((1, LANES)), _full((tm, tm))],
        out_specs=[pl.BlockSpec((tm, QKV_COLS), lambda i: (i, 0)), pl.BlockSpec((tm, QKVB_COLS), lambda i: (i, 0)),
                   pl.BlockSpec((4, LANES, tm), lambda i: (0, 0, i)), pl.BlockSpec((tm, LANES), lambda i: (i, 0)),
                   pl.BlockSpec((2, SUBLANES, tm), lambda i: (0, 0, i))],
        out_shape=[jax.ShapeDtypeStruct((T, QKV_COLS), BF16), jax.ShapeDtypeStruct((T, QKVB_COLS), F32),
                   jax.ShapeDtypeStruct((4, LANES, T), BF16), jax.ShapeDtypeStruct((T, LANES), F32),
                   jax.ShapeDtypeStruct((2, SUBLANES, T), F32)],
        scratch_shapes=[pltpu.VMEM((tm, W_COLS), F32), pltpu.VMEM((8, LANES), F32)],
        compiler_params=_cparams(("arbitrary",)),
        name="inproj",
    )(xt, lw["g1"], lw["wp"], lw["gcq"], lw["gckv"], lw["wuq"], lw["wuk"], lw["wuv"], lw["gq"], lw["gk"],
      tabs["e"], tabs["rc"], tabs["rsa"], tabs["rsb"], lw["fb"], tabs["tril"])


def _split_heads(x):
    half0 = lax.broadcasted_iota(jnp.int32, x.shape, 1) < HEAD_DIM
    zero = jnp.zeros_like(x)
    return jnp.where(half0, x, zero), jnp.where(half0, zero, x)


def _causal_mask(s):
    row = lax.broadcasted_iota(jnp.int32, s.shape, 0)
    col = lax.broadcasted_iota(jnp.int32, s.shape, 1)
    return jnp.where(col <= row, s, NEG_INF)


def _cum_column(blk, h):
    lane = lax.broadcasted_iota(jnp.int32, blk.shape, 1)
    return jnp.sum(jnp.where(lane == FG_LANE + 2 * pl.program_id(1) + h, blk * LOG2E, 0.0), axis=1, keepdims=True)


def _row_cum(cq_ref, h):
    return _cum_column(cq_ref[0], h)


def _dense_bounded_kernel(*refs, tq, tk, fox):
    if fox:
        q_ref, k_ref, vt_ref, ccol_ref, crow_ref, o_ref, acc_scr = refs
    else:
        q_ref, k_ref, vt_ref, o_ref, acc_scr = refs
    qi = pl.program_id(2)
    qh = _split_heads(q_ref[0])
    ones = jnp.ones((ONES_ROWS, tk), BF16)
    acc_scr[...] = jnp.zeros_like(acc_scr)

    def step(start, c0, masked):
        k = k_ref[0, pl.ds(start, tk), :]
        for h in range(2):
            s = lax.dot_general(k, qh[h][c0:, :], (((1,), (1,)), ((), ())), preferred_element_type=F32)
            if fox:
                crow = crow_ref[0, h:h + 1, pl.ds(pl.multiple_of(qi * tq + c0, tk), tq - c0)] * LOG2E
                s = (s + crow) - _cum_column(ccol_ref[0, pl.ds(start, tk), :], h)
            if masked:
                key = lax.broadcasted_iota(jnp.int32, s.shape, 0)
                query = lax.broadcasted_iota(jnp.int32, s.shape, 1)
                s = jnp.where(key <= query, s, NEG_INF)
            lhs = jnp.concatenate([vt_ref[0, h * HEAD_DIM:(h + 1) * HEAD_DIM, pl.ds(start, tk)], ones], axis=0)
            acc_scr[h, :, c0:] += jnp.dot(lhs, jnp.exp2(s).astype(BF16), preferred_element_type=F32)

    def body(j, carry):
        step(pl.multiple_of(j * tk, tk), 0, False)
        return carry

    lax.fori_loop(0, qi * (tq // tk), body, 0)
    for d in range(tq // tk):
        step(pl.multiple_of(qi * tq + d * tk, tk), d * tk, True)
    o_t = jnp.concatenate([acc_scr[h, 0:HEAD_DIM, :] / acc_scr[h, HEAD_DIM:HEAD_DIM + 1, :] for h in range(2)], axis=0)
    o_ref[0] = jnp.transpose(o_t).astype(o_ref.dtype)


def _dense_online_kernel(*refs, tq, tk, fox):
    if fox:
        q_ref, k_ref, v_ref, cq_ref, ck_ref, o_ref, m_scr, l_scr, acc_scr = refs
    else:
        q_ref, k_ref, v_ref, o_ref, m_scr, l_scr, acc_scr = refs
    qi = pl.program_id(2)
    qh = _split_heads(q_ref[0])
    m_scr[...] = jnp.full(m_scr.shape, NEG_INF, F32)
    l_scr[...] = jnp.zeros_like(l_scr)
    acc_scr[...] = jnp.zeros_like(acc_scr)
    if fox:
        cq = [_row_cum(cq_ref, h) for h in range(2)]

    def step(start, r0, masked):
        k = k_ref[0, pl.ds(start, tk), :]
        v = v_ref[0, pl.ds(start, tk), :]
        for h in range(2):
            s = lax.dot_general(qh[h][r0:, :], k, (((1,), (1,)), ((), ())), preferred_element_type=F32)
            if fox:
                s = (s + cq[h][r0:, :]) - ck_ref[0, h:h + 1, pl.ds(start, tk)] * LOG2E
            if masked:
                s = _causal_mask(s)
            m_prev = m_scr[h, r0:, :]
            m_next = jnp.maximum(m_prev, jnp.max(s, axis=1, keepdims=True))
            p = jnp.exp2(s - jnp.concatenate([m_next] * (tk // LANES), axis=1))
            alpha = jnp.exp2(m_prev - m_next)
            l_scr[h, r0:, :] = alpha * l_scr[h, r0:, :] + jnp.sum(p, axis=1, keepdims=True)
            m_scr[h, r0:, :] = m_next
            pv = jnp.dot(p.astype(BF16), v, preferred_element_type=F32)
            acc_scr[h, r0:, :] = acc_scr[h, r0:, :] * alpha + pv

    def body(j, carry):
        step(pl.multiple_of(j * tk, tk), 0, False)
        return carry

    lax.fori_loop(0, qi * (tq // tk), body, 0)
    for d in range(tq // tk):
        step(pl.multiple_of(qi * tq + d * tk, tk), d * tk, True)
    half0 = lax.broadcasted_iota(jnp.int32, (tq, LANES), 1) < HEAD_DIM
    o = jnp.where(half0, acc_scr[0] / l_scr[0], acc_scr[1] / l_scr[1])
    o_ref[0] = o.astype(o_ref.dtype)


def _dense_attn(qkv3, vt, jvt, cum3, cum8, jq, jk, jv, logit_bound, tq_bounded=4096, tq_online=2048, tk=512):
    B, S, _ = qkv3.shape
    fox = cum8 is not None
    k_spec = pl.BlockSpec((1, S, LANES), lambda b, p, i: (b, 0, 2 * jk + p))
    v_spec = pl.BlockSpec((1, S, LANES), lambda b, p, i: (b, 0, 2 * jv + p))
    vt_spec = pl.BlockSpec((1, LANES, S), lambda b, p, i: (jvt + p, 0, b))
    cum_row_spec = pl.BlockSpec((1, SUBLANES, S), lambda b, p, i: (p, 0, b))
    name = "fox_attn" if fox else "mla_attn"

    def build(kern, tq, suffix, kv_specs, cum_specs, scratch):
        tq = min(tq, S)
        assert S % tq == 0 and tq % tk == 0
        q_spec = pl.BlockSpec((1, tq, LANES), lambda b, p, i: (b, i, 2 * jq + p))
        return pl.pallas_call(
            functools.partial(kern, tq=tq, tk=tk, fox=fox),
            grid=(B, 2, S // tq),
            in_specs=[q_spec] + kv_specs + (cum_specs(tq) if fox else []),
            out_specs=pl.BlockSpec((1, tq, LANES), lambda b, p, i: (b, i, p)),
            out_shape=jax.ShapeDtypeStruct((B, S, GROUP_W), BF16),
            scratch_shapes=scratch(tq),
            compiler_params=_cparams(("parallel", "parallel", "arbitrary")),
            name=name + suffix)

    bounded = build(_dense_bounded_kernel, tq_bounded, "_bounded", [k_spec, vt_spec],
                    lambda tq: [pl.BlockSpec((1, S, LANES), lambda b, p, i: (b, 0, 0)), cum_row_spec],
                    lambda tq: [pltpu.VMEM((2, HEAD_DIM + ONES_ROWS, tq), F32)])
    online = build(_dense_online_kernel, tq_online, "_online", [k_spec, v_spec],
                   lambda tq: [pl.BlockSpec((1, tq, LANES), lambda b, p, i: (b, i, 0)), cum_row_spec],
                   lambda tq: [pltpu.VMEM((2, tq, LANES), F32)] * 3)
    cums = (cum3, cum8) if fox else ()
    return lax.cond(logit_bound <= EXP2_SAFE,
                    lambda vt_, *c: bounded(qkv3, qkv3, vt_, *c),
                    lambda vt_, *c: online(qkv3, qkv3, qkv3, *c), vt, *cums)


def _fold_rows(r, n, dil):
    return slice(r, r + n) if dil == 1 else pl.ds(r, n, stride=dil)


def _banded_kernel(*refs, dil, maxdist, slopes, kv_chunks, has_sink, want_lse, bounded):
    refs = list(refs)
    sink_ref = refs.pop(0) if has_sink else None
    q_refs = [refs.pop(0) for _ in range(2)]
    kv_refs = [[refs.pop(0) for _ in range(4)] for _ in range(kv_chunks)]
    o_ref = refs.pop(0)
    lse_ref = refs.pop(0) if want_lse else None
    o_scr = refs.pop(0)
    lse_scr = refs.pop(0) if want_lse else None
    nq = q_refs[0].shape[1] // dil
    row = lax.broadcasted_iota(jnp.int32, (BAND, 2 * BAND), 0)
    col = lax.broadcasted_iota(jnp.int32, (BAND, 2 * BAND), 1)
    dist = row + BAND - col
    valid = (dist >= 0) & (dist <= maxdist)
    distf = dist.astype(F32) * (float(dil) * LOG2E)
    bias = [jnp.where(valid, -slopes[h] * distf, NEG_INF) for h in range(HEADS)]
    no_prev = jnp.where(col < BAND, jnp.where(pl.program_id(1) == 0, NEG_INF, 0.0), 0.0)
    bias_first = [b + no_prev for b in bias]
    half0_k = lax.broadcasted_iota(jnp.int32, (2 * BAND, LANES), 1) < HEAD_DIM
    ones0 = jnp.where(half0_k, 1.0, 0.0).astype(BF16)
    ones1 = jnp.where(half0_k, 0.0, 1.0).astype(BF16)
    half0 = lax.broadcasted_iota(jnp.int32, (BAND, LANES), 1) < HEAD_DIM
    for c in range(2):
        kp_ref, kc_ref, vp_ref, vc_ref = kv_refs[c if kv_chunks == 2 else 0]
        if has_sink:
            sink_lanes = jnp.where(half0[0:1, :], sink_ref[2 * c], sink_ref[2 * c + 1])
        for r in range(dil):
            prev_rows, cur_rows = _fold_rows(r, BAND, dil), _fold_rows(r, nq, dil)
            qh = _split_heads(q_refs[c][0, cur_rows, :].astype(BF16))
            kf = jnp.concatenate([kp_ref[0, prev_rows, :], kc_ref[0, cur_rows, :]], axis=0).astype(BF16)
            vh = _split_heads(jnp.concatenate([vp_ref[0, prev_rows, :], vc_ref[0, cur_rows, :]], axis=0).astype(BF16))
            outs, lses = [], []
            for j in range(nq // BAND):
                keys = slice(j * BAND, (j + 2) * BAND)
                v2 = jnp.concatenate([jnp.concatenate([vh[0][keys], ones0], axis=1),
                                      jnp.concatenate([vh[1][keys], ones1], axis=1)], axis=0)
                ps, ms = [], []
                for e in range(2):
                    s = lax.dot_general(qh[e][j * BAND:(j + 1) * BAND], kf[keys], (((1,), (1,)), ((), ())),
                                        preferred_element_type=F32)
                    s = s + (bias_first if j == 0 else bias)[2 * c + e]
                    if bounded:
                        ps.append(jnp.exp2(s).astype(BF16))
                    else:
                        m = jnp.max(s, axis=1, keepdims=True)
                        ps.append(jnp.exp2(s - m).astype(BF16))
                        ms.append(m)
                acc = jnp.dot(jnp.concatenate(ps, axis=1), v2, preferred_element_type=F32)
                unnorm, den = acc[:, 0:LANES], acc[:, LANES:2 * LANES]
                if bounded:
                    lse = jnp.log(den)
                    outs.append(unnorm / (den + jnp.exp(sink_lanes)) if has_sink else unnorm / den)
                else:
                    shift = jnp.where(half0, ms[0], ms[1]) * LN2
                    lse = shift + jnp.log(den)
                    if has_sink:
                        mx = jnp.maximum(lse, sink_lanes)
                        total = mx + jnp.log(jnp.exp(lse - mx) + jnp.exp(sink_lanes - mx))
                        outs.append(unnorm * jnp.exp(shift - total))
                    else:
                        outs.append(unnorm / den)
                if want_lse:
                    lses.append(lse)
            o_scr[cur_rows, :] = jnp.concatenate(outs, axis=0)
            if want_lse:
                lse_scr[cur_rows, :] = jnp.concatenate(lses, axis=0)
        o_ref[0, :, c * LANES:(c + 1) * LANES] = o_scr[...]
        if want_lse:
            lse_ref[0, :, c * LANES:(c + 1) * LANES] = lse_scr[...]


def _banded(q_arr, jq, kv_arr, jk, jv, kw, dil, maxdist, slopes, sink, want_lse, logit_bound):
    B, S, _ = q_arr.shape
    tb = min(BAND_TOKENS, S)
    pb = BAND * dil
    assert S % tb == 0 and tb % pb == 0 and maxdist <= BAND
    ratio = tb // pb
    has_sink = sink is not None
    kern = functools.partial(_banded_kernel, dil=dil, maxdist=maxdist, slopes=tuple(slopes),
                             kv_chunks=kw // LANES, has_sink=has_sink, want_lse=want_lse)
    prev = lambda j: pl.BlockSpec((1, pb, LANES), lambda b, i: (b, jnp.maximum(i * ratio - 1, 0), j))
    cur = lambda j: pl.BlockSpec((1, tb, LANES), lambda b, i: (b, i, j))
    in_specs = [cur(2 * jq), cur(2 * jq + 1)]
    args = [q_arr, q_arr]
    for c in range(kw // LANES):
        jkc, jvc = jk * (kw // LANES) + c, jv * (kw // LANES) + c
        in_specs += [prev(jkc), cur(jkc), prev(jvc), cur(jvc)]
        args += [kv_arr] * 4
    if has_sink:
        in_specs.insert(0, pl.BlockSpec(memory_space=pltpu.SMEM))
        args.insert(0, sink)
    n_out = 2 if want_lse else 1
    name = f"banded_d{dil}" if want_lse else "swa_attn"
    call = lambda bounded: pl.pallas_call(
        functools.partial(kern, bounded=bounded),
        grid=(B, S // tb),
        in_specs=in_specs,
        out_specs=[pl.BlockSpec((1, tb, 256), lambda b, i: (b, i, 0))] * n_out,
        out_shape=[jax.ShapeDtypeStruct((B, S, 256), F32)] * n_out,
        scratch_shapes=[pltpu.VMEM((tb, LANES), F32)] * n_out,
        compiler_params=_cparams(("parallel", "parallel")),
        name=name + ("_bounded" if bounded else "_rowmax"),
    )
    outs = lax.cond(logit_bound <= EXP2_SAFE, lambda *a: call(True)(*a), lambda *a: call(False)(*a), *args)
    return [o.reshape(B * S, 256) for o in outs]


def _wo_router_kernel(x_ref, oa_ref, ob0_ref, ob1_ref, ob2_ref, l0_ref, l1_ref, l2_ref, oc_ref, od_ref,
                      wo_ref, g2_ref, wr_ref, br_ref, ltri_ref, h_ref, xn_ref, route_ref, route_t_ref, cnt_ref,
                      carry_ref):
    i = pl.program_id(0)
    tm = x_ref.shape[0]

    @pl.when(i == 0)
    def _():
        carry_ref[...] = jnp.zeros_like(carry_ref)

    la, lb, lc = l0_ref[...], l1_ref[...], l2_ref[...]
    mx = jnp.maximum(jnp.maximum(la, lb), lc)
    ea, eb, ec = jnp.exp(la - mx), jnp.exp(lb - mx), jnp.exp(lc - mx)
    ob = (ea * ob0_ref[...] + eb * ob1_ref[...] + ec * ob2_ref[...]) / (ea + eb + ec)
    mix = jnp.concatenate([oa_ref[...], ob.astype(BF16), oc_ref[...], od_ref[...].astype(BF16)], axis=1)
    h = x_ref[...] + jnp.dot(mix, wo_ref[...], preferred_element_type=F32)
    h_ref[...] = h
    xn = h * lax.rsqrt(jnp.mean(h * h, axis=-1, keepdims=True) + NORM_EPS) * g2_ref[...]
    _store_row_tiles(xn_ref, xn)

    xh = xn.astype(BF16)
    xl = (xn - xh.astype(F32)).astype(BF16)
    z = (jnp.dot(xh, wr_ref[0], preferred_element_type=F32) + jnp.dot(xl, wr_ref[0], preferred_element_type=F32)
         + jnp.dot(xh, wr_ref[1], preferred_element_type=F32)) + br_ref[...]
    lane = lax.broadcasted_iota(jnp.int32, (tm, LANES), 1)
    lanef = lane.astype(F32)
    big = float(LANES)
    zg = jnp.where((lane >= N_EXPERTS) & (lane < N_EXPERTS + N_GROUPS), z, NEG_INF)
    mg = jnp.max(zg, axis=1, keepdims=True)
    p_g = 1.0 / jnp.sum(jnp.exp(zg - mg), axis=1, keepdims=True)
    gsel = jnp.min(jnp.where(zg == mg, lanef, big), axis=1, keepdims=True) - float(N_EXPERTS)
    lo = gsel * float(EXPERTS_PER_GROUP)
    ze = jnp.where((lanef >= lo) & (lanef < lo + float(EXPERTS_PER_GROUP)), z, NEG_INF)
    m1 = jnp.max(ze, axis=1, keepdims=True)
    i1 = jnp.min(jnp.where(ze == m1, lanef, big), axis=1, keepdims=True)
    ze2 = jnp.where(lanef == i1, NEG_INF, ze)
    m2 = jnp.max(ze2, axis=1, keepdims=True)
    i2 = jnp.min(jnp.where(ze2 == m2, lanef, big), axis=1, keepdims=True)
    e2 = jnp.exp(m2 - m1)
    gate1 = p_g / (1.0 + e2)
    gate2 = p_g * e2 / (1.0 + e2)

    oh1 = jnp.where(lanef == i1, 1.0, 0.0)
    oh2 = jnp.where(lanef == i2, 1.0, 0.0)
    oh = oh1 + oh2
    before = carry_ref[0:1, :] + jnp.dot(ltri_ref[...], oh.astype(BF16), preferred_element_type=F32)
    r1 = jnp.sum(before * oh1, axis=1, keepdims=True)
    r2 = jnp.sum(before * oh2, axis=1, keepdims=True)
    total = carry_ref[0:1, :] + jnp.sum(oh, axis=0, keepdims=True)
    carry_ref[...] = jnp.broadcast_to(total, carry_ref.shape)
    cnt_ref[...] = jnp.broadcast_to(total, cnt_ref.shape)

    route = jnp.where(lane == 0, i1, jnp.where(lane == 1, i2, jnp.where(lane == 2, r1, jnp.where(
        lane == 3, r2, jnp.where(lane == 4, gate1, jnp.where(lane == 5, gate2, 0.0))))))
    route_ref[...] = route
    route_t_ref[...] = jnp.transpose(route)[0:8, :]


def _wo_router(xt, oa, obs, lses, oc, od, lw, ltri, tm=512):
    T = xt.shape[0]
    row = lambda w: pl.BlockSpec((tm, w), lambda i: (i, 0))
    return pl.pallas_call(
        _wo_router_kernel,
        grid=(T // tm,),
        in_specs=[row(D_MODEL), row(256), row(256), row(256), row(256), row(256), row(256), row(256), row(256),
                  row(256), _full((D_MODEL, D_MODEL)), _full((1, D_MODEL)), _full((2, D_MODEL, LANES)),
                  _full((1, LANES)), _full((tm, tm))],
        out_specs=[row(D_MODEL), pl.BlockSpec((tm * ROW_CHUNKS, LANES), lambda i: (i, 0)), row(LANES),
                   pl.BlockSpec((8, tm), lambda i: (0, i)), _full((8, LANES))],
        out_shape=[jax.ShapeDtypeStruct((T, D_MODEL), F32), jax.ShapeDtypeStruct((T * ROW_CHUNKS, LANES), F32),
                   jax.ShapeDtypeStruct((T, LANES), F32), jax.ShapeDtypeStruct((8, T), F32),
                   jax.ShapeDtypeStruct((8, LANES), F32)],
        scratch_shapes=[pltpu.VMEM((8, LANES), F32)],
        compiler_params=_cparams(("arbitrary",)),
        name="wo_router",
    )(xt, oa, obs[0], obs[1], obs[2], lses[0], lses[1], lses[2], oc, od, lw["wo"], lw["g2"], lw["wr"], lw["br"], ltri)


def _store_row_tiles(ref, x, accumulate=False):
    rows = x.shape[0]
    for c in range(ROW_CHUNKS):
        idx = pl.ds(c, rows, stride=ROW_CHUNKS)
        chunk = x[:, c * LANES:(c + 1) * LANES]
        ref[idx, :] = ref[idx, :] + chunk if accumulate else chunk


def _load_row_tiles(ref, rows, lead=()):
    return jnp.concatenate([ref[lead + (pl.ds(c, rows, stride=ROW_CHUNKS), slice(None))] for c in range(ROW_CHUNKS)],
                           axis=1)


def _tile_copy(src, dst, sem):
    return pltpu.make_async_copy(src, dst, sem)


def _tile_major(dest_t, tile):
    k, T = dest_t.shape
    return dest_t.reshape(k, T // tile, tile).transpose(1, 0, 2).reshape(-1)


def _scatter_kernel(dest_ref, x_ref, buf_out, sem, *, ts):
    def issue(t, carry):
        for k in range(TOP_K):
            _tile_copy(x_ref.at[t], buf_out.at[dest_ref[k * ts + t]], sem).start(priority=k)
        return carry

    lax.fori_loop(0, ts, issue, 0, unroll=8)
    for k in range(TOP_K):
        _tile_copy(x_ref, buf_out.at[pl.ds(0, ts)], sem).wait()


def _scatter_rows(xn_tiles, dest_t, ts=1024):
    T = xn_tiles.shape[0]
    dest_flat = _tile_major(dest_t, ts)
    return pl.pallas_call(
        functools.partial(_scatter_kernel, ts=ts),
        grid=(T // ts,),
        in_specs=[pl.BlockSpec((TOP_K * ts,), lambda i: (i,), memory_space=pltpu.SMEM),
                  pl.BlockSpec((ts, ROW_CHUNKS, LANES), lambda i: (i, 0, 0))],
        out_specs=pl.BlockSpec(memory_space=pl.ANY),
        out_shape=jax.ShapeDtypeStruct((TOP_K * T, ROW_CHUNKS, LANES), xn_tiles.dtype),
        scratch_shapes=[pltpu.SemaphoreType.DMA(())],
        compiler_params=_cparams(("arbitrary",)),
        name="moe_scatter",
    )(dest_flat, xn_tiles)


def _expert_items(counts, n_blocks):
    n_items_max = n_blocks + N_EXPERTS
    end = jnp.cumsum(counts)
    start = end - counts
    first_b = start // ROW_BLOCK
    per_expert = jnp.where(counts > 0, (end - 1) // ROW_BLOCK - first_b + 1, 0)
    item_end = jnp.cumsum(per_expert)
    item_start = item_end - per_expert
    n_items = item_end[-1:]
    idx = jnp.minimum(jnp.arange(n_items_max, dtype=jnp.int32), n_items - 1)
    e = jnp.minimum(jnp.sum((item_end[None, :] <= idx[:, None]).astype(jnp.int32), axis=1), N_EXPERTS - 1)
    onehot = (e[:, None] == jnp.arange(N_EXPERTS, dtype=jnp.int32)[None, :]).astype(jnp.int32)
    pick = lambda table: jnp.sum(onehot * table[None, :], axis=1)
    b = pick(first_b) + idx - pick(item_start)
    lo = jnp.maximum(pick(start) - b * ROW_BLOCK, 0)
    hi = jnp.minimum(pick(end) - b * ROW_BLOCK, ROW_BLOCK)
    return e, b, lo, hi, n_items.astype(jnp.int32), start


def _expert_kernel(e_ref, b_ref, lo_ref, hi_ref, n_ref, x_ref, w1_ref, w3_ref, w2_ref, y_ref):
    del e_ref
    i = pl.program_id(0)

    @pl.when(i < n_ref[0])
    def _():
        x = _load_row_tiles(x_ref, ROW_BLOCK).astype(BF16)
        a = jnp.dot(x, w1_ref[0, 0].astype(BF16), preferred_element_type=F32)
        b = jnp.dot(x, w3_ref[0, 0].astype(BF16), preferred_element_type=F32)
        hid = (a / (1.0 + jnp.exp(-a))) * b
        y = jnp.dot(hid.astype(BF16), w2_ref[0, 0].astype(BF16), preferred_element_type=F32)
        row = lax.broadcasted_iota(jnp.int32, y.shape, 0)
        y = jnp.where((row >= lo_ref[i]) & (row < hi_ref[i]), y, 0.0)
        first_visit = jnp.logical_or(i == 0, b_ref[i] != b_ref[jnp.maximum(i - 1, 0)])

        @pl.when(first_visit)
        def _():
            _store_row_tiles(y_ref, y)

        @pl.when(jnp.logical_not(first_visit))
        def _():
            _store_row_tiles(y_ref, y, accumulate=True)


def _expert_mlp(xbuf, items, layer, w1, w3, w2):
    D = D_MODEL
    e, b, lo, hi, n_items = items
    x_map = lambda i, e_, b_, lo_, hi_, n_: (b_[i], 0)
    w_map = lambda i, e_, b_, lo_, hi_, n_: (layer, e_[i], 0, 0)
    grid_spec = pltpu.PrefetchScalarGridSpec(
        num_scalar_prefetch=5,
        grid=(e.shape[0],),
        in_specs=[pl.BlockSpec((ROW_BLOCK * ROW_CHUNKS, LANES), x_map),
                  pl.BlockSpec((1, 1, D, D_EXPERT), w_map),
                  pl.BlockSpec((1, 1, D, D_EXPERT), w_map),
                  pl.BlockSpec((1, 1, D_EXPERT, D), w_map)],
        out_specs=pl.BlockSpec((ROW_BLOCK * ROW_CHUNKS, LANES), x_map),
    )
    return pl.pallas_call(
        _expert_kernel,
        grid_spec=grid_spec,
        out_shape=jax.ShapeDtypeStruct(xbuf.shape, xbuf.dtype),
        compiler_params=_cparams(("arbitrary",)),
        name="moe_experts",
    )(e, b, lo, hi, n_items, xbuf, w1, w3, w2)


def _combine_kernel(dest_ref, dest_next_ref, h_ref, route_ref, y_hbm, o_ref, rows_scr, sems, *, tc):
    i = pl.program_id(0)
    n = pl.num_programs(0)
    slot = i % 2

    def gather(d_ref, s):
        def issue(t, carry):
            for k in range(TOP_K):
                dst = rows_scr.at[s, k, pl.ds(pl.multiple_of(t * ROW_CHUNKS, ROW_CHUNKS), ROW_CHUNKS)]
                _tile_copy(y_hbm.at[d_ref[k * tc + t]], dst, sems.at[s]).start(priority=k)
            return carry

        lax.fori_loop(0, tc, issue, 0, unroll=8)

    @pl.when(i == 0)
    def _():
        gather(dest_ref, 0)

    @pl.when(i + 1 < n)
    def _():
        gather(dest_next_ref, 1 - slot)

    for k in range(TOP_K):
        _tile_copy(rows_scr.at[slot, k], rows_scr.at[slot, k], sems.at[slot]).wait()
    route = route_ref[...]
    gates = [jnp.broadcast_to(route[:, 4 + k:5 + k], (tc, LANES)) for k in range(TOP_K)]
    rows = [_load_row_tiles(rows_scr, tc, lead=(slot, k)) for k in range(TOP_K)]
    for c in range(ROW_CHUNKS):
        cols = slice(c * LANES, (c + 1) * LANES)
        o_ref[:, cols] = h_ref[:, cols] + gates[0] * rows[0][:, cols] + gates[1] * rows[1][:, cols]


def _combine(h, route, ybuf, dest_t, tc=512):
    T, D = h.shape
    n = T // tc
    dest_flat = _tile_major(dest_t, tc)
    return pl.pallas_call(
        functools.partial(_combine_kernel, tc=tc),
        grid=(n,),
        in_specs=[pl.BlockSpec((TOP_K * tc,), lambda i: (i,), memory_space=pltpu.SMEM),
                  pl.BlockSpec((TOP_K * tc,), lambda i: (jnp.minimum(i + 1, n - 1),), memory_space=pltpu.SMEM),
                  pl.BlockSpec((tc, D), lambda i: (i, 0)), pl.BlockSpec((tc, LANES), lambda i: (i, 0)),
                  pl.BlockSpec(memory_space=pl.ANY)],
        out_specs=pl.BlockSpec((tc, D), lambda i: (i, 0)),
        out_shape=jax.ShapeDtypeStruct((T, D), F32),
        scratch_shapes=[pltpu.VMEM((2, TOP_K, tc * ROW_CHUNKS, LANES), ybuf.dtype), pltpu.SemaphoreType.DMA((2,))],
        compiler_params=_cparams(("arbitrary",)),
        name="moe_combine",
    )(dest_flat, dest_flat, h, route, ybuf)


def _prep_weights(norm1_g, w_in, mla_gcq, mla_gckv, mla_wuq, mla_wukv, qk_gq, qk_gk, fox_bf, sink, w_o,
                  norm2_g, w_rg, b_rg, w_re, b_re):
    L = w_in.shape[0]
    offs = np.concatenate([[0], np.cumsum(IN_SPLITS)])
    cq, ckv, kr, pb, pc, fg, pdq, pdkv = [w_in[:, :, offs[j]:offs[j + 1]] for j in range(8)]
    z = lambda *shape: jnp.zeros((L,) + shape, F32)
    perm = np.array([0, 2, 1, 3])
    ublock = jnp.concatenate([cq, fg, z(D_MODEL, 256 - MLA_Q_RANK - HEADS)], axis=2)
    kr_rep = jnp.concatenate([z(D_MODEL, MLA_NOPE), kr] * HEADS, axis=2)
    dq = pdq.reshape(L, D_MODEL, HEADS, HEAD_DIM)[:, :, perm].reshape(L, D_MODEL, GROUP_W)
    wp = jnp.concatenate([ckv, ublock, kr_rep, pb, pc, dq, pdkv], axis=2).astype(BF16)

    wuq = jnp.concatenate([mla_wuq, z(256 - MLA_Q_RANK, GROUP_W)], axis=1).astype(BF16)
    wukv = mla_wukv.reshape(L, MLA_KV_RANK, HEADS, MLA_NOPE + HEAD_DIM)
    wuk = jnp.concatenate([wukv[..., :MLA_NOPE], z(MLA_KV_RANK, HEADS, MLA_ROPE)], axis=3)
    wuk = wuk.reshape(L, MLA_KV_RANK, GROUP_W).astype(BF16)
    wuv = wukv[..., MLA_NOPE:].reshape(L, MLA_KV_RANK, GROUP_W).astype(BF16)
    gcq = jnp.concatenate([mla_gcq, z(256 - MLA_Q_RANK)], axis=1)[:, None, :]
    scale = HEAD_DIM ** -0.5
    gq = jnp.tile(qk_gq, (1, 1, HEADS)) * (scale * LOG2E)
    gk = jnp.tile(qk_gk, (1, 1, HEADS))
    bound = HEAD_DIM * jnp.max(jnp.abs(gq), axis=2) * jnp.max(jnp.abs(gk), axis=2)
    fb = z(1, LANES).at[:, 0, FG_LANE:FG_LANE + HEADS].set(fox_bf)
    wo_d = w_o[:, 3 * GROUP_W:].reshape(L, HEADS, HEAD_DIM, D_MODEL)[:, perm].reshape(L, GROUP_W, D_MODEL)
    wo = jnp.concatenate([w_o[:, :3 * GROUP_W], wo_d], axis=1).astype(BF16)
    wr = jnp.concatenate([w_re, w_rg, z(D_MODEL, LANES - N_EXPERTS - N_GROUPS)], axis=2)
    wr_hi = wr.astype(BF16)
    wr = jnp.stack([wr_hi, (wr - wr_hi.astype(F32)).astype(BF16)], axis=1)
    br = jnp.concatenate([b_re, b_rg, z(LANES - N_EXPERTS - N_GROUPS)], axis=1)[:, None, :]
    return dict(g1=norm1_g[:, None, :], wp=wp, gcq=gcq, gckv=mla_gckv[:, None, :], wuq=wuq, wuk=wuk, wuv=wuv,
                gq=gq, gk=gk, bound=bound, fb=fb, sink=sink[:, perm], wo=wo, g2=norm2_g[:, None, :], wr=wr, br=br)


def _tables(seq_len, tm):
    half = MLA_ROPE // 2
    inv = ROPE_BASE ** (-jnp.arange(0, MLA_ROPE, 2, dtype=F32) / MLA_ROPE)
    ang = jnp.arange(seq_len, dtype=F32)[:, None] * inv[None, :]
    cos, sin = jnp.cos(ang), jnp.sin(ang)
    one = jnp.ones((seq_len, MLA_NOPE), F32)
    zn = jnp.zeros((seq_len, MLA_NOPE), F32)
    zh = jnp.zeros((seq_len, half), F32)
    rc = jnp.concatenate([one, cos, cos] * 2, axis=1)
    rsa = jnp.concatenate([zn, -sin, zh] * 2, axis=1)
    rsb = jnp.concatenate([zn, zh, sin] * 2, axis=1)
    seg = np.arange(256) // HEAD_DIM
    e = jnp.asarray(seg[:, None] == seg[None, :], BF16)
    idx = np.arange(tm)
    tril = jnp.asarray(idx[None, :] <= idx[:, None], BF16)
    ltri = jnp.asarray(idx[None, :] < idx[:, None], BF16)
    return dict(rc=rc, rsa=rsa, rsb=rsb, e=e, tril=tril, ltri=ltri)


def _alibi():
    n = 2 * HEADS
    s = [2.0 ** (-8.0 * i / n) for i in range(1, n + 1)]
    return s[HEADS:], s[:HEADS]


def kernel(x, norm1_g, w_in, mla_gcq, mla_gckv, mla_wuq, mla_wukv, qk_gq, qk_gk, fox_bf, sink, w_o,
           norm2_g, w_rg, b_rg, w_re, b_re, w1, w3, w2):
    B, S, D = x.shape
    T = B * S
    depth = w_in.shape[0]
    tm = 512
    tabs = _tables(S, tm)
    slopes_dil, slopes_swa = _alibi()
    slopes_swa_perm = [slopes_swa[0], slopes_swa[2], slopes_swa[1], slopes_swa[3]]
    n_assign = T * TOP_K
    assert n_assign % ROW_BLOCK == 0

    all_w = _prep_weights(norm1_g, w_in, mla_gcq, mla_gckv, mla_wuq, mla_wukv, qk_gq, qk_gk, fox_bf, sink,
                          w_o, norm2_g, w_rg, b_rg, w_re, b_re)
    xt = x.reshape(T, D)
    for l in range(depth):
        lw = {name: arr[l] for name, arr in all_w.items()}
        qkv, qkvb, vt, cum, cum8 = _inproj(xt, S, lw, tabs, tm=tm)
        qkv3 = qkv.reshape(B, S, QKV_COLS)
        qkvb3 = qkvb.reshape(B, S, QKVB_COLS)
        oa = _dense_attn(qkv3, vt, 0, None, None, QA, KA, VA, lw["bound"][0]).reshape(T, GROUP_W)
        oc = _dense_attn(qkv3, vt, 2, cum.reshape(B, S, LANES), cum8, QC, KC, VC, lw["bound"][2]).reshape(T, GROUP_W)
        obs, lses = [], []
        for window, dil in DILATED_PAIRS:
            o, lse = _banded(qkvb3, 0, qkvb3, 1, 2, 256, dil, window // dil, slopes_dil, None, True, lw["bound"][1])
            obs.append(o)
            lses.append(lse)
        swa_bound = jnp.maximum(lw["bound"][3], jnp.max(jnp.abs(lw["sink"])) * LOG2E)
        (od,) = _banded(qkv3, QD, qkv3, 2 * KVD, 2 * KVD + 1, 128, 1, SWA_WINDOW - 1, slopes_swa_perm,
                        lw["sink"], False, swa_bound)
        h, xn, route, route_t, cnt = _wo_router(xt, oa, obs, lses, oc, od, lw, tabs["ltri"], tm=tm)

        counts = cnt[0, :N_EXPERTS].astype(jnp.int32)
        *items, start = _expert_items(counts, n_assign // ROW_BLOCK)
        expert = route_t[0:TOP_K].astype(jnp.int32)
        rank = route_t[TOP_K:2 * TOP_K].astype(jnp.int32)
        hit = expert[None] == jnp.arange(N_EXPERTS, dtype=jnp.int32)[:, None, None]
        dest = rank + jnp.sum(jnp.where(hit, start[:, None, None], 0), axis=0)

        xbuf = _scatter_rows(xn.reshape(T, ROW_CHUNKS, LANES), dest)
        ybuf = _expert_mlp(xbuf.reshape(n_assign * ROW_CHUNKS, LANES), items, l, w1, w3, w2)
        xt = _combine(h, route, ybuf.reshape(n_assign, ROW_CHUNKS, LANES), dest)
    return xt.reshape(B, S, D)
```

```python
import functools

import numpy as np
import jax
import jax.numpy as jnp
from jax import lax
from jax.experimental import pallas as pl
from jax.experimental.pallas import tpu as pltpu

F32 = jnp.float32
BF16 = jnp.bfloat16

D_MODEL = 1024
HEAD_DIM = 64
HEADS = 4
GROUP_W = HEADS * HEAD_DIM
NORM_EPS = 1e-6
MLA_Q_RANK, MLA_KV_RANK, MLA_NOPE, MLA_ROPE = 192, 128, 32, 32
ROPE_BASE = 10000.0
DILATED_PAIRS = ((128, 1), (512, 4), (2048, 16))
SWA_WINDOW = 128
N_GROUPS, EXPERTS_PER_GROUP, N_EXPERTS, TOP_K, D_EXPERT = 4, 8, 32, 2, 256
IN_SPLITS = (MLA_Q_RANK, MLA_KV_RANK, MLA_ROPE, 3 * GROUP_W, 3 * GROUP_W, HEADS, GROUP_W, 2 * 2 * HEAD_DIM)

LANES = 128
SUBLANES = 8
ROW_CHUNKS = D_MODEL // LANES
W_COLS = 2688
QKV_COLS = 2048
QA, KA, VA, QC, KC, VC, QD, KVD = range(8)
QKVB_COLS = 768
ONES_ROWS = 16
BAND = 128
BAND_TOKENS = 2048
FG_LANE = 64
ROW_BLOCK = 256
VMEM_LIMIT = 56 * 1024 * 1024

NEG_INF = float("-inf")
LOG2E = 1.4426950408889634
LN2 = 0.6931471805599453
EXP2_SAFE = 60.0


def _cparams(sem):
    return pltpu.CompilerParams(dimension_semantics=sem, vmem_limit_bytes=VMEM_LIMIT)


def _full(shape):
    zeros = (0,) * len(shape)
    return pl.BlockSpec(shape, lambda *_: zeros)


def _head_norm(y, g, e):
    ss = jnp.dot((y * y).astype(BF16), e, preferred_element_type=F32)
    return y * lax.rsqrt(ss * (1.0 / HEAD_DIM) + NORM_EPS) * g


def _rope(y, rc, rsa, rsb):
    outs = []
    for c in range(y.shape[1] // LANES):
        yc = y[:, c * LANES:(c + 1) * LANES]
        outs.append(yc * rc + pltpu.roll(yc, LANES - 16, 1) * rsa + pltpu.roll(yc, 16, 1) * rsb)
    return jnp.concatenate(outs, axis=1)


def _inproj_kernel(x_ref, g1_ref, w_ref, gcq_ref, gckv_ref, wuq_ref, wuk_ref, wuv_ref, gq_ref, gk_ref, e_ref,
                   rc_ref, rsa_ref, rsb_ref, fb_ref, tril_ref, qkv_ref, qkvb_ref, vt_ref, cum_ref, cumt_ref, acc_ref,
                   carry_ref,
                   *, tiles_per_seq):
    i = pl.program_id(0)
    tm = x_ref.shape[0]

    @pl.when(i % tiles_per_seq == 0)
    def _():
        carry_ref[...] = jnp.zeros_like(carry_ref)

    x = x_ref[...]
    ms = jnp.mean(x * x, axis=-1, keepdims=True)
    xn = (x * lax.rsqrt(ms + NORM_EPS) * g1_ref[...]).astype(BF16)
    acc_ref[...] = jnp.dot(xn, w_ref[...], preferred_element_type=F32)

    e = e_ref[...]
    rc, rsa, rsb = rc_ref[...], rsa_ref[...], rsb_ref[...]

    ckv = acc_ref[:, 0:128]
    ckvn = (ckv * lax.rsqrt(jnp.mean(ckv * ckv, axis=-1, keepdims=True) + NORM_EPS) * gckv_ref[...]).astype(BF16)
    u = acc_ref[:, 128:384]
    lane256 = lax.broadcasted_iota(jnp.int32, (tm, 256), 1)
    ssq = jnp.sum(jnp.where(lane256 < MLA_Q_RANK, u * u, 0.0), axis=-1, keepdims=True) * (1.0 / MLA_Q_RANK)
    un = (u * lax.rsqrt(ssq + NORM_EPS) * gcq_ref[...]).astype(BF16)
    qa = jnp.dot(un, wuq_ref[...], preferred_element_type=F32)
    ka = jnp.dot(ckvn, wuk_ref[...], preferred_element_type=F32) + acc_ref[:, 384:640]
    va = jnp.dot(ckvn, wuv_ref[...], preferred_element_type=F32)
    qa = _rope(_head_norm(qa, gq_ref[0:1, :], e), rc, rsa, rsb)
    ka = _rope(_head_norm(ka, gk_ref[0:1, :], e), rc, rsa, rsb)
    qkv_ref[:, QA * 256:(QA + 1) * 256] = qa.astype(BF16)
    qkv_ref[:, KA * 256:(KA + 1) * 256] = ka.astype(BF16)
    qkv_ref[:, VA * 256:(VA + 1) * 256] = va.astype(BF16)
    vt_ref[0:2] = jnp.transpose(va).astype(BF16).reshape(2, LANES, tm)
    vt_ref[2:4] = jnp.transpose(acc_ref[:, 1920:2176]).astype(BF16).reshape(2, LANES, tm)

    qkvb_ref[:, 0:256] = _head_norm(acc_ref[:, 640:896], gq_ref[1:2, :], e)
    qkvb_ref[:, 256:512] = _head_norm(acc_ref[:, 896:1152], gk_ref[1:2, :], e)
    qkvb_ref[:, 512:768] = acc_ref[:, 1152:1408]
    qkv_ref[:, QC * 256:(QC + 1) * 256] = _head_norm(acc_ref[:, 1408:1664], gq_ref[2:3, :], e).astype(BF16)
    qkv_ref[:, KC * 256:(KC + 1) * 256] = _head_norm(acc_ref[:, 1664:1920], gk_ref[2:3, :], e).astype(BF16)
    qkv_ref[:, VC * 256:(VC + 1) * 256] = acc_ref[:, 1920:2176].astype(BF16)

    qd = _head_norm(acc_ref[:, 2176:2432], gq_ref[3:4, :], e)
    kd = _head_norm(acc_ref[:, 2432:2560], gk_ref[3:4, 0:128], e[0:128, 0:128])
    qkv_ref[:, QD * 256:(QD + 1) * 256] = qd.astype(BF16)
    qkv_ref[:, KVD * 256:KVD * 256 + 128] = kd.astype(BF16)
    qkv_ref[:, KVD * 256 + 128:(KVD + 1) * 256] = acc_ref[:, 2560:2688].astype(BF16)

    z = u[:, 128:256] + fb_ref[...]
    ls = jnp.minimum(z, 0.0) - jnp.log(1.0 + jnp.exp(-jnp.abs(z)))
    lane128 = lax.broadcasted_iota(jnp.int32, (tm, LANES), 1)
    ls = jnp.where((lane128 >= FG_LANE) & (lane128 < FG_LANE + HEADS), ls, 0.0)
    hi = ls.astype(BF16)
    r1 = ls - hi.astype(F32)
    mid = r1.astype(BF16)
    lo = (r1 - mid.astype(F32)).astype(BF16)
    tril = tril_ref[...]
    local = (jnp.dot(tril, hi, preferred_element_type=F32) + jnp.dot(tril, mid, preferred_element_type=F32)
             + jnp.dot(tril, lo, preferred_element_type=F32))
    cum = local + carry_ref[0:1, :]
    cum_ref[...] = cum
    cum_t = jnp.transpose(cum)
    pad = jnp.zeros((SUBLANES - 2, tm), F32)
    for p in range(2):
        cumt_ref[p] = jnp.concatenate([cum_t[FG_LANE + 2 * p:FG_LANE + 2 * p + 2, :], pad], axis=0)
    carry_ref[...] = jnp.broadcast_to(cum[tm - 1:tm, :], carry_ref.shape)


def _inproj(xt, seq_len, lw, tabs, tm=512):
    T = xt.shape[0]
    tps = seq_len // tm
    kern = functools.partial(_inproj_kernel, tiles_per_seq=tps)
    tab_spec = pl.BlockSpec((tm, LANES), lambda i: (i % tps, 0))
    return pl.pallas_call(
        kern,
        grid=(T // tm,),
        in_specs=[pl.BlockSpec((tm, D_MODEL), lambda i: (i, 0)),
                  _full((1, D_MODEL)), _full((D_MODEL, W_COLS)), _full((1, 256)), _full((1, 128)),
                  _full((256, 256)), _full((128, 256)), _full((128, 256)), _full((4, 256)), _full((4, 256)),
                  _full((256, 256)), tab_spec, tab_spec, tab_spec, _full((1, LANES)), _full((tm, tm))],
        out_specs=[pl.BlockSpec((tm, QKV_COLS), lambda i: (i, 0)), pl.BlockSpec((tm, QKVB_COLS), lambda i: (i, 0)),
                   pl.BlockSpec((4, LANES, tm), lambda i: (0, 0, i)), pl.BlockSpec((tm, LANES), lambda i: (i, 0)),
                   pl.BlockSpec((2, SUBLANES, tm), lambda i: (0, 0, i))],
        out_shape=[jax.ShapeDtypeStruct((T, QKV_COLS), BF16), jax.ShapeDtypeStruct((T, QKVB_COLS), F32),
                   jax.ShapeDtypeStruct((4, LANES, T), BF16), jax.ShapeDtypeStruct((T, LANES), F32),
                   jax.ShapeDtypeStruct((2, SUBLANES, T), F32)],
        scratch_shapes=[pltpu.VMEM((tm, W_COLS), F32), pltpu.VMEM((8, LANES), F32)],
        compiler_params=_cparams(("arbitrary",)),
        name="inproj",
    )(xt, lw["g1"], lw["wp"], lw["gcq"], lw["gckv"], lw["wuq"], lw["wuk"], lw["wuv"], lw["gq"], lw["gk"],
      tabs["e"], tabs["rc"], tabs["rsa"], tabs["rsb"], lw["fb"], tabs["tril"])


def _split_heads(x):
    half0 = lax.broadcasted_iota(jnp.int32, x.shape, 1) < HEAD_DIM
    zero = jnp.zeros_like(x)
    return jnp.where(half0, x, zero), jnp.where(half0, zero, x)


def _causal_mask(s):
    row = lax.broadcasted_iota(jnp.int32, s.shape, 0)
    col = lax.broadcasted_iota(jnp.int32, s.shape, 1)
    return jnp.where(col <= row, s, NEG_INF)


def _cum_column(blk, h):
    lane = lax.broadcasted_iota(jnp.int32, blk.shape, 1)
    return jnp.sum(jnp.where(lane == FG_LANE + 2 * pl.program_id(1) + h, blk * LOG2E, 0.0), axis=1, keepdims=True)


def _row_cum(cq_ref, h):
    return _cum_column(cq_ref[0], h)


def _dense_bounded_kernel(*refs, tq, tk, fox):
    if fox:
        q_ref, k_ref, vt_ref, ccol_ref, crow_ref, o_ref, acc_scr = refs
    else:
        q_ref, k_ref, vt_ref, o_ref, acc_scr = refs
    qi = pl.program_id(2)
    qh = _split_heads(q_ref[0])
    ones = jnp.ones((ONES_ROWS, tk), BF16)
    acc_scr[...] = jnp.zeros_like(acc_scr)

    def step(start, c0, masked):
        k = k_ref[0, pl.ds(start, tk), :]
        for h in range(2):
            s = lax.dot_general(k, qh[h][c0:, :], (((1,), (1,)), ((), ())), preferred_element_type=F32)
            if fox:
                crow = crow_ref[0, h:h + 1, pl.ds(pl.multiple_of(qi * tq + c0, tk), tq - c0)] * LOG2E
                s = (s + crow) - _cum_column(ccol_ref[0, pl.ds(start, tk), :], h)
            if masked:
                key = lax.broadcasted_iota(jnp.int32, s.shape, 0)
                query = lax.broadcasted_iota(jnp.int32, s.shape, 1)
                s = jnp.where(key <= query, s, NEG_INF)
            lhs = jnp.concatenate([vt_ref[0, h * HEAD_DIM:(h + 1) * HEAD_DIM, pl.ds(start, tk)], ones], axis=0)
            acc_scr[h, :, c0:] += jnp.dot(lhs, jnp.exp2(s).astype(BF16), preferred_element_type=F32)

    def body(j, carry):
        step(pl.multiple_of(j * tk, tk), 0, False)
        return carry

    lax.fori_loop(0, qi * (tq // tk), body, 0)
    for d in range(tq // tk):
        step(pl.multiple_of(qi * tq + d * tk, tk), d * tk, True)
    o_t = jnp.concatenate([acc_scr[h, 0:HEAD_DIM, :] / acc_scr[h, HEAD_DIM:HEAD_DIM + 1, :] for h in range(2)], axis=0)
    o_ref[0] = jnp.transpose(o_t).astype(o_ref.dtype)


def _dense_online_kernel(*refs, tq, tk, fox):
    if fox:
        q_ref, k_ref, v_ref, cq_ref, ck_ref, o_ref, m_scr, l_scr, acc_scr = refs
    else:
        q_ref, k_ref, v_ref, o_ref, m_scr, l_scr, acc_scr = refs
    qi = pl.program_id(2)
    qh = _split_heads(q_ref[0])
    m_scr[...] = jnp.full(m_scr.shape, NEG_INF, F32)
    l_scr[...] = jnp.zeros_like(l_scr)
    acc_scr[...] = jnp.zeros_like(acc_scr)
    if fox:
        cq = [_row_cum(cq_ref, h) for h in range(2)]

    def step(start, r0, masked):
        k = k_ref[0, pl.ds(start, tk), :]
        v = v_ref[0, pl.ds(start, tk), :]
        for h in range(2):
            s = lax.dot_general(qh[h][r0:, :], k, (((1,), (1,)), ((), ())), preferred_element_type=F32)
            if fox:
                s = (s + cq[h][r0:, :]) - ck_ref[0, h:h + 1, pl.ds(start, tk)] * LOG2E
            if masked:
                s = _causal_mask(s)
            m_prev = m_scr[h, r0:, :]
            m_next = jnp.maximum(m_prev, jnp.max(s, axis=1, keepdims=True))
            p = jnp.exp2(s - jnp.concatenate([m_next] * (tk // LANES), axis=1))
            alpha = jnp.exp2(m_prev - m_next)
            l_scr[h, r0:, :] = alpha * l_scr[h, r0:, :] + jnp.sum(p, axis=1, keepdims=True)
            m_scr[h, r0:, :] = m_next
            pv = jnp.dot(p.astype(BF16), v, preferred_element_type=F32)
            acc_scr[h, r0:, :] = acc_scr[h, r0:, :] * alpha + pv

    def body(j, carry):
        step(pl.multiple_of(j * tk, tk), 0, False)
        return carry

    lax.fori_loop(0, qi * (tq // tk), body, 0)
    for d in range(tq // tk):
        step(pl.multiple_of(qi * tq + d * tk, tk), d * tk, True)
    half0 = lax.broadcasted_iota(jnp.int32, (tq, LANES), 1) < HEAD_DIM
    o = jnp.where(half0, acc_scr[0] / l_scr[0], acc_scr[1] / l_scr[1])
    o_ref[0] = o.astype(o_ref.dtype)


def _dense_attn(qkv3, vt, jvt, cum3, cum8, jq, jk, jv, logit_bound, tq_bounded=4096, tq_online=2048, tk=512):
    B, S, _ = qkv3.shape
    fox = cum8 is not None
    k_spec = pl.BlockSpec((1, S, LANES), lambda b, p, i: (b, 0, 2 * jk + p))
    v_spec = pl.BlockSpec((1, S, LANES), lambda b, p, i: (b, 0, 2 * jv + p))
    vt_spec = pl.BlockSpec((1, LANES, S), lambda b, p, i: (jvt + p, 0, b))
    cum_row_spec = pl.BlockSpec((1, SUBLANES, S), lambda b, p, i: (p, 0, b))
    name = "fox_attn" if fox else "mla_attn"

    def build(kern, tq, suffix, kv_specs, cum_specs, scratch):
        tq = min(tq, S)
        assert S % tq == 0 and tq % tk == 0
        q_spec = pl.BlockSpec((1, tq, LANES), lambda b, p, i: (b, i, 2 * jq + p))
        return pl.pallas_call(
            functools.partial(kern, tq=tq, tk=tk, fox=fox),
            grid=(B, 2, S // tq),
            in_specs=[q_spec] + kv_specs + (cum_specs(tq) if fox else []),
            out_specs=pl.BlockSpec((1, tq, LANES), lambda b, p, i: (b, i, p)),
            out_shape=jax.ShapeDtypeStruct((B, S, GROUP_W), BF16),
            scratch_shapes=scratch(tq),
            compiler_params=_cparams(("parallel", "parallel", "arbitrary")),
            name=name + suffix)

    bounded = build(_dense_bounded_kernel, tq_bounded, "_bounded", [k_spec, vt_spec],
                    lambda tq: [pl.BlockSpec((1, S, LANES), lambda b, p, i: (b, 0, 0)), cum_row_spec],
                    lambda tq: [pltpu.VMEM((2, HEAD_DIM + ONES_ROWS, tq), F32)])
    online = build(_dense_online_kernel, tq_online, "_online", [k_spec, v_spec],
                   lambda tq: [pl.BlockSpec((1, tq, LANES), lambda b, p, i: (b, i, 0)), cum_row_spec],
                   lambda tq: [pltpu.VMEM((2, tq, LANES), F32)] * 3)
    cums = (cum3, cum8) if fox else ()
    return lax.cond(logit_bound <= EXP2_SAFE,
                    lambda vt_, *c: bounded(qkv3, qkv3, vt_, *c),
                    lambda vt_, *c: online(qkv3, qkv3, qkv3, *c), vt, *cums)


def _fold_rows(r, n, dil):
    return slice(r, r + n) if dil == 1 else pl.ds(r, n, stride=dil)


def _banded_kernel(*refs, dil, maxdist, slopes, kv_chunks, has_sink, want_lse, bounded):
    refs = list(refs)
    sink_ref = refs.pop(0) if has_sink else None
    q_refs = [refs.pop(0) for _ in range(2)]
    kv_refs = [[refs.pop(0) for _ in range(4)] for _ in range(kv_chunks)]
    o_ref = refs.pop(0)
    lse_ref = refs.pop(0) if want_lse else None
    o_scr = refs.pop(0)
    lse_scr = refs.pop(0) if want_lse else None
    nq = q_refs[0].shape[1] // dil
    row = lax.broadcasted_iota(jnp.int32, (BAND, 2 * BAND), 0)
    col = lax.broadcasted_iota(jnp.int32, (BAND, 2 * BAND), 1)
    dist = row + BAND - col
    valid = (dist >= 0) & (dist <= maxdist)
    distf = dist.astype(F32) * (float(dil) * LOG2E)
    bias = [jnp.where(valid, -slopes[h] * distf, NEG_INF) for h in range(HEADS)]
    no_prev = jnp.where(col < BAND, jnp.where(pl.program_id(1) == 0, NEG_INF, 0.0), 0.0)
    bias_first = [b + no_prev for b in bias]
    half0_k = lax.broadcasted_iota(jnp.int32, (2 * BAND, LANES), 1) < HEAD_DIM
    ones0 = jnp.where(half0_k, 1.0, 0.0).astype(BF16)
    ones1 = jnp.where(half0_k, 0.0, 1.0).astype(BF16)
    half0 = lax.broadcasted_iota(jnp.int32, (BAND, LANES), 1) < HEAD_DIM
    for c in range(2):
        kp_ref, kc_ref, vp_ref, vc_ref = kv_refs[c if kv_chunks == 2 else 0]
        if has_sink:
            sink_lanes = jnp.where(half0[0:1, :], sink_ref[2 * c], sink_ref[2 * c + 1])
        for r in range(dil):
            prev_rows, cur_rows = _fold_rows(r, BAND, dil), _fold_rows(r, nq, dil)
            qh = _split_heads(q_refs[c][0, cur_rows, :].astype(BF16))
            kf = jnp.concatenate([kp_ref[0, prev_rows, :], kc_ref[0, cur_rows, :]], axis=0).astype(BF16)
            vh = _split_heads(jnp.concatenate([vp_ref[0, prev_rows, :], vc_ref[0, cur_rows, :]], axis=0).astype(BF16))
            outs, lses = [], []
            for j in range(nq // BAND):
                keys = slice(j * BAND, (j + 2) * BAND)
                v2 = jnp.concatenate([jnp.concatenate([vh[0][keys], ones0], axis=1),
                                      jnp.concatenate([vh[1][keys], ones1], axis=1)], axis=0)
                ps, ms = [], []
                for e in range(2):
                    s = lax.dot_general(qh[e][j * BAND:(j + 1) * BAND], kf[keys], (((1,), (1,)), ((), ())),
                                        preferred_element_type=F32)
                    s = s + (bias_first if j == 0 else bias)[2 * c + e]
                    if bounded:
                        ps.append(jnp.exp2(s).astype(BF16))
                    else:
                        m = jnp.max(s, axis=1, keepdims=True)
                        ps.append(jnp.exp2(s - m).astype(BF16))
                        ms.append(m)
                acc = jnp.dot(jnp.concatenate(ps, axis=1), v2, preferred_element_type=F32)
                unnorm, den = acc[:, 0:LANES], acc[:, LANES:2 * LANES]
                if bounded:
                    lse = jnp.log(den)
                    outs.append(unnorm / (den + jnp.exp(sink_lanes)) if has_sink else unnorm / den)
                else:
                    shift = jnp.where(half0, ms[0], ms[1]) * LN2
                    lse = shift + jnp.log(den)
                    if has_sink:
                        mx = jnp.maximum(lse, sink_lanes)
                        total = mx + jnp.log(jnp.exp(lse - mx) + jnp.exp(sink_lanes - mx))
                        outs.append(unnorm * jnp.exp(shift - total))
                    else:
                        outs.append(unnorm / den)
                if want_lse:
                    lses.append(lse)
            o_scr[cur_rows, :] = jnp.concatenate(outs, axis=0)
            if want_lse:
                lse_scr[cur_rows, :] = jnp.concatenate(lses, axis=0)
        o_ref[0, :, c * LANES:(c + 1) * LANES] = o_scr[...]
        if want_lse:
            lse_ref[0, :, c * LANES:(c + 1) * LANES] = lse_scr[...]


def _banded(q_arr, jq, kv_arr, jk, jv, kw, dil, maxdist, slopes, sink, want_lse, logit_bound):
    B, S, _ = q_arr.shape
    tb = min(BAND_TOKENS, S)
    pb = BAND * dil
    assert S % tb == 0 and tb % pb == 0 and maxdist <= BAND
    ratio = tb // pb
    has_sink = sink is not None
    kern = functools.partial(_banded_kernel, dil=dil, maxdist=maxdist, slopes=tuple(slopes),
                             kv_chunks=kw // LANES, has_sink=has_sink, want_lse=want_lse)
    prev = lambda j: pl.BlockSpec((1, pb, LANES), lambda b, i: (b, jnp.maximum(i * ratio - 1, 0), j))
    cur = lambda j: pl.BlockSpec((1, tb, LANES), lambda b, i: (b, i, j))
    in_specs = [cur(2 * jq), cur(2 * jq + 1)]
    args = [q_arr, q_arr]
    for c in range(kw // LANES):
        jkc, jvc = jk * (kw // LANES) + c, jv * (kw // LANES) + c
        in_specs += [prev(jkc), cur(jkc), prev(jvc), cur(jvc)]
        args += [kv_arr] * 4
    if has_sink:
        in_specs.insert(0, pl.BlockSpec(memory_space=pltpu.SMEM))
        args.insert(0, sink)
    n_out = 2 if want_lse else 1
    name = f"banded_d{dil}" if want_lse else "swa_attn"
    call = lambda bounded: pl.pallas_call(
        functools.partial(kern, bounded=bounded),
        grid=(B, S // tb),
        in_specs=in_specs,
        out_specs=[pl.BlockSpec((1, tb, 256), lambda b, i: (b, i, 0))] * n_out,
        out_shape=[jax.ShapeDtypeStruct((B, S, 256), F32)] * n_out,
        scratch_shapes=[pltpu.VMEM((tb, LANES), F32)] * n_out,
        compiler_params=_cparams(("parallel", "parallel")),
        name=name + ("_bounded" if bounded else "_rowmax"),
    )
    outs = lax.cond(logit_bound <= EXP2_SAFE, lambda *a: call(True)(*a), lambda *a: call(False)(*a), *args)
    return [o.reshape(B * S, 256) for o in outs]


def _wo_router_kernel(x_ref, oa_ref, ob0_ref, ob1_ref, ob2_ref, l0_ref, l1_ref, l2_ref, oc_ref, od_ref,
                      wo_ref, g2_ref, wr_ref, br_ref, ltri_ref, h_ref, xn_ref, route_ref, route_t_ref, cnt_ref,
                      carry_ref):
    i = pl.program_id(0)
    tm = x_ref.shape[0]

    @pl.when(i == 0)
    def _():
        carry_ref[...] = jnp.zeros_like(carry_ref)

    la, lb, lc = l0_ref[...], l1_ref[...], l2_ref[...]
    mx = jnp.maximum(jnp.maximum(la, lb), lc)
    ea, eb, ec = jnp.exp(la - mx), jnp.exp(lb - mx), jnp.exp(lc - mx)
    ob = (ea * ob0_ref[...] + eb * ob1_ref[...] + ec * ob2_ref[...]) / (ea + eb + ec)
    mix = jnp.concatenate([oa_ref[...], ob.astype(BF16), oc_ref[...], od_ref[...].astype(BF16)], axis=1)
    h = x_ref[...] + jnp.dot(mix, wo_ref[...], preferred_element_type=F32)
    h_ref[...] = h
    xn = h * lax.rsqrt(jnp.mean(h * h, axis=-1, keepdims=True) + NORM_EPS) * g2_ref[...]
    _store_row_tiles(xn_ref, xn)

    xh = xn.astype(BF16)
    xl = (xn - xh.astype(F32)).astype(BF16)
    z = (jnp.dot(xh, wr_ref[0], preferred_element_type=F32) + jnp.dot(xl, wr_ref[0], preferred_element_type=F32)
         + jnp.dot(xh, wr_ref[1], preferred_element_type=F32)) + br_ref[...]
    lane = lax.broadcasted_iota(jnp.int32, (tm, LANES), 1)
    lanef = lane.astype(F32)
    big = float(LANES)
    zg = jnp.where((lane >= N_EXPERTS) & (lane < N_EXPERTS + N_GROUPS), z, NEG_INF)
    mg = jnp.max(zg, axis=1, keepdims=True)
    p_g = 1.0 / jnp.sum(jnp.exp(zg - mg), axis=1, keepdims=True)
    gsel = jnp.min(jnp.where(zg == mg, lanef, big), axis=1, keepdims=True) - float(N_EXPERTS)
    lo = gsel * float(EXPERTS_PER_GROUP)
    ze = jnp.where((lanef >= lo) & (lanef < lo + float(EXPERTS_PER_GROUP)), z, NEG_INF)
    m1 = jnp.max(ze, axis=1, keepdims=True)
    i1 = jnp.min(jnp.where(ze == m1, lanef, big), axis=1, keepdims=True)
    ze2 = jnp.where(lanef == i1, NEG_INF, ze)
    m2 = jnp.max(ze2, axis=1, keepdims=True)
    i2 = jnp.min(jnp.where(ze2 == m2, lanef, big), axis=1, keepdims=True)
    e2 = jnp.exp(m2 - m1)
    gate1 = p_g / (1.0 + e2)
    gate2 = p_g * e2 / (1.0 + e2)

    oh1 = jnp.where(lanef == i1, 1.0, 0.0)
    oh2 = jnp.where(lanef == i2, 1.0, 0.0)
    oh = oh1 + oh2
    before = carry_ref[0:1, :] + jnp.dot(ltri_ref[...], oh.astype(BF16), preferred_element_type=F32)
    r1 = jnp.sum(before * oh1, axis=1, keepdims=True)
    r2 = jnp.sum(before * oh2, axis=1, keepdims=True)
    total = carry_ref[0:1, :] + jnp.sum(oh, axis=0, keepdims=True)
    carry_ref[...] = jnp.broadcast_to(total, carry_ref.shape)
    cnt_ref[...] = jnp.broadcast_to(total, cnt_ref.shape)

    route = jnp.where(lane == 0, i1, jnp.where(lane == 1, i2, jnp.where(lane == 2, r1, jnp.where(
        lane == 3, r2, jnp.where(lane == 4, gate1, jnp.where(lane == 5, gate2, 0.0))))))
    route_ref[...] = route
    route_t_ref[...] = jnp.transpose(route)[0:8, :]


def _wo_router(xt, oa, obs, lses, oc, od, lw, ltri, tm=512):
    T = xt.shape[0]
    row = lambda w: pl.BlockSpec((tm, w), lambda i: (i, 0))
    return pl.pallas_call(
        _wo_router_kernel,
        grid=(T // tm,),
        in_specs=[row(D_MODEL), row(256), row(256), row(256), row(256), row(256), row(256), row(256), row(256),
                  row(256), _full((D_MODEL, D_MODEL)), _full((1, D_MODEL)), _full((2, D_MODEL, LANES)),
                  _full((1, LANES)), _full((tm, tm))],
        out_specs=[row(D_MODEL), pl.BlockSpec((tm * ROW_CHUNKS, LANES), lambda i: (i, 0)), row(LANES),
                   pl.BlockSpec((8, tm), lambda i: (0, i)), _full((8, LANES))],
        out_shape=[jax.ShapeDtypeStruct((T, D_MODEL), F32), jax.ShapeDtypeStruct((T * ROW_CHUNKS, LANES), F32),
                   jax.ShapeDtypeStruct((T, LANES), F32), jax.ShapeDtypeStruct((8, T), F32),
                   jax.ShapeDtypeStruct((8, LANES), F32)],
        scratch_shapes=[pltpu.VMEM((8, LANES), F32)],
        compiler_params=_cparams(("arbitrary",)),
        name="wo_router",
    )(xt, oa, obs[0], obs[1], obs[2], lses[0], lses[1], lses[2], oc, od, lw["wo"], lw["g2"], lw["wr"], lw["br"], ltri)


def _store_row_tiles(ref, x, accumulate=False):
    rows = x.shape[0]
    for c in range(ROW_CHUNKS):
        idx = pl.ds(c, rows, stride=ROW_CHUNKS)
        chunk = x[:, c * LANES:(c + 1) * LANES]
        ref[idx, :] = ref[idx, :] + chunk if accumulate else chunk


def _load_row_tiles(ref, rows, lead=()):
    return jnp.concatenate([ref[lead + (pl.ds(c, rows, stride=ROW_CHUNKS), slice(None))] for c in range(ROW_CHUNKS)],
                           axis=1)


def _tile_copy(src, dst, sem):
    return pltpu.make_async_copy(src, dst, sem)


def _tile_major(dest_t, tile):
    k, T = dest_t.shape
    return dest_t.reshape(k, T // tile, tile).transpose(1, 0, 2).reshape(-1)


def _scatter_kernel(dest_ref, x_ref, buf_out, sem, *, ts):
    def issue(t, carry):
        for k in range(TOP_K):
            _tile_copy(x_ref.at[t], buf_out.at[dest_ref[k * ts + t]], sem).start(priority=k)
        return carry

    lax.fori_loop(0, ts, issue, 0, unroll=8)
    for k in range(TOP_K):
        _tile_copy(x_ref, buf_out.at[pl.ds(0, ts)], sem).wait()


def _scatter_rows(xn_tiles, dest_t, ts=1024):
    T = xn_tiles.shape[0]
    dest_flat = _tile_major(dest_t, ts)
    return pl.pallas_call(
        functools.partial(_scatter_kernel, ts=ts),
        grid=(T // ts,),
        in_specs=[pl.BlockSpec((TOP_K * ts,), lambda i: (i,), memory_space=pltpu.SMEM),
                  pl.BlockSpec((ts, ROW_CHUNKS, LANES), lambda i: (i, 0, 0))],
        out_specs=pl.BlockSpec(memory_space=pl.ANY),
        out_shape=jax.ShapeDtypeStruct((TOP_K * T, ROW_CHUNKS, LANES), xn_tiles.dtype),
        scratch_shapes=[pltpu.SemaphoreType.DMA(())],
        compiler_params=_cparams(("arbitrary",)),
        name="moe_scatter",
    )(dest_flat, xn_tiles)


def _expert_items(counts, n_blocks):
    n_items_max = n_blocks + N_EXPERTS
    end = jnp.cumsum(counts)
    start = end - counts
    first_b = start // ROW_BLOCK
    per_expert = jnp.where(counts > 0, (end - 1) // ROW_BLOCK - first_b + 1, 0)
    item_end = jnp.cumsum(per_expert)
    item_start = item_end - per_expert
    n_items = item_end[-1:]
    idx = jnp.minimum(jnp.arange(n_items_max, dtype=jnp.int32), n_items - 1)
    e = jnp.minimum(jnp.sum((item_end[None, :] <= idx[:, None]).astype(jnp.int32), axis=1), N_EXPERTS - 1)
    onehot = (e[:, None] == jnp.arange(N_EXPERTS, dtype=jnp.int32)[None, :]).astype(jnp.int32)
    pick = lambda table: jnp.sum(onehot * table[None, :], axis=1)
    b = pick(first_b) + idx - pick(item_start)
    lo = jnp.maximum(pick(start) - b * ROW_BLOCK, 0)
    hi = jnp.minimum(pick(end) - b * ROW_BLOCK, ROW_BLOCK)
    return e, b, lo, hi, n_items.astype(jnp.int32), start


def _expert_kernel(e_ref, b_ref, lo_ref, hi_ref, n_ref, x_ref, w1_ref, w3_ref, w2_ref, y_ref):
    del e_ref
    i = pl.program_id(0)

    @pl.when(i < n_ref[0])
    def _():
        x = _load_row_tiles(x_ref, ROW_BLOCK).astype(BF16)
        a = jnp.dot(x, w1_ref[0, 0].astype(BF16), preferred_element_type=F32)
        b = jnp.dot(x, w3_ref[0, 0].astype(BF16), preferred_element_type=F32)
        hid = (a / (1.0 + jnp.exp(-a))) * b
        y = jnp.dot(hid.astype(BF16), w2_ref[0, 0].astype(BF16), preferred_element_type=F32)
        row = lax.broadcasted_iota(jnp.int32, y.shape, 0)
        y = jnp.where((row >= lo_ref[i]) & (row < hi_ref[i]), y, 0.0)
        first_visit = jnp.logical_or(i == 0, b_ref[i] != b_ref[jnp.maximum(i - 1, 0)])

        @pl.when(first_visit)
        def _():
            _store_row_tiles(y_ref, y)

        @pl.when(jnp.logical_not(first_visit))
        def _():
            _store_row_tiles(y_ref, y, accumulate=True)


def _expert_mlp(xbuf, items, layer, w1, w3, w2):
    D = D_MODEL
    e, b, lo, hi, n_items = items
    x_map = lambda i, e_, b_, lo_, hi_, n_: (b_[i], 0)
    w_map = lambda i, e_, b_, lo_, hi_, n_: (layer, e_[i], 0, 0)
    grid_spec = pltpu.PrefetchScalarGridSpec(
        num_scalar_prefetch=5,
        grid=(e.shape[0],),
        in_specs=[pl.BlockSpec((ROW_BLOCK * ROW_CHUNKS, LANES), x_map),
                  pl.BlockSpec((1, 1, D, D_EXPERT), w_map),
                  pl.BlockSpec((1, 1, D, D_EXPERT), w_map),
                  pl.BlockSpec((1, 1, D_EXPERT, D), w_map)],
        out_specs=pl.BlockSpec((ROW_BLOCK * ROW_CHUNKS, LANES), x_map),
    )
    return pl.pallas_call(
        _expert_kernel,
        grid_spec=grid_spec,
        out_shape=jax.ShapeDtypeStruct(xbuf.shape, xbuf.dtype),
        compiler_params=_cparams(("arbitrary",)),
        name="moe_experts",
    )(e, b, lo, hi, n_items, xbuf, w1, w3, w2)


def _combine_kernel(dest_ref, dest_next_ref, h_ref, route_ref, y_hbm, o_ref, rows_scr, sems, *, tc):
    i = pl.program_id(0)
    n = pl.num_programs(0)
    slot = i % 2

    def gather(d_ref, s):
        def issue(t, carry):
            for k in range(TOP_K):
                dst = rows_scr.at[s, k, pl.ds(pl.multiple_of(t * ROW_CHUNKS, ROW_CHUNKS), ROW_CHUNKS)]
                _tile_copy(y_hbm.at[d_ref[k * tc + t]], dst, sems.at[s]).start(priority=k)
            return carry

        lax.fori_loop(0, tc, issue, 0, unroll=8)

    @pl.when(i == 0)
    def _():
        gather(dest_ref, 0)

    @pl.when(i + 1 < n)
    def _():
        gather(dest_next_ref, 1 - slot)

    for k in range(TOP_K):
        _tile_copy(rows_scr.at[slot, k], rows_scr.at[slot, k], sems.at[slot]).wait()
    route = route_ref[...]
    gates = [jnp.broadcast_to(route[:, 4 + k:5 + k], (tc, LANES)) for k in range(TOP_K)]
    rows = [_load_row_tiles(rows_scr, tc, lead=(slot, k)) for k in range(TOP_K)]
    for c in range(ROW_CHUNKS):
        cols = slice(c * LANES, (c + 1) * LANES)
        o_ref[:, cols] = h_ref[:, cols] + gates[0] * rows[0][:, cols] + gates[1] * rows[1][:, cols]


def _combine(h, route, ybuf, dest_t, tc=256):
    T, D = h.shape
    n = T // tc
    dest_flat = _tile_major(dest_t, tc)
    return pl.pallas_call(
        functools.partial(_combine_kernel, tc=tc),
        grid=(n,),
        in_specs=[pl.BlockSpec((TOP_K * tc,), lambda i: (i,), memory_space=pltpu.SMEM),
                  pl.BlockSpec((TOP_K * tc,), lambda i: (jnp.minimum(i + 1, n - 1),), memory_space=pltpu.SMEM),
                  pl.BlockSpec((tc, D), lambda i: (i, 0)), pl.BlockSpec((tc, LANES), lambda i: (i, 0)),
                  pl.BlockSpec(memory_space=pl.ANY)],
        out_specs=pl.BlockSpec((tc, D), lambda i: (i, 0)),
        out_shape=jax.ShapeDtypeStruct((T, D), F32),
        scratch_shapes=[pltpu.VMEM((2, TOP_K, tc * ROW_CHUNKS, LANES), ybuf.dtype), pltpu.SemaphoreType.DMA((2,))],
        compiler_params=_cparams(("arbitrary",)),
        name="moe_combine",
    )(dest_flat, dest_flat, h, route, ybuf)


def _prep_weights(norm1_g, w_in, mla_gcq, mla_gckv, mla_wuq, mla_wukv, qk_gq, qk_gk, fox_bf, sink, w_o,
                  norm2_g, w_rg, b_rg, w_re, b_re):
    L = w_in.shape[0]
    offs = np.concatenate([[0], np.cumsum(IN_SPLITS)])
    cq, ckv, kr, pb, pc, fg, pdq, pdkv = [w_in[:, :, offs[j]:offs[j + 1]] for j in range(8)]
    z = lambda *shape: jnp.zeros((L,) + shape, F32)
    perm = np.array([0, 2, 1, 3])
    ublock = jnp.concatenate([cq, fg, z(D_MODEL, 256 - MLA_Q_RANK - HEADS)], axis=2)
    kr_rep = jnp.concatenate([z(D_MODEL, MLA_NOPE), kr] * HEADS, axis=2)
    dq = pdq.reshape(L, D_MODEL, HEADS, HEAD_DIM)[:, :, perm].reshape(L, D_MODEL, GROUP_W)
    wp = jnp.concatenate([ckv, ublock, kr_rep, pb, pc, dq, pdkv], axis=2).astype(BF16)

    wuq = jnp.concatenate([mla_wuq, z(256 - MLA_Q_RANK, GROUP_W)], axis=1).astype(BF16)
    wukv = mla_wukv.reshape(L, MLA_KV_RANK, HEADS, MLA_NOPE + HEAD_DIM)
    wuk = jnp.concatenate([wukv[..., :MLA_NOPE], z(MLA_KV_RANK, HEADS, MLA_ROPE)], axis=3)
    wuk = wuk.reshape(L, MLA_KV_RANK, GROUP_W).astype(BF16)
    wuv = wukv[..., MLA_NOPE:].reshape(L, MLA_KV_RANK, GROUP_W).astype(BF16)
    gcq = jnp.concatenate([mla_gcq, z(256 - MLA_Q_RANK)], axis=1)[:, None, :]
    scale = HEAD_DIM ** -0.5
    gq = jnp.tile(qk_gq, (1, 1, HEADS)) * (scale * LOG2E)
    gk = jnp.tile(qk_gk, (1, 1, HEADS))
    bound = HEAD_DIM * jnp.max(jnp.abs(gq), axis=2) * jnp.max(jnp.abs(gk), axis=2)
    fb = z(1, LANES).at[:, 0, FG_LANE:FG_LANE + HEADS].set(fox_bf)
    wo_d = w_o[:, 3 * GROUP_W:].reshape(L, HEADS, HEAD_DIM, D_MODEL)[:, perm].reshape(L, GROUP_W, D_MODEL)
    wo = jnp.concatenate([w_o[:, :3 * GROUP_W], wo_d], axis=1).astype(BF16)
    wr = jnp.concatenate([w_re, w_rg, z(D_MODEL, LANES - N_EXPERTS - N_GROUPS)], axis=2)
    wr_hi = wr.astype(BF16)
    wr = jnp.stack([wr_hi, (wr - wr_hi.astype(F32)).astype(BF16)], axis=1)
    br = jnp.concatenate([b_re, b_rg, z(LANES - N_EXPERTS - N_GROUPS)], axis=1)[:, None, :]
    return dict(g1=norm1_g[:, None, :], wp=wp, gcq=gcq, gckv=mla_gckv[:, None, :], wuq=wuq, wuk=wuk, wuv=wuv,
                gq=gq, gk=gk, bound=bound, fb=fb, sink=sink[:, perm], wo=wo, g2=norm2_g[:, None, :], wr=wr, br=br)


def _tables(seq_len, tm):
    half = MLA_ROPE // 2
    inv = ROPE_BASE ** (-jnp.arange(0, MLA_ROPE, 2, dtype=F32) / MLA_ROPE)
    ang = jnp.arange(seq_len, dtype=F32)[:, None] * inv[None, :]
    cos, sin = jnp.cos(ang), jnp.sin(ang)
    one = jnp.ones((seq_len, MLA_NOPE), F32)
    zn = jnp.zeros((seq_len, MLA_NOPE), F32)
    zh = jnp.zeros((seq_len, half), F32)
    rc = jnp.concatenate([one, cos, cos] * 2, axis=1)
    rsa = jnp.concatenate([zn, -sin, zh] * 2, axis=1)
    rsb = jnp.concatenate([zn, zh, sin] * 2, axis=1)
    seg = np.arange(256) // HEAD_DIM
    e = jnp.asarray(seg[:, None] == seg[None, :], BF16)
    idx = np.arange(tm)
    tril = jnp.asarray(idx[None, :] <= idx[:, None], BF16)
    ltri = jnp.asarray(idx[None, :] < idx[:, None], BF16)
    return dict(rc=rc, rsa=rsa, rsb=rsb, e=e, tril=tril, ltri=ltri)


def _alibi():
    n = 2 * HEADS
    s = [2.0 ** (-8.0 * i / n) for i in range(1, n + 1)]
    return s[HEADS:], s[:HEADS]


def kernel(x, norm1_g, w_in, mla_gcq, mla_gckv, mla_wuq, mla_wukv, qk_gq, qk_gk, fox_bf, sink, w_o,
           norm2_g, w_rg, b_rg, w_re, b_re, w1, w3, w2):
    B, S, D = x.shape
    T = B * S
    depth = w_in.shape[0]
    tm = 512
    tabs = _tables(S, tm)
    slopes_dil, slopes_swa = _alibi()
    slopes_swa_perm = [slopes_swa[0], slopes_swa[2], slopes_swa[1], slopes_swa[3]]
    n_assign = T * TOP_K
    assert n_assign % ROW_BLOCK == 0

    all_w = _prep_weights(norm1_g, w_in, mla_gcq, mla_gckv, mla_wuq, mla_wukv, qk_gq, qk_gk, fox_bf, sink,
                          w_o, norm2_g, w_rg, b_rg, w_re, b_re)
    xt = x.reshape(T, D)
    for l in range(depth):
        lw = {name: arr[l] for name, arr in all_w.items()}
        qkv, qkvb, vt, cum, cum8 = _inproj(xt, S, lw, tabs, tm=tm)
        qkv3 = qkv.reshape(B, S, QKV_COLS)
        qkvb3 = qkvb.reshape(B, S, QKVB_COLS)
        oa = _dense_attn(qkv3, vt, 0, None, None, QA, KA, VA, lw["bound"][0]).reshape(T, GROUP_W)
        oc = _dense_attn(qkv3, vt, 2, cum.reshape(B, S, LANES), cum8, QC, KC, VC, lw["bound"][2]).reshape(T, GROUP_W)
        obs, lses = [], []
        for window, dil in DILATED_PAIRS:
            o, lse = _banded(qkvb3, 0, qkvb3, 1, 2, 256, dil, window // dil, slopes_dil, None, True, lw["bound"][1])
            obs.append(o)
            lses.append(lse)
        swa_bound = jnp.maximum(lw["bound"][3], jnp.max(jnp.abs(lw["sink"])) * LOG2E)
        (od,) = _banded(qkv3, QD, qkv3, 2 * KVD, 2 * KVD + 1, 128, 1, SWA_WINDOW - 1, slopes_swa_perm,
                        lw["sink"], False, swa_bound)
        h, xn, route, route_t, cnt = _wo_router(xt, oa, obs, lses, oc, od, lw, tabs["ltri"], tm=tm)

        counts = cnt[0, :N_EXPERTS].astype(jnp.int32)
        *items, start = _expert_items(counts, n_assign // ROW_BLOCK)
        expert = route_t[0:TOP_K].astype(jnp.int32)
        rank = route_t[TOP_K:2 * TOP_K].astype(jnp.int32)
        hit = expert[None] == jnp.arange(N_EXPERTS, dtype=jnp.int32)[:, None, None]
        dest = rank + jnp.sum(jnp.where(hit, start[:, None, None], 0), axis=0)

        xbuf = _scatter_rows(xn.reshape(T, ROW_CHUNKS, LANES), dest)
        ybuf = _expert_mlp(xbuf.reshape(n_assign * ROW_CHUNKS, LANES), items, l, w1, w3, w2)
        xt = _combine(h, route, ybuf.reshape(n_assign, ROW_CHUNKS, LANES), dest)
    return xt.reshape(B, S, D)
```
